```python
import math
import jax, jax.numpy as jnp
from jax import lax
import numpy as np

D_MODEL = 2048
BATCH = 8
SEQ = 4096
DEPTH = 2

MEM_LEN = 256
CONV_WIDTH = D_MODEL // 4
CONV_KERNEL = 31
HEAD_DIM = 128
ATTN_WIDTH = D_MODEL // 2
ATTN_HEADS = ATTN_WIDTH // HEAD_DIM
SB_BLOCK = 128
LRU_WIDTH = D_MODEL // 4
LRU_HEAD_DIM = 128
LRU_HEADS = LRU_WIDTH // LRU_HEAD_DIM
LRU_CONV_KERNEL = 4
LRU_C = 8.0
MIX_WIDTH = CONV_WIDTH + ATTN_WIDTH + LRU_WIDTH
XATTN_HEADS = 4
XATTN_HEAD_DIM = 128
XATTN_WIDTH = XATTN_HEADS * XATTN_HEAD_DIM
IN_SIZES = (CONV_WIDTH, CONV_WIDTH, CONV_WIDTH,
            ATTN_WIDTH, ATTN_WIDTH, ATTN_WIDTH, ATTN_WIDTH,
            LRU_WIDTH, LRU_WIDTH)
IN_WIDTH = 3 * CONV_WIDTH + 4 * ATTN_WIDTH + 2 * LRU_WIDTH

kernel_name = "hymba_style_conv_stickbreak_rglru_trunk"


def rms_norm(x, g, eps=1e-6):
    xf = x.astype(jnp.float32)
    y = xf * lax.rsqrt(jnp.mean(xf * xf, axis=-1, keepdims=True) + eps)
    return (y * g.astype(jnp.float32)).astype(x.dtype)


def layer_norm(x, g, b, eps=1e-5):
    xf = x.astype(jnp.float32)
    mu = jnp.mean(xf, axis=-1, keepdims=True)
    var = jnp.mean(jnp.square(xf - mu), axis=-1, keepdims=True)
    y = (xf - mu) * lax.rsqrt(var + eps)
    return (y * g.astype(jnp.float32) + b.astype(jnp.float32)).astype(x.dtype)


def causal_depthwise_conv(x, w, b):
    K, C = w.shape
    y = lax.conv_general_dilated(
        x, w[:, None, :].astype(x.dtype), window_strides=(1,),
        padding=((K - 1, 0),), dimension_numbers=("NWC", "WIO", "NWC"),
        feature_group_count=C)
    return y + b.astype(x.dtype)


def conformer_conv(val, glu_gate, dw_w, dw_b, ln_g, ln_b, pw_w):
    u = val * jax.nn.sigmoid(glu_gate)
    u = causal_depthwise_conv(u, dw_w, dw_b)
    u = jax.nn.silu(layer_norm(u, ln_g, ln_b))
    return u @ pw_w


def stick_breaking_attention(q, k, v):
    S = q.shape[2]
    scale = HEAD_DIM ** -0.5
    outs = []
    for qb in range(S // SB_BLOCK):
        start = qb * SB_BLOCK
        end = start + SB_BLOCK
        z = jnp.einsum("bhqd,bhkd->bhqk", q[:, :, start:end], k[:, :, :end]).astype(jnp.float32) * scale
        t_pos = start + jnp.arange(SB_BLOCK)
        s_pos = jnp.arange(end)
        causal = s_pos[None, :] < t_pos[:, None]
        log_1mb = jnp.where(causal, jax.nn.log_sigmoid(-z), 0.0)
        after = jnp.sum(log_1mb, axis=-1, keepdims=True) - jnp.cumsum(log_1mb, axis=-1)
        w = jnp.where(causal, jnp.exp(jax.nn.log_sigmoid(z) + after), 0.0)
        outs.append(jnp.einsum("bhqk,bhkd->bhqd", w.astype(v.dtype), v[:, :, :end]))
    return jnp.concatenate(outs, axis=2)


def rg_lru(xc, wa, ba, wx, bx, lam):
    B, S, W = xc.shape
    xh = xc.reshape(B, S, LRU_HEADS, LRU_HEAD_DIM)
    r = jax.nn.sigmoid(jnp.einsum("bsnd,nde->bsne", xh, wa).reshape(B, S, W) + ba)
    i = jax.nn.sigmoid(jnp.einsum("bsnd,nde->bsne", xh, wx).reshape(B, S, W) + bx)
    log_a = -LRU_C * r.astype(jnp.float32) * jax.nn.softplus(-lam.astype(jnp.float32))
    a = jnp.exp(log_a)
    mult = jnp.sqrt(-jnp.expm1(2.0 * log_a))
    b = mult * (i * xc).astype(jnp.float32)

    def combine(left, right):
        a1, b1 = left
        a2, b2 = right
        return a1 * a2, a2 * b1 + b2

    _, h = lax.associative_scan(combine, (a, b), axis=1)
    return h.astype(xc.dtype)


def hybrid_mixer(h, w_in, conv_dw_w, conv_dw_b, conv_ln_g, conv_ln_b, conv_pw_w,
                 lru_conv_w, lru_conv_b, lru_wa, lru_ba, lru_wx, lru_bx, lru_lambda,
                 out_norm_conv, out_norm_attn, out_norm_lru, w_out):
    B, S, _ = h.shape
    u = h @ w_in
    points = []
    acc = 0
    for size in IN_SIZES[:-1]:
        acc += size
        points.append(acc)
    c_val, c_glu, c_gate, q, k, v, a_gate, r_x, r_gate = jnp.split(u, points, axis=-1)

    y_conv = conformer_conv(c_val, c_glu, conv_dw_w, conv_dw_b, conv_ln_g, conv_ln_b, conv_pw_w)

    def heads(t):
        return t.reshape(B, S, ATTN_HEADS, HEAD_DIM).transpose(0, 2, 1, 3)
    y_attn = stick_breaking_attention(heads(q), heads(k), heads(v))
    y_attn = y_attn.transpose(0, 2, 1, 3).reshape(B, S, ATTN_WIDTH)

    xc = causal_depthwise_conv(r_x, lru_conv_w, lru_conv_b)
    y_lru = rg_lru(xc, lru_wa, lru_ba, lru_wx, lru_bx, lru_lambda)

    y = jnp.concatenate([
        rms_norm(y_conv, out_norm_conv) * jax.nn.silu(c_gate),
        rms_norm(y_attn, out_norm_attn) * jax.nn.silu(a_gate),
        rms_norm(y_lru, out_norm_lru) * jax.nn.silu(r_gate),
    ], axis=-1)
    return y @ w_out


def memory_cross_attention(h, memn, wq, wkv, wo):
    B, S, _ = h.shape
    M = memn.shape[1]
    q = (h @ wq).reshape(B, S, XATTN_HEADS, XATTN_HEAD_DIM)
    k, v = jnp.split(memn @ wkv, 2, axis=-1)
    k = k.reshape(B, M, XATTN_HEADS, XATTN_HEAD_DIM)
    v = v.reshape(B, M, XATTN_HEADS, XATTN_HEAD_DIM)
    s = jnp.einsum("bqhd,bkhd->bhqk", q, k).astype(jnp.float32) * (XATTN_HEAD_DIM ** -0.5)
    p = jax.nn.softmax(s, axis=-1).astype(v.dtype)
    o = jnp.einsum("bhqk,bkhd->bqhd", p, v).reshape(B, S, XATTN_WIDTH)
    return o @ wo


def _fwd_setup_inputs(seed: int = 0) -> dict:
    key = jax.random.key(seed)
    ks = iter(jax.random.split(key, 32))
    f32 = jnp.float32

    def normal(shape, scale):
        return jax.random.normal(next(ks), shape, f32) * scale

    def gain(shape):
        return 1.0 + normal(shape, 0.02)

    u = jax.random.uniform(next(ks), (DEPTH, LRU_WIDTH), f32, minval=0.9, maxval=0.999)
    a0 = u ** (1.0 / LRU_C)
    lru_lambda = jnp.log(a0) - jnp.log1p(-a0)

    return {
        "x": normal((BATCH, SEQ, D_MODEL), 1.0),
        "mem": normal((BATCH, MEM_LEN, D_MODEL), 1.0),
        "mix_norm_g": gain((DEPTH, D_MODEL)),
        "w_in": normal((DEPTH, D_MODEL, IN_WIDTH), D_MODEL ** -0.5),
        "conv_dw_w": normal((DEPTH, CONV_KERNEL, CONV_WIDTH), CONV_KERNEL ** -0.5),
        "conv_dw_b": normal((DEPTH, CONV_WIDTH), 0.01),
        "conv_ln_g": gain((DEPTH, CONV_WIDTH)),
        "conv_ln_b": normal((DEPTH, CONV_WIDTH), 0.01),
        "conv_pw_w": normal((DEPTH, CONV_WIDTH, CONV_WIDTH), CONV_WIDTH ** -0.5),
        "lru_conv_w": normal((DEPTH, LRU_CONV_KERNEL, LRU_WIDTH), LRU_CONV_KERNEL ** -0.5),
        "lru_conv_b": normal((DEPTH, LRU_WIDTH), 0.01),
        "lru_wa": normal((DEPTH, LRU_HEADS, LRU_HEAD_DIM, LRU_HEAD_DIM), LRU_HEAD_DIM ** -0.5),
        "lru_ba": normal((DEPTH, LRU_WIDTH), 0.01),
        "lru_wx": normal((DEPTH, LRU_HEADS, LRU_HEAD_DIM, LRU_HEAD_DIM), LRU_HEAD_DIM ** -0.5),
        "lru_bx": normal((DEPTH, LRU_WIDTH), 0.01),
        "lru_lambda": lru_lambda,
        "out_norm_conv": gain((DEPTH, CONV_WIDTH)),
        "out_norm_attn": gain((DEPTH, ATTN_WIDTH)),
        "out_norm_lru": gain((DEPTH, LRU_WIDTH)),
        "w_out": normal((DEPTH, MIX_WIDTH, D_MODEL), MIX_WIDTH ** -0.5),
        "xattn_norm_g": gain((DEPTH, D_MODEL)),
        "mem_norm_g": gain((DEPTH, D_MODEL)),
        "xattn_wq": normal((DEPTH, D_MODEL, XATTN_WIDTH), D_MODEL ** -0.5),
        "xattn_wkv": normal((DEPTH, D_MODEL, 2 * XATTN_WIDTH), D_MODEL ** -0.5),
        "xattn_wo": normal((DEPTH, XATTN_WIDTH, D_MODEL), XATTN_WIDTH ** -0.5),
        "final_norm_g": gain((D_MODEL,)),
    }


def _fwd_reference(x, mem, mix_norm_g, w_in, conv_dw_w, conv_dw_b, conv_ln_g, conv_ln_b, conv_pw_w,
              lru_conv_w, lru_conv_b, lru_wa, lru_ba, lru_wx, lru_bx, lru_lambda,
              out_norm_conv, out_norm_attn, out_norm_lru, w_out,
              xattn_norm_g, mem_norm_g, xattn_wq, xattn_wkv, xattn_wo, final_norm_g):
    for l in range(DEPTH):
        h = rms_norm(x, mix_norm_g[l])
        x = x + hybrid_mixer(h, w_in[l], conv_dw_w[l], conv_dw_b[l], conv_ln_g[l], conv_ln_b[l],
                             conv_pw_w[l], lru_conv_w[l], lru_conv_b[l], lru_wa[l], lru_ba[l],
                             lru_wx[l], lru_bx[l], lru_lambda[l], out_norm_conv[l],
                             out_norm_attn[l], out_norm_lru[l], w_out[l])
        h = rms_norm(x, xattn_norm_g[l])
        memn = rms_norm(mem, mem_norm_g[l])
        x = x + memory_cross_attention(h, memn, xattn_wq[l], xattn_wkv[l], xattn_wo[l])
    return rms_norm(x, final_norm_g)


import jax as _jax
import jax.numpy as _jnp

TWIN_FORMAT = 'train_step'
FWD_PARAMS = ['x', 'mem', 'mix_norm_g', 'w_in', 'conv_dw_w', 'conv_dw_b', 'conv_ln_g', 'conv_ln_b', 'conv_pw_w', 'lru_conv_w', 'lru_conv_b', 'lru_wa', 'lru_ba', 'lru_wx', 'lru_bx', 'lru_lambda', 'out_norm_conv', 'out_norm_attn', 'out_norm_lru', 'w_out', 'xattn_norm_g', 'mem_norm_g', 'xattn_wq', 'xattn_wkv', 'xattn_wo', 'final_norm_g']
TWIN_WEIGHTS = ['mix_norm_g', 'w_in', 'conv_dw_w', 'conv_dw_b', 'conv_ln_g', 'conv_ln_b', 'conv_pw_w', 'lru_conv_w', 'lru_conv_b', 'lru_wa', 'lru_ba', 'lru_wx', 'lru_bx', 'lru_lambda', 'out_norm_conv', 'out_norm_attn', 'out_norm_lru', 'w_out', 'xattn_norm_g', 'mem_norm_g', 'xattn_wq', 'xattn_wkv', 'xattn_wo', 'final_norm_g']
TWIN_DIFF_INPUT = 'x'
TWIN_INPUTS = ['x', 'mem', 'mix_norm_g', 'w_in', 'conv_dw_w', 'conv_dw_b', 'conv_ln_g', 'conv_ln_b', 'conv_pw_w', 'lru_conv_w', 'lru_conv_b', 'lru_wa', 'lru_ba', 'lru_wx', 'lru_bx', 'lru_lambda', 'out_norm_conv', 'out_norm_attn', 'out_norm_lru', 'w_out', 'xattn_norm_g', 'mem_norm_g', 'xattn_wq', 'xattn_wkv', 'xattn_wo', 'final_norm_g', 'loss_target', 'm_mix_norm_g', 'm_w_in', 'm_conv_dw_w', 'm_conv_dw_b', 'm_conv_ln_g', 'm_conv_ln_b', 'm_conv_pw_w', 'm_lru_conv_w', 'm_lru_conv_b', 'm_lru_wa', 'm_lru_ba', 'm_lru_wx', 'm_lru_bx', 'm_lru_lambda', 'm_out_norm_conv', 'm_out_norm_attn', 'm_out_norm_lru', 'm_w_out', 'm_xattn_norm_g', 'm_mem_norm_g', 'm_xattn_wq', 'm_xattn_wkv', 'm_xattn_wo', 'm_final_norm_g', 'v_mix_norm_g', 'v_w_in', 'v_conv_dw_w', 'v_conv_dw_b', 'v_conv_ln_g', 'v_conv_ln_b', 'v_conv_pw_w', 'v_lru_conv_w', 'v_lru_conv_b', 'v_lru_wa', 'v_lru_ba', 'v_lru_wx', 'v_lru_bx', 'v_lru_lambda', 'v_out_norm_conv', 'v_out_norm_attn', 'v_out_norm_lru', 'v_w_out', 'v_xattn_norm_g', 'v_mem_norm_g', 'v_xattn_wq', 'v_xattn_wkv', 'v_xattn_wo', 'v_final_norm_g']
TWIN_OUTPUTS = ['loss', 'grad_x', 'grad_mix_norm_g', 'grad_w_in', 'grad_conv_dw_w', 'grad_conv_dw_b', 'grad_conv_ln_g', 'grad_conv_ln_b', 'grad_conv_pw_w', 'grad_lru_conv_w', 'grad_lru_conv_b', 'grad_lru_wa', 'grad_lru_ba', 'grad_lru_wx', 'grad_lru_bx', 'grad_lru_lambda', 'grad_out_norm_conv', 'grad_out_norm_attn', 'grad_out_norm_lru', 'grad_w_out', 'grad_xattn_norm_g', 'grad_mem_norm_g', 'grad_xattn_wq', 'grad_xattn_wkv', 'grad_xattn_wo', 'grad_final_norm_g', 'delta_mix_norm_g', 'delta_w_in', 'delta_conv_dw_w', 'delta_conv_dw_b', 'delta_conv_ln_g', 'delta_conv_ln_b', 'delta_conv_pw_w', 'delta_lru_conv_w', 'delta_lru_conv_b', 'delta_lru_wa', 'delta_lru_ba', 'delta_lru_wx', 'delta_lru_bx', 'delta_lru_lambda', 'delta_out_norm_conv', 'delta_out_norm_attn', 'delta_out_norm_lru', 'delta_w_out', 'delta_xattn_norm_g', 'delta_mem_norm_g', 'delta_xattn_wq', 'delta_xattn_wkv', 'delta_xattn_wo', 'delta_final_norm_g', 'new_m_mix_norm_g', 'new_m_w_in', 'new_m_conv_dw_w', 'new_m_conv_dw_b', 'new_m_conv_ln_g', 'new_m_conv_ln_b', 'new_m_conv_pw_w', 'new_m_lru_conv_w', 'new_m_lru_conv_b', 'new_m_lru_wa', 'new_m_lru_ba', 'new_m_lru_wx', 'new_m_lru_bx', 'new_m_lru_lambda', 'new_m_out_norm_conv', 'new_m_out_norm_attn', 'new_m_out_norm_lru', 'new_m_w_out', 'new_m_xattn_norm_g', 'new_m_mem_norm_g', 'new_m_xattn_wq', 'new_m_xattn_wkv', 'new_m_xattn_wo', 'new_m_final_norm_g', 'new_v_mix_norm_g', 'new_v_w_in', 'new_v_conv_dw_w', 'new_v_conv_dw_b', 'new_v_conv_ln_g', 'new_v_conv_ln_b', 'new_v_conv_pw_w', 'new_v_lru_conv_w', 'new_v_lru_conv_b', 'new_v_lru_wa', 'new_v_lru_ba', 'new_v_lru_wx', 'new_v_lru_bx', 'new_v_lru_lambda', 'new_v_out_norm_conv', 'new_v_out_norm_attn', 'new_v_out_norm_lru', 'new_v_w_out', 'new_v_xattn_norm_g', 'new_v_mem_norm_g', 'new_v_xattn_wq', 'new_v_xattn_wkv', 'new_v_xattn_wo', 'new_v_final_norm_g']
TWIN_LEAF_KINDS = {'loss': 'loss', 'grad_x': 'grad_x', 'grad_mix_norm_g': 'grad_w', 'grad_w_in': 'grad_w', 'grad_conv_dw_w': 'grad_w', 'grad_conv_dw_b': 'grad_w', 'grad_conv_ln_g': 'grad_w', 'grad_conv_ln_b': 'grad_w', 'grad_conv_pw_w': 'grad_w', 'grad_lru_conv_w': 'grad_w', 'grad_lru_conv_b': 'grad_w', 'grad_lru_wa': 'grad_w', 'grad_lru_ba': 'grad_w', 'grad_lru_wx': 'grad_w', 'grad_lru_bx': 'grad_w', 'grad_lru_lambda': 'grad_w', 'grad_out_norm_conv': 'grad_w', 'grad_out_norm_attn': 'grad_w', 'grad_out_norm_lru': 'grad_w', 'grad_w_out': 'grad_w', 'grad_xattn_norm_g': 'grad_w', 'grad_mem_norm_g': 'grad_w', 'grad_xattn_wq': 'grad_w', 'grad_xattn_wkv': 'grad_w', 'grad_xattn_wo': 'grad_w', 'grad_final_norm_g': 'grad_w', 'delta_mix_norm_g': 'delta_w', 'delta_w_in': 'delta_w', 'delta_conv_dw_w': 'delta_w', 'delta_conv_dw_b': 'delta_w', 'delta_conv_ln_g': 'delta_w', 'delta_conv_ln_b': 'delta_w', 'delta_conv_pw_w': 'delta_w', 'delta_lru_conv_w': 'delta_w', 'delta_lru_conv_b': 'delta_w', 'delta_lru_wa': 'delta_w', 'delta_lru_ba': 'delta_w', 'delta_lru_wx': 'delta_w', 'delta_lru_bx': 'delta_w', 'delta_lru_lambda': 'delta_w', 'delta_out_norm_conv': 'delta_w', 'delta_out_norm_attn': 'delta_w', 'delta_out_norm_lru': 'delta_w', 'delta_w_out': 'delta_w', 'delta_xattn_norm_g': 'delta_w', 'delta_mem_norm_g': 'delta_w', 'delta_xattn_wq': 'delta_w', 'delta_xattn_wkv': 'delta_w', 'delta_xattn_wo': 'delta_w', 'delta_final_norm_g': 'delta_w', 'new_m_mix_norm_g': 'new_m', 'new_m_w_in': 'new_m', 'new_m_conv_dw_w': 'new_m', 'new_m_conv_dw_b': 'new_m', 'new_m_conv_ln_g': 'new_m', 'new_m_conv_ln_b': 'new_m', 'new_m_conv_pw_w': 'new_m', 'new_m_lru_conv_w': 'new_m', 'new_m_lru_conv_b': 'new_m', 'new_m_lru_wa': 'new_m', 'new_m_lru_ba': 'new_m', 'new_m_lru_wx': 'new_m', 'new_m_lru_bx': 'new_m', 'new_m_lru_lambda': 'new_m', 'new_m_out_norm_conv': 'new_m', 'new_m_out_norm_attn': 'new_m', 'new_m_out_norm_lru': 'new_m', 'new_m_w_out': 'new_m', 'new_m_xattn_norm_g': 'new_m', 'new_m_mem_norm_g': 'new_m', 'new_m_xattn_wq': 'new_m', 'new_m_xattn_wkv': 'new_m', 'new_m_xattn_wo': 'new_m', 'new_m_final_norm_g': 'new_m', 'new_v_mix_norm_g': 'new_v', 'new_v_w_in': 'new_v', 'new_v_conv_dw_w': 'new_v', 'new_v_conv_dw_b': 'new_v', 'new_v_conv_ln_g': 'new_v', 'new_v_conv_ln_b': 'new_v', 'new_v_conv_pw_w': 'new_v', 'new_v_lru_conv_w': 'new_v', 'new_v_lru_conv_b': 'new_v', 'new_v_lru_wa': 'new_v', 'new_v_lru_ba': 'new_v', 'new_v_lru_wx': 'new_v', 'new_v_lru_bx': 'new_v', 'new_v_lru_lambda': 'new_v', 'new_v_out_norm_conv': 'new_v', 'new_v_out_norm_attn': 'new_v', 'new_v_out_norm_lru': 'new_v', 'new_v_w_out': 'new_v', 'new_v_xattn_norm_g': 'new_v', 'new_v_mem_norm_g': 'new_v', 'new_v_xattn_wq': 'new_v', 'new_v_xattn_wkv': 'new_v', 'new_v_xattn_wo': 'new_v', 'new_v_final_norm_g': 'new_v'}


def _forward(args):
    return _fwd_reference(*[args[k] for k in FWD_PARAMS])


def _output_shape():
    def fwd():
        inp = _fwd_setup_inputs(0)
        return _fwd_reference(*[inp[k] for k in FWD_PARAMS])
    out = _jax.eval_shape(fwd)
    return out.shape, out.dtype

N_MICROBATCH = 1
ADAM_LR = 0.001
ADAM_B1 = 0.9
ADAM_B2 = 0.999
ADAM_EPS = 1e-08
ADAM_WD = 0.01
ADAM_STEP = 10
PER_EXAMPLE_BATCH_AXIS = {'x': 0, 'mem': 0, 'loss_target': 0}
SHARED_INPUTS = []
_WEIGHT_DTYPES = {'mix_norm_g': _jnp.float32, 'w_in': _jnp.float32, 'conv_dw_w': _jnp.float32, 'conv_dw_b': _jnp.float32, 'conv_ln_g': _jnp.float32, 'conv_ln_b': _jnp.float32, 'conv_pw_w': _jnp.float32, 'lru_conv_w': _jnp.float32, 'lru_conv_b': _jnp.float32, 'lru_wa': _jnp.float32, 'lru_ba': _jnp.float32, 'lru_wx': _jnp.float32, 'lru_bx': _jnp.float32, 'lru_lambda': _jnp.float32, 'out_norm_conv': _jnp.float32, 'out_norm_attn': _jnp.float32, 'out_norm_lru': _jnp.float32, 'w_out': _jnp.float32, 'xattn_norm_g': _jnp.float32, 'mem_norm_g': _jnp.float32, 'xattn_wq': _jnp.float32, 'xattn_wkv': _jnp.float32, 'xattn_wo': _jnp.float32, 'final_norm_g': _jnp.float32}
MOMENT_SCALE = {'mix_norm_g': 7.107526e-02, 'w_in': 3.907951e-02, 'conv_dw_w': 4.787679e-02, 'conv_dw_b': 1.055433e-01, 'conv_ln_g': 5.876754e-02, 'conv_ln_b': 5.286879e-02, 'conv_pw_w': 4.689201e-02, 'lru_conv_w': 4.966644e-02, 'lru_conv_b': 4.526518e-01, 'lru_wa': 1.354229e-02, 'lru_ba': 1.137313e-02, 'lru_wx': 2.409021e-02, 'lru_bx': 1.754660e-02, 'lru_lambda': 2.335320e-02, 'out_norm_conv': 4.562195e-02, 'out_norm_attn': 4.659585e-02, 'out_norm_lru': 4.920018e-02, 'w_out': 4.646613e-02, 'xattn_norm_g': 7.964390e-03, 'mem_norm_g': 1.305296e-02, 'xattn_wq': 1.574835e-02, 'xattn_wkv': 1.811389e-02, 'xattn_wo': 1.002657e-02, 'final_norm_g': 1.599588e+01}


def _to_microbatches(a, axis):
    t = _jnp.moveaxis(a, axis, 0)
    t = t.reshape((N_MICROBATCH, t.shape[0] // N_MICROBATCH) + t.shape[1:])
    return _jnp.moveaxis(t, 1, axis + 1)


def setup_inputs(seed: int = 0) -> dict:
    inp = _fwd_setup_inputs(seed)
    key = _jax.random.fold_in(_jax.random.key(seed), 7919)
    shape, _ = _output_shape()
    out = dict(inp)
    out["loss_target"] = _jax.random.normal(_jax.random.fold_in(key, 0), shape, _jnp.float32)
    for i, name in enumerate(TWIN_WEIGHTS):
        w = inp[name].astype(_jnp.float32)
        if MOMENT_SCALE is None:
            s = _jnp.sqrt(_jnp.mean(_jnp.square(w)) + 1e-30)
        else:
            s = MOMENT_SCALE[name]
        km, kv = _jax.random.split(_jax.random.fold_in(key, i + 1))
        out[name] = w
        out["m_" + name] = s * _jax.random.normal(km, w.shape, _jnp.float32)
        out["v_" + name] = (s * s) * _jax.random.uniform(kv, w.shape, _jnp.float32, 0.5, 1.5)
    if N_MICROBATCH > 1:
        for name, axis in PER_EXAMPLE_BATCH_AXIS.items():
            out[name] = _to_microbatches(out[name], axis)
    return {'x': out['x'], 'mem': out['mem'], 'mix_norm_g': out['mix_norm_g'], 'w_in': out['w_in'], 'conv_dw_w': out['conv_dw_w'], 'conv_dw_b': out['conv_dw_b'], 'conv_ln_g': out['conv_ln_g'], 'conv_ln_b': out['conv_ln_b'], 'conv_pw_w': out['conv_pw_w'], 'lru_conv_w': out['lru_conv_w'], 'lru_conv_b': out['lru_conv_b'], 'lru_wa': out['lru_wa'], 'lru_ba': out['lru_ba'], 'lru_wx': out['lru_wx'], 'lru_bx': out['lru_bx'], 'lru_lambda': out['lru_lambda'], 'out_norm_conv': out['out_norm_conv'], 'out_norm_attn': out['out_norm_attn'], 'out_norm_lru': out['out_norm_lru'], 'w_out': out['w_out'], 'xattn_norm_g': out['xattn_norm_g'], 'mem_norm_g': out['mem_norm_g'], 'xattn_wq': out['xattn_wq'], 'xattn_wkv': out['xattn_wkv'], 'xattn_wo': out['xattn_wo'], 'final_norm_g': out['final_norm_g'], 'loss_target': out['loss_target'], 'm_mix_norm_g': out['m_mix_norm_g'], 'm_w_in': out['m_w_in'], 'm_conv_dw_w': out['m_conv_dw_w'], 'm_conv_dw_b': out['m_conv_dw_b'], 'm_conv_ln_g': out['m_conv_ln_g'], 'm_conv_ln_b': out['m_conv_ln_b'], 'm_conv_pw_w': out['m_conv_pw_w'], 'm_lru_conv_w': out['m_lru_conv_w'], 'm_lru_conv_b': out['m_lru_conv_b'], 'm_lru_wa': out['m_lru_wa'], 'm_lru_ba': out['m_lru_ba'], 'm_lru_wx': out['m_lru_wx'], 'm_lru_bx': out['m_lru_bx'], 'm_lru_lambda': out['m_lru_lambda'], 'm_out_norm_conv': out['m_out_norm_conv'], 'm_out_norm_attn': out['m_out_norm_attn'], 'm_out_norm_lru': out['m_out_norm_lru'], 'm_w_out': out['m_w_out'], 'm_xattn_norm_g': out['m_xattn_norm_g'], 'm_mem_norm_g': out['m_mem_norm_g'], 'm_xattn_wq': out['m_xattn_wq'], 'm_xattn_wkv': out['m_xattn_wkv'], 'm_xattn_wo': out['m_xattn_wo'], 'm_final_norm_g': out['m_final_norm_g'], 'v_mix_norm_g': out['v_mix_norm_g'], 'v_w_in': out['v_w_in'], 'v_conv_dw_w': out['v_conv_dw_w'], 'v_conv_dw_b': out['v_conv_dw_b'], 'v_conv_ln_g': out['v_conv_ln_g'], 'v_conv_ln_b': out['v_conv_ln_b'], 'v_conv_pw_w': out['v_conv_pw_w'], 'v_lru_conv_w': out['v_lru_conv_w'], 'v_lru_conv_b': out['v_lru_conv_b'], 'v_lru_wa': out['v_lru_wa'], 'v_lru_ba': out['v_lru_ba'], 'v_lru_wx': out['v_lru_wx'], 'v_lru_bx': out['v_lru_bx'], 'v_lru_lambda': out['v_lru_lambda'], 'v_out_norm_conv': out['v_out_norm_conv'], 'v_out_norm_attn': out['v_out_norm_attn'], 'v_out_norm_lru': out['v_out_norm_lru'], 'v_w_out': out['v_w_out'], 'v_xattn_norm_g': out['v_xattn_norm_g'], 'v_mem_norm_g': out['v_mem_norm_g'], 'v_xattn_wq': out['v_xattn_wq'], 'v_xattn_wkv': out['v_xattn_wkv'], 'v_xattn_wo': out['v_xattn_wo'], 'v_final_norm_g': out['v_final_norm_g']}


def _loss(weights, diff, rest, loss_target):
    with _jax.named_scope("forward"):
        args = {**rest, TWIN_DIFF_INPUT: diff, **{k: w.astype(_WEIGHT_DTYPES[k]) for k, w in weights.items()}}
        y = _forward(args)
    with _jax.named_scope("loss_head"):
        err = _jnp.square(y.astype(_jnp.float32) - loss_target)
        return 0.5 * _jnp.sum(_jnp.mean(err, axis=-1)) if err.ndim else 0.5 * err


def _adamw(w, g, m, v):
    m = ADAM_B1 * m + (1.0 - ADAM_B1) * g
    v = ADAM_B2 * v + (1.0 - ADAM_B2) * _jnp.square(g)
    m_hat = m / (1.0 - ADAM_B1 ** ADAM_STEP)
    v_hat = v / (1.0 - ADAM_B2 ** ADAM_STEP)
    delta = -ADAM_LR * (m_hat / (_jnp.sqrt(v_hat) + ADAM_EPS) + ADAM_WD * w)
    return delta, m, v


def reference(x, mem, mix_norm_g, w_in, conv_dw_w, conv_dw_b, conv_ln_g, conv_ln_b, conv_pw_w, lru_conv_w, lru_conv_b, lru_wa, lru_ba, lru_wx, lru_bx, lru_lambda, out_norm_conv, out_norm_attn, out_norm_lru, w_out, xattn_norm_g, mem_norm_g, xattn_wq, xattn_wkv, xattn_wo, final_norm_g, loss_target, m_mix_norm_g, m_w_in, m_conv_dw_w, m_conv_dw_b, m_conv_ln_g, m_conv_ln_b, m_conv_pw_w, m_lru_conv_w, m_lru_conv_b, m_lru_wa, m_lru_ba, m_lru_wx, m_lru_bx, m_lru_lambda, m_out_norm_conv, m_out_norm_attn, m_out_norm_lru, m_w_out, m_xattn_norm_g, m_mem_norm_g, m_xattn_wq, m_xattn_wkv, m_xattn_wo, m_final_norm_g, v_mix_norm_g, v_w_in, v_conv_dw_w, v_conv_dw_b, v_conv_ln_g, v_conv_ln_b, v_conv_pw_w, v_lru_conv_w, v_lru_conv_b, v_lru_wa, v_lru_ba, v_lru_wx, v_lru_bx, v_lru_lambda, v_out_norm_conv, v_out_norm_attn, v_out_norm_lru, v_w_out, v_xattn_norm_g, v_mem_norm_g, v_xattn_wq, v_xattn_wkv, v_xattn_wo, v_final_norm_g):
    given = dict(x=x, mem=mem, mix_norm_g=mix_norm_g, w_in=w_in, conv_dw_w=conv_dw_w, conv_dw_b=conv_dw_b, conv_ln_g=conv_ln_g, conv_ln_b=conv_ln_b, conv_pw_w=conv_pw_w, lru_conv_w=lru_conv_w, lru_conv_b=lru_conv_b, lru_wa=lru_wa, lru_ba=lru_ba, lru_wx=lru_wx, lru_bx=lru_bx, lru_lambda=lru_lambda, out_norm_conv=out_norm_conv, out_norm_attn=out_norm_attn, out_norm_lru=out_norm_lru, w_out=w_out, xattn_norm_g=xattn_norm_g, mem_norm_g=mem_norm_g, xattn_wq=xattn_wq, xattn_wkv=xattn_wkv, xattn_wo=xattn_wo, final_norm_g=final_norm_g, loss_target=loss_target, m_mix_norm_g=m_mix_norm_g, m_w_in=m_w_in, m_conv_dw_w=m_conv_dw_w, m_conv_dw_b=m_conv_dw_b, m_conv_ln_g=m_conv_ln_g, m_conv_ln_b=m_conv_ln_b, m_conv_pw_w=m_conv_pw_w, m_lru_conv_w=m_lru_conv_w, m_lru_conv_b=m_lru_conv_b, m_lru_wa=m_lru_wa, m_lru_ba=m_lru_ba, m_lru_wx=m_lru_wx, m_lru_bx=m_lru_bx, m_lru_lambda=m_lru_lambda, m_out_norm_conv=m_out_norm_conv, m_out_norm_attn=m_out_norm_attn, m_out_norm_lru=m_out_norm_lru, m_w_out=m_w_out, m_xattn_norm_g=m_xattn_norm_g, m_mem_norm_g=m_mem_norm_g, m_xattn_wq=m_xattn_wq, m_xattn_wkv=m_xattn_wkv, m_xattn_wo=m_xattn_wo, m_final_norm_g=m_final_norm_g, v_mix_norm_g=v_mix_norm_g, v_w_in=v_w_in, v_conv_dw_w=v_conv_dw_w, v_conv_dw_b=v_conv_dw_b, v_conv_ln_g=v_conv_ln_g, v_conv_ln_b=v_conv_ln_b, v_conv_pw_w=v_conv_pw_w, v_lru_conv_w=v_lru_conv_w, v_lru_conv_b=v_lru_conv_b, v_lru_wa=v_lru_wa, v_lru_ba=v_lru_ba, v_lru_wx=v_lru_wx, v_lru_bx=v_lru_bx, v_lru_lambda=v_lru_lambda, v_out_norm_conv=v_out_norm_conv, v_out_norm_attn=v_out_norm_attn, v_out_norm_lru=v_out_norm_lru, v_w_out=v_w_out, v_xattn_norm_g=v_xattn_norm_g, v_mem_norm_g=v_mem_norm_g, v_xattn_wq=v_xattn_wq, v_xattn_wkv=v_xattn_wkv, v_xattn_wo=v_xattn_wo, v_final_norm_g=v_final_norm_g)
    weights = {n: given[n] for n in TWIN_WEIGHTS}
    shared = {n: given[n] for n in SHARED_INPUTS}
    per_example = {n: given[n] for n in ['x', 'mem']}
    grad_fn = _jax.value_and_grad(_loss, argnums=(0, 1))

    def one_microbatch(ex, loss_target):
        ex = dict(ex)
        diff = ex.pop(TWIN_DIFF_INPUT)
        return grad_fn(weights, diff, {**shared, **ex}, loss_target)

    if N_MICROBATCH == 1:
        loss, (grad_w, grad_x) = one_microbatch(per_example, given["loss_target"])
    else:
        def body(carry, xs):
            loss_sum, grad_sum = carry
            l_k, (gw_k, gx_k) = one_microbatch(xs[0], xs[1])
            with _jax.named_scope("update"):
                return (loss_sum + l_k, _jax.tree.map(_jnp.add, grad_sum, gw_k)), gx_k

        init = (_jnp.zeros((), _jnp.float32), _jax.tree.map(_jnp.zeros_like, weights))
        (loss, grad_w), grad_x = _jax.lax.scan(body, init, (per_example, given["loss_target"]))
    with _jax.named_scope("update"):
        delta_w, new_m, new_v = {}, {}, {}
        for n in TWIN_WEIGHTS:
            delta_w[n], new_m[n], new_v[n] = _adamw(weights[n], grad_w[n], given["m_" + n], given["v_" + n])
    return (loss, grad_x, *[grad_w[n] for n in TWIN_WEIGHTS], *[delta_w[n] for n in TWIN_WEIGHTS],
            *[new_m[n] for n in TWIN_WEIGHTS], *[new_v[n] for n in TWIN_WEIGHTS])
```

```python
import functools

import jax
import jax.numpy as jnp
from jax import lax
from jax.experimental import pallas as pl
from jax.experimental.pallas import tpu as pltpu

F32 = jnp.float32
BF16 = jnp.bfloat16
MESH = pl.DeviceIdType.MESH

V7X_VMEM_LIMIT_BYTES = 56 * 1024 * 1024
LANES = 128
HEAD_DIM = 128
LRU_C = 8.0
RMS_EPS = 1e-6
LN_EPS = 1e-5
CONV_HALO = 32
LRU_HALO = 8
ADAM_LR = 0.001
ADAM_B1 = 0.9
ADAM_B2 = 0.999
ADAM_EPS = 1e-08
ADAM_WD = 0.01
ADAM_STEP = 10


def _cparams(*sem):
    return pltpu.CompilerParams(dimension_semantics=sem, vmem_limit_bytes=V7X_VMEM_LIMIT_BYTES)


def _tile(n, pref):
    t = min(n, pref)
    while n % t:
        t //= 2
    return t


def _dot(a, b, dims):
    return lax.dot_general(a, b, (dims, ((), ())), preferred_element_type=F32)


def _nn(a, b):
    return _dot(a, b, ((1,), (0,)))


def _nt(a, b):
    return _dot(a, b, ((1,), (1,)))


def _tn(a, b):
    return _dot(a, b, ((0,), (0,)))


def _sigmoid(x):
    return jax.nn.sigmoid(x)


def _silu_and_grad(x):
    s = _sigmoid(x)
    return x * s, s * (1.0 + x * (1.0 - s))


def _matmul(a, b, *, mode, name, layer=None, n=None, b_off=0, add=None, out_dtype=F32, tm=512, tn=512, tk=2048):
    bshape = b.shape if layer is None else b.shape[1:]
    if mode == "nn":
        m, k = a.shape
        n = bshape[1] if n is None else n
    elif mode == "nt":
        m, k = a.shape
        n = bshape[0]
    else:
        k, m = a.shape
        n = bshape[1]
    tm, tk = _tile(m, tm), _tile(k, tk)
    tn = _tile(n, tn)
    while b_off % tn or n % tn:
        tn -= LANES
    nk = k // tk
    off = b_off // tn
    lead = () if layer is None else (None,)
    li = () if layer is None else (layer,)
    if mode == "nn":
        a_spec = pl.BlockSpec((tm, tk), lambda i, j, kk: (i, kk))
        b_spec = pl.BlockSpec(lead + (tk, tn), lambda i, j, kk: li + (kk, j + off))
        dot = _nn
    elif mode == "nt":
        a_spec = pl.BlockSpec((tm, tk), lambda i, j, kk: (i, kk))
        b_spec = pl.BlockSpec(lead + (tn, tk), lambda i, j, kk: li + (j, kk))
        dot = _nt
    else:
        a_spec = pl.BlockSpec((tk, tm), lambda i, j, kk: (kk, i))
        b_spec = pl.BlockSpec(lead + (tk, tn), lambda i, j, kk: li + (kk, j))
        dot = _tn
    o_spec = pl.BlockSpec((tm, tn), lambda i, j, kk: (i, j))
    has_add = add is not None

    def body(*refs):
        if has_add:
            a_ref, b_ref, add_ref, o_ref, acc_ref = refs
        else:
            a_ref, b_ref, o_ref, acc_ref = refs
        kk = pl.program_id(2)
        part = dot(a_ref[...].astype(BF16), b_ref[...].astype(BF16))

        @pl.when(kk == 0)
        def _():
            acc_ref[...] = part

        @pl.when(kk > 0)
        def _():
            acc_ref[...] += part

        @pl.when(kk == nk - 1)
        def _():
            r = acc_ref[...]
            if has_add:
                r = r + add_ref[...]
            o_ref[...] = r.astype(o_ref.dtype)

    ins = [a, b] + ([add] if has_add else [])
    specs = [a_spec, b_spec] + ([o_spec] if has_add else [])
    return pl.pallas_call(
        body, name=name, grid=(m // tm, n // tn, nk), in_specs=specs, out_specs=o_spec,
        out_shape=jax.ShapeDtypeStruct((m, n), out_dtype), scratch_shapes=[pltpu.VMEM((tm, tn), F32)],
        compiler_params=_cparams("parallel", "parallel", "arbitrary"))(*ins)


def _rmsnorm_fwd(x, g, *, name):
    s, d = x.shape
    tm = _tile(s, 256)

    def body(x_ref, g_ref, h_ref, r_ref):
        xf = x_ref[...]
        r = lax.rsqrt(jnp.mean(xf * xf, axis=-1, keepdims=True) + RMS_EPS)
        h_ref[...] = ((xf * r) * g_ref[...]).astype(h_ref.dtype)
        r_ref[...] = r

    return pl.pallas_call(
        body, name=name, grid=(s // tm,),
        in_specs=[pl.BlockSpec((tm, d), lambda i: (i, 0)), pl.BlockSpec((1, d), lambda i: (0, 0))],
        out_specs=[pl.BlockSpec((tm, d), lambda i: (i, 0)), pl.BlockSpec((tm, 1), lambda i: (i, 0))],
        out_shape=[jax.ShapeDtypeStruct((s, d), BF16), jax.ShapeDtypeStruct((s, 1), F32)],
        compiler_params=_cparams("parallel"))(x, g.reshape(1, d))


def _rms_bwd_math(dh, x, r, g):
    xr = x * r
    dyg = dh * g
    m = jnp.mean(dyg * xr, axis=-1, keepdims=True)
    return r * (dyg - xr * m), dh * xr


def _rmsnorm_bwd(dh, x, r, g, dres, *, name):
    s, d = x.shape
    tm = _tile(s, 256)
    has_res = dres is not None

    def body(*refs):
        if has_res:
            dh_ref, x_ref, r_ref, g_ref, res_ref, dx_ref, dg_ref = refs
        else:
            dh_ref, x_ref, r_ref, g_ref, dx_ref, dg_ref = refs
        dx, dgp = _rms_bwd_math(dh_ref[...].astype(F32), x_ref[...], r_ref[...], g_ref[...])
        if has_res:
            dx = dx + res_ref[...]
        dx_ref[...] = dx

        @pl.when(pl.program_id(0) == 0)
        def _():
            dg_ref[...] = jnp.zeros_like(dg_ref)

        dg_ref[...] += jnp.sum(dgp, axis=0, keepdims=True)

    row = pl.BlockSpec((tm, d), lambda i: (i, 0))
    vec = pl.BlockSpec((1, d), lambda i: (0, 0))
    ins = [dh, x, r, g.reshape(1, d)] + ([dres] if has_res else [])
    specs = [row, row, pl.BlockSpec((tm, 1), lambda i: (i, 0)), vec] + ([row] if has_res else [])
    return pl.pallas_call(
        body, name=name, grid=(s // tm,), in_specs=specs, out_specs=[row, vec],
        out_shape=[jax.ShapeDtypeStruct((s, d), F32), jax.ShapeDtypeStruct((1, d), F32)],
        compiler_params=_cparams("arbitrary"))(*ins)


def _final_loss(x, g, target, *, name):
    s, d = x.shape
    tm = _tile(s, 256)

    def body(x_ref, g_ref, t_ref, loss_ref, dx_ref, dg_ref):
        xf = x_ref[...]
        gv = g_ref[...]
        r = lax.rsqrt(jnp.mean(xf * xf, axis=-1, keepdims=True) + RMS_EPS)
        diff = (xf * r) * gv - t_ref[...]
        part = 0.5 * jnp.sum(jnp.mean(diff * diff, axis=-1, keepdims=True))
        dx, dgp = _rms_bwd_math(diff * (1.0 / d), xf, r, gv)
        dx_ref[...] = dx

        @pl.when(pl.program_id(0) == 0)
        def _():
            dg_ref[...] = jnp.zeros_like(dg_ref)
            loss_ref[...] = jnp.zeros_like(loss_ref)

        dg_ref[...] += jnp.sum(dgp, axis=0, keepdims=True)
        loss_ref[...] += part

    row = pl.BlockSpec((tm, d), lambda i: (i, 0))
    vec = pl.BlockSpec((1, d), lambda i: (0, 0))
    return pl.pallas_call(
        body, name=name, grid=(s // tm,), in_specs=[row, vec, row],
        out_specs=[pl.BlockSpec((8, LANES), lambda i: (0, 0)), row, vec],
        out_shape=[jax.ShapeDtypeStruct((8, LANES), F32), jax.ShapeDtypeStruct((s, d), F32), jax.ShapeDtypeStruct((1, d), F32)],
        compiler_params=_cparams("arbitrary"))(x, g.reshape(1, d), target)


def _conv_taps(gp_ref, w, bias, taps, tm):
    halo = gp_ref.shape[0] - tm
    acc = jnp.broadcast_to(bias, (tm, w.shape[1]))
    for k in range(taps):
        acc = acc + w[k:k + 1, :] * gp_ref[pl.ds(halo - (taps - 1) + k, tm), :]
    return acc


def _conv_core(val, glu, valh, gluh, first, gp_ref, w, bias, lg, lb, taps, tm):
    sg = _sigmoid(glu)
    g = val * sg
    gh = jnp.where(first, 0.0, valh * _sigmoid(gluh))
    gp_ref[0:CONV_HALO, :] = gh
    gp_ref[CONV_HALO:, :] = g
    d = _conv_taps(gp_ref, w, bias, taps, tm)
    mu = jnp.mean(d, axis=-1, keepdims=True)
    dc = d - mu
    rstd = lax.rsqrt(jnp.mean(dc * dc, axis=-1, keepdims=True) + LN_EPS)
    xhat = dc * rstd
    ln = xhat * lg + lb
    return sg, xhat, rstd, ln


def _conv_fwd(ua, dw_w, dw_b, ln_g, ln_b, pw, *, name):
    s = ua.shape[0]
    taps, c = dw_w.shape
    tm = _tile(s, 512)
    hb = tm // CONV_HALO

    def body(val_ref, glu_ref, valh_ref, gluh_ref, w_ref, b_ref, lg_ref, lb_ref, pw_ref, y_ref, gp_ref):
        first = pl.program_id(0) == 0
        _, _, _, ln = _conv_core(val_ref[...], glu_ref[...], valh_ref[...], gluh_ref[...], first, gp_ref,
                                 w_ref[...], b_ref[...], lg_ref[...], lb_ref[...], taps, tm)
        sw = ln * _sigmoid(ln)
        y_ref[...] = _nn(sw.astype(BF16), pw_ref[...])

    cur = lambda col: pl.BlockSpec((tm, c), lambda i: (i, col))
    prev = lambda col: pl.BlockSpec((CONV_HALO, c), lambda i: (jnp.maximum(i * hb - 1, 0), col))
    full = lambda a: pl.BlockSpec(a.shape, lambda i: (0,) * a.ndim)
    vecs = [dw_w, dw_b.reshape(1, c), ln_g.reshape(1, c), ln_b.reshape(1, c), pw]
    return pl.pallas_call(
        body, name=name, grid=(s // tm,),
        in_specs=[cur(0), cur(1), prev(0), prev(1)] + [full(a) for a in vecs],
        out_specs=pl.BlockSpec((tm, c), lambda i: (i, 0)),
        out_shape=jax.ShapeDtypeStruct((s, c), F32),
        scratch_shapes=[pltpu.VMEM((tm + CONV_HALO, c), F32)],
        compiler_params=_cparams("parallel"))(ua, ua, ua, ua, *vecs)


def _conv_bwd_a(ua, dy, dw_w, dw_b, ln_g, ln_b, pw, *, name):
    s = ua.shape[0]
    taps, c = dw_w.shape
    tm = _tile(s, 512)
    hb = tm // CONV_HALO

    def body(val_ref, glu_ref, valh_ref, gluh_ref, dy_ref, w_ref, b_ref, lg_ref, lb_ref, pw_ref,
             dd_ref, dpw_ref, dlg_ref, dlb_ref, gp_ref):
        first = pl.program_id(0) == 0
        lg = lg_ref[...]
        _, xhat, rstd, ln = _conv_core(val_ref[...], glu_ref[...], valh_ref[...], gluh_ref[...], first, gp_ref,
                                       w_ref[...], b_ref[...], lg, lb_ref[...], taps, tm)
        sw, dsw = _silu_and_grad(ln)
        dyb = dy_ref[...].astype(BF16)
        ds = _nt(dyb, pw_ref[...])
        dln = ds * dsw
        dxhat = dln * lg
        m1 = jnp.mean(dxhat, axis=-1, keepdims=True)
        m2 = jnp.mean(dxhat * xhat, axis=-1, keepdims=True)
        dd_ref[...] = rstd * (dxhat - m1 - xhat * m2)

        @pl.when(first)
        def _():
            dpw_ref[...] = jnp.zeros_like(dpw_ref)
            dlg_ref[...] = jnp.zeros_like(dlg_ref)
            dlb_ref[...] = jnp.zeros_like(dlb_ref)

        dpw_ref[...] += _tn(sw.astype(BF16), dyb)
        dlg_ref[...] += jnp.sum(dln * xhat, axis=0, keepdims=True)
        dlb_ref[...] += jnp.sum(dln, axis=0, keepdims=True)

    cur = lambda col: pl.BlockSpec((tm, c), lambda i: (i, col))
    prev = lambda col: pl.BlockSpec((CONV_HALO, c), lambda i: (jnp.maximum(i * hb - 1, 0), col))
    full = lambda a: pl.BlockSpec(a.shape, lambda i: (0,) * a.ndim)
    vec = pl.BlockSpec((1, c), lambda i: (0, 0))
    vecs = [dw_w, dw_b.reshape(1, c), ln_g.reshape(1, c), ln_b.reshape(1, c), pw]
    return pl.pallas_call(
        body, name=name, grid=(s // tm,),
        in_specs=[cur(0), cur(1), prev(0), prev(1), pl.BlockSpec((tm, c), lambda i: (i, 0))] + [full(a) for a in vecs],
        out_specs=[pl.BlockSpec((tm, c), lambda i: (i, 0)), pl.BlockSpec((c, c), lambda i: (0, 0)), vec, vec],
        out_shape=[jax.ShapeDtypeStruct((s, c), F32), jax.ShapeDtypeStruct((c, c), F32),
                   jax.ShapeDtypeStruct((1, c), F32), jax.ShapeDtypeStruct((1, c), F32)],
        scratch_shapes=[pltpu.VMEM((tm + CONV_HALO, c), F32)],
        compiler_params=_cparams("arbitrary"))(ua, ua, ua, ua, dy, *vecs)


def _conv_bwd_b(ua, dd, dw_w, *, name):
    s = ua.shape[0]
    taps, c = dw_w.shape
    tm = _tile(s, 512)
    hb = tm // CONV_HALO
    nt = s // tm

    def body(val_ref, glu_ref, valh_ref, gluh_ref, dd_ref, ddn_ref, w_ref, dval_ref, dglu_ref, dw_ref, db_ref, gp_ref, ddp_ref):
        i = pl.program_id(0)
        val = val_ref[...]
        sg = _sigmoid(glu_ref[...])
        gp_ref[0:CONV_HALO, :] = jnp.where(i == 0, 0.0, valh_ref[...] * _sigmoid(gluh_ref[...]))
        gp_ref[CONV_HALO:, :] = val * sg
        dd = dd_ref[...]
        ddp_ref[0:tm, :] = dd
        ddp_ref[tm:, :] = jnp.where(i == nt - 1, 0.0, ddn_ref[...])
        w = w_ref[...]
        dg = jnp.zeros((tm, c), F32)
        dws = []
        for k in range(taps):
            dg = dg + w[k:k + 1, :] * ddp_ref[pl.ds(taps - 1 - k, tm), :]
            dws.append(jnp.sum(dd * gp_ref[pl.ds(CONV_HALO - (taps - 1) + k, tm), :], axis=0, keepdims=True))
        dval_ref[...] = (dg * sg).astype(dval_ref.dtype)
        dglu_ref[...] = (dg * val * sg * (1.0 - sg)).astype(dglu_ref.dtype)

        @pl.when(i == 0)
        def _():
            dw_ref[...] = jnp.zeros_like(dw_ref)
            db_ref[...] = jnp.zeros_like(db_ref)

        dw_ref[...] += jnp.concatenate(dws, axis=0)
        db_ref[...] += jnp.sum(dd, axis=0, keepdims=True)

    cur = lambda col: pl.BlockSpec((tm, c), lambda i: (i, col))
    prev = lambda col: pl.BlockSpec((CONV_HALO, c), lambda i: (jnp.maximum(i * hb - 1, 0), col))
    nxt = pl.BlockSpec((CONV_HALO, c), lambda i: (jnp.minimum((i + 1) * hb, s // CONV_HALO - 1), 0))
    return pl.pallas_call(
        body, name=name, grid=(nt,),
        in_specs=[cur(0), cur(1), prev(0), prev(1), pl.BlockSpec((tm, c), lambda i: (i, 0)), nxt,
                  pl.BlockSpec((taps, c), lambda i: (0, 0))],
        out_specs=[pl.BlockSpec((tm, c), lambda i: (i, 0)), pl.BlockSpec((tm, c), lambda i: (i, 0)),
                   pl.BlockSpec((taps, c), lambda i: (0, 0)), pl.BlockSpec((1, c), lambda i: (0, 0))],
        out_shape=[jax.ShapeDtypeStruct((s, c), BF16), jax.ShapeDtypeStruct((s, c), BF16),
                   jax.ShapeDtypeStruct((taps, c), F32), jax.ShapeDtypeStruct((1, c), F32)],
        scratch_shapes=[pltpu.VMEM((tm + CONV_HALO, c), F32), pltpu.VMEM((tm + CONV_HALO, c), F32)],
        compiler_params=_cparams("arbitrary"))(ua, ua, ua, ua, dd, dd, dw_w)


def _sb_scores(q, kb, t_pos, s0, bk):
    z = _nt(q, kb) * (HEAD_DIM ** -0.5)
    s_pos = s0 + lax.broadcasted_iota(jnp.int32, (1, bk), 1)
    mask = s_pos < t_pos
    ls = jnp.minimum(z, 0.0) - jnp.log(1.0 + jnp.exp(-jnp.abs(z)))
    lm = jnp.where(mask, ls - z, 0.0)
    return mask, ls, lm


def _split_dot(x, tri):
    hi = x.astype(BF16)
    lo = (x - hi.astype(F32)).astype(BF16)
    return _nn(hi, tri) + _nn(lo, tri)


def _tri(bk, cmp):
    r = lax.broadcasted_iota(jnp.int32, (bk, bk), 0)
    c = lax.broadcasted_iota(jnp.int32, (bk, bk), 1)
    return cmp(r, c).astype(BF16)


def _sb_fwd(qkv, heads, *, name, bq=256, bk=256):
    s = qkv.shape[0]
    bq, bk = _tile(s, bq), _tile(s, bk)
    nq = s // bq

    def body(q_ref, k_ref, v_ref, o_ref, tot_ref):
        i = pl.program_id(1)
        q = q_ref[...]
        t_pos = i * bq + lax.broadcasted_iota(jnp.int32, (bq, 1), 0)
        tri = _tri(bk, lambda r, c: r > c)
        nkb = ((i + 1) * bq + bk - 1) // bk

        def step(jj, carry):
            acc, c = carry
            j = nkb - 1 - jj
            s0 = pl.multiple_of(j * bk, bk)
            kb = k_ref[pl.ds(s0, bk), :]
            vb = v_ref[pl.ds(s0, bk), :]
            mask, ls, lm = _sb_scores(q, kb, t_pos, s0, bk)
            after = _split_dot(lm, tri) + c
            a = jnp.where(mask, jnp.exp(ls + after), 0.0)
            acc = acc + _nn(a.astype(BF16), vb)
            c = c + jnp.sum(lm, axis=1, keepdims=True)
            return acc, c

        acc, c = lax.fori_loop(0, nkb, step, (jnp.zeros((bq, HEAD_DIM), F32), jnp.zeros((bq, 1), F32)))
        o_ref[...] = acc
        tot_ref[...] = c

    return pl.pallas_call(
        body, name=name, grid=(heads, nq),
        in_specs=[pl.BlockSpec((bq, HEAD_DIM), lambda h, i: (i, h)),
                  pl.BlockSpec((s, HEAD_DIM), lambda h, i: (0, heads + h)),
                  pl.BlockSpec((s, HEAD_DIM), lambda h, i: (0, 2 * heads + h))],
        out_specs=[pl.BlockSpec((bq, HEAD_DIM), lambda h, i: (i, h)),
                   pl.BlockSpec((None, bq, 1), lambda h, i: (h, i, 0))],
        out_shape=[jax.ShapeDtypeStruct((s, heads * HEAD_DIM), F32), jax.ShapeDtypeStruct((heads, s, 1), F32)],
        compiler_params=_cparams("parallel", "arbitrary"))(qkv, qkv, qkv)


def _sb_bwd(qkv, do, tot, heads, *, name, bq=256, bk=256):
    s = qkv.shape[0]
    bq, bk = _tile(s, bq), _tile(s, bk)
    nq = s // bq
    scale = HEAD_DIM ** -0.5

    def body(q_ref, k_ref, v_ref, do_ref, tot_ref, dq_ref, dk_ref, dv_ref, dk_acc, dv_acc):
        i = pl.program_id(1)

        @pl.when(i == 0)
        def _():
            dk_acc[...] = jnp.zeros_like(dk_acc)
            dv_acc[...] = jnp.zeros_like(dv_acc)

        q = q_ref[...]
        dob = do_ref[...].astype(BF16)
        tot_q = tot_ref[...]
        t_pos = i * bq + lax.broadcasted_iota(jnp.int32, (bq, 1), 0)
        tri_incl = _tri(bk, lambda r, c: r <= c)
        tri_excl = _tri(bk, lambda r, c: r < c)
        nkb = ((i + 1) * bq + bk - 1) // bk

        def step(j, carry):
            dq, cl, cg = carry
            s0 = pl.multiple_of(j * bk, bk)
            kb = k_ref[pl.ds(s0, bk), :]
            vb = v_ref[pl.ds(s0, bk), :]
            mask, ls, lm = _sb_scores(q, kb, t_pos, s0, bk)
            cum = _split_dot(lm, tri_incl) + cl
            a = jnp.where(mask, jnp.exp(ls + (tot_q - cum)), 0.0)
            g = a * _nt(dob, vb)
            hsum = _nn(g.astype(BF16), tri_excl) + cg
            beta = jnp.exp(ls)
            dz = jnp.where(mask, g * (1.0 - beta) - hsum * beta, 0.0) * scale
            dzb = dz.astype(BF16)
            dq = dq + _nn(dzb, kb)
            dk_acc[pl.ds(s0, bk), :] += _tn(dzb, q)
            dv_acc[pl.ds(s0, bk), :] += _tn(a.astype(BF16), dob)
            return dq, cl + jnp.sum(lm, axis=1, keepdims=True), cg + jnp.sum(g, axis=1, keepdims=True)

        zero = jnp.zeros((bq, 1), F32)
        dq, _, _ = lax.fori_loop(0, nkb, step, (jnp.zeros((bq, HEAD_DIM), F32), zero, zero))
        dq_ref[...] = dq.astype(dq_ref.dtype)

        @pl.when(i == nq - 1)
        def _():
            dk_ref[...] = dk_acc[...].astype(dk_ref.dtype)
            dv_ref[...] = dv_acc[...].astype(dv_ref.dtype)

    blk = pl.BlockSpec((bq, HEAD_DIM), lambda h, i: (i, h))
    col = lambda off: pl.BlockSpec((s, HEAD_DIM), lambda h, i: (0, off + h))
    shp = jax.ShapeDtypeStruct((s, heads * HEAD_DIM), BF16)
    return pl.pallas_call(
        body, name=name, grid=(heads, nq),
        in_specs=[blk, col(heads), col(2 * heads), blk, pl.BlockSpec((None, bq, 1), lambda h, i: (h, i, 0))],
        out_specs=[blk, col(0), col(0)], out_shape=[shp, shp, shp],
        scratch_shapes=[pltpu.VMEM((s, HEAD_DIM), F32), pltpu.VMEM((s, HEAD_DIM), F32)],
        compiler_params=_cparams("parallel", "arbitrary"))(qkv, qkv, qkv, do, tot)


def _shift_rows(x, n, fill, *, down):
    rows = x.shape[0]
    if n % 8 == 0:
        pad = jnp.full((n, x.shape[1]), fill, x.dtype)
        return jnp.concatenate([pad, x[:rows - n]], axis=0) if down else jnp.concatenate([x[n:], pad], axis=0)
    t = lax.broadcasted_iota(jnp.int32, x.shape, 0)
    if down:
        return jnp.where(t >= n, pltpu.roll(x, n, 0), fill)
    return jnp.where(t < rows - n, pltpu.roll(x, rows - n, 0), fill)


def _scan_rows(a, b, *, reverse):
    n = 1
    while n < a.shape[0]:
        b = a * _shift_rows(b, n, 0.0, down=not reverse) + b
        a = a * _shift_rows(a, n, 1.0, down=not reverse)
        n *= 2
    return a, b


def _neg_expm1(x):
    p = 1.0 + x * (1.0 / 7.0)
    for k in (6.0, 5.0, 4.0, 3.0, 2.0):
        p = 1.0 + x * (1.0 / k) * p
    return jnp.where(x > -0.25, -(x * p), 1.0 - jnp.exp(x))


def _softplus_neg(lam):
    z = -lam
    e = jnp.exp(-jnp.abs(z))
    u = 1.0 + e
    d = u - 1.0
    log1p_e = jnp.where(d == 0.0, e, jnp.log(u) * (e / jnp.where(d == 0.0, 1.0, d)))
    return jnp.maximum(z, 0.0) + log1p_e


def _lru_gates(xp_ref, w, bias, wa_ref, ba, wx_ref, bx, sp, taps, tm, heads):
    halo = xp_ref.shape[0] - tm
    xc = jnp.broadcast_to(bias, (tm, w.shape[1]))
    for k in range(taps):
        xc = xc + w[k:k + 1, :] * xp_ref[pl.ds(halo - (taps - 1) + k, tm), :]
    xb = xc.astype(BF16)
    pr, pi = [], []
    for n in range(heads):
        xh = xb[:, n * HEAD_DIM:(n + 1) * HEAD_DIM]
        pr.append(_nn(xh, wa_ref[n]))
        pi.append(_nn(xh, wx_ref[n]))
    r = _sigmoid(jnp.concatenate(pr, axis=1) + ba)
    ig = _sigmoid(jnp.concatenate(pi, axis=1) + bx)
    log_a = (-LRU_C) * r * sp
    a = jnp.exp(log_a)
    mult = jnp.sqrt(_neg_expm1(2.0 * log_a))
    return xc, r, ig, a, mult


def _lru_fwd(ub, x_col, conv_w, conv_b, wa, ba, wx, bx, lam, *, name):
    s = ub.shape[0]
    taps, w = conv_w.shape
    heads = w // HEAD_DIM
    tm = _tile(s, 256)
    hb = tm // LRU_HALO

    def body(x_ref, xh_ref, cw_ref, cb_ref, wa_ref, ba_ref, wx_ref, bx_ref, lam_ref, h_ref, xp_ref, carry_ref):
        i = pl.program_id(0)

        @pl.when(i == 0)
        def _():
            carry_ref[...] = jnp.zeros_like(carry_ref)

        xp_ref[0:LRU_HALO, :] = jnp.where(i == 0, 0.0, xh_ref[...])
        xp_ref[LRU_HALO:, :] = x_ref[...]
        sp = _softplus_neg(lam_ref[...])
        xc, _, ig, a, mult = _lru_gates(xp_ref, cw_ref[...], cb_ref[...], wa_ref, ba_ref[...], wx_ref, bx_ref[...],
                                        sp, taps, tm, heads)
        ac, bc = _scan_rows(a, mult * (ig * xc), reverse=False)
        h = ac * carry_ref[0:1, :] + bc
        h_ref[...] = h
        carry_ref[...] = jnp.broadcast_to(h[tm - 1:tm, :], carry_ref.shape)

    full = lambda arr: pl.BlockSpec(arr.shape, lambda i: (0,) * arr.ndim)
    vecs = [conv_w, conv_b.reshape(1, w), wa.astype(BF16), ba.reshape(1, w), wx.astype(BF16), bx.reshape(1, w), lam.reshape(1, w)]
    return pl.pallas_call(
        body, name=name, grid=(s // tm,),
        in_specs=[pl.BlockSpec((tm, w), lambda i: (i, x_col)),
                  pl.BlockSpec((LRU_HALO, w), lambda i: (jnp.maximum(i * hb - 1, 0), x_col))] + [full(v) for v in vecs],
        out_specs=pl.BlockSpec((tm, w), lambda i: (i, 0)),
        out_shape=jax.ShapeDtypeStruct((s, w), F32),
        scratch_shapes=[pltpu.VMEM((tm + LRU_HALO, w), F32), pltpu.VMEM((8, w), F32)],
        compiler_params=_cparams("arbitrary"))(ub, ub, *vecs)


def _lru_bwd(ub, x_col, h, dh, conv_w, conv_b, wa, ba, wx, bx, lam, *, name):
    s = ub.shape[0]
    taps, w = conv_w.shape
    heads = w // HEAD_DIM
    tm = _tile(s, 256)
    hb = tm // LRU_HALO
    nt = s // tm

    def body(x_ref, xh_ref, h_ref, hh_ref, dh_ref, cw_ref, cb_ref, wa_ref, ba_ref, wx_ref, bx_ref, lam_ref,
             dx_ref, dcw_ref, dcb_ref, dwa_ref, dba_ref, dwx_ref, dbx_ref, dlam_ref,
             xp_ref, dxp_ref, dlt_ref, afirst_ref, dxc_next_ref, dsp_ref):
        step = pl.program_id(0)
        i = nt - 1 - step

        @pl.when(step == 0)
        def _():
            for ref in (dcw_ref, dcb_ref, dwa_ref, dba_ref, dwx_ref, dbx_ref, dlam_ref, dlt_ref, dxc_next_ref, dsp_ref):
                ref[...] = jnp.zeros_like(ref)
            afirst_ref[...] = jnp.ones_like(afirst_ref)

        xp_ref[0:LRU_HALO, :] = jnp.where(i == 0, 0.0, xh_ref[...])
        xp_ref[LRU_HALO:, :] = x_ref[...]
        cw = cw_ref[...]
        lam_v = lam_ref[...]
        sp = _softplus_neg(lam_v)
        xc, r, ig, a, mult = _lru_gates(xp_ref, cw, cb_ref[...], wa_ref, ba_ref[...], wx_ref, bx_ref[...], sp, taps, tm, heads)
        rows = lax.broadcasted_iota(jnp.int32, (tm, w), 0)
        a_next = jnp.where(rows == tm - 1, afirst_ref[0:1, :], _shift_rows(a, 1, 1.0, down=False))
        ac, bc = _scan_rows(a_next, dh_ref[...], reverse=True)
        delta = ac * dlt_ref[0:1, :] + bc
        hv = h_ref[...]
        h_last_prev = jnp.where(i == 0, 0.0, hh_ref[LRU_HALO - 1:LRU_HALO, :])
        h_prev = jnp.where(rows == 0, h_last_prev, _shift_rows(hv, 1, 0.0, down=True))
        gated = ig * xc
        da = delta * h_prev
        dmult = delta * gated
        dgated = delta * mult
        dlog_a = da * a - dmult * (a * a) / mult
        dpr = dlog_a * ((-LRU_C) * sp) * r * (1.0 - r)
        dpi = dgated * xc * ig * (1.0 - ig)
        dxc = dgated * ig
        dsp_ref[...] += jnp.sum(dlog_a * ((-LRU_C) * r), axis=0, keepdims=True)
        dba_ref[...] += jnp.sum(dpr, axis=0, keepdims=True)
        dbx_ref[...] += jnp.sum(dpi, axis=0, keepdims=True)
        xb = xc.astype(BF16)
        dprb = dpr.astype(BF16)
        dpib = dpi.astype(BF16)
        back = []
        for n in range(heads):
            sl = slice(n * HEAD_DIM, (n + 1) * HEAD_DIM)
            dwa_ref[n] += _tn(xb[:, sl], dprb[:, sl])
            dwx_ref[n] += _tn(xb[:, sl], dpib[:, sl])
            back.append(_nt(dprb[:, sl], wa_ref[n]) + _nt(dpib[:, sl], wx_ref[n]))
        dxc = dxc + jnp.concatenate(back, axis=1)
        dxp_ref[0:tm, :] = dxc
        dxp_ref[tm:, :] = dxc_next_ref[...]
        dx = jnp.zeros((tm, w), F32)
        dws = []
        for k in range(taps):
            dx = dx + cw[k:k + 1, :] * dxp_ref[pl.ds(taps - 1 - k, tm), :]
            dws.append(jnp.sum(dxc * xp_ref[pl.ds(LRU_HALO - (taps - 1) + k, tm), :], axis=0, keepdims=True))
        dx_ref[...] = dx.astype(dx_ref.dtype)
        dcw_ref[...] += jnp.concatenate(dws, axis=0)
        dcb_ref[...] += jnp.sum(dxc, axis=0, keepdims=True)
        dlt_ref[...] = jnp.broadcast_to(delta[0:1, :], dlt_ref.shape)
        afirst_ref[...] = jnp.broadcast_to(a[0:1, :], afirst_ref.shape)
        dxc_next_ref[...] = dxc[0:LRU_HALO, :]

        @pl.when(step == nt - 1)
        def _():
            dlam_ref[...] = dsp_ref[...] * (-_sigmoid(-lam_v))

    rev = lambda col: pl.BlockSpec((tm, w), lambda st: (nt - 1 - st, col))
    prev = lambda col: pl.BlockSpec((LRU_HALO, w), lambda st: (jnp.maximum((nt - 1 - st) * hb - 1, 0), col))
    full = lambda arr: pl.BlockSpec(arr.shape, lambda st: (0,) * arr.ndim)
    vec = pl.BlockSpec((1, w), lambda st: (0, 0))
    vecs = [conv_w, conv_b.reshape(1, w), wa.astype(BF16), ba.reshape(1, w), wx.astype(BF16), bx.reshape(1, w), lam.reshape(1, w)]
    vshape = jax.ShapeDtypeStruct((1, w), F32)
    return pl.pallas_call(
        body, name=name, grid=(nt,),
        in_specs=[rev(x_col), prev(x_col), rev(0), prev(0), rev(0)] + [full(v) for v in vecs],
        out_specs=[rev(0), full(conv_w), vec, full(wa), vec, full(wx), vec, vec],
        out_shape=[jax.ShapeDtypeStruct((s, w), BF16), jax.ShapeDtypeStruct(conv_w.shape, F32), vshape,
                   jax.ShapeDtypeStruct(wa.shape, F32), vshape, jax.ShapeDtypeStruct(wx.shape, F32), vshape, vshape],
        scratch_shapes=[pltpu.VMEM((tm + LRU_HALO, w), F32), pltpu.VMEM((tm + LRU_HALO, w), F32),
                        pltpu.VMEM((8, w), F32), pltpu.VMEM((8, w), F32), pltpu.VMEM((LRU_HALO, w), F32), pltpu.VMEM((1, w), F32)],
        compiler_params=_cparams("arbitrary"))(ub, ub, h, h, dh, *vecs)


def _group_fwd(y, w, gate):
    r = lax.rsqrt(jnp.mean(y * y, axis=-1, keepdims=True) + RMS_EPS)
    return ((y * r) * w) * (gate * _sigmoid(gate))


def _mix_out_fwd(y_conv, y_attn, y_lru, ua, ub, n_conv, n_attn, n_lru, *, name):
    s, c = y_conv.shape
    wa_ = y_attn.shape[1]
    d = 2 * c + wa_
    tm = _tile(s, 256)
    assert wa_ == 2 * c

    def body(yc_ref, ya_ref, yl_ref, gc_ref, ga_ref, gl_ref, nc_ref, na_ref, nl_ref, o_ref):
        o_ref[:, 0:c] = _group_fwd(yc_ref[...], nc_ref[...], gc_ref[...]).astype(o_ref.dtype)
        o_ref[:, c:c + wa_] = _group_fwd(ya_ref[...], na_ref[...], ga_ref[...]).astype(o_ref.dtype)
        o_ref[:, c + wa_:] = _group_fwd(yl_ref[...], nl_ref[...], gl_ref[...]).astype(o_ref.dtype)

    blk = lambda width, col: pl.BlockSpec((tm, width), lambda i: (i, col))
    vec = lambda width: pl.BlockSpec((1, width), lambda i: (0, 0))
    return pl.pallas_call(
        body, name=name, grid=(s // tm,),
        in_specs=[blk(c, 0), blk(wa_, 0), blk(c, 0), blk(c, 2), blk(wa_, 0), blk(c, 3), vec(c), vec(wa_), vec(c)],
        out_specs=blk(d, 0), out_shape=jax.ShapeDtypeStruct((s, d), BF16),
        compiler_params=_cparams("parallel"))(y_conv, y_attn, y_lru, ua, ub, ub, n_conv.reshape(1, c), n_attn.reshape(1, wa_), n_lru.reshape(1, c))


def _group_bwd(dout, y, w, gate):
    r = lax.rsqrt(jnp.mean(y * y, axis=-1, keepdims=True) + RMS_EPS)
    silu, dsilu = _silu_and_grad(gate)
    dy, dwp = _rms_bwd_math(dout * silu, y, r, w)
    return dy, dout * ((y * r) * w) * dsilu, dwp


def _mix_out_bwd(dy, y_conv, y_attn, y_lru, ua, ub, n_conv, n_attn, n_lru, *, name):
    s, c = y_conv.shape
    wa_ = y_attn.shape[1]
    tm = _tile(s, 256)

    def body(dy_ref, yc_ref, ya_ref, yl_ref, gc_ref, ga_ref, gl_ref, nc_ref, na_ref, nl_ref,
             dyc_ref, dya_ref, dyl_ref, dgc_ref, dga_ref, dgl_ref, dnc_ref, dna_ref, dnl_ref):
        @pl.when(pl.program_id(0) == 0)
        def _():
            for ref in (dnc_ref, dna_ref, dnl_ref):
                ref[...] = jnp.zeros_like(ref)

        groups = ((dy_ref[:, 0:c], yc_ref, nc_ref, gc_ref, dyc_ref, dgc_ref, dnc_ref),
                  (dy_ref[:, c:c + wa_], ya_ref, na_ref, ga_ref, dya_ref, dga_ref, dna_ref),
                  (dy_ref[:, c + wa_:], yl_ref, nl_ref, gl_ref, dyl_ref, dgl_ref, dnl_ref))
        for dout, y_ref, n_ref, g_ref, dyo_ref, dgo_ref, dn_ref in groups:
            dyv, dgv, dwp = _group_bwd(dout, y_ref[...], n_ref[...], g_ref[...])
            dyo_ref[...] = dyv
            dgo_ref[...] = dgv.astype(dgo_ref.dtype)
            dn_ref[...] += jnp.sum(dwp, axis=0, keepdims=True)

    blk = lambda width, col: pl.BlockSpec((tm, width), lambda i: (i, col))
    vec = lambda width: pl.BlockSpec((1, width), lambda i: (0, 0))
    sh = lambda width, dt: jax.ShapeDtypeStruct((s, width), dt)
    vs = lambda width: jax.ShapeDtypeStruct((1, width), F32)
    return pl.pallas_call(
        body, name=name, grid=(s // tm,),
        in_specs=[blk(2 * c + wa_, 0), blk(c, 0), blk(wa_, 0), blk(c, 0), blk(c, 2), blk(wa_, 0), blk(c, 3), vec(c), vec(wa_), vec(c)],
        out_specs=[blk(c, 0), blk(wa_, 0), blk(c, 0), blk(c, 0), blk(wa_, 0), blk(c, 0), vec(c), vec(wa_), vec(c)],
        out_shape=[sh(c, F32), sh(wa_, F32), sh(c, F32), sh(c, BF16), sh(wa_, BF16), sh(c, BF16), vs(c), vs(wa_), vs(c)],
        compiler_params=_cparams("arbitrary"))(dy, y_conv, y_attn, y_lru, ua, ub, ub, n_conv.reshape(1, c), n_attn.reshape(1, wa_), n_lru.reshape(1, c))


def _xattn_probs(qh, kh):
    sc = _nt(qh, kh) * (HEAD_DIM ** -0.5)
    e = jnp.exp(sc - jnp.max(sc, axis=-1, keepdims=True))
    return e / jnp.sum(e, axis=-1, keepdims=True)


def _xattn_fwd(q, kv, *, name):
    s, w = q.shape
    heads = w // HEAD_DIM
    tm = _tile(s, 512)

    def body(q_ref, kv_ref, o_ref):
        for n in range(heads):
            sl = slice(n * HEAD_DIM, (n + 1) * HEAD_DIM)
            p = _xattn_probs(q_ref[:, sl], kv_ref[:, sl])
            o_ref[:, sl] = _nn(p.astype(BF16), kv_ref[:, w + n * HEAD_DIM:w + (n + 1) * HEAD_DIM]).astype(o_ref.dtype)

    return pl.pallas_call(
        body, name=name, grid=(s // tm,),
        in_specs=[pl.BlockSpec((tm, w), lambda i: (i, 0)), pl.BlockSpec(kv.shape, lambda i: (0, 0))],
        out_specs=pl.BlockSpec((tm, w), lambda i: (i, 0)), out_shape=jax.ShapeDtypeStruct((s, w), BF16),
        compiler_params=_cparams("parallel"))(q, kv)


def _xattn_bwd(q, kv, do, *, name):
    s, w = q.shape
    heads = w // HEAD_DIM
    tm = _tile(s, 512)
    scale = HEAD_DIM ** -0.5

    def body(q_ref, kv_ref, do_ref, dq_ref, dkv_ref):
        @pl.when(pl.program_id(0) == 0)
        def _():
            dkv_ref[...] = jnp.zeros_like(dkv_ref)

        for n in range(heads):
            sl = slice(n * HEAD_DIM, (n + 1) * HEAD_DIM)
            vsl = slice(w + n * HEAD_DIM, w + (n + 1) * HEAD_DIM)
            qh, kh, vh, doh = q_ref[:, sl], kv_ref[:, sl], kv_ref[:, vsl], do_ref[:, sl]
            p = _xattn_probs(qh, kh)
            dp = _nt(doh, vh)
            ds = (p * (dp - jnp.sum(dp * p, axis=-1, keepdims=True)) * scale).astype(BF16)
            dq_ref[:, sl] = _nn(ds, kh).astype(dq_ref.dtype)
            dkv_ref[:, sl] += _tn(ds, qh)
            dkv_ref[:, vsl] += _tn(p.astype(BF16), doh)

    row = pl.BlockSpec((tm, w), lambda i: (i, 0))
    kvs = pl.BlockSpec(kv.shape, lambda i: (0, 0))
    return pl.pallas_call(
        body, name=name, grid=(s // tm,), in_specs=[row, kvs, row], out_specs=[row, kvs],
        out_shape=[jax.ShapeDtypeStruct((s, w), BF16), jax.ShapeDtypeStruct(kv.shape, F32)],
        compiler_params=_cparams("arbitrary"))(q, kv, do)


ROW_BLOCK_BYTES = 1 << 20


def _row_tile(rows, cols):
    limit = max(8, ROW_BLOCK_BYTES // (4 * cols))
    t = 8
    while t * 2 <= limit and rows % (t * 2) == 0:
        t *= 2
    assert rows % t == 0
    return t


def _cast_bf16(w, *, name):
    rows, cols = w.shape
    tr = _row_tile(rows, cols)

    def body(w_ref, o_ref):
        o_ref[...] = w_ref[...].astype(BF16)

    blk = pl.BlockSpec((tr, cols), lambda i: (i, 0))
    return pl.pallas_call(body, name=name, grid=(rows // tr,), in_specs=[blk], out_specs=blk,
                          out_shape=jax.ShapeDtypeStruct((rows, cols), BF16), compiler_params=_cparams("parallel"))(w)


def _sum_slots(land, *, name):
    slots, rows, cols = land.shape
    tr = _row_tile(rows, cols * slots)

    def body(l_ref, o_ref):
        acc = l_ref[0].astype(F32)
        for j in range(1, slots):
            acc = acc + l_ref[j].astype(F32)
        o_ref[...] = acc

    return pl.pallas_call(
        body, name=name, grid=(rows // tr,), in_specs=[pl.BlockSpec((slots, tr, cols), lambda i: (0, i, 0))],
        out_specs=pl.BlockSpec((tr, cols), lambda i: (i, 0)), out_shape=jax.ShapeDtypeStruct((rows, cols), F32),
        compiler_params=_cparams("parallel"))(land)


def _adamw(w, m, v, gs, *, name):
    rows, cols = w.shape
    tr = _row_tile(rows, cols * 4)
    ng = len(gs)

    def body(*refs):
        w_ref, m_ref, v_ref = refs[:3]
        g_refs = refs[3:3 + ng]
        g_out, d_out, m_out, v_out = refs[3 + ng:]
        g = g_refs[0][...]
        for r in g_refs[1:]:
            g = g + r[...]
        mn = ADAM_B1 * m_ref[...] + (1.0 - ADAM_B1) * g
        vn = ADAM_B2 * v_ref[...] + (1.0 - ADAM_B2) * (g * g)
        m_hat = mn / (1.0 - ADAM_B1 ** ADAM_STEP)
        v_hat = vn / (1.0 - ADAM_B2 ** ADAM_STEP)
        g_out[...] = g
        d_out[...] = -ADAM_LR * (m_hat / (jnp.sqrt(v_hat) + ADAM_EPS) + ADAM_WD * w_ref[...])
        m_out[...] = mn
        v_out[...] = vn

    blk = pl.BlockSpec((tr, cols), lambda i: (i, 0))
    shp = jax.ShapeDtypeStruct((rows, cols), F32)
    return pl.pallas_call(body, name=name, grid=(rows // tr,), in_specs=[blk] * (3 + ng), out_specs=[blk] * 4,
                          out_shape=[shp] * 4, compiler_params=_cparams("parallel"))(w, m, v, *gs)


OTHER_CHIPS = ((1, 0), (0, 1), (1, 1))
N_CHIPS = 4
ANY = pl.BlockSpec(memory_space=pl.ANY)


def _place():
    return lax.axis_index("x"), lax.axis_index("y"), lax.axis_index("c")


def _flip(v, f):
    return 1 - v if f else v


def _part(ref, lead, axis, chip, size):
    idx = list(lead) + [slice(None)] * (len(ref.shape) - len(lead))
    idx[len(lead) + axis] = pl.ds(pl.multiple_of(chip * size, size), size)
    return ref.at[tuple(idx)]


def _allgather_chips(shards, axes, *, name):
    n = len(shards)
    sizes = [sh.shape[ax] for sh, ax in zip(shards, axes)]

    def full_shape(sh, ax):
        return tuple(d * N_CHIPS if i == ax else d for i, d in enumerate(sh.shape))

    def body(*refs):
        ins, outs = refs[:n], refs[n:2 * n]
        send_sems, recv_sems, loc_sems = refs[2 * n:]
        x, y, c = _place()
        me = 2 * x + y
        local = []
        for a in range(n):
            cp = pltpu.make_async_copy(ins[a], _part(outs[a], (), axes[a], me, sizes[a]), loc_sems.at[a])
            cp.start()
            local.append(cp)

        def remote(a, j, chip):
            fx, fy = OTHER_CHIPS[j]
            return pltpu.make_async_remote_copy(
                src_ref=ins[a], dst_ref=_part(outs[a], (), axes[a], chip, sizes[a]),
                send_sem=send_sems.at[a, j], recv_sem=recv_sems.at[a, j],
                device_id=(_flip(x, fx), _flip(y, fy), c), device_id_type=MESH)

        for a in range(n):
            for j in range(len(OTHER_CHIPS)):
                remote(a, j, me).start()
        for a in range(n):
            for j, (fx, fy) in enumerate(OTHER_CHIPS):
                remote(a, j, 2 * _flip(x, fx) + _flip(y, fy)).wait()
        for cp in local:
            cp.wait()

    return pl.pallas_call(
        body, name=name, in_specs=[ANY] * n, out_specs=[ANY] * n,
        out_shape=[jax.ShapeDtypeStruct(full_shape(sh, ax), sh.dtype) for sh, ax in zip(shards, axes)],
        scratch_shapes=[pltpu.SemaphoreType.DMA((n, 3)), pltpu.SemaphoreType.DMA((n, 3)), pltpu.SemaphoreType.DMA((n,))],
    )(*shards)


def _scatter_chips(grads, axes, *, name):
    n = len(grads)
    layers = len(grads[0])
    sizes = [g[0].shape[ax] // N_CHIPS for g, ax in zip(grads, axes)]

    def land_shape(g, ax):
        return (N_CHIPS, layers) + tuple(d // N_CHIPS if i == ax else d for i, d in enumerate(g[0].shape))

    def body(*refs):
        ins = [refs[a * layers:(a + 1) * layers] for a in range(n)]
        outs = refs[n * layers:n * layers + n]
        send_sems, recv_sems, loc_sems = refs[n * layers + n:]
        x, y, c = _place()
        me = 2 * x + y
        local = []
        for a in range(n):
            for l in range(layers):
                cp = pltpu.make_async_copy(_part(ins[a][l], (), axes[a], me, sizes[a]), outs[a].at[3, l], loc_sems.at[a, l])
                cp.start()
                local.append(cp)

        def remote(a, l, j):
            fx, fy = OTHER_CHIPS[j]
            px, py = _flip(x, fx), _flip(y, fy)
            return pltpu.make_async_remote_copy(
                src_ref=_part(ins[a][l], (), axes[a], 2 * px + py, sizes[a]), dst_ref=outs[a].at[j, l],
                send_sem=send_sems.at[a, l, j], recv_sem=recv_sems.at[a, l, j],
                device_id=(px, py, c), device_id_type=MESH)

        todo = [(a, l, j) for a in range(n) for l in range(layers) for j in range(len(OTHER_CHIPS))]
        for t in todo:
            remote(*t).start()
        for t in todo:
            remote(*t).wait()
        for cp in local:
            cp.wait()

    flat = [g for gs in grads for g in gs]
    return pl.pallas_call(
        body, name=name, in_specs=[ANY] * len(flat), out_specs=[ANY] * n,
        out_shape=[jax.ShapeDtypeStruct(land_shape(g, ax), g[0].dtype) for g, ax in zip(grads, axes)],
        scratch_shapes=[pltpu.SemaphoreType.DMA((n, layers, 3)), pltpu.SemaphoreType.DMA((n, layers, 3)),
                        pltpu.SemaphoreType.DMA((n, layers))],
    )(*flat)


def _swap_sibling(arrs, *, name):
    n = len(arrs)

    def body(*refs):
        ins, outs = refs[:n], refs[n:2 * n]
        send_sems, recv_sems = refs[2 * n:]
        x, y, c = _place()
        copies = [pltpu.make_async_remote_copy(src_ref=ins[a], dst_ref=outs[a], send_sem=send_sems.at[a], recv_sem=recv_sems.at[a],
                                               device_id=(x, y, 1 - c), device_id_type=MESH) for a in range(n)]
        for cp in copies:
            cp.start()
        for cp in copies:
            cp.wait()

    return pl.pallas_call(
        body, name=name, in_specs=[ANY] * n, out_specs=[ANY] * n,
        out_shape=[jax.ShapeDtypeStruct(a.shape, a.dtype) for a in arrs],
        scratch_shapes=[pltpu.SemaphoreType.DMA((n,)), pltpu.SemaphoreType.DMA((n,))],
    )(*arrs)


def _allreduce_small(p, *, name):
    rows, cols = p.shape
    ndev = 8

    def body(p_ref, o_ref, land_ref, send_sems, recv_sems):
        x, y, c = _place()
        me = 4 * x + 2 * y + c
        land_ref[0] = p_ref[...]
        copies = []
        for r in range(1, ndev):
            peer = (_flip(x, r & 4), _flip(y, r & 2), _flip(c, r & 1))
            copies.append(pltpu.make_async_remote_copy(src_ref=p_ref, dst_ref=land_ref.at[r], send_sem=send_sems.at[r],
                                                       recv_sem=recv_sems.at[r], device_id=peer, device_id_type=MESH))
        for cp in copies:
            cp.start()
        for cp in copies:
            cp.wait()
        acc = land_ref[me]
        for d in range(1, ndev):
            acc = acc + land_ref[jnp.bitwise_xor(me, d)]
        o_ref[...] = acc

    vm = pl.BlockSpec(memory_space=pltpu.VMEM)
    return pl.pallas_call(
        body, name=name, in_specs=[vm], out_specs=vm, out_shape=jax.ShapeDtypeStruct((rows, cols), F32),
        scratch_shapes=[pltpu.VMEM((ndev, rows, cols), F32), pltpu.SemaphoreType.DMA((ndev,)), pltpu.SemaphoreType.DMA((ndev,))],
        compiler_params=pltpu.CompilerParams(vmem_limit_bytes=V7X_VMEM_LIMIT_BYTES))(p)


WEIGHTS = ("mix_norm_g", "w_in", "conv_dw_w", "conv_dw_b", "conv_ln_g", "conv_ln_b", "conv_pw_w", "lru_conv_w", "lru_conv_b",
           "lru_wa", "lru_ba", "lru_wx", "lru_bx", "lru_lambda", "out_norm_conv", "out_norm_attn", "out_norm_lru", "w_out",
           "xattn_norm_g", "mem_norm_g", "xattn_wq", "xattn_wkv", "xattn_wo", "final_norm_g")
BIG = {"w_in": 2, "conv_pw_w": 1, "w_out": 1, "xattn_wq": 1, "xattn_wkv": 1, "xattn_wo": 2}
SMALL_SHARDED = {"conv_dw_w": 2, "lru_conv_w": 2}


def _trunk(x, mem, target, p):
    depth = p["mix_norm_g"].shape[0]
    c = p["conv_dw_w"].shape[2]
    aw = p["out_norm_attn"].shape[1]
    heads = aw // HEAD_DIM
    saved = []
    for l in range(depth):
        t = f"l{l}_"
        h1, r1 = _rmsnorm_fwd(x, p["mix_norm_g"][l], name=t + "mix_norm")
        ua = _matmul(h1, p["w_in"], mode="nn", layer=l, n=3 * c, b_off=0, name=t + "in_conv")
        qkv = _matmul(h1, p["w_in"], mode="nn", layer=l, n=3 * aw, b_off=3 * c, out_dtype=BF16, name=t + "in_qkv")
        ub = _matmul(h1, p["w_in"], mode="nn", layer=l, n=aw + 2 * c, b_off=3 * c + 3 * aw, name=t + "in_gates")
        y_conv = _conv_fwd(ua, p["conv_dw_w"][l], p["conv_dw_b"][l], p["conv_ln_g"][l], p["conv_ln_b"][l], p["conv_pw_w"][l],
                           name=t + "conv_fwd")
        y_attn, tot = _sb_fwd(qkv, heads, name=t + "sb_fwd")
        y_lru = _lru_fwd(ub, aw // c, p["lru_conv_w"][l], p["lru_conv_b"][l], p["lru_wa"][l], p["lru_ba"][l], p["lru_wx"][l],
                         p["lru_bx"][l], p["lru_lambda"][l], name=t + "lru_fwd")
        y = _mix_out_fwd(y_conv, y_attn, y_lru, ua, ub, p["out_norm_conv"][l], p["out_norm_attn"][l], p["out_norm_lru"][l],
                         name=t + "mix_out_fwd")
        x2 = _matmul(y, p["w_out"], mode="nn", layer=l, add=x, name=t + "out_proj")
        h2, r2 = _rmsnorm_fwd(x2, p["xattn_norm_g"][l], name=t + "xattn_norm")
        qx = _matmul(h2, p["xattn_wq"], mode="nn", layer=l, out_dtype=BF16, name=t + "xattn_q")
        memn, rm = _rmsnorm_fwd(mem, p["mem_norm_g"][l], name=t + "mem_norm")
        kv = _matmul(memn, p["xattn_wkv"], mode="nn", layer=l, out_dtype=BF16, name=t + "xattn_kv")
        o = _xattn_fwd(qx, kv, name=t + "xattn_fwd")
        x3 = _matmul(o, p["xattn_wo"], mode="nn", layer=l, add=x2, name=t + "xattn_o")
        saved.append(dict(x=x, h1=h1, r1=r1, ua=ua, qkv=qkv, ub=ub, y_conv=y_conv, y_attn=y_attn, tot=tot, y_lru=y_lru, y=y,
                          x2=x2, h2=h2, r2=r2, qx=qx, memn=memn, rm=rm, kv=kv, o=o))
        x = x3

    loss, dx, dg_final = _final_loss(x, p["final_norm_g"], target, name="final_loss")
    big = {k: [None] * depth for k in BIG}
    small = {k: [None] * depth for k in WEIGHTS if k not in BIG and k != "final_norm_g"}
    for l in reversed(range(depth)):
        t = f"l{l}_"
        s = saved[l]
        do = _matmul(dx, p["xattn_wo"], mode="nt", layer=l, out_dtype=BF16, name=t + "d_xattn_o")
        big["xattn_wo"][l] = _matmul(s["o"], dx, mode="tn", out_dtype=BF16, name=t + "dw_xattn_o")
        dqx, dkv = _xattn_bwd(s["qx"], s["kv"], do, name=t + "xattn_bwd")
        big["xattn_wq"][l] = _matmul(s["h2"], dqx, mode="tn", out_dtype=BF16, name=t + "dw_xattn_q")
        dh2 = _matmul(dqx, p["xattn_wq"], mode="nt", layer=l, name=t + "d_xattn_q")
        dx2, dg = _rmsnorm_bwd(dh2, s["x2"], s["r2"], p["xattn_norm_g"][l], dx, name=t + "xattn_norm_bwd")
        small["xattn_norm_g"][l] = dg[0]
        dmemn = _matmul(dkv, p["xattn_wkv"], mode="nt", layer=l, name=t + "d_xattn_kv")
        big["xattn_wkv"][l] = _matmul(s["memn"], dkv, mode="tn", out_dtype=BF16, name=t + "dw_xattn_kv")
        _, dg = _rmsnorm_bwd(dmemn, mem, s["rm"], p["mem_norm_g"][l], None, name=t + "mem_norm_bwd")
        small["mem_norm_g"][l] = dg[0]
        dy = _matmul(dx2, p["w_out"], mode="nt", layer=l, name=t + "d_out_proj")
        big["w_out"][l] = _matmul(s["y"], dx2, mode="tn", out_dtype=BF16, name=t + "dw_out_proj")
        dyc, dya, dyl, dgc, dga, dgl, dnc, dna, dnl = _mix_out_bwd(
            dy, s["y_conv"], s["y_attn"], s["y_lru"], s["ua"], s["ub"], p["out_norm_conv"][l], p["out_norm_attn"][l],
            p["out_norm_lru"][l], name=t + "mix_out_bwd")
        small["out_norm_conv"][l], small["out_norm_attn"][l], small["out_norm_lru"][l] = dnc[0], dna[0], dnl[0]
        dd, dpw, dlg, dlb = _conv_bwd_a(s["ua"], dyc, p["conv_dw_w"][l], p["conv_dw_b"][l], p["conv_ln_g"][l], p["conv_ln_b"][l],
                                        p["conv_pw_w"][l], name=t + "conv_bwd_a")
        dval, dglu, ddw, ddb = _conv_bwd_b(s["ua"], dd, p["conv_dw_w"][l], name=t + "conv_bwd_b")
        big["conv_pw_w"][l] = dpw
        small["conv_ln_g"][l], small["conv_ln_b"][l], small["conv_dw_w"][l], small["conv_dw_b"][l] = dlg[0], dlb[0], ddw, ddb[0]
        dq, dk, dv = _sb_bwd(s["qkv"], dya, s["tot"], heads, name=t + "sb_bwd")
        drx, dcw, dcb, dwa, dba, dwx, dbx, dlam = _lru_bwd(
            s["ub"], aw // c, s["y_lru"], dyl, p["lru_conv_w"][l], p["lru_conv_b"][l], p["lru_wa"][l], p["lru_ba"][l],
            p["lru_wx"][l], p["lru_bx"][l], p["lru_lambda"][l], name=t + "lru_bwd")
        small["lru_conv_w"][l], small["lru_conv_b"][l], small["lru_wa"][l], small["lru_ba"][l] = dcw, dcb[0], dwa, dba[0]
        small["lru_wx"][l], small["lru_bx"][l], small["lru_lambda"][l] = dwx, dbx[0], dlam[0]
        du = jnp.concatenate([dval, dglu, dgc, dq, dk, dv, dga, drx, dgl], axis=1)
        big["w_in"][l] = _matmul(s["h1"], du, mode="tn", out_dtype=BF16, name=t + "dw_in")
        dh1 = _matmul(du, p["w_in"], mode="nt", layer=l, name=t + "d_in")
        dx, dg = _rmsnorm_bwd(dh1, s["x"], s["r1"], p["mix_norm_g"][l], dx2, name=t + "mix_norm_bwd")
        small["mix_norm_g"][l] = dg[0]
    small = {k: jnp.stack(v) for k, v in small.items()}
    small["final_norm_g"] = dg_final[0]
    return loss, dx, big, small


def _pack(arrs):
    flat = jnp.concatenate([a.reshape(-1) for a in arrs])
    pad = (-flat.shape[0]) % (8 * LANES)
    return jnp.pad(flat, (0, pad)).reshape(-1, LANES)


def _unpack(packed, like):
    flat = packed.reshape(-1)
    out, at = [], 0
    for a in like:
        out.append(flat[at:at + a.size].reshape(a.shape))
        at += a.size
    return out


def _as_rows(a):
    return a.reshape(-1, a.shape[-1])


def kernel(x, mem, mix_norm_g, w_in, conv_dw_w, conv_dw_b, conv_ln_g, conv_ln_b, conv_pw_w, lru_conv_w, lru_conv_b, lru_wa, lru_ba, lru_wx, lru_bx, lru_lambda, out_norm_conv, out_norm_attn, out_norm_lru, w_out, xattn_norm_g, mem_norm_g, xattn_wq, xattn_wkv, xattn_wo, final_norm_g, loss_target, m_mix_norm_g, m_w_in, m_conv_dw_w, m_conv_dw_b, m_conv_ln_g, m_conv_ln_b, m_conv_pw_w, m_lru_conv_w, m_lru_conv_b, m_lru_wa, m_lru_ba, m_lru_wx, m_lru_bx, m_lru_lambda, m_out_norm_conv, m_out_norm_attn, m_out_norm_lru, m_w_out, m_xattn_norm_g, m_mem_norm_g, m_xattn_wq, m_xattn_wkv, m_xattn_wo, m_final_norm_g, v_mix_norm_g, v_w_in, v_conv_dw_w, v_conv_dw_b, v_conv_ln_g, v_conv_ln_b, v_conv_pw_w, v_lru_conv_w, v_lru_conv_b, v_lru_wa, v_lru_ba, v_lru_wx, v_lru_bx, v_lru_lambda, v_out_norm_conv, v_out_norm_attn, v_out_norm_lru, v_w_out, v_xattn_norm_g, v_mem_norm_g, v_xattn_wq, v_xattn_wkv, v_xattn_wo, v_final_norm_g):
    given = dict(locals())
    w = {k: given[k] for k in WEIGHTS}
    m = {k: given["m_" + k] for k in WEIGHTS}
    v = {k: given["v_" + k] for k in WEIGHTS}
    depth = mix_norm_g.shape[0]
    chip = 2 * lax.axis_index("x") + lax.axis_index("y")

    gather = [_cast_bf16(_as_rows(w[k]), name="cast_" + k).reshape(w[k].shape) for k in BIG] + [w[k] for k in SMALL_SHARDED]
    whole = _allgather_chips(gather, list(BIG.values()) + list(SMALL_SHARDED.values()), name="gather_weights")
    p = dict(w)
    p.update(zip(list(BIG) + list(SMALL_SHARDED), whole))

    loss, grad_x, big, small = _trunk(x[0], mem[0], loss_target[0], p)
    loss = lax.psum(loss[0, 0], ("x", "y", "c"))

    names = list(BIG)
    big["conv_pw_w"] = [_cast_bf16(g, name=f"cast_dpw{l}") for l, g in enumerate(big["conv_pw_w"])]
    land = _scatter_chips([big[k] for k in names], [BIG[k] - 1 for k in names], name="scatter_grads")
    sums = [_sum_slots(ld.reshape(N_CHIPS, -1, ld.shape[-1]), name="sum_" + k) for k, ld in zip(names, land)]
    theirs = _swap_sibling(sums, name="swap_sums")
    out = {}
    for k, mine, other in zip(names, sums, theirs):
        res = _adamw(_as_rows(w[k]), _as_rows(m[k]), _as_rows(v[k]), [mine, other], name="adamw_" + k)
        out[k] = [r.reshape(w[k].shape) for r in res]

    small_names = [k for k in WEIGHTS if k not in BIG]
    total = _unpack(_allreduce_small(_pack([small[k] for k in small_names]), name="allreduce_small"), [small[k] for k in small_names])
    g_small = dict(zip(small_names, total))
    for k, ax in SMALL_SHARDED.items():
        size = w[k].shape[ax]
        g_small[k] = lax.dynamic_slice_in_dim(g_small[k], chip * size, size, axis=ax)
    res = _adamw(_pack([w[k] for k in small_names]), _pack([m[k] for k in small_names]), _pack([v[k] for k in small_names]),
                 [_pack([g_small[k] for k in small_names])], name="adamw_small")
    res = [_unpack(r, [w[k] for k in small_names]) for r in res]
    for i, k in enumerate(small_names):
        out[k] = [r[i] for r in res]

    outs = [loss, grad_x[None]]
    for part in range(4):
        outs += [out[k][part] for k in WEIGHTS]
    return tuple(outs)
```

```python
import functools

import jax
import jax.numpy as jnp
from jax import lax
from jax.experimental import pallas as pl
from jax.experimental.pallas import tpu as pltpu

F32 = jnp.float32
BF16 = jnp.bfloat16
MESH = pl.DeviceIdType.MESH

V7X_VMEM_LIMIT_BYTES = 56 * 1024 * 1024
LANES = 128
HEAD_DIM = 128
LRU_C = 8.0
RMS_EPS = 1e-6
LN_EPS = 1e-5
CONV_HALO = 32
LRU_HALO = 8
ADAM_LR = 0.001
ADAM_B1 = 0.9
ADAM_B2 = 0.999
ADAM_EPS = 1e-08
ADAM_WD = 0.01
ADAM_STEP = 10


def _cparams(*sem):
    return pltpu.CompilerParams(dimension_semantics=sem, vmem_limit_bytes=V7X_VMEM_LIMIT_BYTES)


def _tile(n, pref):
    if n <= pref:
        return n
    for t in range(pref - pref % LANES, 0, -LANES):
        if n % t == 0:
            return t
    t = pref
    while n % t:
        t //= 2
    return t


def _dot(a, b, dims):
    return lax.dot_general(a, b, (dims, ((), ())), preferred_element_type=F32)


def _nn(a, b):
    return _dot(a, b, ((1,), (0,)))


def _nt(a, b):
    return _dot(a, b, ((1,), (1,)))


def _tn(a, b):
    return _dot(a, b, ((0,), (0,)))


def _sigmoid(x):
    return jax.nn.sigmoid(x)


def _silu_and_grad(x):
    s = _sigmoid(x)
    return x * s, s * (1.0 + x * (1.0 - s))


def _matmul(a, b, *, mode, name, layer=None, n=None, b_off=0, add=None, dep=None, out_dtype=F32, tm=512, tn=512, tk=2048):
    bshape = b.shape if layer is None else b.shape[1:]
    if mode == "nn":
        m, k = a.shape
        n = bshape[1] if n is None else n
    elif mode == "nt":
        m, k = a.shape
        n = bshape[0]
    else:
        k, m = a.shape
        n = bshape[1]
    tm, tk = _tile(m, tm), _tile(k, tk)
    tn = _tile(n, tn)
    while b_off % tn or n % tn:
        tn -= LANES
    nk = k // tk
    off = b_off // tn
    lead = () if layer is None else (None,)
    li = () if layer is None else (layer,)
    if mode == "nn":
        a_spec = pl.BlockSpec((tm, tk), lambda i, j, kk: (i, kk))
        b_spec = pl.BlockSpec(lead + (tk, tn), lambda i, j, kk: li + (kk, j + off))
        dot = _nn
    elif mode == "nt":
        a_spec = pl.BlockSpec((tm, tk), lambda i, j, kk: (i, kk))
        b_spec = pl.BlockSpec(lead + (tn, tk), lambda i, j, kk: li + (j, kk))
        dot = _nt
    else:
        a_spec = pl.BlockSpec((tk, tm), lambda i, j, kk: (kk, i))
        b_spec = pl.BlockSpec(lead + (tk, tn), lambda i, j, kk: li + (kk, j))
        dot = _tn
    o_spec = pl.BlockSpec((tm, tn), lambda i, j, kk: (i, j))
    has_add = add is not None

    def body(*refs):
        refs = refs[:-3] + refs[-2:] if dep is not None else refs
        if has_add:
            a_ref, b_ref, add_ref, o_ref, acc_ref = refs
        else:
            a_ref, b_ref, o_ref, acc_ref = refs
        kk = pl.program_id(2)
        part = dot(a_ref[...].astype(BF16), b_ref[...].astype(BF16))

        @pl.when(kk == 0)
        def _():
            acc_ref[...] = part

        @pl.when(kk > 0)
        def _():
            acc_ref[...] += part

        @pl.when(kk == nk - 1)
        def _():
            r = acc_ref[...]
            if has_add:
                r = r + add_ref[...]
            o_ref[...] = r.astype(o_ref.dtype)

    ins = [a, b] + ([add] if has_add else [])
    specs = [a_spec, b_spec] + ([o_spec] if has_add else [])
    if dep is not None:
        ins.append(dep)
        specs.append(pl.BlockSpec((8, LANES), lambda i, j, kk: (0, 0)))
    return pl.pallas_call(
        body, name=name, grid=(m // tm, n // tn, nk), in_specs=specs, out_specs=o_spec,
        out_shape=jax.ShapeDtypeStruct((m, n), out_dtype), scratch_shapes=[pltpu.VMEM((tm, tn), F32)],
        compiler_params=_cparams("parallel", "parallel", "arbitrary"))(*ins)


def _rmsnorm_fwd(x, g, *, name):
    s, d = x.shape
    tm = _tile(s, 256)

    def body(x_ref, g_ref, h_ref, r_ref):
        xf = x_ref[...]
        r = lax.rsqrt(jnp.mean(xf * xf, axis=-1, keepdims=True) + RMS_EPS)
        h_ref[...] = ((xf * r) * g_ref[...]).astype(h_ref.dtype)
        r_ref[...] = r

    return pl.pallas_call(
        body, name=name, grid=(s // tm,),
        in_specs=[pl.BlockSpec((tm, d), lambda i: (i, 0)), pl.BlockSpec((1, d), lambda i: (0, 0))],
        out_specs=[pl.BlockSpec((tm, d), lambda i: (i, 0)), pl.BlockSpec((tm, 1), lambda i: (i, 0))],
        out_shape=[jax.ShapeDtypeStruct((s, d), BF16), jax.ShapeDtypeStruct((s, 1), F32)],
        compiler_params=_cparams("parallel"))(x, g.reshape(1, d))


def _rms_bwd_math(dh, x, r, g):
    xr = x * r
    dyg = dh * g
    m = jnp.mean(dyg * xr, axis=-1, keepdims=True)
    return r * (dyg - xr * m), dh * xr


def _rmsnorm_bwd(dh, x, r, g, dres, *, name):
    s, d = x.shape
    tm = _tile(s, 256)
    has_res = dres is not None

    def body(*refs):
        if has_res:
            dh_ref, x_ref, r_ref, g_ref, res_ref, dx_ref, dg_ref = refs
        else:
            dh_ref, x_ref, r_ref, g_ref, dx_ref, dg_ref = refs
        dx, dgp = _rms_bwd_math(dh_ref[...].astype(F32), x_ref[...], r_ref[...], g_ref[...])
        if has_res:
            dx = dx + res_ref[...]
        dx_ref[...] = dx

        @pl.when(pl.program_id(0) == 0)
        def _():
            dg_ref[...] = jnp.zeros_like(dg_ref)

        dg_ref[...] += jnp.sum(dgp, axis=0, keepdims=True)

    row = pl.BlockSpec((tm, d), lambda i: (i, 0))
    vec = pl.BlockSpec((1, d), lambda i: (0, 0))
    ins = [dh, x, r, g.reshape(1, d)] + ([dres] if has_res else [])
    specs = [row, row, pl.BlockSpec((tm, 1), lambda i: (i, 0)), vec] + ([row] if has_res else [])
    return pl.pallas_call(
        body, name=name, grid=(s // tm,), in_specs=specs, out_specs=[row, vec],
        out_shape=[jax.ShapeDtypeStruct((s, d), F32), jax.ShapeDtypeStruct((1, d), F32)],
        compiler_params=_cparams("arbitrary"))(*ins)


def _final_loss(x, g, target, *, name):
    s, d = x.shape
    tm = _tile(s, 256)

    def body(x_ref, g_ref, t_ref, loss_ref, dx_ref, dg_ref):
        xf = x_ref[...]
        gv = g_ref[...]
        r = lax.rsqrt(jnp.mean(xf * xf, axis=-1, keepdims=True) + RMS_EPS)
        diff = (xf * r) * gv - t_ref[...]
        part = 0.5 * jnp.sum(jnp.mean(diff * diff, axis=-1, keepdims=True))
        dx, dgp = _rms_bwd_math(diff * (1.0 / d), xf, r, gv)
        dx_ref[...] = dx

        @pl.when(pl.program_id(0) == 0)
        def _():
            dg_ref[...] = jnp.zeros_like(dg_ref)
            loss_ref[...] = jnp.zeros_like(loss_ref)

        dg_ref[...] += jnp.sum(dgp, axis=0, keepdims=True)
        loss_ref[...] += part

    row = pl.BlockSpec((tm, d), lambda i: (i, 0))
    vec = pl.BlockSpec((1, d), lambda i: (0, 0))
    return pl.pallas_call(
        body, name=name, grid=(s // tm,), in_specs=[row, vec, row],
        out_specs=[pl.BlockSpec((8, LANES), lambda i: (0, 0)), row, vec],
        out_shape=[jax.ShapeDtypeStruct((8, LANES), F32), jax.ShapeDtypeStruct((s, d), F32), jax.ShapeDtypeStruct((1, d), F32)],
        compiler_params=_cparams("arbitrary"))(x, g.reshape(1, d), target)


def _conv_taps(gp_ref, w, bias, taps, tm):
    halo = gp_ref.shape[0] - tm
    acc = jnp.broadcast_to(bias, (tm, w.shape[1]))
    for k in range(taps):
        acc = acc + w[k:k + 1, :] * gp_ref[pl.ds(halo - (taps - 1) + k, tm), :]
    return acc


def _conv_core(val, glu, valh, gluh, first, gp_ref, w, bias, lg, lb, taps, tm):
    sg = _sigmoid(glu)
    g = val * sg
    gh = jnp.where(first, 0.0, valh * _sigmoid(gluh))
    gp_ref[0:CONV_HALO, :] = gh
    gp_ref[CONV_HALO:, :] = g
    d = _conv_taps(gp_ref, w, bias, taps, tm)
    mu = jnp.mean(d, axis=-1, keepdims=True)
    dc = d - mu
    rstd = lax.rsqrt(jnp.mean(dc * dc, axis=-1, keepdims=True) + LN_EPS)
    xhat = dc * rstd
    ln = xhat * lg + lb
    return sg, xhat, rstd, ln


def _conv_fwd(ua, dw_w, dw_b, ln_g, ln_b, pw, *, name):
    s = ua.shape[0]
    taps, c = dw_w.shape
    tm = _tile(s, 512)
    hb = tm // CONV_HALO

    def body(val_ref, glu_ref, valh_ref, gluh_ref, w_ref, b_ref, lg_ref, lb_ref, pw_ref, y_ref, gp_ref):
        first = pl.program_id(0) == 0
        _, _, _, ln = _conv_core(val_ref[...], glu_ref[...], valh_ref[...], gluh_ref[...], first, gp_ref,
                                 w_ref[...], b_ref[...], lg_ref[...], lb_ref[...], taps, tm)
        sw = ln * _sigmoid(ln)
        y_ref[...] = _nn(sw.astype(BF16), pw_ref[...])

    cur = lambda col: pl.BlockSpec((tm, c), lambda i: (i, col))
    prev = lambda col: pl.BlockSpec((CONV_HALO, c), lambda i: (jnp.maximum(i * hb - 1, 0), col))
    full = lambda a: pl.BlockSpec(a.shape, lambda i: (0,) * a.ndim)
    vecs = [dw_w, dw_b.reshape(1, c), ln_g.reshape(1, c), ln_b.reshape(1, c), pw]
    return pl.pallas_call(
        body, name=name, grid=(s // tm,),
        in_specs=[cur(0), cur(1), prev(0), prev(1)] + [full(a) for a in vecs],
        out_specs=pl.BlockSpec((tm, c), lambda i: (i, 0)),
        out_shape=jax.ShapeDtypeStruct((s, c), F32),
        scratch_shapes=[pltpu.VMEM((tm + CONV_HALO, c), F32)],
        compiler_params=_cparams("parallel"))(ua, ua, ua, ua, *vecs)


def _conv_bwd_a(ua, dy, dw_w, dw_b, ln_g, ln_b, pw, *, name):
    s = ua.shape[0]
    taps, c = dw_w.shape
    tm = _tile(s, 512)
    hb = tm // CONV_HALO

    def body(val_ref, glu_ref, valh_ref, gluh_ref, dy_ref, w_ref, b_ref, lg_ref, lb_ref, pw_ref,
             dd_ref, dpw_ref, dlg_ref, dlb_ref, gp_ref):
        first = pl.program_id(0) == 0
        lg = lg_ref[...]
        _, xhat, rstd, ln = _conv_core(val_ref[...], glu_ref[...], valh_ref[...], gluh_ref[...], first, gp_ref,
                                       w_ref[...], b_ref[...], lg, lb_ref[...], taps, tm)
        sw, dsw = _silu_and_grad(ln)
        dyb = dy_ref[...].astype(BF16)
        ds = _nt(dyb, pw_ref[...])
        dln = ds * dsw
        dxhat = dln * lg
        m1 = jnp.mean(dxhat, axis=-1, keepdims=True)
        m2 = jnp.mean(dxhat * xhat, axis=-1, keepdims=True)
        dd_ref[...] = rstd * (dxhat - m1 - xhat * m2)

        @pl.when(first)
        def _():
            dpw_ref[...] = jnp.zeros_like(dpw_ref)
            dlg_ref[...] = jnp.zeros_like(dlg_ref)
            dlb_ref[...] = jnp.zeros_like(dlb_ref)

        dpw_ref[...] += _tn(sw.astype(BF16), dyb)
        dlg_ref[...] += jnp.sum(dln * xhat, axis=0, keepdims=True)
        dlb_ref[...] += jnp.sum(dln, axis=0, keepdims=True)

    cur = lambda col: pl.BlockSpec((tm, c), lambda i: (i, col))
    prev = lambda col: pl.BlockSpec((CONV_HALO, c), lambda i: (jnp.maximum(i * hb - 1, 0), col))
    full = lambda a: pl.BlockSpec(a.shape, lambda i: (0,) * a.ndim)
    vec = pl.BlockSpec((1, c), lambda i: (0, 0))
    vecs = [dw_w, dw_b.reshape(1, c), ln_g.reshape(1, c), ln_b.reshape(1, c), pw]
    return pl.pallas_call(
        body, name=name, grid=(s // tm,),
        in_specs=[cur(0), cur(1), prev(0), prev(1), pl.BlockSpec((tm, c), lambda i: (i, 0))] + [full(a) for a in vecs],
        out_specs=[pl.BlockSpec((tm, c), lambda i: (i, 0)), pl.BlockSpec((c, c), lambda i: (0, 0)), vec, vec],
        out_shape=[jax.ShapeDtypeStruct((s, c), F32), jax.ShapeDtypeStruct((c, c), F32),
                   jax.ShapeDtypeStruct((1, c), F32), jax.ShapeDtypeStruct((1, c), F32)],
        scratch_shapes=[pltpu.VMEM((tm + CONV_HALO, c), F32)],
        compiler_params=_cparams("arbitrary"))(ua, ua, ua, ua, dy, *vecs)


def _conv_bwd_b(ua, dd, dw_w, *, name):
    s = ua.shape[0]
    taps, c = dw_w.shape
    tm = _tile(s, 512)
    hb = tm // CONV_HALO
    nt = s // tm

    def body(val_ref, glu_ref, valh_ref, gluh_ref, dd_ref, ddn_ref, w_ref, dval_ref, dglu_ref, dw_ref, db_ref, gp_ref, ddp_ref):
        i = pl.program_id(0)
        val = val_ref[...]
        sg = _sigmoid(glu_ref[...])
        gp_ref[0:CONV_HALO, :] = jnp.where(i == 0, 0.0, valh_ref[...] * _sigmoid(gluh_ref[...]))
        gp_ref[CONV_HALO:, :] = val * sg
        dd = dd_ref[...]
        ddp_ref[0:tm, :] = dd
        ddp_ref[tm:, :] = jnp.where(i == nt - 1, 0.0, ddn_ref[...])
        w = w_ref[...]
        dg = jnp.zeros((tm, c), F32)
        dws = []
        for k in range(taps):
            dg = dg + w[k:k + 1, :] * ddp_ref[pl.ds(taps - 1 - k, tm), :]
            dws.append(jnp.sum(dd * gp_ref[pl.ds(CONV_HALO - (taps - 1) + k, tm), :], axis=0, keepdims=True))
        dval_ref[...] = (dg * sg).astype(dval_ref.dtype)
        dglu_ref[...] = (dg * val * sg * (1.0 - sg)).astype(dglu_ref.dtype)

        @pl.when(i == 0)
        def _():
            dw_ref[...] = jnp.zeros_like(dw_ref)
            db_ref[...] = jnp.zeros_like(db_ref)

        dw_ref[...] += jnp.concatenate(dws, axis=0)
        db_ref[...] += jnp.sum(dd, axis=0, keepdims=True)

    cur = lambda col: pl.BlockSpec((tm, c), lambda i: (i, col))
    prev = lambda col: pl.BlockSpec((CONV_HALO, c), lambda i: (jnp.maximum(i * hb - 1, 0), col))
    nxt = pl.BlockSpec((CONV_HALO, c), lambda i: (jnp.minimum((i + 1) * hb, s // CONV_HALO - 1), 0))
    return pl.pallas_call(
        body, name=name, grid=(nt,),
        in_specs=[cur(0), cur(1), prev(0), prev(1), pl.BlockSpec((tm, c), lambda i: (i, 0)), nxt,
                  pl.BlockSpec((taps, c), lambda i: (0, 0))],
        out_specs=[pl.BlockSpec((tm, c), lambda i: (i, 0)), pl.BlockSpec((tm, c), lambda i: (i, 0)),
                   pl.BlockSpec((taps, c), lambda i: (0, 0)), pl.BlockSpec((1, c), lambda i: (0, 0))],
        out_shape=[jax.ShapeDtypeStruct((s, c), BF16), jax.ShapeDtypeStruct((s, c), BF16),
                   jax.ShapeDtypeStruct((taps, c), F32), jax.ShapeDtypeStruct((1, c), F32)],
        scratch_shapes=[pltpu.VMEM((tm + CONV_HALO, c), F32), pltpu.VMEM((tm + CONV_HALO, c), F32)],
        compiler_params=_cparams("arbitrary"))(ua, ua, ua, ua, dd, dd, dw_w)


LOG2_E = 1.4426950408889634
SB_HEADS_PER_STEP = 4
SB_ROWS = 256


def _sb_scores(q, kb, mask):
    return _sb_logs(_nt(q, kb), mask)


def _sb_logs(qk, mask):
    z = qk * (HEAD_DIM ** -0.5 * LOG2_E)
    ls = jnp.minimum(z, 0.0) - jnp.log2(1.0 + jnp.exp2(-jnp.abs(z)))
    lm = ls - z
    if mask is not None:
        lm = jnp.where(mask, lm, 0.0)
    return ls, lm


def _diag_mask(b):
    return lax.broadcasted_iota(jnp.int32, (b, b), 1) < lax.broadcasted_iota(jnp.int32, (b, b), 0)


def _split_dot(x, tri):
    hi = x.astype(BF16)
    lo = (x - hi.astype(F32)).astype(BF16)
    return _nn(hi, tri) + _nn(lo, tri)


def _tri(bk, cmp):
    r = lax.broadcasted_iota(jnp.int32, (bk, bk), 0)
    c = lax.broadcasted_iota(jnp.int32, (bk, bk), 1)
    return cmp(r, c).astype(BF16)


def _sb_fwd(qkv, heads, *, name, blk=256):
    s = qkv.shape[0]
    b = _tile(s, blk)
    nq = s // b
    hp = min(SB_HEADS_PER_STEP, heads)
    assert heads % hp == 0
    groups = heads // hp
    wide = hp * HEAD_DIM

    rs = min(b, SB_ROWS)
    chains = [(n, r) for n in range(hp) for r in range(b // rs)]

    def body(q_ref, k_ref, v_ref, o_ref, tot_ref):
        i = pl.program_id(1)
        qs = [q_ref[r * rs:(r + 1) * rs, n * HEAD_DIM:(n + 1) * HEAD_DIM] for n, r in chains]
        tri = _tri(b, lambda r, c: r > c)
        diag = _diag_mask(b)

        def tile(j, carry, masked):
            s0 = pl.multiple_of(j * b, b)
            kbs = [k_ref[pl.ds(s0, b), n * HEAD_DIM:(n + 1) * HEAD_DIM] for n in range(hp)]
            vbs = [v_ref[pl.ds(s0, b), n * HEAD_DIM:(n + 1) * HEAD_DIM] for n in range(hp)]
            masks = [diag[r * rs:(r + 1) * rs] if masked else None for _, r in chains]
            zs = [_nt(q, kbs[n]) for (n, _), q in zip(chains, qs)]
            sc = [_sb_logs(z, mask) for z, mask in zip(zs, masks)]
            after = [_split_dot(lm, tri) for _, lm in sc]
            out = []
            for (n, _), mask, (ls, lm), af, (acc, c) in zip(chains, masks, sc, after, carry):
                a = jnp.exp2(ls + (af + c))
                if masked:
                    a = jnp.where(mask, a, 0.0)
                out.append((a.astype(BF16), acc, c + jnp.sum(lm, axis=1, keepdims=True)))
            return tuple((acc + _nn(a, vbs[n]), c) for (n, _), (a, acc, c) in zip(chains, out))

        zero = tuple((jnp.zeros((rs, HEAD_DIM), F32), jnp.zeros((rs, 1), F32)) for _ in chains)
        carry = tile(i, zero, True)
        carry = lax.fori_loop(0, i, lambda jj, cr: tile(i - 1 - jj, cr, False), carry)
        for (n, r), (acc, c) in zip(chains, carry):
            o_ref[r * rs:(r + 1) * rs, n * HEAD_DIM:(n + 1) * HEAD_DIM] = acc
            tot_ref[n, r * rs:(r + 1) * rs, :] = c

    return pl.pallas_call(
        body, name=name, grid=(groups, nq),
        in_specs=[pl.BlockSpec((b, wide), lambda g, i: (i, g)),
                  pl.BlockSpec((s, wide), lambda g, i: (0, groups + g)),
                  pl.BlockSpec((s, wide), lambda g, i: (0, 2 * groups + g))],
        out_specs=[pl.BlockSpec((b, wide), lambda g, i: (i, g)),
                   pl.BlockSpec((hp, b, 1), lambda g, i: (g, i, 0))],
        out_shape=[jax.ShapeDtypeStruct((s, heads * HEAD_DIM), F32), jax.ShapeDtypeStruct((heads, s, 1), F32)],
        compiler_params=_cparams("parallel", "arbitrary"))(qkv, qkv, qkv)


def _sb_bwd(qkv, do, tot, heads, *, name, blk=256):
    s = qkv.shape[0]
    b = _tile(s, blk)
    nq = s // b
    hp = min(SB_HEADS_PER_STEP, heads)
    assert heads % hp == 0
    groups = heads // hp
    wide = hp * HEAD_DIM
    scale = HEAD_DIM ** -0.5
    rs = min(b, SB_ROWS)
    chains = [(n, r) for n in range(hp) for r in range(b // rs)]

    def body(q_ref, k_ref, v_ref, do_ref, tot_ref, dq_ref, dk_ref, dv_ref, dk_acc, dv_acc):
        i = pl.program_id(1)

        @pl.when(i == 0)
        def _():
            dk_acc[...] = jnp.zeros_like(dk_acc)
            dv_acc[...] = jnp.zeros_like(dv_acc)

        sls = [slice(n * HEAD_DIM, (n + 1) * HEAD_DIM) for n in range(hp)]
        rws = [slice(r * rs, (r + 1) * rs) for r in range(b // rs)]
        q_full = [q_ref[:, sl] for sl in sls]
        do_full = [do_ref[:, sl].astype(BF16) for sl in sls]
        qs = [q_full[n][rws[r]] for n, r in chains]
        dos = [do_full[n][rws[r]] for n, r in chains]
        tots = [tot_ref[n, rws[r], :] for n, r in chains]
        tri_incl = _tri(b, lambda r, c: r <= c)
        tri_excl = _tri(b, lambda r, c: r < c)
        diag = _diag_mask(b)

        def tile(j, carry, masked):
            s0 = pl.multiple_of(j * b, b)
            kbs = [k_ref[pl.ds(s0, b), sl] for sl in sls]
            vbs = [v_ref[pl.ds(s0, b), sl] for sl in sls]
            masks = [diag[rws[r]] if masked else None for _, r in chains]
            zs = [_nt(q, kbs[n]) for (n, _), q in zip(chains, qs)]
            ps = [_nt(dob, vbs[n]) for (n, _), dob in zip(chains, dos)]
            sc = [_sb_logs(z, mask) for z, mask in zip(zs, masks)]
            cums = [_split_dot(lm, tri_incl) for _, lm in sc]
            ab, gs = [], []
            for mask, (ls, _), cum, p, tot_q, (_, cl, _) in zip(masks, sc, cums, ps, tots, carry):
                a = jnp.exp2(ls + (tot_q - (cum + cl)))
                if masked:
                    a = jnp.where(mask, a, 0.0)
                ab.append(a.astype(BF16))
                gs.append(a * p)
            hs = [_nn(g.astype(BF16), tri_excl) for g in gs]
            dzb = []
            for mask, (ls, _), g, h, (_, _, cg) in zip(masks, sc, gs, hs, carry):
                dz = (g - (g + (h + cg)) * jnp.exp2(ls)) * scale
                if masked:
                    dz = jnp.where(mask, dz, 0.0)
                dzb.append(dz.astype(BF16))
            out = tuple((dq + _nn(dz, kbs[n]), cl + jnp.sum(lm, axis=1, keepdims=True), cg + jnp.sum(g, axis=1, keepdims=True))
                        for (n, _), dz, (_, lm), g, (dq, cl, cg) in zip(chains, dzb, sc, gs, carry))
            per = b // rs
            for n, sl in enumerate(sls):
                dz_all = jnp.concatenate(dzb[n * per:(n + 1) * per], axis=0) if per > 1 else dzb[n * per]
                a_all = jnp.concatenate(ab[n * per:(n + 1) * per], axis=0) if per > 1 else ab[n * per]
                dk_acc[pl.ds(s0, b), sl] += _tn(dz_all, q_full[n])
                dv_acc[pl.ds(s0, b), sl] += _tn(a_all, do_full[n])
            return out

        zero = jnp.zeros((rs, 1), F32)
        carry = tuple((jnp.zeros((rs, HEAD_DIM), F32), zero, zero) for _ in chains)
        carry = lax.fori_loop(0, i, lambda j, cr: tile(j, cr, False), carry)
        carry = tile(i, carry, True)
        for (n, r), (dq, _, _) in zip(chains, carry):
            dq_ref[rws[r], sls[n]] = dq.astype(dq_ref.dtype)

        @pl.when(i == nq - 1)
        def _():
            dk_ref[...] = dk_acc[...].astype(dk_ref.dtype)
            dv_ref[...] = dv_acc[...].astype(dv_ref.dtype)

    row = pl.BlockSpec((b, wide), lambda g, i: (i, g))
    col = lambda off: pl.BlockSpec((s, wide), lambda g, i: (0, off + g), pipeline_mode=pl.Buffered(1))
    shp = jax.ShapeDtypeStruct((s, heads * HEAD_DIM), BF16)
    return pl.pallas_call(
        body, name=name, grid=(groups, nq),
        in_specs=[row, col(groups), col(2 * groups), row, pl.BlockSpec((hp, b, 1), lambda g, i: (g, i, 0))],
        out_specs=[row, col(0), col(0)], out_shape=[shp, shp, shp],
        scratch_shapes=[pltpu.VMEM((s, wide), F32), pltpu.VMEM((s, wide), F32)],
        compiler_params=_cparams("parallel", "arbitrary"))(qkv, qkv, qkv, do, tot)


def _shift_rows(x, n, fill, *, down):
    rows = x.shape[0]
    if n % 8 == 0:
        pad = jnp.full((n, x.shape[1]), fill, x.dtype)
        return jnp.concatenate([pad, x[:rows - n]], axis=0) if down else jnp.concatenate([x[n:], pad], axis=0)
    t = lax.broadcasted_iota(jnp.int32, x.shape, 0)
    if down:
        return jnp.where(t >= n, pltpu.roll(x, n, 0), fill)
    return jnp.where(t < rows - n, pltpu.roll(x, rows - n, 0), fill)


def _scan_rows(a, b, *, reverse):
    n = 1
    while n < a.shape[0]:
        b = a * _shift_rows(b, n, 0.0, down=not reverse) + b
        a = a * _shift_rows(a, n, 1.0, down=not reverse)
        n *= 2
    return a, b


def _neg_expm1(x):
    p = 1.0 + x * (1.0 / 7.0)
    for k in (6.0, 5.0, 4.0, 3.0, 2.0):
        p = 1.0 + x * (1.0 / k) * p
    return jnp.where(x > -0.25, -(x * p), 1.0 - jnp.exp(x))


def _softplus_neg(lam):
    z = -lam
    e = jnp.exp(-jnp.abs(z))
    u = 1.0 + e
    d = u - 1.0
    log1p_e = jnp.where(d == 0.0, e, jnp.log(u) * (e / jnp.where(d == 0.0, 1.0, d)))
    return jnp.maximum(z, 0.0) + log1p_e


def _lru_gates(xp_ref, w, bias, wa_ref, ba, wx_ref, bx, sp, taps, tm, heads):
    halo = xp_ref.shape[0] - tm
    xc = jnp.broadcast_to(bias, (tm, w.shape[1]))
    for k in range(taps):
        xc = xc + w[k:k + 1, :] * xp_ref[pl.ds(halo - (taps - 1) + k, tm), :]
    xb = xc.astype(BF16)
    pr, pi = [], []
    for n in range(heads):
        xh = xb[:, n * HEAD_DIM:(n + 1) * HEAD_DIM]
        pr.append(_nn(xh, wa_ref[n]))
        pi.append(_nn(xh, wx_ref[n]))
    r = _sigmoid(jnp.concatenate(pr, axis=1) + ba)
    ig = _sigmoid(jnp.concatenate(pi, axis=1) + bx)
    log_a = (-LRU_C) * r * sp
    a = jnp.exp(log_a)
    mult = jnp.sqrt(_neg_expm1(2.0 * log_a))
    return xc, r, ig, a, mult


def _lru_fwd(ub, x_col, conv_w, conv_b, wa, ba, wx, bx, lam, *, name):
    s = ub.shape[0]
    taps, w = conv_w.shape
    heads = w // HEAD_DIM
    tm = _tile(s, 256)
    hb = tm // LRU_HALO

    def body(x_ref, xh_ref, cw_ref, cb_ref, wa_ref, ba_ref, wx_ref, bx_ref, lam_ref, h_ref, xp_ref, carry_ref):
        i = pl.program_id(0)

        @pl.when(i == 0)
        def _():
            carry_ref[...] = jnp.zeros_like(carry_ref)

        xp_ref[0:LRU_HALO, :] = jnp.where(i == 0, 0.0, xh_ref[...])
        xp_ref[LRU_HALO:, :] = x_ref[...]
        sp = _softplus_neg(lam_ref[...])
        xc, _, ig, a, mult = _lru_gates(xp_ref, cw_ref[...], cb_ref[...], wa_ref, ba_ref[...], wx_ref, bx_ref[...],
                                        sp, taps, tm, heads)
        ac, bc = _scan_rows(a, mult * (ig * xc), reverse=False)
        h = ac * carry_ref[0:1, :] + bc
        h_ref[...] = h
        carry_ref[...] = jnp.broadcast_to(h[tm - 1:tm, :], carry_ref.shape)

    full = lambda arr: pl.BlockSpec(arr.shape, lambda i: (0,) * arr.ndim)
    vecs = [conv_w, conv_b.reshape(1, w), wa.astype(BF16), ba.reshape(1, w), wx.astype(BF16), bx.reshape(1, w), lam.reshape(1, w)]
    return pl.pallas_call(
        body, name=name, grid=(s // tm,),
        in_specs=[pl.BlockSpec((tm, w), lambda i: (i, x_col)),
                  pl.BlockSpec((LRU_HALO, w), lambda i: (jnp.maximum(i * hb - 1, 0), x_col))] + [full(v) for v in vecs],
        out_specs=pl.BlockSpec((tm, w), lambda i: (i, 0)),
        out_shape=jax.ShapeDtypeStruct((s, w), F32),
        scratch_shapes=[pltpu.VMEM((tm + LRU_HALO, w), F32), pltpu.VMEM((8, w), F32)],
        compiler_params=_cparams("arbitrary"))(ub, ub, *vecs)


def _lru_bwd(ub, x_col, h, dh, conv_w, conv_b, wa, ba, wx, bx, lam, *, name):
    s = ub.shape[0]
    taps, w = conv_w.shape
    heads = w // HEAD_DIM
    tm = _tile(s, 256)
    hb = tm // LRU_HALO
    nt = s // tm

    def body(x_ref, xh_ref, h_ref, hh_ref, dh_ref, cw_ref, cb_ref, wa_ref, ba_ref, wx_ref, bx_ref, lam_ref,
             dx_ref, dcw_ref, dcb_ref, dwa_ref, dba_ref, dwx_ref, dbx_ref, dlam_ref,
             xp_ref, dxp_ref, dlt_ref, afirst_ref, dxc_next_ref, dsp_ref):
        step = pl.program_id(0)
        i = nt - 1 - step

        @pl.when(step == 0)
        def _():
            for ref in (dcw_ref, dcb_ref, dwa_ref, dba_ref, dwx_ref, dbx_ref, dlam_ref, dlt_ref, dxc_next_ref, dsp_ref):
                ref[...] = jnp.zeros_like(ref)
            afirst_ref[...] = jnp.ones_like(afirst_ref)

        xp_ref[0:LRU_HALO, :] = jnp.where(i == 0, 0.0, xh_ref[...])
        xp_ref[LRU_HALO:, :] = x_ref[...]
        cw = cw_ref[...]
        lam_v = lam_ref[...]
        sp = _softplus_neg(lam_v)
        xc, r, ig, a, mult = _lru_gates(xp_ref, cw, cb_ref[...], wa_ref, ba_ref[...], wx_ref, bx_ref[...], sp, taps, tm, heads)
        rows = lax.broadcasted_iota(jnp.int32, (tm, w), 0)
        a_next = jnp.where(rows == tm - 1, afirst_ref[0:1, :], _shift_rows(a, 1, 1.0, down=False))
        ac, bc = _scan_rows(a_next, dh_ref[...], reverse=True)
        delta = ac * dlt_ref[0:1, :] + bc
        hv = h_ref[...]
        h_last_prev = jnp.where(i == 0, 0.0, hh_ref[LRU_HALO - 1:LRU_HALO, :])
        h_prev = jnp.where(rows == 0, h_last_prev, _shift_rows(hv, 1, 0.0, down=True))
        gated = ig * xc
        da = delta * h_prev
        dmult = delta * gated
        dgated = delta * mult
        dlog_a = da * a - dmult * (a * a) / mult
        dpr = dlog_a * ((-LRU_C) * sp) * r * (1.0 - r)
        dpi = dgated * xc * ig * (1.0 - ig)
        dxc = dgated * ig
        dsp_ref[...] += jnp.sum(dlog_a * ((-LRU_C) * r), axis=0, keepdims=True)
        dba_ref[...] += jnp.sum(dpr, axis=0, keepdims=True)
        dbx_ref[...] += jnp.sum(dpi, axis=0, keepdims=True)
        xb = xc.astype(BF16)
        dprb = dpr.astype(BF16)
        dpib = dpi.astype(BF16)
        back = []
        for n in range(heads):
            sl = slice(n * HEAD_DIM, (n + 1) * HEAD_DIM)
            dwa_ref[n] += _tn(xb[:, sl], dprb[:, sl])
            dwx_ref[n] += _tn(xb[:, sl], dpib[:, sl])
            back.append(_nt(dprb[:, sl], wa_ref[n]) + _nt(dpib[:, sl], wx_ref[n]))
        dxc = dxc + jnp.concatenate(back, axis=1)
        dxp_ref[0:tm, :] = dxc
        dxp_ref[tm:, :] = dxc_next_ref[...]
        dx = jnp.zeros((tm, w), F32)
        dws = []
        for k in range(taps):
            dx = dx + cw[k:k + 1, :] * dxp_ref[pl.ds(taps - 1 - k, tm), :]
            dws.append(jnp.sum(dxc * xp_ref[pl.ds(LRU_HALO - (taps - 1) + k, tm), :], axis=0, keepdims=True))
        dx_ref[...] = dx.astype(dx_ref.dtype)
        dcw_ref[...] += jnp.concatenate(dws, axis=0)
        dcb_ref[...] += jnp.sum(dxc, axis=0, keepdims=True)
        dlt_ref[...] = jnp.broadcast_to(delta[0:1, :], dlt_ref.shape)
        afirst_ref[...] = jnp.broadcast_to(a[0:1, :], afirst_ref.shape)
        dxc_next_ref[...] = dxc[0:LRU_HALO, :]

        @pl.when(step == nt - 1)
        def _():
            dlam_ref[...] = dsp_ref[...] * (-_sigmoid(-lam_v))

    rev = lambda col: pl.BlockSpec((tm, w), lambda st: (nt - 1 - st, col))
    prev = lambda col: pl.BlockSpec((LRU_HALO, w), lambda st: (jnp.maximum((nt - 1 - st) * hb - 1, 0), col))
    full = lambda arr: pl.BlockSpec(arr.shape, lambda st: (0,) * arr.ndim)
    vec = pl.BlockSpec((1, w), lambda st: (0, 0))
    vecs = [conv_w, conv_b.reshape(1, w), wa.astype(BF16), ba.reshape(1, w), wx.astype(BF16), bx.reshape(1, w), lam.reshape(1, w)]
    vshape = jax.ShapeDtypeStruct((1, w), F32)
    return pl.pallas_call(
        body, name=name, grid=(nt,),
        in_specs=[rev(x_col), prev(x_col), rev(0), prev(0), rev(0)] + [full(v) for v in vecs],
        out_specs=[rev(0), full(conv_w), vec, full(wa), vec, full(wx), vec, vec],
        out_shape=[jax.ShapeDtypeStruct((s, w), BF16), jax.ShapeDtypeStruct(conv_w.shape, F32), vshape,
                   jax.ShapeDtypeStruct(wa.shape, F32), vshape, jax.ShapeDtypeStruct(wx.shape, F32), vshape, vshape],
        scratch_shapes=[pltpu.VMEM((tm + LRU_HALO, w), F32), pltpu.VMEM((tm + LRU_HALO, w), F32),
                        pltpu.VMEM((8, w), F32), pltpu.VMEM((8, w), F32), pltpu.VMEM((LRU_HALO, w), F32), pltpu.VMEM((1, w), F32)],
        compiler_params=_cparams("arbitrary"))(ub, ub, h, h, dh, *vecs)


def _group_fwd(y, w, gate):
    r = lax.rsqrt(jnp.mean(y * y, axis=-1, keepdims=True) + RMS_EPS)
    return ((y * r) * w) * (gate * _sigmoid(gate))


def _mix_out_fwd(y_conv, y_attn, y_lru, ua, ub, n_conv, n_attn, n_lru, *, name):
    s, c = y_conv.shape
    wa_ = y_attn.shape[1]
    d = 2 * c + wa_
    tm = _tile(s, 256)
    assert wa_ == 2 * c

    def body(yc_ref, ya_ref, yl_ref, gc_ref, ga_ref, gl_ref, nc_ref, na_ref, nl_ref, o_ref):
        o_ref[:, 0:c] = _group_fwd(yc_ref[...], nc_ref[...], gc_ref[...]).astype(o_ref.dtype)
        o_ref[:, c:c + wa_] = _group_fwd(ya_ref[...], na_ref[...], ga_ref[...]).astype(o_ref.dtype)
        o_ref[:, c + wa_:] = _group_fwd(yl_ref[...], nl_ref[...], gl_ref[...]).astype(o_ref.dtype)

    blk = lambda width, col: pl.BlockSpec((tm, width), lambda i: (i, col))
    vec = lambda width: pl.BlockSpec((1, width), lambda i: (0, 0))
    return pl.pallas_call(
        body, name=name, grid=(s // tm,),
        in_specs=[blk(c, 0), blk(wa_, 0), blk(c, 0), blk(c, 2), blk(wa_, 0), blk(c, 3), vec(c), vec(wa_), vec(c)],
        out_specs=blk(d, 0), out_shape=jax.ShapeDtypeStruct((s, d), BF16),
        compiler_params=_cparams("parallel"))(y_conv, y_attn, y_lru, ua, ub, ub, n_conv.reshape(1, c), n_attn.reshape(1, wa_), n_lru.reshape(1, c))


def _group_bwd(dout, y, w, gate):
    r = lax.rsqrt(jnp.mean(y * y, axis=-1, keepdims=True) + RMS_EPS)
    silu, dsilu = _silu_and_grad(gate)
    dy, dwp = _rms_bwd_math(dout * silu, y, r, w)
    return dy, dout * ((y * r) * w) * dsilu, dwp


def _mix_out_bwd(dy, y_conv, y_attn, y_lru, ua, ub, n_conv, n_attn, n_lru, *, name):
    s, c = y_conv.shape
    wa_ = y_attn.shape[1]
    tm = _tile(s, 256)

    def body(dy_ref, yc_ref, ya_ref, yl_ref, gc_ref, ga_ref, gl_ref, nc_ref, na_ref, nl_ref,
             dyc_ref, dya_ref, dyl_ref, dgc_ref, dga_ref, dgl_ref, dnc_ref, dna_ref, dnl_ref):
        @pl.when(pl.program_id(0) == 0)
        def _():
            for ref in (dnc_ref, dna_ref, dnl_ref):
                ref[...] = jnp.zeros_like(ref)

        groups = ((dy_ref[:, 0:c], yc_ref, nc_ref, gc_ref, dyc_ref, dgc_ref, dnc_ref),
                  (dy_ref[:, c:c + wa_], ya_ref, na_ref, ga_ref, dya_ref, dga_ref, dna_ref),
                  (dy_ref[:, c + wa_:], yl_ref, nl_ref, gl_ref, dyl_ref, dgl_ref, dnl_ref))
        for dout, y_ref, n_ref, g_ref, dyo_ref, dgo_ref, dn_ref in groups:
            dyv, dgv, dwp = _group_bwd(dout, y_ref[...], n_ref[...], g_ref[...])
            dyo_ref[...] = dyv
            dgo_ref[...] = dgv.astype(dgo_ref.dtype)
            dn_ref[...] += jnp.sum(dwp, axis=0, keepdims=True)

    blk = lambda width, col: pl.BlockSpec((tm, width), lambda i: (i, col))
    vec = lambda width: pl.BlockSpec((1, width), lambda i: (0, 0))
    sh = lambda width, dt: jax.ShapeDtypeStruct((s, width), dt)
    vs = lambda width: jax.ShapeDtypeStruct((1, width), F32)
    return pl.pallas_call(
        body, name=name, grid=(s // tm,),
        in_specs=[blk(2 * c + wa_, 0), blk(c, 0), blk(wa_, 0), blk(c, 0), blk(c, 2), blk(wa_, 0), blk(c, 3), vec(c), vec(wa_), vec(c)],
        out_specs=[blk(c, 0), blk(wa_, 0), blk(c, 0), blk(c, 0), blk(wa_, 0), blk(c, 0), vec(c), vec(wa_), vec(c)],
        out_shape=[sh(c, F32), sh(wa_, F32), sh(c, F32), sh(c, BF16), sh(wa_, BF16), sh(c, BF16), vs(c), vs(wa_), vs(c)],
        compiler_params=_cparams("arbitrary"))(dy, y_conv, y_attn, y_lru, ua, ub, ub, n_conv.reshape(1, c), n_attn.reshape(1, wa_), n_lru.reshape(1, c))


def _xattn_probs(qh, kh):
    sc = _nt(qh, kh) * (HEAD_DIM ** -0.5)
    e = jnp.exp(sc - jnp.max(sc, axis=-1, keepdims=True))
    return e / jnp.sum(e, axis=-1, keepdims=True)


def _xattn_fwd(q, kv, *, name):
    s, w = q.shape
    heads = w // HEAD_DIM
    tm = _tile(s, 512)

    def body(q_ref, kv_ref, o_ref):
        for n in range(heads):
            sl = slice(n * HEAD_DIM, (n + 1) * HEAD_DIM)
            p = _xattn_probs(q_ref[:, sl], kv_ref[:, sl])
            o_ref[:, sl] = _nn(p.astype(BF16), kv_ref[:, w + n * HEAD_DIM:w + (n + 1) * HEAD_DIM]).astype(o_ref.dtype)

    return pl.pallas_call(
        body, name=name, grid=(s // tm,),
        in_specs=[pl.BlockSpec((tm, w), lambda i: (i, 0)), pl.BlockSpec(kv.shape, lambda i: (0, 0))],
        out_specs=pl.BlockSpec((tm, w), lambda i: (i, 0)), out_shape=jax.ShapeDtypeStruct((s, w), BF16),
        compiler_params=_cparams("parallel"))(q, kv)


def _xattn_bwd(q, kv, do, *, name):
    s, w = q.shape
    heads = w // HEAD_DIM
    tm = _tile(s, 512)
    scale = HEAD_DIM ** -0.5

    def body(q_ref, kv_ref, do_ref, dq_ref, dkv_ref):
        @pl.when(pl.program_id(0) == 0)
        def _():
            dkv_ref[...] = jnp.zeros_like(dkv_ref)

        for n in range(heads):
            sl = slice(n * HEAD_DIM, (n + 1) * HEAD_DIM)
            vsl = slice(w + n * HEAD_DIM, w + (n + 1) * HEAD_DIM)
            qh, kh, vh, doh = q_ref[:, sl], kv_ref[:, sl], kv_ref[:, vsl], do_ref[:, sl]
            p = _xattn_probs(qh, kh)
            dp = _nt(doh, vh)
            ds = (p * (dp - jnp.sum(dp * p, axis=-1, keepdims=True)) * scale).astype(BF16)
            dq_ref[:, sl] = _nn(ds, kh).astype(dq_ref.dtype)
            dkv_ref[:, sl] += _tn(ds, qh)
            dkv_ref[:, vsl] += _tn(p.astype(BF16), doh)

    row = pl.BlockSpec((tm, w), lambda i: (i, 0))
    kvs = pl.BlockSpec(kv.shape, lambda i: (0, 0))
    return pl.pallas_call(
        body, name=name, grid=(s // tm,), in_specs=[row, kvs, row], out_specs=[row, kvs],
        out_shape=[jax.ShapeDtypeStruct((s, w), BF16), jax.ShapeDtypeStruct(kv.shape, F32)],
        compiler_params=_cparams("arbitrary"))(q, kv, do)


ROW_BLOCK_BYTES = 1 << 20


def _row_tile(rows, cols):
    limit = max(8, ROW_BLOCK_BYTES // (4 * cols))
    t = 8
    while t * 2 <= limit and rows % (t * 2) == 0:
        t *= 2
    assert rows % t == 0
    return t


def _cast_bf16(w, *, name):
    rows, cols = w.shape
    tr = _row_tile(rows, cols)

    def body(w_ref, o_ref):
        o_ref[...] = w_ref[...].astype(BF16)

    blk = pl.BlockSpec((tr, cols), lambda i: (i, 0))
    return pl.pallas_call(body, name=name, grid=(rows // tr,), in_specs=[blk], out_specs=blk,
                          out_shape=jax.ShapeDtypeStruct((rows, cols), BF16), compiler_params=_cparams("parallel"))(w)


def _sum_slots(land, *, name):
    slots, rows, cols = land.shape
    tr = _row_tile(rows, cols * slots)

    def body(l_ref, o_ref):
        acc = l_ref[0].astype(F32)
        for j in range(1, slots):
            acc = acc + l_ref[j].astype(F32)
        o_ref[...] = acc

    return pl.pallas_call(
        body, name=name, grid=(rows // tr,), in_specs=[pl.BlockSpec((slots, tr, cols), lambda i: (0, i, 0))],
        out_specs=pl.BlockSpec((tr, cols), lambda i: (i, 0)), out_shape=jax.ShapeDtypeStruct((rows, cols), F32),
        compiler_params=_cparams("parallel"))(land)


def _adamw(w, m, v, gs, *, name):
    rows, cols = w.shape
    tr = _row_tile(rows, cols * 4)
    ng = len(gs)

    def body(*refs):
        w_ref, m_ref, v_ref = refs[:3]
        g_refs = refs[3:3 + ng]
        g_out, d_out, m_out, v_out = refs[3 + ng:]
        g = g_refs[0][...]
        for r in g_refs[1:]:
            g = g + r[...]
        mn = ADAM_B1 * m_ref[...] + (1.0 - ADAM_B1) * g
        vn = ADAM_B2 * v_ref[...] + (1.0 - ADAM_B2) * (g * g)
        m_hat = mn / (1.0 - ADAM_B1 ** ADAM_STEP)
        v_hat = vn / (1.0 - ADAM_B2 ** ADAM_STEP)
        g_out[...] = g
        d_out[...] = -ADAM_LR * (m_hat / (jnp.sqrt(v_hat) + ADAM_EPS) + ADAM_WD * w_ref[...])
        m_out[...] = mn
        v_out[...] = vn

    blk = pl.BlockSpec((tr, cols), lambda i: (i, 0))
    shp = jax.ShapeDtypeStruct((rows, cols), F32)
    return pl.pallas_call(body, name=name, grid=(rows // tr,), in_specs=[blk] * (3 + ng), out_specs=[blk] * 4,
                          out_shape=[shp] * 4, compiler_params=_cparams("parallel"))(w, m, v, *gs)


def _adamw_layers(w, m, v, gs, *, name):
    layers, rows, cols = w.shape
    tr = _row_tile(rows, cols * 4)
    nblk = rows // tr
    ng = len(gs[0])

    def body(*refs):
        w_ref, m_ref, v_ref = refs[:3]
        g_refs = refs[3:3 + layers * ng]
        g_out, d_out, m_out, v_out = refs[3 + layers * ng:]
        l = pl.program_id(0)
        g = jnp.zeros((tr, cols), F32)
        for ll in range(layers):
            gl = g_refs[ll * ng][...]
            for r in g_refs[ll * ng + 1:(ll + 1) * ng]:
                gl = gl + r[...]
            g = jnp.where(l == ll, gl, g)
        mn = ADAM_B1 * m_ref[...] + (1.0 - ADAM_B1) * g
        vn = ADAM_B2 * v_ref[...] + (1.0 - ADAM_B2) * (g * g)
        m_hat = mn / (1.0 - ADAM_B1 ** ADAM_STEP)
        v_hat = vn / (1.0 - ADAM_B2 ** ADAM_STEP)
        g_out[...] = g
        d_out[...] = -ADAM_LR * (m_hat / (jnp.sqrt(v_hat) + ADAM_EPS) + ADAM_WD * w_ref[...])
        m_out[...] = mn
        v_out[...] = vn

    blk = pl.BlockSpec((None, tr, cols), lambda l, i: (l, i, 0))

    def g_spec(ll):
        return pl.BlockSpec((tr, cols), lambda l, i: (jnp.where(l == ll, i, jnp.where(l < ll, 0, nblk - 1)), 0))

    shp = jax.ShapeDtypeStruct(w.shape, F32)
    return pl.pallas_call(
        body, name=name, grid=(layers, nblk), in_specs=[blk] * 3 + [g_spec(ll) for ll in range(layers) for _ in range(ng)],
        out_specs=[blk] * 4, out_shape=[shp] * 4, compiler_params=_cparams("arbitrary", "arbitrary"),
    )(w, m, v, *[g for gl in gs for g in gl])


OTHER_CHIPS = ((1, 0), (0, 1), (1, 1))
N_CHIPS = 4
ANY = pl.BlockSpec(memory_space=pl.ANY)


def _place():
    return lax.axis_index("x"), lax.axis_index("y"), lax.axis_index("c")


def _flip(v, f):
    return 1 - v if f else v


def _part(ref, lead, axis, chip, size):
    idx = list(lead) + [slice(None)] * (len(ref.shape) - len(lead))
    idx[len(lead) + axis] = pl.ds(pl.multiple_of(chip * size, size), size)
    return ref.at[tuple(idx)]


def _allgather_chips(shards, axes, *, name):
    n = len(shards)
    sizes = [sh.shape[ax] for sh, ax in zip(shards, axes)]

    def full_shape(sh, ax):
        return tuple(d * N_CHIPS if i == ax else d for i, d in enumerate(sh.shape))

    def body(*refs):
        ins, outs = refs[:n], refs[n:2 * n]
        send_sems, recv_sems, loc_sems = refs[2 * n:]
        x, y, c = _place()
        me = 2 * x + y
        local = []
        for a in range(n):
            cp = pltpu.make_async_copy(ins[a], _part(outs[a], (), axes[a], me, sizes[a]), loc_sems.at[a])
            cp.start()
            local.append(cp)

        def remote(a, j, chip):
            fx, fy = OTHER_CHIPS[j]
            return pltpu.make_async_remote_copy(
                src_ref=ins[a], dst_ref=_part(outs[a], (), axes[a], chip, sizes[a]),
                send_sem=send_sems.at[a, j], recv_sem=recv_sems.at[a, j],
                device_id=(_flip(x, fx), _flip(y, fy), c), device_id_type=MESH)

        for a in range(n):
            for j in range(len(OTHER_CHIPS)):
                remote(a, j, me).start()
        for a in range(n):
            for j, (fx, fy) in enumerate(OTHER_CHIPS):
                remote(a, j, 2 * _flip(x, fx) + _flip(y, fy)).wait()
        for cp in local:
            cp.wait()

    return pl.pallas_call(
        body, name=name, in_specs=[ANY] * n, out_specs=[ANY] * n,
        out_shape=[jax.ShapeDtypeStruct(full_shape(sh, ax), sh.dtype) for sh, ax in zip(shards, axes)],
        scratch_shapes=[pltpu.SemaphoreType.DMA((n, 3)), pltpu.SemaphoreType.DMA((n, 3)), pltpu.SemaphoreType.DMA((n,))],
    )(*shards)


HBM = pl.BlockSpec(memory_space=pltpu.HBM)
SEM = pl.BlockSpec(memory_space=pltpu.SEMAPHORE)
SPLIT_COPY = pltpu.CompilerParams(has_side_effects=pltpu.SideEffectType.DATAFLOW_SIDE_EFFECTING)


def _cast_place(w, layer, axis, chip, *, name):
    _, rows, cols = w.shape
    tr = _row_tile(rows, cols)
    nblk = rows // tr

    def body(chip_ref, w_ref, o_ref):
        o_ref[...] = w_ref[...].astype(BF16)

    if axis == 1:
        shape = (rows, cols * N_CHIPS)
        o_spec = pl.BlockSpec((tr, cols), lambda i, chip_ref: (i, chip_ref[0]))
    else:
        shape = (rows * N_CHIPS, cols)
        o_spec = pl.BlockSpec((tr, cols), lambda i, chip_ref: (chip_ref[0] * nblk + i, 0))
    return pl.pallas_call(
        body, name=name,
        grid_spec=pltpu.PrefetchScalarGridSpec(
            num_scalar_prefetch=1, grid=(nblk,),
            in_specs=[pl.BlockSpec((None, tr, cols), lambda i, chip_ref: (layer, i, 0))], out_specs=o_spec),
        out_shape=jax.ShapeDtypeStruct(shape, BF16), compiler_params=_cparams("parallel"))(chip, w)


def _gather_copy(refs, a, j, send_sems, recv_sems, *, axes, sizes, arriving):
    x, y, c = _place()
    fx, fy = OTHER_CHIPS[j]
    px, py = _flip(x, fx), _flip(y, fy)
    part = _part(refs[a], (), axes[a], (2 * px + py) if arriving else (2 * x + y), sizes[a])
    k = a * len(OTHER_CHIPS) + j
    return pltpu.make_async_remote_copy(src_ref=part, dst_ref=part, send_sem=send_sems.at[k], recv_sem=recv_sems.at[k],
                                        device_id=(px, py, c), device_id_type=MESH)


def _scatter_copy(srcs, lands, a, j, axes, sizes, send_sems, recv_sems):
    x, y, c = _place()
    fx, fy = OTHER_CHIPS[j]
    px, py = _flip(x, fx), _flip(y, fy)
    k = a * len(OTHER_CHIPS) + j
    return pltpu.make_async_remote_copy(src_ref=_part(srcs[a], (), axes[a], 2 * px + py, sizes[a]), dst_ref=lands[a].at[j],
                                        send_sem=send_sems.at[k], recv_sem=recv_sems.at[k],
                                        device_id=(px, py, c), device_id_type=MESH)


def _split_start(arrs, make_copy, ncopies, dep, *, name):
    n = len(arrs)

    def body(*refs):
        ins = refs[:n]
        send_sems, recv_sems = refs[n + 1], refs[n + 2]
        token = refs[n + 3 + n]
        for a in range(ncopies):
            for j in range(len(OTHER_CHIPS)):
                make_copy(ins, a, j, send_sems, recv_sems).start()
        token[...] = jnp.zeros_like(token)

    sem = pltpu.SemaphoreType.DMA((ncopies * len(OTHER_CHIPS),))
    res = pl.pallas_call(
        body, name=name,
        out_shape=(sem, sem, *[pltpu.HBM(a.shape, a.dtype) for a in arrs], jax.ShapeDtypeStruct((8, LANES), F32)),
        in_specs=[HBM] * n + [pl.BlockSpec(memory_space=pl.ANY)],
        out_specs=(SEM, SEM, *[HBM] * n, pl.BlockSpec(memory_space=pltpu.VMEM)),
        input_output_aliases={a: 2 + a for a in range(n)}, compiler_params=SPLIT_COPY,
    )(*[pltpu.with_memory_space_constraint(a, pltpu.HBM) for a in arrs], dep)
    return res[0], res[1], list(res[2:2 + n]), res[2 + n]


def _split_wait(arrs, send_sems, recv_sems, make_copy, ncopies, after, *, name):
    n = len(arrs)

    def body(*refs):
        ins = refs[:n]
        send, recv = refs[n], refs[n + 1]
        for a in range(ncopies):
            for j in range(len(OTHER_CHIPS)):
                cp = make_copy(ins, a, j, send, recv)
                cp.wait_send()
                cp.wait_recv()

    res = pl.pallas_call(
        body, name=name, out_shape=tuple(pltpu.HBM(a.shape, a.dtype) for a in arrs),
        in_specs=[HBM] * n + [SEM, SEM, pl.BlockSpec(memory_space=pl.ANY)], out_specs=tuple([HBM] * n),
        input_output_aliases={a: a for a in range(n)}, compiler_params=SPLIT_COPY,
    )(*arrs, send_sems, recv_sems, after)
    return list(res)


def _sum_own_and_slots(g, land, axis, chip, *, name):
    slots, rows, cols = land.shape
    tr = _row_tile(rows, cols * 4)
    nblk = rows // tr

    def body(chip_ref, g_ref, l_ref, o_ref):
        acc = g_ref[...].astype(F32)
        for j in range(slots):
            acc = acc + l_ref[j].astype(F32)
        o_ref[...] = acc

    if axis == 1:
        g_spec = pl.BlockSpec((tr, cols), lambda i, chip_ref: (i, chip_ref[0]))
    else:
        g_spec = pl.BlockSpec((tr, cols), lambda i, chip_ref: (chip_ref[0] * nblk + i, 0))
    return pl.pallas_call(
        body, name=name,
        grid_spec=pltpu.PrefetchScalarGridSpec(
            num_scalar_prefetch=1, grid=(nblk,),
            in_specs=[g_spec, pl.BlockSpec((slots, tr, cols), lambda i, chip_ref: (0, i, 0))],
            out_specs=pl.BlockSpec((tr, cols), lambda i, chip_ref: (i, 0))),
        out_shape=jax.ShapeDtypeStruct((rows, cols), F32), compiler_params=_cparams("parallel"))(chip, g, land)


def _scatter_chips(grads, axes, *, name):
    n = len(grads)
    layers = len(grads[0])
    sizes = [g[0].shape[ax] // N_CHIPS for g, ax in zip(grads, axes)]

    def land_shape(g, ax):
        return (N_CHIPS, layers) + tuple(d // N_CHIPS if i == ax else d for i, d in enumerate(g[0].shape))

    def body(*refs):
        ins = [refs[a * layers:(a + 1) * layers] for a in range(n)]
        outs = refs[n * layers:n * layers + n]
        send_sems, recv_sems, loc_sems = refs[n * layers + n:]
        x, y, c = _place()
        me = 2 * x + y
        local = []
        for a in range(n):
            for l in range(layers):
                cp = pltpu.make_async_copy(_part(ins[a][l], (), axes[a], me, sizes[a]), outs[a].at[3, l], loc_sems.at[a, l])
                cp.start()
                local.append(cp)

        def remote(a, l, j):
            fx, fy = OTHER_CHIPS[j]
            px, py = _flip(x, fx), _flip(y, fy)
            return pltpu.make_async_remote_copy(
                src_ref=_part(ins[a][l], (), axes[a], 2 * px + py, sizes[a]), dst_ref=outs[a].at[j, l],
                send_sem=send_sems.at[a, l, j], recv_sem=recv_sems.at[a, l, j],
                device_id=(px, py, c), device_id_type=MESH)

        todo = [(a, l, j) for a in range(n) for l in range(layers) for j in range(len(OTHER_CHIPS))]
        for t in todo:
            remote(*t).start()
        for t in todo:
            remote(*t).wait()
        for cp in local:
            cp.wait()

    flat = [g for gs in grads for g in gs]
    return pl.pallas_call(
        body, name=name, in_specs=[ANY] * len(flat), out_specs=[ANY] * n,
        out_shape=[jax.ShapeDtypeStruct(land_shape(g, ax), g[0].dtype) for g, ax in zip(grads, axes)],
        scratch_shapes=[pltpu.SemaphoreType.DMA((n, layers, 3)), pltpu.SemaphoreType.DMA((n, layers, 3)),
                        pltpu.SemaphoreType.DMA((n, layers))],
    )(*flat)


def _swap_sibling(arrs, *, name):
    n = len(arrs)

    def body(*refs):
        ins, outs = refs[:n], refs[n:2 * n]
        send_sems, recv_sems = refs[2 * n:]
        x, y, c = _place()
        copies = [pltpu.make_async_remote_copy(src_ref=ins[a], dst_ref=outs[a], send_sem=send_sems.at[a], recv_sem=recv_sems.at[a],
                                               device_id=(x, y, 1 - c), device_id_type=MESH) for a in range(n)]
        for cp in copies:
            cp.start()
        for cp in copies:
            cp.wait()

    return pl.pallas_call(
        body, name=name, in_specs=[ANY] * n, out_specs=[ANY] * n,
        out_shape=[jax.ShapeDtypeStruct(a.shape, a.dtype) for a in arrs],
        scratch_shapes=[pltpu.SemaphoreType.DMA((n,)), pltpu.SemaphoreType.DMA((n,))],
    )(*arrs)


def _allreduce_small(p, *, name):
    rows, cols = p.shape
    ndev = 8

    def body(p_ref, o_ref, land_ref, send_sems, recv_sems):
        x, y, c = _place()
        me = 4 * x + 2 * y + c
        land_ref[0] = p_ref[...]
        copies = []
        for r in range(1, ndev):
            peer = (_flip(x, r & 4), _flip(y, r & 2), _flip(c, r & 1))
            copies.append(pltpu.make_async_remote_copy(src_ref=p_ref, dst_ref=land_ref.at[r], send_sem=send_sems.at[r],
                                                       recv_sem=recv_sems.at[r], device_id=peer, device_id_type=MESH))
        for cp in copies:
            cp.start()
        for cp in copies:
            cp.wait()
        acc = land_ref[me]
        for d in range(1, ndev):
            acc = acc + land_ref[jnp.bitwise_xor(me, d)]
        o_ref[...] = acc

    vm = pl.BlockSpec(memory_space=pltpu.VMEM)
    return pl.pallas_call(
        body, name=name, in_specs=[vm], out_specs=vm, out_shape=jax.ShapeDtypeStruct((rows, cols), F32),
        scratch_shapes=[pltpu.VMEM((ndev, rows, cols), F32), pltpu.SemaphoreType.DMA((ndev,)), pltpu.SemaphoreType.DMA((ndev,))],
        compiler_params=pltpu.CompilerParams(vmem_limit_bytes=V7X_VMEM_LIMIT_BYTES))(p)


WEIGHTS = ("mix_norm_g", "w_in", "conv_dw_w", "conv_dw_b", "conv_ln_g", "conv_ln_b", "conv_pw_w", "lru_conv_w", "lru_conv_b",
           "lru_wa", "lru_ba", "lru_wx", "lru_bx", "lru_lambda", "out_norm_conv", "out_norm_attn", "out_norm_lru", "w_out",
           "xattn_norm_g", "mem_norm_g", "xattn_wq", "xattn_wkv", "xattn_wo", "final_norm_g")
BIG = {"w_in": 2, "conv_pw_w": 1, "w_out": 1, "xattn_wq": 1, "xattn_wkv": 1, "xattn_wo": 2}
SMALL_SHARDED = {"conv_dw_w": 2, "lru_conv_w": 2}


IN_GROUP = ("w_in", "conv_pw_w")
REST_GROUP = ("w_out", "xattn_wq", "xattn_wkv", "xattn_wo")


def _trunk(x, mem, target, p, fetch, grads_ready):
    depth = p["mix_norm_g"].shape[0]
    c = p["conv_dw_w"].shape[2]
    aw = p["out_norm_attn"].shape[1]
    heads = aw // HEAD_DIM
    saved = []
    for l in range(depth):
        t = f"l{l}_"
        h1, r1 = _rmsnorm_fwd(x, p["mix_norm_g"][l], name=t + "mix_norm")
        wl = dict(fetch(IN_GROUP, l, r1))
        ua = _matmul(h1, wl["w_in"], mode="nn", n=3 * c, b_off=0, name=t + "in_conv")
        qkv = _matmul(h1, wl["w_in"], mode="nn", n=3 * aw, b_off=3 * c, out_dtype=BF16, name=t + "in_qkv")
        ub = _matmul(h1, wl["w_in"], mode="nn", n=aw + 2 * c, b_off=3 * c + 3 * aw, name=t + "in_gates")
        y_conv = _conv_fwd(ua, p["conv_dw_w"][l], p["conv_dw_b"][l], p["conv_ln_g"][l], p["conv_ln_b"][l], wl["conv_pw_w"],
                           name=t + "conv_fwd")
        y_attn, tot = _sb_fwd(qkv, heads, name=t + "sb_fwd")
        y_lru = _lru_fwd(ub, aw // c, p["lru_conv_w"][l], p["lru_conv_b"][l], p["lru_wa"][l], p["lru_ba"][l], p["lru_wx"][l],
                         p["lru_bx"][l], p["lru_lambda"][l], name=t + "lru_fwd")
        y = _mix_out_fwd(y_conv, y_attn, y_lru, ua, ub, p["out_norm_conv"][l], p["out_norm_attn"][l], p["out_norm_lru"][l],
                         name=t + "mix_out_fwd")
        wl.update(fetch(REST_GROUP, l, y))
        x2 = _matmul(y, wl["w_out"], mode="nn", add=x, name=t + "out_proj")
        h2, r2 = _rmsnorm_fwd(x2, p["xattn_norm_g"][l], name=t + "xattn_norm")
        qx = _matmul(h2, wl["xattn_wq"], mode="nn", out_dtype=BF16, name=t + "xattn_q")
        memn, rm = _rmsnorm_fwd(mem, p["mem_norm_g"][l], name=t + "mem_norm")
        kv = _matmul(memn, wl["xattn_wkv"], mode="nn", out_dtype=BF16, name=t + "xattn_kv")
        o = _xattn_fwd(qx, kv, name=t + "xattn_fwd")
        x3 = _matmul(o, wl["xattn_wo"], mode="nn", add=x2, name=t + "xattn_o")
        saved.append(dict(x=x, h1=h1, r1=r1, ua=ua, qkv=qkv, ub=ub, y_conv=y_conv, y_attn=y_attn, tot=tot, y_lru=y_lru, y=y,
                          x2=x2, h2=h2, r2=r2, qx=qx, memn=memn, rm=rm, kv=kv, o=o, w=wl))
        x = x3

    loss, dx, dg_final = _final_loss(x, p["final_norm_g"], target, name="final_loss")
    small = {k: [None] * depth for k in WEIGHTS if k not in BIG and k != "final_norm_g"}
    token = None
    for l in reversed(range(depth)):
        t = f"l{l}_"
        s = saved[l]
        wl = s["w"]
        do = _matmul(dx, wl["xattn_wo"], mode="nt", out_dtype=BF16, dep=token, name=t + "d_xattn_o")
        dwo = _matmul(s["o"], dx, mode="tn", out_dtype=BF16, name=t + "dw_xattn_o")
        dqx, dkv = _xattn_bwd(s["qx"], s["kv"], do, name=t + "xattn_bwd")
        dwq = _matmul(s["h2"], dqx, mode="tn", out_dtype=BF16, name=t + "dw_xattn_q")
        dh2 = _matmul(dqx, wl["xattn_wq"], mode="nt", name=t + "d_xattn_q")
        dx2, dg = _rmsnorm_bwd(dh2, s["x2"], s["r2"], p["xattn_norm_g"][l], dx, name=t + "xattn_norm_bwd")
        small["xattn_norm_g"][l] = dg[0]
        dmemn = _matmul(dkv, wl["xattn_wkv"], mode="nt", name=t + "d_xattn_kv")
        dwkv = _matmul(s["memn"], dkv, mode="tn", out_dtype=BF16, name=t + "dw_xattn_kv")
        _, dg = _rmsnorm_bwd(dmemn, mem, s["rm"], p["mem_norm_g"][l], None, name=t + "mem_norm_bwd")
        small["mem_norm_g"][l] = dg[0]
        dwout = _matmul(s["y"], dx2, mode="tn", out_dtype=BF16, name=t + "dw_out_proj")
        token = grads_ready(REST_GROUP, l, dict(w_out=dwout, xattn_wq=dwq, xattn_wkv=dwkv, xattn_wo=dwo))
        dy = _matmul(dx2, wl["w_out"], mode="nt", dep=token, name=t + "d_out_proj")
        dyc, dya, dyl, dgc, dga, dgl, dnc, dna, dnl = _mix_out_bwd(
            dy, s["y_conv"], s["y_attn"], s["y_lru"], s["ua"], s["ub"], p["out_norm_conv"][l], p["out_norm_attn"][l],
            p["out_norm_lru"][l], name=t + "mix_out_bwd")
        small["out_norm_conv"][l], small["out_norm_attn"][l], small["out_norm_lru"][l] = dnc[0], dna[0], dnl[0]
        dd, dpw, dlg, dlb = _conv_bwd_a(s["ua"], dyc, p["conv_dw_w"][l], p["conv_dw_b"][l], p["conv_ln_g"][l], p["conv_ln_b"][l],
                                        wl["conv_pw_w"], name=t + "conv_bwd_a")
        dval, dglu, ddw, ddb = _conv_bwd_b(s["ua"], dd, p["conv_dw_w"][l], name=t + "conv_bwd_b")
        small["conv_ln_g"][l], small["conv_ln_b"][l], small["conv_dw_w"][l], small["conv_dw_b"][l] = dlg[0], dlb[0], ddw, ddb[0]
        dq, dk, dv = _sb_bwd(s["qkv"], dya, s["tot"], heads, name=t + "sb_bwd")
        drx, dcw, dcb, dwa, dba, dwx, dbx, dlam = _lru_bwd(
            s["ub"], aw // c, s["y_lru"], dyl, p["lru_conv_w"][l], p["lru_conv_b"][l], p["lru_wa"][l], p["lru_ba"][l],
            p["lru_wx"][l], p["lru_bx"][l], p["lru_lambda"][l], name=t + "lru_bwd")
        small["lru_conv_w"][l], small["lru_conv_b"][l], small["lru_wa"][l], small["lru_ba"][l] = dcw, dcb[0], dwa, dba[0]
        small["lru_wx"][l], small["lru_bx"][l], small["lru_lambda"][l] = dwx, dbx[0], dlam[0]
        du = jnp.concatenate([dval, dglu, dgc, dq, dk, dv, dga, drx, dgl], axis=1)
        dwin = _matmul(s["h1"], du, mode="tn", out_dtype=BF16, name=t + "dw_in")
        token = grads_ready(IN_GROUP, l, dict(w_in=dwin, conv_pw_w=_cast_bf16(dpw, name=t + "cast_dpw")))
        dh1 = _matmul(du, wl["w_in"], mode="nt", dep=token, name=t + "d_in")
        dx, dg = _rmsnorm_bwd(dh1, s["x"], s["r1"], p["mix_norm_g"][l], dx2, name=t + "mix_norm_bwd")
        small["mix_norm_g"][l] = dg[0]
    small = {k: jnp.stack(v) for k, v in small.items()}
    small["final_norm_g"] = dg_final[0]
    return loss, dx, small


def _pack(arrs):
    flat = jnp.concatenate([a.reshape(-1) for a in arrs])
    pad = (-flat.shape[0]) % (8 * LANES)
    return jnp.pad(flat, (0, pad)).reshape(-1, LANES)


def _unpack(packed, like):
    flat = packed.reshape(-1)
    out, at = [], 0
    for a in like:
        out.append(flat[at:at + a.size].reshape(a.shape))
        at += a.size
    return out


def _as_rows(a):
    return a.reshape(-1, a.shape[-1])


def kernel(x, mem, mix_norm_g, w_in, conv_dw_w, conv_dw_b, conv_ln_g, conv_ln_b, conv_pw_w, lru_conv_w, lru_conv_b, lru_wa, lru_ba, lru_wx, lru_bx, lru_lambda, out_norm_conv, out_norm_attn, out_norm_lru, w_out, xattn_norm_g, mem_norm_g, xattn_wq, xattn_wkv, xattn_wo, final_norm_g, loss_target, m_mix_norm_g, m_w_in, m_conv_dw_w, m_conv_dw_b, m_conv_ln_g, m_conv_ln_b, m_conv_pw_w, m_lru_conv_w, m_lru_conv_b, m_lru_wa, m_lru_ba, m_lru_wx, m_lru_bx, m_lru_lambda, m_out_norm_conv, m_out_norm_attn, m_out_norm_lru, m_w_out, m_xattn_norm_g, m_mem_norm_g, m_xattn_wq, m_xattn_wkv, m_xattn_wo, m_final_norm_g, v_mix_norm_g, v_w_in, v_conv_dw_w, v_conv_dw_b, v_conv_ln_g, v_conv_ln_b, v_conv_pw_w, v_lru_conv_w, v_lru_conv_b, v_lru_wa, v_lru_ba, v_lru_wx, v_lru_bx, v_lru_lambda, v_out_norm_conv, v_out_norm_attn, v_out_norm_lru, v_w_out, v_xattn_norm_g, v_mem_norm_g, v_xattn_wq, v_xattn_wkv, v_xattn_wo, v_final_norm_g):
    given = dict(locals())
    w = {k: given[k] for k in WEIGHTS}
    m = {k: given["m_" + k] for k in WEIGHTS}
    v = {k: given["v_" + k] for k in WEIGHTS}
    depth = mix_norm_g.shape[0]
    chip = 2 * lax.axis_index("x") + lax.axis_index("y")

    chip_arr = chip.astype(jnp.int32).reshape(1)

    p = dict(w)
    p.update(zip(SMALL_SHARDED, _allgather_chips([w[k] for k in SMALL_SHARDED], list(SMALL_SHARDED.values()), name="gather_small")))
    axis2d = {k: BIG[k] - 1 for k in BIG}
    groups = [(IN_GROUP, 0), (REST_GROUP, 0)] + [(IN_GROUP + REST_GROUP, l) for l in range(1, depth)]
    pending, token = {}, jnp.zeros((8, LANES), F32)
    for names, l in groups:
        arrs = [_cast_place(w[k], l, axis2d[k], chip_arr, name=f"place{l}_{k}") for k in names]
        axes = [axis2d[k] for k in names]
        sizes = [a.shape[ax] // N_CHIPS for a, ax in zip(arrs, axes)]
        start = functools.partial(_gather_copy, axes=axes, sizes=sizes, arriving=False)
        land = functools.partial(_gather_copy, axes=axes, sizes=sizes, arriving=True)
        send, recv, arrs, token = _split_start(arrs, start, len(arrs), token, name=f"gather_start{l}_{names[0]}")
        pending[(names[0], l)] = (names, arrs, send, recv, land)
    last_token = token
    have = {}

    def fetch(group, l, after):
        key = (group[0], l)
        if key in pending:
            names, arrs, send, recv, land = pending.pop(key)
            after = last_token if (group, l) == groups[0] else after
            arrs = _split_wait(arrs, send, recv, land, len(arrs), after, name=f"gather_wait{l}_{names[0]}")
            have.update({(k, l): a for k, a in zip(names, arrs)})
        return {k: have[(k, l)] for k in group}

    flying = []
    held = {}

    def grads_ready(group, l, grads):
        held.update({(k, l): g for k, g in grads.items()})
        if l > 0 and group == REST_GROUP:
            return None
        names = [k for k in (IN_GROUP + REST_GROUP if l > 0 else group)]
        srcs = [held[(k, l)] for k in names]
        axes = [axis2d[k] for k in names]
        sizes = [g.shape[ax] // N_CHIPS for g, ax in zip(srcs, axes)]
        lands = [lax.empty((len(OTHER_CHIPS),) + tuple(sz if i == ax else d for i, d in enumerate(g.shape)), g.dtype)
                 for g, ax, sz in zip(srcs, axes, sizes)]
        n = len(names)
        copy = lambda refs, a, j, ss, rs_: _scatter_copy(refs[:n], refs[n:], a, j, axes, sizes, ss, rs_)
        send, recv, arrs, token = _split_start(srcs + lands, copy, n, jnp.zeros((8, LANES), F32), name=f"scatter_start{l}_{names[0]}")
        flying.append((names, l, axes, arrs, send, recv, copy))
        return token

    loss, grad_x, small = _trunk(x[0], mem[0], loss_target[0], p, fetch, grads_ready)
    loss = lax.psum(loss[0, 0], ("x", "y", "c"))

    sums = {}
    out = {}

    def arrive(entry, after):
        names, l, axes, arrs, send, recv, copy = entry
        n = len(names)
        arrs = _split_wait(arrs, send, recv, copy, n, after, name=f"scatter_wait{l}_{names[0]}")
        for k, ax, g, ld in zip(names, axes, arrs[:n], arrs[n:]):
            ld = ld.reshape((len(OTHER_CHIPS), -1, ld.shape[-1]))
            sums[(k, l)] = _sum_own_and_slots(g, ld, ax, chip_arr, name=f"sum{l}_{k}")
        return sums[(names[-1], l)]

    def update(names):
        mine = [sums[(k, l)] for k in names for l in range(depth)]
        theirs = _swap_sibling(mine, name="swap_sums_" + names[0])
        for i, k in enumerate(names):
            gs = [[mine[i * depth + l], theirs[i * depth + l]] for l in range(depth)]
            out[k] = _adamw_layers(w[k], m[k], v[k], gs, name="adamw_" + k)

    after = grad_x
    for entry in flying[:-1]:
        after = arrive(entry, after)
    update(REST_GROUP)

    small_names = [k for k in WEIGHTS if k not in BIG]
    total = _unpack(_allreduce_small(_pack([small[k] for k in small_names]), name="allreduce_small"), [small[k] for k in small_names])
    g_small = dict(zip(small_names, total))
    for k, ax in SMALL_SHARDED.items():
        size = w[k].shape[ax]
        g_small[k] = lax.dynamic_slice_in_dim(g_small[k], chip * size, size, axis=ax)
    res = _adamw(_pack([w[k] for k in small_names]), _pack([m[k] for k in small_names]), _pack([v[k] for k in small_names]),
                 [_pack([g_small[k] for k in small_names])], name="adamw_small")
    last = res[0]
    res = [_unpack(r, [w[k] for k in small_names]) for r in res]
    for i, k in enumerate(small_names):
        out[k] = [r[i] for r in res]

    arrive(flying[-1], last)
    update(IN_GROUP)

    outs = [loss, grad_x[None]]
    for part in range(4):
        outs += [out[k][part] for k in WEIGHTS]
    return tuple(outs)
```

```python
import functools

import jax
import jax.numpy as jnp
from jax import lax
from jax.experimental import pallas as pl
from jax.experimental.pallas import tpu as pltpu

F32 = jnp.float32
BF16 = jnp.bfloat16
MESH = pl.DeviceIdType.MESH

V7X_VMEM_LIMIT_BYTES = 56 * 1024 * 1024
LANES = 128
HEAD_DIM = 128
LRU_C = 8.0
RMS_EPS = 1e-6
LN_EPS = 1e-5
CONV_HALO = 32
LRU_HALO = 8
ADAM_LR = 0.001
ADAM_B1 = 0.9
ADAM_B2 = 0.999
ADAM_EPS = 1e-08
ADAM_WD = 0.01
ADAM_STEP = 10


def _cparams(*sem):
    return pltpu.CompilerParams(dimension_semantics=sem, vmem_limit_bytes=V7X_VMEM_LIMIT_BYTES)


def _tile(n, pref):
    if n <= pref:
        return n
    for t in range(pref - pref % LANES, 0, -LANES):
        if n % t == 0:
            return t
    t = pref
    while n % t:
        t //= 2
    return t


def _dot(a, b, dims):
    return lax.dot_general(a, b, (dims, ((), ())), preferred_element_type=F32)


def _nn(a, b):
    return _dot(a, b, ((1,), (0,)))


def _nt(a, b):
    return _dot(a, b, ((1,), (1,)))


def _tn(a, b):
    return _dot(a, b, ((0,), (0,)))


def _sigmoid(x):
    return jax.nn.sigmoid(x)


def _silu_and_grad(x):
    s = _sigmoid(x)
    return x * s, s * (1.0 + x * (1.0 - s))


def _matmul(a, b, *, mode, name, layer=None, n=None, b_off=0, add=None, dep=None, out_dtype=F32, tm=1024, tn=1024, tk=2048):
    bshape = b.shape if layer is None else b.shape[1:]
    if mode == "nn":
        m, k = a.shape
        n = bshape[1] if n is None else n
    elif mode == "nt":
        m, k = a.shape
        n = bshape[0]
    else:
        k, m = a.shape
        n = bshape[1]
    tm, tk = _tile(m, tm), _tile(k, tk)
    tn = _tile(n, tn)
    while b_off % tn or n % tn:
        tn -= LANES
    nk = k // tk
    off = b_off // tn
    lead = () if layer is None else (None,)
    li = () if layer is None else (layer,)
    if mode == "nn":
        a_spec = pl.BlockSpec((tm, tk), lambda i, j, kk: (i, kk))
        b_spec = pl.BlockSpec(lead + (tk, tn), lambda i, j, kk: li + (kk, j + off))
        dot = _nn
    elif mode == "nt":
        a_spec = pl.BlockSpec((tm, tk), lambda i, j, kk: (i, kk))
        b_spec = pl.BlockSpec(lead + (tn, tk), lambda i, j, kk: li + (j, kk))
        dot = _nt
    else:
        a_spec = pl.BlockSpec((tk, tm), lambda i, j, kk: (kk, i))
        b_spec = pl.BlockSpec(lead + (tk, tn), lambda i, j, kk: li + (kk, j))
        dot = _tn
    o_spec = pl.BlockSpec((tm, tn), lambda i, j, kk: (i, j))
    has_add = add is not None

    def body(*refs):
        refs = refs[:-3] + refs[-2:] if dep is not None else refs
        if has_add:
            a_ref, b_ref, add_ref, o_ref, acc_ref = refs
        else:
            a_ref, b_ref, o_ref, acc_ref = refs
        kk = pl.program_id(2)
        part = dot(a_ref[...].astype(BF16), b_ref[...].astype(BF16))

        @pl.when(kk == 0)
        def _():
            acc_ref[...] = part

        @pl.when(kk > 0)
        def _():
            acc_ref[...] += part

        @pl.when(kk == nk - 1)
        def _():
            r = acc_ref[...]
            if has_add:
                r = r + add_ref[...]
            o_ref[...] = r.astype(o_ref.dtype)

    ins = [a, b] + ([add] if has_add else [])
    specs = [a_spec, b_spec] + ([o_spec] if has_add else [])
    if dep is not None:
        ins.append(dep)
        specs.append(pl.BlockSpec((8, LANES), lambda i, j, kk: (0, 0)))
    return pl.pallas_call(
        body, name=name, grid=(m // tm, n // tn, nk), in_specs=specs, out_specs=o_spec,
        out_shape=jax.ShapeDtypeStruct((m, n), out_dtype), scratch_shapes=[pltpu.VMEM((tm, tn), F32)],
        compiler_params=_cparams("parallel", "parallel", "arbitrary"))(*ins)


def _rmsnorm_fwd(x, g, *, name):
    s, d = x.shape
    tm = _tile(s, 256)

    def body(x_ref, g_ref, h_ref, r_ref):
        xf = x_ref[...]
        r = lax.rsqrt(jnp.mean(xf * xf, axis=-1, keepdims=True) + RMS_EPS)
        h_ref[...] = ((xf * r) * g_ref[...]).astype(h_ref.dtype)
        r_ref[...] = r

    return pl.pallas_call(
        body, name=name, grid=(s // tm,),
        in_specs=[pl.BlockSpec((tm, d), lambda i: (i, 0)), pl.BlockSpec((1, d), lambda i: (0, 0))],
        out_specs=[pl.BlockSpec((tm, d), lambda i: (i, 0)), pl.BlockSpec((tm, 1), lambda i: (i, 0))],
        out_shape=[jax.ShapeDtypeStruct((s, d), BF16), jax.ShapeDtypeStruct((s, 1), F32)],
        compiler_params=_cparams("parallel"))(x, g.reshape(1, d))


def _rms_bwd_math(dh, x, r, g):
    xr = x * r
    dyg = dh * g
    m = jnp.mean(dyg * xr, axis=-1, keepdims=True)
    return r * (dyg - xr * m), dh * xr


def _rmsnorm_bwd(dh, x, r, g, dres, *, name):
    s, d = x.shape
    tm = _tile(s, 256)
    has_res = dres is not None

    def body(*refs):
        if has_res:
            dh_ref, x_ref, r_ref, g_ref, res_ref, dx_ref, dg_ref = refs
        else:
            dh_ref, x_ref, r_ref, g_ref, dx_ref, dg_ref = refs
        dx, dgp = _rms_bwd_math(dh_ref[...].astype(F32), x_ref[...], r_ref[...], g_ref[...])
        if has_res:
            dx = dx + res_ref[...]
        dx_ref[...] = dx

        @pl.when(pl.program_id(0) == 0)
        def _():
            dg_ref[...] = jnp.zeros_like(dg_ref)

        dg_ref[...] += jnp.sum(dgp, axis=0, keepdims=True)

    row = pl.BlockSpec((tm, d), lambda i: (i, 0))
    vec = pl.BlockSpec((1, d), lambda i: (0, 0))
    ins = [dh, x, r, g.reshape(1, d)] + ([dres] if has_res else [])
    specs = [row, row, pl.BlockSpec((tm, 1), lambda i: (i, 0)), vec] + ([row] if has_res else [])
    return pl.pallas_call(
        body, name=name, grid=(s // tm,), in_specs=specs, out_specs=[row, vec],
        out_shape=[jax.ShapeDtypeStruct((s, d), F32), jax.ShapeDtypeStruct((1, d), F32)],
        compiler_params=_cparams("arbitrary"))(*ins)


def _final_loss(x, g, target, *, name):
    s, d = x.shape
    tm = _tile(s, 256)

    def body(x_ref, g_ref, t_ref, loss_ref, dx_ref, dg_ref):
        xf = x_ref[...]
        gv = g_ref[...]
        r = lax.rsqrt(jnp.mean(xf * xf, axis=-1, keepdims=True) + RMS_EPS)
        diff = (xf * r) * gv - t_ref[...]
        part = 0.5 * jnp.sum(jnp.mean(diff * diff, axis=-1, keepdims=True))
        dx, dgp = _rms_bwd_math(diff * (1.0 / d), xf, r, gv)
        dx_ref[...] = dx

        @pl.when(pl.program_id(0) == 0)
        def _():
            dg_ref[...] = jnp.zeros_like(dg_ref)
            loss_ref[...] = jnp.zeros_like(loss_ref)

        dg_ref[...] += jnp.sum(dgp, axis=0, keepdims=True)
        loss_ref[...] += part

    row = pl.BlockSpec((tm, d), lambda i: (i, 0))
    vec = pl.BlockSpec((1, d), lambda i: (0, 0))
    return pl.pallas_call(
        body, name=name, grid=(s // tm,), in_specs=[row, vec, row],
        out_specs=[pl.BlockSpec((8, LANES), lambda i: (0, 0)), row, vec],
        out_shape=[jax.ShapeDtypeStruct((8, LANES), F32), jax.ShapeDtypeStruct((s, d), F32), jax.ShapeDtypeStruct((1, d), F32)],
        compiler_params=_cparams("arbitrary"))(x, g.reshape(1, d), target)


def _conv_taps(gp_ref, w, bias, taps, tm):
    halo = gp_ref.shape[0] - tm
    acc = jnp.broadcast_to(bias, (tm, w.shape[1]))
    for k in range(taps):
        acc = acc + w[k:k + 1, :] * gp_ref[pl.ds(halo - (taps - 1) + k, tm), :]
    return acc


def _conv_core(val, glu, valh, gluh, first, gp_ref, w, bias, lg, lb, taps, tm):
    sg = _sigmoid(glu)
    g = val * sg
    gh = jnp.where(first, 0.0, valh * _sigmoid(gluh))
    gp_ref[0:CONV_HALO, :] = gh
    gp_ref[CONV_HALO:, :] = g
    d = _conv_taps(gp_ref, w, bias, taps, tm)
    mu = jnp.mean(d, axis=-1, keepdims=True)
    dc = d - mu
    rstd = lax.rsqrt(jnp.mean(dc * dc, axis=-1, keepdims=True) + LN_EPS)
    xhat = dc * rstd
    ln = xhat * lg + lb
    return sg, xhat, rstd, ln


def _conv_fwd(ua, dw_w, dw_b, ln_g, ln_b, pw, *, name):
    s = ua.shape[0]
    taps, c = dw_w.shape
    tm = _tile(s, 512)
    hb = tm // CONV_HALO

    def body(val_ref, glu_ref, valh_ref, gluh_ref, w_ref, b_ref, lg_ref, lb_ref, pw_ref, y_ref, gp_ref):
        first = pl.program_id(0) == 0
        _, _, _, ln = _conv_core(val_ref[...], glu_ref[...], valh_ref[...], gluh_ref[...], first, gp_ref,
                                 w_ref[...], b_ref[...], lg_ref[...], lb_ref[...], taps, tm)
        sw = ln * _sigmoid(ln)
        y_ref[...] = _nn(sw.astype(BF16), pw_ref[...])

    cur = lambda col: pl.BlockSpec((tm, c), lambda i: (i, col))
    prev = lambda col: pl.BlockSpec((CONV_HALO, c), lambda i: (jnp.maximum(i * hb - 1, 0), col))
    full = lambda a: pl.BlockSpec(a.shape, lambda i: (0,) * a.ndim)
    vecs = [dw_w, dw_b.reshape(1, c), ln_g.reshape(1, c), ln_b.reshape(1, c), pw]
    return pl.pallas_call(
        body, name=name, grid=(s // tm,),
        in_specs=[cur(0), cur(1), prev(0), prev(1)] + [full(a) for a in vecs],
        out_specs=pl.BlockSpec((tm, c), lambda i: (i, 0)),
        out_shape=jax.ShapeDtypeStruct((s, c), F32),
        scratch_shapes=[pltpu.VMEM((tm + CONV_HALO, c), F32)],
        compiler_params=_cparams("parallel"))(ua, ua, ua, ua, *vecs)


def _conv_bwd_a(ua, dy, dw_w, dw_b, ln_g, ln_b, pw, *, name):
    s = ua.shape[0]
    taps, c = dw_w.shape
    tm = _tile(s, 512)
    hb = tm // CONV_HALO

    def body(val_ref, glu_ref, valh_ref, gluh_ref, dy_ref, w_ref, b_ref, lg_ref, lb_ref, pw_ref,
             dd_ref, dpw_ref, dlg_ref, dlb_ref, gp_ref):
        first = pl.program_id(0) == 0
        lg = lg_ref[...]
        _, xhat, rstd, ln = _conv_core(val_ref[...], glu_ref[...], valh_ref[...], gluh_ref[...], first, gp_ref,
                                       w_ref[...], b_ref[...], lg, lb_ref[...], taps, tm)
        sw, dsw = _silu_and_grad(ln)
        dyb = dy_ref[...].astype(BF16)
        ds = _nt(dyb, pw_ref[...])
        dln = ds * dsw
        dxhat = dln * lg
        m1 = jnp.mean(dxhat, axis=-1, keepdims=True)
        m2 = jnp.mean(dxhat * xhat, axis=-1, keepdims=True)
        dd_ref[...] = rstd * (dxhat - m1 - xhat * m2)

        @pl.when(first)
        def _():
            dpw_ref[...] = jnp.zeros_like(dpw_ref)
            dlg_ref[...] = jnp.zeros_like(dlg_ref)
            dlb_ref[...] = jnp.zeros_like(dlb_ref)

        dpw_ref[...] += _tn(sw.astype(BF16), dyb)
        dlg_ref[...] += jnp.sum(dln * xhat, axis=0, keepdims=True)
        dlb_ref[...] += jnp.sum(dln, axis=0, keepdims=True)

    cur = lambda col: pl.BlockSpec((tm, c), lambda i: (i, col))
    prev = lambda col: pl.BlockSpec((CONV_HALO, c), lambda i: (jnp.maximum(i * hb - 1, 0), col))
    full = lambda a: pl.BlockSpec(a.shape, lambda i: (0,) * a.ndim)
    vec = pl.BlockSpec((1, c), lambda i: (0, 0))
    vecs = [dw_w, dw_b.reshape(1, c), ln_g.reshape(1, c), ln_b.reshape(1, c), pw]
    return pl.pallas_call(
        body, name=name, grid=(s // tm,),
        in_specs=[cur(0), cur(1), prev(0), prev(1), pl.BlockSpec((tm, c), lambda i: (i, 0))] + [full(a) for a in vecs],
        out_specs=[pl.BlockSpec((tm, c), lambda i: (i, 0)), pl.BlockSpec((c, c), lambda i: (0, 0)), vec, vec],
        out_shape=[jax.ShapeDtypeStruct((s, c), F32), jax.ShapeDtypeStruct((c, c), F32),
                   jax.ShapeDtypeStruct((1, c), F32), jax.ShapeDtypeStruct((1, c), F32)],
        scratch_shapes=[pltpu.VMEM((tm + CONV_HALO, c), F32)],
        compiler_params=_cparams("arbitrary"))(ua, ua, ua, ua, dy, *vecs)


def _conv_bwd_b(ua, dd, dw_w, *, name):
    s = ua.shape[0]
    taps, c = dw_w.shape
    tm = _tile(s, 512)
    hb = tm // CONV_HALO
    nt = s // tm

    def body(val_ref, glu_ref, valh_ref, gluh_ref, dd_ref, ddn_ref, w_ref, dval_ref, dglu_ref, dw_ref, db_ref, gp_ref, ddp_ref):
        i = pl.program_id(0)
        val = val_ref[...]
        sg = _sigmoid(glu_ref[...])
        gp_ref[0:CONV_HALO, :] = jnp.where(i == 0, 0.0, valh_ref[...] * _sigmoid(gluh_ref[...]))
        gp_ref[CONV_HALO:, :] = val * sg
        dd = dd_ref[...]
        ddp_ref[0:tm, :] = dd
        ddp_ref[tm:, :] = jnp.where(i == nt - 1, 0.0, ddn_ref[...])
        w = w_ref[...]
        dg = jnp.zeros((tm, c), F32)
        dws = []
        for k in range(taps):
            dg = dg + w[k:k + 1, :] * ddp_ref[pl.ds(taps - 1 - k, tm), :]
            dws.append(jnp.sum(dd * gp_ref[pl.ds(CONV_HALO - (taps - 1) + k, tm), :], axis=0, keepdims=True))
        dval_ref[...] = (dg * sg).astype(dval_ref.dtype)
        dglu_ref[...] = (dg * val * sg * (1.0 - sg)).astype(dglu_ref.dtype)

        @pl.when(i == 0)
        def _():
            dw_ref[...] = jnp.zeros_like(dw_ref)
            db_ref[...] = jnp.zeros_like(db_ref)

        dw_ref[...] += jnp.concatenate(dws, axis=0)
        db_ref[...] += jnp.sum(dd, axis=0, keepdims=True)

    cur = lambda col: pl.BlockSpec((tm, c), lambda i: (i, col))
    prev = lambda col: pl.BlockSpec((CONV_HALO, c), lambda i: (jnp.maximum(i * hb - 1, 0), col))
    nxt = pl.BlockSpec((CONV_HALO, c), lambda i: (jnp.minimum((i + 1) * hb, s // CONV_HALO - 1), 0))
    return pl.pallas_call(
        body, name=name, grid=(nt,),
        in_specs=[cur(0), cur(1), prev(0), prev(1), pl.BlockSpec((tm, c), lambda i: (i, 0)), nxt,
                  pl.BlockSpec((taps, c), lambda i: (0, 0))],
        out_specs=[pl.BlockSpec((tm, c), lambda i: (i, 0)), pl.BlockSpec((tm, c), lambda i: (i, 0)),
                   pl.BlockSpec((taps, c), lambda i: (0, 0)), pl.BlockSpec((1, c), lambda i: (0, 0))],
        out_shape=[jax.ShapeDtypeStruct((s, c), BF16), jax.ShapeDtypeStruct((s, c), BF16),
                   jax.ShapeDtypeStruct((taps, c), F32), jax.ShapeDtypeStruct((1, c), F32)],
        scratch_shapes=[pltpu.VMEM((tm + CONV_HALO, c), F32), pltpu.VMEM((tm + CONV_HALO, c), F32)],
        compiler_params=_cparams("arbitrary"))(ua, ua, ua, ua, dd, dd, dw_w)


LOG2_E = 1.4426950408889634
SB_HEADS_PER_STEP = 4
SB_ROWS = 256


def _sb_scores(q, kb, mask):
    return _sb_logs(_nt(q, kb), mask)


def _sb_logs(qk, mask):
    z = qk * (HEAD_DIM ** -0.5 * LOG2_E)
    ls = jnp.minimum(z, 0.0) - jnp.log2(1.0 + jnp.exp2(-jnp.abs(z)))
    lm = ls - z
    if mask is not None:
        lm = jnp.where(mask, lm, 0.0)
    return ls, lm


def _diag_mask(b):
    return lax.broadcasted_iota(jnp.int32, (b, b), 1) < lax.broadcasted_iota(jnp.int32, (b, b), 0)


def _split_dot(x, tri):
    hi = x.astype(BF16)
    lo = (x - hi.astype(F32)).astype(BF16)
    return _nn(hi, tri) + _nn(lo, tri)


def _tri(bk, cmp):
    r = lax.broadcasted_iota(jnp.int32, (bk, bk), 0)
    c = lax.broadcasted_iota(jnp.int32, (bk, bk), 1)
    return cmp(r, c).astype(BF16)


def _sb_fwd(qkv, heads, *, name, blk=256):
    s = qkv.shape[0]
    b = _tile(s, blk)
    nq = s // b
    hp = min(SB_HEADS_PER_STEP, heads)
    assert heads % hp == 0
    groups = heads // hp
    wide = hp * HEAD_DIM

    rs = min(b, SB_ROWS)
    chains = [(n, r) for n in range(hp) for r in range(b // rs)]

    def body(q_ref, k_ref, v_ref, o_ref, tot_ref):
        i = pl.program_id(1)
        qs = [q_ref[r * rs:(r + 1) * rs, n * HEAD_DIM:(n + 1) * HEAD_DIM] for n, r in chains]
        tri = _tri(b, lambda r, c: r > c)
        diag = _diag_mask(b)

        def tile(j, carry, masked):
            s0 = pl.multiple_of(j * b, b)
            kbs = [k_ref[pl.ds(s0, b), n * HEAD_DIM:(n + 1) * HEAD_DIM] for n in range(hp)]
            vbs = [v_ref[pl.ds(s0, b), n * HEAD_DIM:(n + 1) * HEAD_DIM] for n in range(hp)]
            masks = [diag[r * rs:(r + 1) * rs] if masked else None for _, r in chains]
            zs = [_nt(q, kbs[n]) for (n, _), q in zip(chains, qs)]
            sc = [_sb_logs(z, mask) for z, mask in zip(zs, masks)]
            after = [_split_dot(lm, tri) for _, lm in sc]
            out = []
            for (n, _), mask, (ls, lm), af, (acc, c) in zip(chains, masks, sc, after, carry):
                a = jnp.exp2(ls + (af + c))
                if masked:
                    a = jnp.where(mask, a, 0.0)
                out.append((a.astype(BF16), acc, c + jnp.sum(lm, axis=1, keepdims=True)))
            return tuple((acc + _nn(a, vbs[n]), c) for (n, _), (a, acc, c) in zip(chains, out))

        zero = tuple((jnp.zeros((rs, HEAD_DIM), F32), jnp.zeros((rs, 1), F32)) for _ in chains)
        carry = tile(i, zero, True)
        carry = lax.fori_loop(0, i, lambda jj, cr: tile(i - 1 - jj, cr, False), carry)
        for (n, r), (acc, c) in zip(chains, carry):
            o_ref[r * rs:(r + 1) * rs, n * HEAD_DIM:(n + 1) * HEAD_DIM] = acc
            tot_ref[n, r * rs:(r + 1) * rs, :] = c

    return pl.pallas_call(
        body, name=name, grid=(groups, nq),
        in_specs=[pl.BlockSpec((b, wide), lambda g, i: (i, g)),
                  pl.BlockSpec((s, wide), lambda g, i: (0, groups + g)),
                  pl.BlockSpec((s, wide), lambda g, i: (0, 2 * groups + g))],
        out_specs=[pl.BlockSpec((b, wide), lambda g, i: (i, g)),
                   pl.BlockSpec((hp, b, 1), lambda g, i: (g, i, 0))],
        out_shape=[jax.ShapeDtypeStruct((s, heads * HEAD_DIM), F32), jax.ShapeDtypeStruct((heads, s, 1), F32)],
        compiler_params=_cparams("parallel", "arbitrary"))(qkv, qkv, qkv)


def _sb_bwd(qkv, do, tot, heads, *, name, blk=256):
    s = qkv.shape[0]
    b = _tile(s, blk)
    nq = s // b
    hp = min(SB_HEADS_PER_STEP, heads)
    assert heads % hp == 0
    groups = heads // hp
    wide = hp * HEAD_DIM
    scale = HEAD_DIM ** -0.5
    rs = min(b, SB_ROWS)
    chains = [(n, r) for n in range(hp) for r in range(b // rs)]

    def body(q_ref, k_ref, v_ref, do_ref, tot_ref, dq_ref, dk_ref, dv_ref, dk_acc, dv_acc):
        i = pl.program_id(1)

        @pl.when(i == 0)
        def _():
            dk_acc[...] = jnp.zeros_like(dk_acc)
            dv_acc[...] = jnp.zeros_like(dv_acc)

        sls = [slice(n * HEAD_DIM, (n + 1) * HEAD_DIM) for n in range(hp)]
        rws = [slice(r * rs, (r + 1) * rs) for r in range(b // rs)]
        q_full = [q_ref[:, sl] for sl in sls]
        do_full = [do_ref[:, sl].astype(BF16) for sl in sls]
        qs = [q_full[n][rws[r]] for n, r in chains]
        dos = [do_full[n][rws[r]] for n, r in chains]
        tots = [tot_ref[n, rws[r], :] for n, r in chains]
        tri_incl = _tri(b, lambda r, c: r <= c)
        tri_excl = _tri(b, lambda r, c: r < c)
        diag = _diag_mask(b)

        def tile(j, carry, masked):
            s0 = pl.multiple_of(j * b, b)
            kbs = [k_ref[pl.ds(s0, b), sl] for sl in sls]
            vbs = [v_ref[pl.ds(s0, b), sl] for sl in sls]
            masks = [diag[rws[r]] if masked else None for _, r in chains]
            zs = [_nt(q, kbs[n]) for (n, _), q in zip(chains, qs)]
            ps = [_nt(dob, vbs[n]) for (n, _), dob in zip(chains, dos)]
            sc = [_sb_logs(z, mask) for z, mask in zip(zs, masks)]
            cums = [_split_dot(lm, tri_incl) for _, lm in sc]
            ab, gs = [], []
            for mask, (ls, _), cum, p, tot_q, (_, cl, _) in zip(masks, sc, cums, ps, tots, carry):
                a = jnp.exp2(ls + (tot_q - (cum + cl)))
                if masked:
                    a = jnp.where(mask, a, 0.0)
                ab.append(a.astype(BF16))
                gs.append(a * p)
            hs = [_nn(g.astype(BF16), tri_excl) for g in gs]
            dzb = []
            for mask, (ls, _), g, h, (_, _, cg) in zip(masks, sc, gs, hs, carry):
                dz = (g - (g + (h + cg)) * jnp.exp2(ls)) * scale
                if masked:
                    dz = jnp.where(mask, dz, 0.0)
                dzb.append(dz.astype(BF16))
            out = tuple((dq + _nn(dz, kbs[n]), cl + jnp.sum(lm, axis=1, keepdims=True), cg + jnp.sum(g, axis=1, keepdims=True))
                        for (n, _), dz, (_, lm), g, (dq, cl, cg) in zip(chains, dzb, sc, gs, carry))
            per = b // rs
            for n, sl in enumerate(sls):
                dz_all = jnp.concatenate(dzb[n * per:(n + 1) * per], axis=0) if per > 1 else dzb[n * per]
                a_all = jnp.concatenate(ab[n * per:(n + 1) * per], axis=0) if per > 1 else ab[n * per]
                dk_acc[pl.ds(s0, b), sl] += _tn(dz_all, q_full[n])
                dv_acc[pl.ds(s0, b), sl] += _tn(a_all, do_full[n])
            return out

        zero = jnp.zeros((rs, 1), F32)
        carry = tuple((jnp.zeros((rs, HEAD_DIM), F32), zero, zero) for _ in chains)
        carry = lax.fori_loop(0, i, lambda j, cr: tile(j, cr, False), carry)
        carry = tile(i, carry, True)
        for (n, r), (dq, _, _) in zip(chains, carry):
            dq_ref[rws[r], sls[n]] = dq.astype(dq_ref.dtype)

        @pl.when(i == nq - 1)
        def _():
            dk_ref[...] = dk_acc[...].astype(dk_ref.dtype)
            dv_ref[...] = dv_acc[...].astype(dv_ref.dtype)

    row = pl.BlockSpec((b, wide), lambda g, i: (i, g))
    col = lambda off: pl.BlockSpec((s, wide), lambda g, i: (0, off + g), pipeline_mode=pl.Buffered(1))
    shp = jax.ShapeDtypeStruct((s, heads * HEAD_DIM), BF16)
    return pl.pallas_call(
        body, name=name, grid=(groups, nq),
        in_specs=[row, col(groups), col(2 * groups), row, pl.BlockSpec((hp, b, 1), lambda g, i: (g, i, 0))],
        out_specs=[row, col(0), col(0)], out_shape=[shp, shp, shp],
        scratch_shapes=[pltpu.VMEM((s, wide), F32), pltpu.VMEM((s, wide), F32)],
        compiler_params=_cparams("parallel", "arbitrary"))(qkv, qkv, qkv, do, tot)


def _shift_rows(x, n, fill, *, down):
    rows = x.shape[0]
    if n % 8 == 0:
        pad = jnp.full((n, x.shape[1]), fill, x.dtype)
        return jnp.concatenate([pad, x[:rows - n]], axis=0) if down else jnp.concatenate([x[n:], pad], axis=0)
    t = lax.broadcasted_iota(jnp.int32, x.shape, 0)
    if down:
        return jnp.where(t >= n, pltpu.roll(x, n, 0), fill)
    return jnp.where(t < rows - n, pltpu.roll(x, rows - n, 0), fill)


def _scan_rows(a, b, *, reverse):
    n = 1
    while n < a.shape[0]:
        b = a * _shift_rows(b, n, 0.0, down=not reverse) + b
        a = a * _shift_rows(a, n, 1.0, down=not reverse)
        n *= 2
    return a, b


def _neg_expm1(x):
    p = 1.0 + x * (1.0 / 7.0)
    for k in (6.0, 5.0, 4.0, 3.0, 2.0):
        p = 1.0 + x * (1.0 / k) * p
    return jnp.where(x > -0.25, -(x * p), 1.0 - jnp.exp(x))


def _softplus_neg(lam):
    z = -lam
    e = jnp.exp(-jnp.abs(z))
    u = 1.0 + e
    d = u - 1.0
    log1p_e = jnp.where(d == 0.0, e, jnp.log(u) * (e / jnp.where(d == 0.0, 1.0, d)))
    return jnp.maximum(z, 0.0) + log1p_e


def _lru_gates(xp_ref, w, bias, wa_ref, ba, wx_ref, bx, sp, taps, tm, heads):
    halo = xp_ref.shape[0] - tm
    xc = jnp.broadcast_to(bias, (tm, w.shape[1]))
    for k in range(taps):
        xc = xc + w[k:k + 1, :] * xp_ref[pl.ds(halo - (taps - 1) + k, tm), :]
    xb = xc.astype(BF16)
    pr, pi = [], []
    for n in range(heads):
        xh = xb[:, n * HEAD_DIM:(n + 1) * HEAD_DIM]
        pr.append(_nn(xh, wa_ref[n]))
        pi.append(_nn(xh, wx_ref[n]))
    r = _sigmoid(jnp.concatenate(pr, axis=1) + ba)
    ig = _sigmoid(jnp.concatenate(pi, axis=1) + bx)
    log_a = (-LRU_C) * r * sp
    a = jnp.exp(log_a)
    mult = jnp.sqrt(_neg_expm1(2.0 * log_a))
    return xc, r, ig, a, mult


def _lru_fwd(ub, x_col, conv_w, conv_b, wa, ba, wx, bx, lam, *, name):
    s = ub.shape[0]
    taps, w = conv_w.shape
    heads = w // HEAD_DIM
    tm = _tile(s, 256)
    hb = tm // LRU_HALO

    def body(x_ref, xh_ref, cw_ref, cb_ref, wa_ref, ba_ref, wx_ref, bx_ref, lam_ref, h_ref, xp_ref, carry_ref):
        i = pl.program_id(0)

        @pl.when(i == 0)
        def _():
            carry_ref[...] = jnp.zeros_like(carry_ref)

        xp_ref[0:LRU_HALO, :] = jnp.where(i == 0, 0.0, xh_ref[...])
        xp_ref[LRU_HALO:, :] = x_ref[...]
        sp = _softplus_neg(lam_ref[...])
        xc, _, ig, a, mult = _lru_gates(xp_ref, cw_ref[...], cb_ref[...], wa_ref, ba_ref[...], wx_ref, bx_ref[...],
                                        sp, taps, tm, heads)
        ac, bc = _scan_rows(a, mult * (ig * xc), reverse=False)
        h = ac * carry_ref[0:1, :] + bc
        h_ref[...] = h
        carry_ref[...] = jnp.broadcast_to(h[tm - 1:tm, :], carry_ref.shape)

    full = lambda arr: pl.BlockSpec(arr.shape, lambda i: (0,) * arr.ndim)
    vecs = [conv_w, conv_b.reshape(1, w), wa.astype(BF16), ba.reshape(1, w), wx.astype(BF16), bx.reshape(1, w), lam.reshape(1, w)]
    return pl.pallas_call(
        body, name=name, grid=(s // tm,),
        in_specs=[pl.BlockSpec((tm, w), lambda i: (i, x_col)),
                  pl.BlockSpec((LRU_HALO, w), lambda i: (jnp.maximum(i * hb - 1, 0), x_col))] + [full(v) for v in vecs],
        out_specs=pl.BlockSpec((tm, w), lambda i: (i, 0)),
        out_shape=jax.ShapeDtypeStruct((s, w), F32),
        scratch_shapes=[pltpu.VMEM((tm + LRU_HALO, w), F32), pltpu.VMEM((8, w), F32)],
        compiler_params=_cparams("arbitrary"))(ub, ub, *vecs)


def _lru_bwd(ub, x_col, h, dh, conv_w, conv_b, wa, ba, wx, bx, lam, *, name):
    s = ub.shape[0]
    taps, w = conv_w.shape
    heads = w // HEAD_DIM
    tm = _tile(s, 256)
    hb = tm // LRU_HALO
    nt = s // tm

    def body(x_ref, xh_ref, h_ref, hh_ref, dh_ref, cw_ref, cb_ref, wa_ref, ba_ref, wx_ref, bx_ref, lam_ref,
             dx_ref, dcw_ref, dcb_ref, dwa_ref, dba_ref, dwx_ref, dbx_ref, dlam_ref,
             xp_ref, dxp_ref, dlt_ref, afirst_ref, dxc_next_ref, dsp_ref):
        step = pl.program_id(0)
        i = nt - 1 - step

        @pl.when(step == 0)
        def _():
            for ref in (dcw_ref, dcb_ref, dwa_ref, dba_ref, dwx_ref, dbx_ref, dlam_ref, dlt_ref, dxc_next_ref, dsp_ref):
                ref[...] = jnp.zeros_like(ref)
            afirst_ref[...] = jnp.ones_like(afirst_ref)

        xp_ref[0:LRU_HALO, :] = jnp.where(i == 0, 0.0, xh_ref[...])
        xp_ref[LRU_HALO:, :] = x_ref[...]
        cw = cw_ref[...]
        lam_v = lam_ref[...]
        sp = _softplus_neg(lam_v)
        xc, r, ig, a, mult = _lru_gates(xp_ref, cw, cb_ref[...], wa_ref, ba_ref[...], wx_ref, bx_ref[...], sp, taps, tm, heads)
        rows = lax.broadcasted_iota(jnp.int32, (tm, w), 0)
        a_next = jnp.where(rows == tm - 1, afirst_ref[0:1, :], _shift_rows(a, 1, 1.0, down=False))
        ac, bc = _scan_rows(a_next, dh_ref[...], reverse=True)
        delta = ac * dlt_ref[0:1, :] + bc
        hv = h_ref[...]
        h_last_prev = jnp.where(i == 0, 0.0, hh_ref[LRU_HALO - 1:LRU_HALO, :])
        h_prev = jnp.where(rows == 0, h_last_prev, _shift_rows(hv, 1, 0.0, down=True))
        gated = ig * xc
        da = delta * h_prev
        dmult = delta * gated
        dgated = delta * mult
        dlog_a = da * a - dmult * (a * a) / mult
        dpr = dlog_a * ((-LRU_C) * sp) * r * (1.0 - r)
        dpi = dgated * xc * ig * (1.0 - ig)
        dxc = dgated * ig
        dsp_ref[...] += jnp.sum(dlog_a * ((-LRU_C) * r), axis=0, keepdims=True)
        dba_ref[...] += jnp.sum(dpr, axis=0, keepdims=True)
        dbx_ref[...] += jnp.sum(dpi, axis=0, keepdims=True)
        xb = xc.astype(BF16)
        dprb = dpr.astype(BF16)
        dpib = dpi.astype(BF16)
        back = []
        for n in range(heads):
            sl = slice(n * HEAD_DIM, (n + 1) * HEAD_DIM)
            dwa_ref[n] += _tn(xb[:, sl], dprb[:, sl])
            dwx_ref[n] += _tn(xb[:, sl], dpib[:, sl])
            back.append(_nt(dprb[:, sl], wa_ref[n]) + _nt(dpib[:, sl], wx_ref[n]))
        dxc = dxc + jnp.concatenate(back, axis=1)
        dxp_ref[0:tm, :] = dxc
        dxp_ref[tm:, :] = dxc_next_ref[...]
        dx = jnp.zeros((tm, w), F32)
        dws = []
        for k in range(taps):
            dx = dx + cw[k:k + 1, :] * dxp_ref[pl.ds(taps - 1 - k, tm), :]
            dws.append(jnp.sum(dxc * xp_ref[pl.ds(LRU_HALO - (taps - 1) + k, tm), :], axis=0, keepdims=True))
        dx_ref[...] = dx.astype(dx_ref.dtype)
        dcw_ref[...] += jnp.concatenate(dws, axis=0)
        dcb_ref[...] += jnp.sum(dxc, axis=0, keepdims=True)
        dlt_ref[...] = jnp.broadcast_to(delta[0:1, :], dlt_ref.shape)
        afirst_ref[...] = jnp.broadcast_to(a[0:1, :], afirst_ref.shape)
        dxc_next_ref[...] = dxc[0:LRU_HALO, :]

        @pl.when(step == nt - 1)
        def _():
            dlam_ref[...] = dsp_ref[...] * (-_sigmoid(-lam_v))

    rev = lambda col: pl.BlockSpec((tm, w), lambda st: (nt - 1 - st, col))
    prev = lambda col: pl.BlockSpec((LRU_HALO, w), lambda st: (jnp.maximum((nt - 1 - st) * hb - 1, 0), col))
    full = lambda arr: pl.BlockSpec(arr.shape, lambda st: (0,) * arr.ndim)
    vec = pl.BlockSpec((1, w), lambda st: (0, 0))
    vecs = [conv_w, conv_b.reshape(1, w), wa.astype(BF16), ba.reshape(1, w), wx.astype(BF16), bx.reshape(1, w), lam.reshape(1, w)]
    vshape = jax.ShapeDtypeStruct((1, w), F32)
    return pl.pallas_call(
        body, name=name, grid=(nt,),
        in_specs=[rev(x_col), prev(x_col), rev(0), prev(0), rev(0)] + [full(v) for v in vecs],
        out_specs=[rev(0), full(conv_w), vec, full(wa), vec, full(wx), vec, vec],
        out_shape=[jax.ShapeDtypeStruct((s, w), BF16), jax.ShapeDtypeStruct(conv_w.shape, F32), vshape,
                   jax.ShapeDtypeStruct(wa.shape, F32), vshape, jax.ShapeDtypeStruct(wx.shape, F32), vshape, vshape],
        scratch_shapes=[pltpu.VMEM((tm + LRU_HALO, w), F32), pltpu.VMEM((tm + LRU_HALO, w), F32),
                        pltpu.VMEM((8, w), F32), pltpu.VMEM((8, w), F32), pltpu.VMEM((LRU_HALO, w), F32), pltpu.VMEM((1, w), F32)],
        compiler_params=_cparams("arbitrary"))(ub, ub, h, h, dh, *vecs)


def _group_fwd(y, w, gate):
    r = lax.rsqrt(jnp.mean(y * y, axis=-1, keepdims=True) + RMS_EPS)
    return ((y * r) * w) * (gate * _sigmoid(gate))


def _mix_out_fwd(y_conv, y_attn, y_lru, ua, ub, n_conv, n_attn, n_lru, *, name):
    s, c = y_conv.shape
    wa_ = y_attn.shape[1]
    d = 2 * c + wa_
    tm = _tile(s, 256)
    assert wa_ == 2 * c

    def body(yc_ref, ya_ref, yl_ref, gc_ref, ga_ref, gl_ref, nc_ref, na_ref, nl_ref, o_ref):
        o_ref[:, 0:c] = _group_fwd(yc_ref[...], nc_ref[...], gc_ref[...]).astype(o_ref.dtype)
        o_ref[:, c:c + wa_] = _group_fwd(ya_ref[...], na_ref[...], ga_ref[...]).astype(o_ref.dtype)
        o_ref[:, c + wa_:] = _group_fwd(yl_ref[...], nl_ref[...], gl_ref[...]).astype(o_ref.dtype)

    blk = lambda width, col: pl.BlockSpec((tm, width), lambda i: (i, col))
    vec = lambda width: pl.BlockSpec((1, width), lambda i: (0, 0))
    return pl.pallas_call(
        body, name=name, grid=(s // tm,),
        in_specs=[blk(c, 0), blk(wa_, 0), blk(c, 0), blk(c, 2), blk(wa_, 0), blk(c, 3), vec(c), vec(wa_), vec(c)],
        out_specs=blk(d, 0), out_shape=jax.ShapeDtypeStruct((s, d), BF16),
        compiler_params=_cparams("parallel"))(y_conv, y_attn, y_lru, ua, ub, ub, n_conv.reshape(1, c), n_attn.reshape(1, wa_), n_lru.reshape(1, c))


def _group_bwd(dout, y, w, gate):
    r = lax.rsqrt(jnp.mean(y * y, axis=-1, keepdims=True) + RMS_EPS)
    silu, dsilu = _silu_and_grad(gate)
    dy, dwp = _rms_bwd_math(dout * silu, y, r, w)
    return dy, dout * ((y * r) * w) * dsilu, dwp


def _mix_out_bwd(dy, y_conv, y_attn, y_lru, ua, ub, n_conv, n_attn, n_lru, *, name):
    s, c = y_conv.shape
    wa_ = y_attn.shape[1]
    tm = _tile(s, 256)

    def body(dy_ref, yc_ref, ya_ref, yl_ref, gc_ref, ga_ref, gl_ref, nc_ref, na_ref, nl_ref,
             dyc_ref, dya_ref, dyl_ref, dgc_ref, dga_ref, dgl_ref, dnc_ref, dna_ref, dnl_ref):
        @pl.when(pl.program_id(0) == 0)
        def _():
            for ref in (dnc_ref, dna_ref, dnl_ref):
                ref[...] = jnp.zeros_like(ref)

        groups = ((dy_ref[:, 0:c], yc_ref, nc_ref, gc_ref, dyc_ref, dgc_ref, dnc_ref),
                  (dy_ref[:, c:c + wa_], ya_ref, na_ref, ga_ref, dya_ref, dga_ref, dna_ref),
                  (dy_ref[:, c + wa_:], yl_ref, nl_ref, gl_ref, dyl_ref, dgl_ref, dnl_ref))
        for dout, y_ref, n_ref, g_ref, dyo_ref, dgo_ref, dn_ref in groups:
            dyv, dgv, dwp = _group_bwd(dout, y_ref[...], n_ref[...], g_ref[...])
            dyo_ref[...] = dyv
            dgo_ref[...] = dgv.astype(dgo_ref.dtype)
            dn_ref[...] += jnp.sum(dwp, axis=0, keepdims=True)

    blk = lambda width, col: pl.BlockSpec((tm, width), lambda i: (i, col))
    vec = lambda width: pl.BlockSpec((1, width), lambda i: (0, 0))
    sh = lambda width, dt: jax.ShapeDtypeStruct((s, width), dt)
    vs = lambda width: jax.ShapeDtypeStruct((1, width), F32)
    return pl.pallas_call(
        body, name=name, grid=(s // tm,),
        in_specs=[blk(2 * c + wa_, 0), blk(c, 0), blk(wa_, 0), blk(c, 0), blk(c, 2), blk(wa_, 0), blk(c, 3), vec(c), vec(wa_), vec(c)],
        out_specs=[blk(c, 0), blk(wa_, 0), blk(c, 0), blk(c, 0), blk(wa_, 0), blk(c, 0), vec(c), vec(wa_), vec(c)],
        out_shape=[sh(c, F32), sh(wa_, F32), sh(c, F32), sh(c, BF16), sh(wa_, BF16), sh(c, BF16), vs(c), vs(wa_), vs(c)],
        compiler_params=_cparams("arbitrary"))(dy, y_conv, y_attn, y_lru, ua, ub, ub, n_conv.reshape(1, c), n_attn.reshape(1, wa_), n_lru.reshape(1, c))


def _xattn_probs(qh, kh):
    sc = _nt(qh, kh) * (HEAD_DIM ** -0.5)
    e = jnp.exp(sc - jnp.max(sc, axis=-1, keepdims=True))
    return e / jnp.sum(e, axis=-1, keepdims=True)


def _xattn_fwd(q, kv, *, name):
    s, w = q.shape
    heads = w // HEAD_DIM
    tm = _tile(s, 512)

    def body(q_ref, kv_ref, o_ref):
        for n in range(heads):
            sl = slice(n * HEAD_DIM, (n + 1) * HEAD_DIM)
            p = _xattn_probs(q_ref[:, sl], kv_ref[:, sl])
            o_ref[:, sl] = _nn(p.astype(BF16), kv_ref[:, w + n * HEAD_DIM:w + (n + 1) * HEAD_DIM]).astype(o_ref.dtype)

    return pl.pallas_call(
        body, name=name, grid=(s // tm,),
        in_specs=[pl.BlockSpec((tm, w), lambda i: (i, 0)), pl.BlockSpec(kv.shape, lambda i: (0, 0))],
        out_specs=pl.BlockSpec((tm, w), lambda i: (i, 0)), out_shape=jax.ShapeDtypeStruct((s, w), BF16),
        compiler_params=_cparams("parallel"))(q, kv)


def _xattn_bwd(q, kv, do, *, name):
    s, w = q.shape
    heads = w // HEAD_DIM
    tm = _tile(s, 512)
    scale = HEAD_DIM ** -0.5

    def body(q_ref, kv_ref, do_ref, dq_ref, dkv_ref):
        @pl.when(pl.program_id(0) == 0)
        def _():
            dkv_ref[...] = jnp.zeros_like(dkv_ref)

        for n in range(heads):
            sl = slice(n * HEAD_DIM, (n + 1) * HEAD_DIM)
            vsl = slice(w + n * HEAD_DIM, w + (n + 1) * HEAD_DIM)
            qh, kh, vh, doh = q_ref[:, sl], kv_ref[:, sl], kv_ref[:, vsl], do_ref[:, sl]
            p = _xattn_probs(qh, kh)
            dp = _nt(doh, vh)
            ds = (p * (dp - jnp.sum(dp * p, axis=-1, keepdims=True)) * scale).astype(BF16)
            dq_ref[:, sl] = _nn(ds, kh).astype(dq_ref.dtype)
            dkv_ref[:, sl] += _tn(ds, qh)
            dkv_ref[:, vsl] += _tn(p.astype(BF16), doh)

    row = pl.BlockSpec((tm, w), lambda i: (i, 0))
    kvs = pl.BlockSpec(kv.shape, lambda i: (0, 0))
    return pl.pallas_call(
        body, name=name, grid=(s // tm,), in_specs=[row, kvs, row], out_specs=[row, kvs],
        out_shape=[jax.ShapeDtypeStruct((s, w), BF16), jax.ShapeDtypeStruct(kv.shape, F32)],
        compiler_params=_cparams("arbitrary"))(q, kv, do)


ROW_BLOCK_BYTES = 1 << 20


def _row_tile(rows, cols):
    limit = max(8, ROW_BLOCK_BYTES // (4 * cols))
    t = 8
    while t * 2 <= limit and rows % (t * 2) == 0:
        t *= 2
    assert rows % t == 0
    return t


def _cast_bf16(w, *, name):
    rows, cols = w.shape
    tr = _row_tile(rows, cols)

    def body(w_ref, o_ref):
        o_ref[...] = w_ref[...].astype(BF16)

    blk = pl.BlockSpec((tr, cols), lambda i: (i, 0))
    return pl.pallas_call(body, name=name, grid=(rows // tr,), in_specs=[blk], out_specs=blk,
                          out_shape=jax.ShapeDtypeStruct((rows, cols), BF16), compiler_params=_cparams("parallel"))(w)


def _sum_slots(land, *, name):
    slots, rows, cols = land.shape
    tr = _row_tile(rows, cols * slots)

    def body(l_ref, o_ref):
        acc = l_ref[0].astype(F32)
        for j in range(1, slots):
            acc = acc + l_ref[j].astype(F32)
        o_ref[...] = acc

    return pl.pallas_call(
        body, name=name, grid=(rows // tr,), in_specs=[pl.BlockSpec((slots, tr, cols), lambda i: (0, i, 0))],
        out_specs=pl.BlockSpec((tr, cols), lambda i: (i, 0)), out_shape=jax.ShapeDtypeStruct((rows, cols), F32),
        compiler_params=_cparams("parallel"))(land)


def _adamw(w, m, v, gs, *, name):
    rows, cols = w.shape
    tr = _row_tile(rows, cols * 4)
    ng = len(gs)

    def body(*refs):
        w_ref, m_ref, v_ref = refs[:3]
        g_refs = refs[3:3 + ng]
        g_out, d_out, m_out, v_out = refs[3 + ng:]
        g = g_refs[0][...]
        for r in g_refs[1:]:
            g = g + r[...]
        mn = ADAM_B1 * m_ref[...] + (1.0 - ADAM_B1) * g
        vn = ADAM_B2 * v_ref[...] + (1.0 - ADAM_B2) * (g * g)
        m_hat = mn / (1.0 - ADAM_B1 ** ADAM_STEP)
        v_hat = vn / (1.0 - ADAM_B2 ** ADAM_STEP)
        g_out[...] = g
        d_out[...] = -ADAM_LR * (m_hat / (jnp.sqrt(v_hat) + ADAM_EPS) + ADAM_WD * w_ref[...])
        m_out[...] = mn
        v_out[...] = vn

    blk = pl.BlockSpec((tr, cols), lambda i: (i, 0))
    shp = jax.ShapeDtypeStruct((rows, cols), F32)
    return pl.pallas_call(body, name=name, grid=(rows // tr,), in_specs=[blk] * (3 + ng), out_specs=[blk] * 4,
                          out_shape=[shp] * 4, compiler_params=_cparams("parallel"))(w, m, v, *gs)


def _adamw_layers(w, m, v, gs, *, name):
    layers, rows, cols = w.shape
    tr = _row_tile(rows, cols * 4)
    nblk = rows // tr
    ng = len(gs[0])

    def body(*refs):
        w_ref, m_ref, v_ref = refs[:3]
        g_refs = refs[3:3 + layers * ng]
        g_out, d_out, m_out, v_out = refs[3 + layers * ng:]
        l = pl.program_id(0)
        g = jnp.zeros((tr, cols), F32)
        for ll in range(layers):
            gl = g_refs[ll * ng][...]
            for r in g_refs[ll * ng + 1:(ll + 1) * ng]:
                gl = gl + r[...]
            g = jnp.where(l == ll, gl, g)
        mn = ADAM_B1 * m_ref[...] + (1.0 - ADAM_B1) * g
        vn = ADAM_B2 * v_ref[...] + (1.0 - ADAM_B2) * (g * g)
        m_hat = mn / (1.0 - ADAM_B1 ** ADAM_STEP)
        v_hat = vn / (1.0 - ADAM_B2 ** ADAM_STEP)
        g_out[...] = g
        d_out[...] = -ADAM_LR * (m_hat / (jnp.sqrt(v_hat) + ADAM_EPS) + ADAM_WD * w_ref[...])
        m_out[...] = mn
        v_out[...] = vn

    blk = pl.BlockSpec((None, tr, cols), lambda l, i: (l, i, 0))

    def g_spec(ll):
        return pl.BlockSpec((tr, cols), lambda l, i: (jnp.where(l == ll, i, jnp.where(l < ll, 0, nblk - 1)), 0))

    shp = jax.ShapeDtypeStruct(w.shape, F32)
    return pl.pallas_call(
        body, name=name, grid=(layers, nblk), in_specs=[blk] * 3 + [g_spec(ll) for ll in range(layers) for _ in range(ng)],
        out_specs=[blk] * 4, out_shape=[shp] * 4, compiler_params=_cparams("arbitrary", "arbitrary"),
    )(w, m, v, *[g for gl in gs for g in gl])


OTHER_CHIPS = ((1, 0), (0, 1), (1, 1))
N_CHIPS = 4
ANY = pl.BlockSpec(memory_space=pl.ANY)


def _place():
    return lax.axis_index("x"), lax.axis_index("y"), lax.axis_index("c")


def _flip(v, f):
    return 1 - v if f else v


def _part(ref, lead, axis, chip, size):
    idx = list(lead) + [slice(None)] * (len(ref.shape) - len(lead))
    idx[len(lead) + axis] = pl.ds(pl.multiple_of(chip * size, size), size)
    return ref.at[tuple(idx)]


def _allgather_chips(shards, axes, *, name):
    n = len(shards)
    sizes = [sh.shape[ax] for sh, ax in zip(shards, axes)]

    def full_shape(sh, ax):
        return tuple(d * N_CHIPS if i == ax else d for i, d in enumerate(sh.shape))

    def body(*refs):
        ins, outs = refs[:n], refs[n:2 * n]
        send_sems, recv_sems, loc_sems = refs[2 * n:]
        x, y, c = _place()
        me = 2 * x + y
        local = []
        for a in range(n):
            cp = pltpu.make_async_copy(ins[a], _part(outs[a], (), axes[a], me, sizes[a]), loc_sems.at[a])
            cp.start()
            local.append(cp)

        def remote(a, j, chip):
            fx, fy = OTHER_CHIPS[j]
            return pltpu.make_async_remote_copy(
                src_ref=ins[a], dst_ref=_part(outs[a], (), axes[a], chip, sizes[a]),
                send_sem=send_sems.at[a, j], recv_sem=recv_sems.at[a, j],
                device_id=(_flip(x, fx), _flip(y, fy), c), device_id_type=MESH)

        for a in range(n):
            for j in range(len(OTHER_CHIPS)):
                remote(a, j, me).start()
        for a in range(n):
            for j, (fx, fy) in enumerate(OTHER_CHIPS):
                remote(a, j, 2 * _flip(x, fx) + _flip(y, fy)).wait()
        for cp in local:
            cp.wait()

    return pl.pallas_call(
        body, name=name, in_specs=[ANY] * n, out_specs=[ANY] * n,
        out_shape=[jax.ShapeDtypeStruct(full_shape(sh, ax), sh.dtype) for sh, ax in zip(shards, axes)],
        scratch_shapes=[pltpu.SemaphoreType.DMA((n, 3)), pltpu.SemaphoreType.DMA((n, 3)), pltpu.SemaphoreType.DMA((n,))],
    )(*shards)


HBM = pl.BlockSpec(memory_space=pltpu.HBM)
SEM = pl.BlockSpec(memory_space=pltpu.SEMAPHORE)
SPLIT_COPY = pltpu.CompilerParams(has_side_effects=pltpu.SideEffectType.DATAFLOW_SIDE_EFFECTING)


def _cast_place(w, layer, axis, chip, *, name):
    _, rows, cols = w.shape
    tr = _row_tile(rows, cols)
    nblk = rows // tr

    def body(chip_ref, w_ref, o_ref):
        o_ref[...] = w_ref[...].astype(BF16)

    if axis == 1:
        shape = (rows, cols * N_CHIPS)
        o_spec = pl.BlockSpec((tr, cols), lambda i, chip_ref: (i, chip_ref[0]))
    else:
        shape = (rows * N_CHIPS, cols)
        o_spec = pl.BlockSpec((tr, cols), lambda i, chip_ref: (chip_ref[0] * nblk + i, 0))
    return pl.pallas_call(
        body, name=name,
        grid_spec=pltpu.PrefetchScalarGridSpec(
            num_scalar_prefetch=1, grid=(nblk,),
            in_specs=[pl.BlockSpec((None, tr, cols), lambda i, chip_ref: (layer, i, 0))], out_specs=o_spec),
        out_shape=jax.ShapeDtypeStruct(shape, BF16), compiler_params=_cparams("parallel"))(chip, w)


def _gather_copy(refs, a, j, send_sems, recv_sems, *, axes, sizes, arriving):
    x, y, c = _place()
    fx, fy = OTHER_CHIPS[j]
    px, py = _flip(x, fx), _flip(y, fy)
    part = _part(refs[a], (), axes[a], (2 * px + py) if arriving else (2 * x + y), sizes[a])
    k = a * len(OTHER_CHIPS) + j
    return pltpu.make_async_remote_copy(src_ref=part, dst_ref=part, send_sem=send_sems.at[k], recv_sem=recv_sems.at[k],
                                        device_id=(px, py, c), device_id_type=MESH)


def _scatter_copy(srcs, lands, a, j, axes, sizes, send_sems, recv_sems):
    x, y, c = _place()
    fx, fy = OTHER_CHIPS[j]
    px, py = _flip(x, fx), _flip(y, fy)
    k = a * len(OTHER_CHIPS) + j
    return pltpu.make_async_remote_copy(src_ref=_part(srcs[a], (), axes[a], 2 * px + py, sizes[a]), dst_ref=lands[a].at[j],
                                        send_sem=send_sems.at[k], recv_sem=recv_sems.at[k],
                                        device_id=(px, py, c), device_id_type=MESH)


def _split_start(arrs, make_copy, ncopies, dep, *, name):
    n = len(arrs)

    def body(*refs):
        ins = refs[:n]
        send_sems, recv_sems = refs[n + 1], refs[n + 2]
        token = refs[n + 3 + n]
        for a in range(ncopies):
            for j in range(len(OTHER_CHIPS)):
                make_copy(ins, a, j, send_sems, recv_sems).start()
        token[...] = jnp.zeros_like(token)

    sem = pltpu.SemaphoreType.DMA((ncopies * len(OTHER_CHIPS),))
    res = pl.pallas_call(
        body, name=name,
        out_shape=(sem, sem, *[pltpu.HBM(a.shape, a.dtype) for a in arrs], jax.ShapeDtypeStruct((8, LANES), F32)),
        in_specs=[HBM] * n + [pl.BlockSpec(memory_space=pl.ANY)],
        out_specs=(SEM, SEM, *[HBM] * n, pl.BlockSpec(memory_space=pltpu.VMEM)),
        input_output_aliases={a: 2 + a for a in range(n)}, compiler_params=SPLIT_COPY,
    )(*[pltpu.with_memory_space_constraint(a, pltpu.HBM) for a in arrs], dep)
    return res[0], res[1], list(res[2:2 + n]), res[2 + n]


def _split_wait(arrs, send_sems, recv_sems, make_copy, ncopies, after, *, name):
    n = len(arrs)

    def body(*refs):
        ins = refs[:n]
        send, recv = refs[n], refs[n + 1]
        for a in range(ncopies):
            for j in range(len(OTHER_CHIPS)):
                cp = make_copy(ins, a, j, send, recv)
                cp.wait_send()
                cp.wait_recv()

    res = pl.pallas_call(
        body, name=name, out_shape=tuple(pltpu.HBM(a.shape, a.dtype) for a in arrs),
        in_specs=[HBM] * n + [SEM, SEM, pl.BlockSpec(memory_space=pl.ANY)], out_specs=tuple([HBM] * n),
        input_output_aliases={a: a for a in range(n)}, compiler_params=SPLIT_COPY,
    )(*arrs, send_sems, recv_sems, after)
    return list(res)


def _sum_own_and_slots(g, land, axis, chip, *, name):
    slots, rows, cols = land.shape
    tr = _row_tile(rows, cols * 4)
    nblk = rows // tr

    def body(chip_ref, g_ref, l_ref, o_ref):
        acc = g_ref[...].astype(F32)
        for j in range(slots):
            acc = acc + l_ref[j].astype(F32)
        o_ref[...] = acc

    if axis == 1:
        g_spec = pl.BlockSpec((tr, cols), lambda i, chip_ref: (i, chip_ref[0]))
    else:
        g_spec = pl.BlockSpec((tr, cols), lambda i, chip_ref: (chip_ref[0] * nblk + i, 0))
    return pl.pallas_call(
        body, name=name,
        grid_spec=pltpu.PrefetchScalarGridSpec(
            num_scalar_prefetch=1, grid=(nblk,),
            in_specs=[g_spec, pl.BlockSpec((slots, tr, cols), lambda i, chip_ref: (0, i, 0))],
            out_specs=pl.BlockSpec((tr, cols), lambda i, chip_ref: (i, 0))),
        out_shape=jax.ShapeDtypeStruct((rows, cols), F32), compiler_params=_cparams("parallel"))(chip, g, land)


def _scatter_chips(grads, axes, *, name):
    n = len(grads)
    layers = len(grads[0])
    sizes = [g[0].shape[ax] // N_CHIPS for g, ax in zip(grads, axes)]

    def land_shape(g, ax):
        return (N_CHIPS, layers) + tuple(d // N_CHIPS if i == ax else d for i, d in enumerate(g[0].shape))

    def body(*refs):
        ins = [refs[a * layers:(a + 1) * layers] for a in range(n)]
        outs = refs[n * layers:n * layers + n]
        send_sems, recv_sems, loc_sems = refs[n * layers + n:]
        x, y, c = _place()
        me = 2 * x + y
        local = []
        for a in range(n):
            for l in range(layers):
                cp = pltpu.make_async_copy(_part(ins[a][l], (), axes[a], me, sizes[a]), outs[a].at[3, l], loc_sems.at[a, l])
                cp.start()
                local.append(cp)

        def remote(a, l, j):
            fx, fy = OTHER_CHIPS[j]
            px, py = _flip(x, fx), _flip(y, fy)
            return pltpu.make_async_remote_copy(
                src_ref=_part(ins[a][l], (), axes[a], 2 * px + py, sizes[a]), dst_ref=outs[a].at[j, l],
                send_sem=send_sems.at[a, l, j], recv_sem=recv_sems.at[a, l, j],
                device_id=(px, py, c), device_id_type=MESH)

        todo = [(a, l, j) for a in range(n) for l in range(layers) for j in range(len(OTHER_CHIPS))]
        for t in todo:
            remote(*t).start()
        for t in todo:
            remote(*t).wait()
        for cp in local:
            cp.wait()

    flat = [g for gs in grads for g in gs]
    return pl.pallas_call(
        body, name=name, in_specs=[ANY] * len(flat), out_specs=[ANY] * n,
        out_shape=[jax.ShapeDtypeStruct(land_shape(g, ax), g[0].dtype) for g, ax in zip(grads, axes)],
        scratch_shapes=[pltpu.SemaphoreType.DMA((n, layers, 3)), pltpu.SemaphoreType.DMA((n, layers, 3)),
                        pltpu.SemaphoreType.DMA((n, layers))],
    )(*flat)


def _swap_sibling(arrs, *, name):
    n = len(arrs)

    def body(*refs):
        ins, outs = refs[:n], refs[n:2 * n]
        send_sems, recv_sems = refs[2 * n:]
        x, y, c = _place()
        copies = [pltpu.make_async_remote_copy(src_ref=ins[a], dst_ref=outs[a], send_sem=send_sems.at[a], recv_sem=recv_sems.at[a],
                                               device_id=(x, y, 1 - c), device_id_type=MESH) for a in range(n)]
        for cp in copies:
            cp.start()
        for cp in copies:
            cp.wait()

    return pl.pallas_call(
        body, name=name, in_specs=[ANY] * n, out_specs=[ANY] * n,
        out_shape=[jax.ShapeDtypeStruct(a.shape, a.dtype) for a in arrs],
        scratch_shapes=[pltpu.SemaphoreType.DMA((n,)), pltpu.SemaphoreType.DMA((n,))],
    )(*arrs)


def _allreduce_small(p, *, name):
    rows, cols = p.shape
    nrel = len(OTHER_CHIPS)

    def body(p_ref, o_ref, sib_ref, land_ref, send_sems, recv_sems):
        x, y, c = _place()
        me = 2 * x + y
        pair = pltpu.make_async_remote_copy(src_ref=p_ref, dst_ref=sib_ref, send_sem=send_sems.at[nrel], recv_sem=recv_sems.at[nrel],
                                            device_id=(x, y, 1 - c), device_id_type=MESH)
        pair.start()
        pair.wait()
        land_ref[nrel] = p_ref[...] + sib_ref[...]
        copies = []
        for j, (fx, fy) in enumerate(OTHER_CHIPS):
            copies.append(pltpu.make_async_remote_copy(src_ref=land_ref.at[nrel], dst_ref=land_ref.at[j], send_sem=send_sems.at[j],
                                                       recv_sem=recv_sems.at[j], device_id=(_flip(x, fx), _flip(y, fy), c),
                                                       device_id_type=MESH))
        for cp in copies:
            cp.start()
        for cp in copies:
            cp.wait()

        def slot_of(chip):
            r = jnp.bitwise_xor(chip, me)
            return jnp.where(r == 0, nrel, jnp.where(r == 2, 0, jnp.where(r == 1, 1, 2)))

        acc = land_ref[slot_of(0)]
        for chip in range(1, N_CHIPS):
            acc = acc + land_ref[slot_of(chip)]
        o_ref[...] = acc

    vm = pl.BlockSpec(memory_space=pltpu.VMEM)
    return pl.pallas_call(
        body, name=name, in_specs=[vm], out_specs=vm, out_shape=jax.ShapeDtypeStruct((rows, cols), F32),
        scratch_shapes=[pltpu.VMEM((rows, cols), F32), pltpu.VMEM((nrel + 1, rows, cols), F32),
                        pltpu.SemaphoreType.DMA((nrel + 1,)), pltpu.SemaphoreType.DMA((nrel + 1,))],
        compiler_params=pltpu.CompilerParams(vmem_limit_bytes=V7X_VMEM_LIMIT_BYTES))(p)


WEIGHTS = ("mix_norm_g", "w_in", "conv_dw_w", "conv_dw_b", "conv_ln_g", "conv_ln_b", "conv_pw_w", "lru_conv_w", "lru_conv_b",
           "lru_wa", "lru_ba", "lru_wx", "lru_bx", "lru_lambda", "out_norm_conv", "out_norm_attn", "out_norm_lru", "w_out",
           "xattn_norm_g", "mem_norm_g", "xattn_wq", "xattn_wkv", "xattn_wo", "final_norm_g")
BIG = {"w_in": 2, "conv_pw_w": 1, "w_out": 1, "xattn_wq": 1, "xattn_wkv": 1, "xattn_wo": 2}
SMALL_SHARDED = {"conv_dw_w": 2, "lru_conv_w": 2}


IN_GROUP = ("w_in", "conv_pw_w")
REST_GROUP = ("w_out", "xattn_wq", "xattn_wkv", "xattn_wo")


def _trunk(x, mem, target, p, fetch, grads_ready):
    depth = p["mix_norm_g"].shape[0]
    c = p["conv_dw_w"].shape[2]
    aw = p["out_norm_attn"].shape[1]
    heads = aw // HEAD_DIM
    saved = []
    for l in range(depth):
        t = f"l{l}_"
        h1, r1 = _rmsnorm_fwd(x, p["mix_norm_g"][l], name=t + "mix_norm")
        wl = dict(fetch(IN_GROUP, l, r1))
        ua = _matmul(h1, wl["w_in"], mode="nn", n=3 * c, b_off=0, name=t + "in_conv")
        qkv = _matmul(h1, wl["w_in"], mode="nn", n=3 * aw, b_off=3 * c, out_dtype=BF16, name=t + "in_qkv")
        ub = _matmul(h1, wl["w_in"], mode="nn", n=aw + 2 * c, b_off=3 * c + 3 * aw, name=t + "in_gates")
        y_conv = _conv_fwd(ua, p["conv_dw_w"][l], p["conv_dw_b"][l], p["conv_ln_g"][l], p["conv_ln_b"][l], wl["conv_pw_w"],
                           name=t + "conv_fwd")
        y_attn, tot = _sb_fwd(qkv, heads, name=t + "sb_fwd")
        y_lru = _lru_fwd(ub, aw // c, p["lru_conv_w"][l], p["lru_conv_b"][l], p["lru_wa"][l], p["lru_ba"][l], p["lru_wx"][l],
                         p["lru_bx"][l], p["lru_lambda"][l], name=t + "lru_fwd")
        y = _mix_out_fwd(y_conv, y_attn, y_lru, ua, ub, p["out_norm_conv"][l], p["out_norm_attn"][l], p["out_norm_lru"][l],
                         name=t + "mix_out_fwd")
        wl.update(fetch(REST_GROUP, l, y))
        x2 = _matmul(y, wl["w_out"], mode="nn", add=x, name=t + "out_proj")
        h2, r2 = _rmsnorm_fwd(x2, p["xattn_norm_g"][l], name=t + "xattn_norm")
        qx = _matmul(h2, wl["xattn_wq"], mode="nn", out_dtype=BF16, name=t + "xattn_q")
        memn, rm = _rmsnorm_fwd(mem, p["mem_norm_g"][l], name=t + "mem_norm")
        kv = _matmul(memn, wl["xattn_wkv"], mode="nn", out_dtype=BF16, name=t + "xattn_kv")
        o = _xattn_fwd(qx, kv, name=t + "xattn_fwd")
        x3 = _matmul(o, wl["xattn_wo"], mode="nn", add=x2, name=t + "xattn_o")
        saved.append(dict(x=x, h1=h1, r1=r1, ua=ua, qkv=qkv, ub=ub, y_conv=y_conv, y_attn=y_attn, tot=tot, y_lru=y_lru, y=y,
                          x2=x2, h2=h2, r2=r2, qx=qx, memn=memn, rm=rm, kv=kv, o=o, w=wl))
        x = x3

    loss, dx, dg_final = _final_loss(x, p["final_norm_g"], target, name="final_loss")
    small = {k: [None] * depth for k in WEIGHTS if k not in BIG and k != "final_norm_g"}
    token = None
    for l in reversed(range(depth)):
        t = f"l{l}_"
        s = saved[l]
        wl = s["w"]
        do = _matmul(dx, wl["xattn_wo"], mode="nt", out_dtype=BF16, dep=token, name=t + "d_xattn_o")
        dwo = _matmul(s["o"], dx, mode="tn", out_dtype=BF16, name=t + "dw_xattn_o")
        dqx, dkv = _xattn_bwd(s["qx"], s["kv"], do, name=t + "xattn_bwd")
        dwq = _matmul(s["h2"], dqx, mode="tn", out_dtype=BF16, name=t + "dw_xattn_q")
        dh2 = _matmul(dqx, wl["xattn_wq"], mode="nt", name=t + "d_xattn_q")
        dx2, dg = _rmsnorm_bwd(dh2, s["x2"], s["r2"], p["xattn_norm_g"][l], dx, name=t + "xattn_norm_bwd")
        small["xattn_norm_g"][l] = dg[0]
        dmemn = _matmul(dkv, wl["xattn_wkv"], mode="nt", name=t + "d_xattn_kv")
        dwkv = _matmul(s["memn"], dkv, mode="tn", out_dtype=BF16, name=t + "dw_xattn_kv")
        _, dg = _rmsnorm_bwd(dmemn, mem, s["rm"], p["mem_norm_g"][l], None, name=t + "mem_norm_bwd")
        small["mem_norm_g"][l] = dg[0]
        dwout = _matmul(s["y"], dx2, mode="tn", out_dtype=BF16, name=t + "dw_out_proj")
        token = grads_ready(REST_GROUP, l, dict(w_out=dwout, xattn_wq=dwq, xattn_wkv=dwkv, xattn_wo=dwo))
        dy = _matmul(dx2, wl["w_out"], mode="nt", dep=token, name=t + "d_out_proj")
        dyc, dya, dyl, dgc, dga, dgl, dnc, dna, dnl = _mix_out_bwd(
            dy, s["y_conv"], s["y_attn"], s["y_lru"], s["ua"], s["ub"], p["out_norm_conv"][l], p["out_norm_attn"][l],
            p["out_norm_lru"][l], name=t + "mix_out_bwd")
        small["out_norm_conv"][l], small["out_norm_attn"][l], small["out_norm_lru"][l] = dnc[0], dna[0], dnl[0]
        dd, dpw, dlg, dlb = _conv_bwd_a(s["ua"], dyc, p["conv_dw_w"][l], p["conv_dw_b"][l], p["conv_ln_g"][l], p["conv_ln_b"][l],
                                        wl["conv_pw_w"], name=t + "conv_bwd_a")
        dval, dglu, ddw, ddb = _conv_bwd_b(s["ua"], dd, p["conv_dw_w"][l], name=t + "conv_bwd_b")
        small["conv_ln_g"][l], small["conv_ln_b"][l], small["conv_dw_w"][l], small["conv_dw_b"][l] = dlg[0], dlb[0], ddw, ddb[0]
        dq, dk, dv = _sb_bwd(s["qkv"], dya, s["tot"], heads, name=t + "sb_bwd")
        drx, dcw, dcb, dwa, dba, dwx, dbx, dlam = _lru_bwd(
            s["ub"], aw // c, s["y_lru"], dyl, p["lru_conv_w"][l], p["lru_conv_b"][l], p["lru_wa"][l], p["lru_ba"][l],
            p["lru_wx"][l], p["lru_bx"][l], p["lru_lambda"][l], name=t + "lru_bwd")
        small["lru_conv_w"][l], small["lru_conv_b"][l], small["lru_wa"][l], small["lru_ba"][l] = dcw, dcb[0], dwa, dba[0]
        small["lru_wx"][l], small["lru_bx"][l], small["lru_lambda"][l] = dwx, dbx[0], dlam[0]
        du = jnp.concatenate([dval, dglu, dgc, dq, dk, dv, dga, drx, dgl], axis=1)
        dwin = _matmul(s["h1"], du, mode="tn", out_dtype=BF16, name=t + "dw_in")
        token = grads_ready(IN_GROUP, l, dict(w_in=dwin, conv_pw_w=_cast_bf16(dpw, name=t + "cast_dpw")))
        dh1 = _matmul(du, wl["w_in"], mode="nt", dep=token, name=t + "d_in")
        dx, dg = _rmsnorm_bwd(dh1, s["x"], s["r1"], p["mix_norm_g"][l], dx2, name=t + "mix_norm_bwd")
        small["mix_norm_g"][l] = dg[0]
    small = {k: jnp.stack(v) for k, v in small.items()}
    small["final_norm_g"] = dg_final[0]
    return loss, dx, small


def _pack(arrs):
    flat = jnp.concatenate([a.reshape(-1) for a in arrs])
    pad = (-flat.shape[0]) % (8 * LANES)
    return jnp.pad(flat, (0, pad)).reshape(-1, LANES)


def _unpack(packed, like):
    flat = packed.reshape(-1)
    out, at = [], 0
    for a in like:
        out.append(flat[at:at + a.size].reshape(a.shape))
        at += a.size
    return out


def _as_rows(a):
    return a.reshape(-1, a.shape[-1])


def kernel(x, mem, mix_norm_g, w_in, conv_dw_w, conv_dw_b, conv_ln_g, conv_ln_b, conv_pw_w, lru_conv_w, lru_conv_b, lru_wa, lru_ba, lru_wx, lru_bx, lru_lambda, out_norm_conv, out_norm_attn, out_norm_lru, w_out, xattn_norm_g, mem_norm_g, xattn_wq, xattn_wkv, xattn_wo, final_norm_g, loss_target, m_mix_norm_g, m_w_in, m_conv_dw_w, m_conv_dw_b, m_conv_ln_g, m_conv_ln_b, m_conv_pw_w, m_lru_conv_w, m_lru_conv_b, m_lru_wa, m_lru_ba, m_lru_wx, m_lru_bx, m_lru_lambda, m_out_norm_conv, m_out_norm_attn, m_out_norm_lru, m_w_out, m_xattn_norm_g, m_mem_norm_g, m_xattn_wq, m_xattn_wkv, m_xattn_wo, m_final_norm_g, v_mix_norm_g, v_w_in, v_conv_dw_w, v_conv_dw_b, v_conv_ln_g, v_conv_ln_b, v_conv_pw_w, v_lru_conv_w, v_lru_conv_b, v_lru_wa, v_lru_ba, v_lru_wx, v_lru_bx, v_lru_lambda, v_out_norm_conv, v_out_norm_attn, v_out_norm_lru, v_w_out, v_xattn_norm_g, v_mem_norm_g, v_xattn_wq, v_xattn_wkv, v_xattn_wo, v_final_norm_g):
    given = dict(locals())
    w = {k: given[k] for k in WEIGHTS}
    m = {k: given["m_" + k] for k in WEIGHTS}
    v = {k: given["v_" + k] for k in WEIGHTS}
    depth = mix_norm_g.shape[0]
    chip = 2 * lax.axis_index("x") + lax.axis_index("y")

    chip_arr = chip.astype(jnp.int32).reshape(1)

    p = dict(w)
    p.update(zip(SMALL_SHARDED, _allgather_chips([w[k] for k in SMALL_SHARDED], list(SMALL_SHARDED.values()), name="gather_small")))
    axis2d = {k: BIG[k] - 1 for k in BIG}
    groups = [(IN_GROUP, 0), (REST_GROUP, 0)] + [(IN_GROUP + REST_GROUP, l) for l in range(1, depth)]
    pending, token = {}, p[next(iter(SMALL_SHARDED))]
    for names, l in groups:
        arrs = [_cast_place(w[k], l, axis2d[k], chip_arr, name=f"place{l}_{k}") for k in names]
        axes = [axis2d[k] for k in names]
        sizes = [a.shape[ax] // N_CHIPS for a, ax in zip(arrs, axes)]
        start = functools.partial(_gather_copy, axes=axes, sizes=sizes, arriving=False)
        land = functools.partial(_gather_copy, axes=axes, sizes=sizes, arriving=True)
        send, recv, arrs, token = _split_start(arrs, start, len(arrs), token, name=f"gather_start{l}_{names[0]}")
        pending[(names[0], l)] = (names, arrs, send, recv, land)
    last_token = token
    have = {}

    def fetch(group, l, after):
        key = (group[0], l)
        if key in pending:
            names, arrs, send, recv, land = pending.pop(key)
            after = last_token if (group, l) == groups[0] else after
            arrs = _split_wait(arrs, send, recv, land, len(arrs), after, name=f"gather_wait{l}_{names[0]}")
            have.update({(k, l): a for k, a in zip(names, arrs)})
        return {k: have[(k, l)] for k in group}

    flying = []
    held = {}

    def grads_ready(group, l, grads):
        held.update({(k, l): g for k, g in grads.items()})
        if l > 0 and group == REST_GROUP:
            return None
        names = [k for k in (IN_GROUP + REST_GROUP if l > 0 else group)]
        srcs = [held[(k, l)] for k in names]
        axes = [axis2d[k] for k in names]
        sizes = [g.shape[ax] // N_CHIPS for g, ax in zip(srcs, axes)]
        lands = [lax.empty((len(OTHER_CHIPS),) + tuple(sz if i == ax else d for i, d in enumerate(g.shape)), g.dtype)
                 for g, ax, sz in zip(srcs, axes, sizes)]
        n = len(names)
        copy = lambda refs, a, j, ss, rs_: _scatter_copy(refs[:n], refs[n:], a, j, axes, sizes, ss, rs_)
        send, recv, arrs, token = _split_start(srcs + lands, copy, n, jnp.zeros((8, LANES), F32), name=f"scatter_start{l}_{names[0]}")
        flying.append((names, l, axes, arrs, send, recv, copy))
        return token

    loss, grad_x, small = _trunk(x[0], mem[0], loss_target[0], p, fetch, grads_ready)
    loss = lax.psum(loss[0, 0], ("x", "y", "c"))

    sums = {}
    out = {}

    def arrive(entry, after):
        names, l, axes, arrs, send, recv, copy = entry
        n = len(names)
        arrs = _split_wait(arrs, send, recv, copy, n, after, name=f"scatter_wait{l}_{names[0]}")
        for k, ax, g, ld in zip(names, axes, arrs[:n], arrs[n:]):
            ld = ld.reshape((len(OTHER_CHIPS), -1, ld.shape[-1]))
            sums[(k, l)] = _sum_own_and_slots(g, ld, ax, chip_arr, name=f"sum{l}_{k}")
        return sums[(names[-1], l)]

    def update(names):
        mine = [sums[(k, l)] for k in names for l in range(depth)]
        theirs = _swap_sibling(mine, name="swap_sums_" + names[0])
        for i, k in enumerate(names):
            gs = [[mine[i * depth + l], theirs[i * depth + l]] for l in range(depth)]
            out[k] = _adamw_layers(w[k], m[k], v[k], gs, name="adamw_" + k)

    after = grad_x
    for entry in flying[:-1]:
        after = arrive(entry, after)
    update(REST_GROUP)

    small_names = [k for k in WEIGHTS if k not in BIG]
    total = _unpack(_allreduce_small(_pack([small[k] for k in small_names]), name="allreduce_small"), [small[k] for k in small_names])
    g_small = dict(zip(small_names, total))
    for k, ax in SMALL_SHARDED.items():
        size = w[k].shape[ax]
        g_small[k] = lax.dynamic_slice_in_dim(g_small[k], chip * size, size, axis=ax)
    res = _adamw(_pack([w[k] for k in small_names]), _pack([m[k] for k in small_names]), _pack([v[k] for k in small_names]),
                 [_pack([g_small[k] for k in small_names])], name="adamw_small")
    last = res[0]
    res = [_unpack(r, [w[k] for k in small_names]) for r in res]
    for i, k in enumerate(small_names):
        out[k] = [r[i] for r in res]

    arrive(flying[-1], last)
    update(IN_GROUP)

    outs = [loss, grad_x[None]]
    for part in range(4):
        outs += [out[k][part] for k in WEIGHTS]
    return tuple(outs)
```

```python
import functools

import jax
import jax.numpy as jnp
from jax import lax
from jax.experimental import pallas as pl
from jax.experimental.pallas import tpu as pltpu

F32 = jnp.float32
BF16 = jnp.bfloat16
MESH = pl.DeviceIdType.MESH

V7X_VMEM_LIMIT_BYTES = 56 * 1024 * 1024
LANES = 128
HEAD_DIM = 128
LRU_C = 8.0
RMS_EPS = 1e-6
LN_EPS = 1e-5
CONV_HALO = 32
LRU_HALO = 8
ADAM_LR = 0.001
ADAM_B1 = 0.9
ADAM_B2 = 0.999
ADAM_EPS = 1e-08
ADAM_WD = 0.01
ADAM_STEP = 10


def _cparams(*sem):
    return pltpu.CompilerParams(dimension_semantics=sem, vmem_limit_bytes=V7X_VMEM_LIMIT_BYTES)


def _tile(n, pref):
    if n <= pref:
        return n
    for t in range(pref - pref % LANES, 0, -LANES):
        if n % t == 0:
            return t
    t = pref
    while n % t:
        t //= 2
    return t


def _dot(a, b, dims):
    return lax.dot_general(a, b, (dims, ((), ())), preferred_element_type=F32)


def _nn(a, b):
    return _dot(a, b, ((1,), (0,)))


def _nt(a, b):
    return _dot(a, b, ((1,), (1,)))


def _tn(a, b):
    return _dot(a, b, ((0,), (0,)))


def _sigmoid(x):
    return jax.nn.sigmoid(x)


def _silu_and_grad(x):
    s = _sigmoid(x)
    return x * s, s * (1.0 + x * (1.0 - s))


def _matmul(a, b, *, mode, name, layer=None, n=None, b_off=0, add=None, dep=None, out_dtype=F32, tm=1024, tn=1024, tk=2048):
    bshape = b.shape if layer is None else b.shape[1:]
    if mode == "nn":
        m, k = a.shape
        n = bshape[1] if n is None else n
    elif mode == "nt":
        m, k = a.shape
        n = bshape[0]
    else:
        k, m = a.shape
        n = bshape[1]
    tm, tk = _tile(m, tm), _tile(k, tk)
    tn = _tile(n, tn)
    while b_off % tn or n % tn:
        tn -= LANES
    nk = k // tk
    off = b_off // tn
    lead = () if layer is None else (None,)
    li = () if layer is None else (layer,)
    if mode == "nn":
        a_spec = pl.BlockSpec((tm, tk), lambda i, j, kk: (i, kk))
        b_spec = pl.BlockSpec(lead + (tk, tn), lambda i, j, kk: li + (kk, j + off))
        dot = _nn
    elif mode == "nt":
        a_spec = pl.BlockSpec((tm, tk), lambda i, j, kk: (i, kk))
        b_spec = pl.BlockSpec(lead + (tn, tk), lambda i, j, kk: li + (j, kk))
        dot = _nt
    else:
        a_spec = pl.BlockSpec((tk, tm), lambda i, j, kk: (kk, i))
        b_spec = pl.BlockSpec(lead + (tk, tn), lambda i, j, kk: li + (kk, j))
        dot = _tn
    o_spec = pl.BlockSpec((tm, tn), lambda i, j, kk: (i, j))
    has_add = add is not None

    def body(*refs):
        refs = refs[:-3] + refs[-2:] if dep is not None else refs
        if has_add:
            a_ref, b_ref, add_ref, o_ref, acc_ref = refs
        else:
            a_ref, b_ref, o_ref, acc_ref = refs
        kk = pl.program_id(2)
        part = dot(a_ref[...].astype(BF16), b_ref[...].astype(BF16))

        @pl.when(kk == 0)
        def _():
            acc_ref[...] = part

        @pl.when(kk > 0)
        def _():
            acc_ref[...] += part

        @pl.when(kk == nk - 1)
        def _():
            r = acc_ref[...]
            if has_add:
                r = r + add_ref[...]
            o_ref[...] = r.astype(o_ref.dtype)

    ins = [a, b] + ([add] if has_add else [])
    specs = [a_spec, b_spec] + ([o_spec] if has_add else [])
    if dep is not None:
        ins.append(dep)
        specs.append(pl.BlockSpec((8, LANES), lambda i, j, kk: (0, 0)))
    return pl.pallas_call(
        body, name=name, grid=(m // tm, n // tn, nk), in_specs=specs, out_specs=o_spec,
        out_shape=jax.ShapeDtypeStruct((m, n), out_dtype), scratch_shapes=[pltpu.VMEM((tm, tn), F32)],
        compiler_params=_cparams("parallel", "parallel", "arbitrary"))(*ins)


def _rmsnorm_fwd(x, g, *, name):
    s, d = x.shape
    tm = _tile(s, 256)

    def body(x_ref, g_ref, h_ref, r_ref):
        xf = x_ref[...]
        r = lax.rsqrt(jnp.mean(xf * xf, axis=-1, keepdims=True) + RMS_EPS)
        h_ref[...] = ((xf * r) * g_ref[...]).astype(h_ref.dtype)
        r_ref[...] = r

    return pl.pallas_call(
        body, name=name, grid=(s // tm,),
        in_specs=[pl.BlockSpec((tm, d), lambda i: (i, 0)), pl.BlockSpec((1, d), lambda i: (0, 0))],
        out_specs=[pl.BlockSpec((tm, d), lambda i: (i, 0)), pl.BlockSpec((tm, 1), lambda i: (i, 0))],
        out_shape=[jax.ShapeDtypeStruct((s, d), BF16), jax.ShapeDtypeStruct((s, 1), F32)],
        compiler_params=_cparams("parallel"))(x, g.reshape(1, d))


def _rms_bwd_math(dh, x, r, g):
    xr = x * r
    dyg = dh * g
    m = jnp.mean(dyg * xr, axis=-1, keepdims=True)
    return r * (dyg - xr * m), dh * xr


def _rmsnorm_bwd(dh, x, r, g, dres, *, name):
    s, d = x.shape
    tm = _tile(s, 256)
    has_res = dres is not None

    def body(*refs):
        if has_res:
            dh_ref, x_ref, r_ref, g_ref, res_ref, dx_ref, dg_ref = refs
        else:
            dh_ref, x_ref, r_ref, g_ref, dx_ref, dg_ref = refs
        dx, dgp = _rms_bwd_math(dh_ref[...].astype(F32), x_ref[...], r_ref[...], g_ref[...])
        if has_res:
            dx = dx + res_ref[...]
        dx_ref[...] = dx

        @pl.when(pl.program_id(0) == 0)
        def _():
            dg_ref[...] = jnp.zeros_like(dg_ref)

        dg_ref[...] += jnp.sum(dgp, axis=0, keepdims=True)

    row = pl.BlockSpec((tm, d), lambda i: (i, 0))
    vec = pl.BlockSpec((1, d), lambda i: (0, 0))
    ins = [dh, x, r, g.reshape(1, d)] + ([dres] if has_res else [])
    specs = [row, row, pl.BlockSpec((tm, 1), lambda i: (i, 0)), vec] + ([row] if has_res else [])
    return pl.pallas_call(
        body, name=name, grid=(s // tm,), in_specs=specs, out_specs=[row, vec],
        out_shape=[jax.ShapeDtypeStruct((s, d), F32), jax.ShapeDtypeStruct((1, d), F32)],
        compiler_params=_cparams("arbitrary"))(*ins)


def _final_loss(x, g, target, *, name):
    s, d = x.shape
    tm = _tile(s, 256)

    def body(x_ref, g_ref, t_ref, loss_ref, dx_ref, dg_ref):
        xf = x_ref[...]
        gv = g_ref[...]
        r = lax.rsqrt(jnp.mean(xf * xf, axis=-1, keepdims=True) + RMS_EPS)
        diff = (xf * r) * gv - t_ref[...]
        part = 0.5 * jnp.sum(jnp.mean(diff * diff, axis=-1, keepdims=True))
        dx, dgp = _rms_bwd_math(diff * (1.0 / d), xf, r, gv)
        dx_ref[...] = dx

        @pl.when(pl.program_id(0) == 0)
        def _():
            dg_ref[...] = jnp.zeros_like(dg_ref)
            loss_ref[...] = jnp.zeros_like(loss_ref)

        dg_ref[...] += jnp.sum(dgp, axis=0, keepdims=True)
        loss_ref[...] += part

    row = pl.BlockSpec((tm, d), lambda i: (i, 0))
    vec = pl.BlockSpec((1, d), lambda i: (0, 0))
    return pl.pallas_call(
        body, name=name, grid=(s // tm,), in_specs=[row, vec, row],
        out_specs=[pl.BlockSpec((8, LANES), lambda i: (0, 0)), row, vec],
        out_shape=[jax.ShapeDtypeStruct((8, LANES), F32), jax.ShapeDtypeStruct((s, d), F32), jax.ShapeDtypeStruct((1, d), F32)],
        compiler_params=_cparams("arbitrary"))(x, g.reshape(1, d), target)


def _conv_taps(gp_ref, w, bias, taps, tm):
    halo = gp_ref.shape[0] - tm
    acc = jnp.broadcast_to(bias, (tm, w.shape[1]))
    for k in range(taps):
        acc = acc + w[k:k + 1, :] * gp_ref[pl.ds(halo - (taps - 1) + k, tm), :]
    return acc


def _conv_core(val, glu, valh, gluh, first, gp_ref, w, bias, lg, lb, taps, tm):
    sg = _sigmoid(glu)
    g = val * sg
    gh = jnp.where(first, 0.0, valh * _sigmoid(gluh))
    gp_ref[0:CONV_HALO, :] = gh
    gp_ref[CONV_HALO:, :] = g
    d = _conv_taps(gp_ref, w, bias, taps, tm)
    mu = jnp.mean(d, axis=-1, keepdims=True)
    dc = d - mu
    rstd = lax.rsqrt(jnp.mean(dc * dc, axis=-1, keepdims=True) + LN_EPS)
    xhat = dc * rstd
    ln = xhat * lg + lb
    return sg, xhat, rstd, ln


def _conv_fwd(ua, dw_w, dw_b, ln_g, ln_b, pw, *, name):
    s = ua.shape[0]
    taps, c = dw_w.shape
    tm = _tile(s, 512)
    hb = tm // CONV_HALO

    def body(val_ref, glu_ref, valh_ref, gluh_ref, w_ref, b_ref, lg_ref, lb_ref, pw_ref, y_ref, gp_ref):
        first = pl.program_id(0) == 0
        _, _, _, ln = _conv_core(val_ref[...], glu_ref[...], valh_ref[...], gluh_ref[...], first, gp_ref,
                                 w_ref[...], b_ref[...], lg_ref[...], lb_ref[...], taps, tm)
        sw = ln * _sigmoid(ln)
        y_ref[...] = _nn(sw.astype(BF16), pw_ref[...])

    cur = lambda col: pl.BlockSpec((tm, c), lambda i: (i, col))
    prev = lambda col: pl.BlockSpec((CONV_HALO, c), lambda i: (jnp.maximum(i * hb - 1, 0), col))
    full = lambda a: pl.BlockSpec(a.shape, lambda i: (0,) * a.ndim)
    vecs = [dw_w, dw_b.reshape(1, c), ln_g.reshape(1, c), ln_b.reshape(1, c), pw]
    return pl.pallas_call(
        body, name=name, grid=(s // tm,),
        in_specs=[cur(0), cur(1), prev(0), prev(1)] + [full(a) for a in vecs],
        out_specs=pl.BlockSpec((tm, c), lambda i: (i, 0)),
        out_shape=jax.ShapeDtypeStruct((s, c), F32),
        scratch_shapes=[pltpu.VMEM((tm + CONV_HALO, c), F32)],
        compiler_params=_cparams("parallel"))(ua, ua, ua, ua, *vecs)


def _conv_bwd_a(ua, dy, dw_w, dw_b, ln_g, ln_b, pw, *, name):
    s = ua.shape[0]
    taps, c = dw_w.shape
    tm = _tile(s, 512)
    hb = tm // CONV_HALO

    def body(val_ref, glu_ref, valh_ref, gluh_ref, dy_ref, w_ref, b_ref, lg_ref, lb_ref, pw_ref,
             dd_ref, dpw_ref, dlg_ref, dlb_ref, gp_ref):
        first = pl.program_id(0) == 0
        lg = lg_ref[...]
        _, xhat, rstd, ln = _conv_core(val_ref[...], glu_ref[...], valh_ref[...], gluh_ref[...], first, gp_ref,
                                       w_ref[...], b_ref[...], lg, lb_ref[...], taps, tm)
        sw, dsw = _silu_and_grad(ln)
        dyb = dy_ref[...].astype(BF16)
        ds = _nt(dyb, pw_ref[...])
        dln = ds * dsw
        dxhat = dln * lg
        m1 = jnp.mean(dxhat, axis=-1, keepdims=True)
        m2 = jnp.mean(dxhat * xhat, axis=-1, keepdims=True)
        dd_ref[...] = rstd * (dxhat - m1 - xhat * m2)

        @pl.when(first)
        def _():
            dpw_ref[...] = jnp.zeros_like(dpw_ref)
            dlg_ref[...] = jnp.zeros_like(dlg_ref)
            dlb_ref[...] = jnp.zeros_like(dlb_ref)

        dpw_ref[...] += _tn(sw.astype(BF16), dyb)
        dlg_ref[...] += jnp.sum(dln * xhat, axis=0, keepdims=True)
        dlb_ref[...] += jnp.sum(dln, axis=0, keepdims=True)

    cur = lambda col: pl.BlockSpec((tm, c), lambda i: (i, col))
    prev = lambda col: pl.BlockSpec((CONV_HALO, c), lambda i: (jnp.maximum(i * hb - 1, 0), col))
    full = lambda a: pl.BlockSpec(a.shape, lambda i: (0,) * a.ndim)
    vec = pl.BlockSpec((1, c), lambda i: (0, 0))
    vecs = [dw_w, dw_b.reshape(1, c), ln_g.reshape(1, c), ln_b.reshape(1, c), pw]
    return pl.pallas_call(
        body, name=name, grid=(s // tm,),
        in_specs=[cur(0), cur(1), prev(0), prev(1), pl.BlockSpec((tm, c), lambda i: (i, 0))] + [full(a) for a in vecs],
        out_specs=[pl.BlockSpec((tm, c), lambda i: (i, 0)), pl.BlockSpec((c, c), lambda i: (0, 0)), vec, vec],
        out_shape=[jax.ShapeDtypeStruct((s, c), F32), jax.ShapeDtypeStruct((c, c), F32),
                   jax.ShapeDtypeStruct((1, c), F32), jax.ShapeDtypeStruct((1, c), F32)],
        scratch_shapes=[pltpu.VMEM((tm + CONV_HALO, c), F32)],
        compiler_params=_cparams("arbitrary"))(ua, ua, ua, ua, dy, *vecs)


def _conv_bwd_b(ua, dd, dw_w, *, name):
    s = ua.shape[0]
    taps, c = dw_w.shape
    tm = _tile(s, 512)
    hb = tm // CONV_HALO
    nt = s // tm

    def body(val_ref, glu_ref, valh_ref, gluh_ref, dd_ref, ddn_ref, w_ref, dval_ref, dglu_ref, dw_ref, db_ref, gp_ref, ddp_ref):
        i = pl.program_id(0)
        val = val_ref[...]
        sg = _sigmoid(glu_ref[...])
        gp_ref[0:CONV_HALO, :] = jnp.where(i == 0, 0.0, valh_ref[...] * _sigmoid(gluh_ref[...]))
        gp_ref[CONV_HALO:, :] = val * sg
        dd = dd_ref[...]
        ddp_ref[0:tm, :] = dd
        ddp_ref[tm:, :] = jnp.where(i == nt - 1, 0.0, ddn_ref[...])
        w = w_ref[...]
        dg = jnp.zeros((tm, c), F32)
        dws = []
        for k in range(taps):
            dg = dg + w[k:k + 1, :] * ddp_ref[pl.ds(taps - 1 - k, tm), :]
            dws.append(jnp.sum(dd * gp_ref[pl.ds(CONV_HALO - (taps - 1) + k, tm), :], axis=0, keepdims=True))
        dval_ref[...] = (dg * sg).astype(dval_ref.dtype)
        dglu_ref[...] = (dg * val * sg * (1.0 - sg)).astype(dglu_ref.dtype)

        @pl.when(i == 0)
        def _():
            dw_ref[...] = jnp.zeros_like(dw_ref)
            db_ref[...] = jnp.zeros_like(db_ref)

        dw_ref[...] += jnp.concatenate(dws, axis=0)
        db_ref[...] += jnp.sum(dd, axis=0, keepdims=True)

    cur = lambda col: pl.BlockSpec((tm, c), lambda i: (i, col))
    prev = lambda col: pl.BlockSpec((CONV_HALO, c), lambda i: (jnp.maximum(i * hb - 1, 0), col))
    nxt = pl.BlockSpec((CONV_HALO, c), lambda i: (jnp.minimum((i + 1) * hb, s // CONV_HALO - 1), 0))
    return pl.pallas_call(
        body, name=name, grid=(nt,),
        in_specs=[cur(0), cur(1), prev(0), prev(1), pl.BlockSpec((tm, c), lambda i: (i, 0)), nxt,
                  pl.BlockSpec((taps, c), lambda i: (0, 0))],
        out_specs=[pl.BlockSpec((tm, c), lambda i: (i, 0)), pl.BlockSpec((tm, c), lambda i: (i, 0)),
                   pl.BlockSpec((taps, c), lambda i: (0, 0)), pl.BlockSpec((1, c), lambda i: (0, 0))],
        out_shape=[jax.ShapeDtypeStruct((s, c), BF16), jax.ShapeDtypeStruct((s, c), BF16),
                   jax.ShapeDtypeStruct((taps, c), F32), jax.ShapeDtypeStruct((1, c), F32)],
        scratch_shapes=[pltpu.VMEM((tm + CONV_HALO, c), F32), pltpu.VMEM((tm + CONV_HALO, c), F32)],
        compiler_params=_cparams("arbitrary"))(ua, ua, ua, ua, dd, dd, dw_w)


LOG2_E = 1.4426950408889634
SB_HEADS_PER_STEP = 4


def _sb_logs(qk, mask):
    z = qk * (HEAD_DIM ** -0.5 * LOG2_E)
    ls = jnp.minimum(z, 0.0) - jnp.log2(1.0 + jnp.exp2(-jnp.abs(z)))
    lm = ls - z
    if mask is not None:
        lm = jnp.where(mask, lm, 0.0)
    return ls, lm


def _diag_mask(b):
    return lax.broadcasted_iota(jnp.int32, (b, b), 1) < lax.broadcasted_iota(jnp.int32, (b, b), 0)


def _split_dot(x, tri):
    hi = x.astype(BF16)
    lo = (x - hi.astype(F32)).astype(BF16)
    return _nn(hi, tri) + _nn(lo, tri)


def _tri(bk, cmp):
    r = lax.broadcasted_iota(jnp.int32, (bk, bk), 0)
    c = lax.broadcasted_iota(jnp.int32, (bk, bk), 1)
    return cmp(r, c).astype(BF16)


def _sb_fwd(qkv, heads, *, name, blk=256):
    s = qkv.shape[0]
    b = _tile(s, blk)
    nq = s // b
    hp = min(SB_HEADS_PER_STEP, heads)
    assert heads % hp == 0
    groups = heads // hp
    wide = hp * HEAD_DIM

    def body(q_ref, k_ref, v_ref, o_ref, w_hbm, stage, sems):
        g = pl.program_id(0)
        i = pl.program_id(1)
        sls = [slice(n * HEAD_DIM, (n + 1) * HEAD_DIM) for n in range(hp)]
        qs = [q_ref[:, sl] for sl in sls]
        tri = _tri(b, lambda r, c: r > c)
        diag = _diag_mask(b)
        r0 = pl.multiple_of(i * b, b)

        def saves(slot, j):
            c0 = pl.multiple_of(j * b, b)
            return [pltpu.make_async_copy(stage.at[slot, n, w], w_hbm.at[w, g * hp + n, pl.ds(r0, b), pl.ds(c0, b)], sems.at[slot])
                    for n in range(hp) for w in range(2)]

        def tile(t, j, carry, masked):
            slot = t % 2
            if not masked:
                @pl.when(t >= 2)
                def _():
                    for cp in saves(slot, j):
                        cp.wait()

            s0 = pl.multiple_of(j * b, b)
            kbs = [k_ref[pl.ds(s0, b), sl] for sl in sls]
            vbs = [v_ref[pl.ds(s0, b), sl] for sl in sls]
            zs = [_nt(q, kb) for q, kb in zip(qs, kbs)]
            sc = [_sb_logs(z, diag if masked else None) for z in zs]
            after = [_split_dot(lm, tri) for _, lm in sc]
            out = []
            for n, ((ls, lm), af, (acc, c)) in enumerate(zip(sc, after, carry)):
                a = jnp.exp2(ls + (af + c))
                if masked:
                    a = jnp.where(diag, a, 0.0)
                ab = a.astype(BF16)
                stage[slot, n, 0] = ab
                stage[slot, n, 1] = jnp.exp2(ls).astype(BF16)
                out.append((ab, acc, c + jnp.sum(lm, axis=1, keepdims=True)))
            for cp in saves(slot, j):
                cp.start()
            return tuple((acc + _nn(ab, vb), c) for vb, (ab, acc, c) in zip(vbs, out))

        zero = tuple((jnp.zeros((b, HEAD_DIM), F32), jnp.zeros((b, 1), F32)) for _ in range(hp))
        carry = tile(0, i, zero, True)
        carry = lax.fori_loop(0, i, lambda jj, cr: tile(jj + 1, i - 1 - jj, cr, False), carry)
        for sl, (acc, _) in zip(sls, carry):
            o_ref[:, sl] = acc
        for cp in saves(i % 2, 0):
            cp.wait()

        @pl.when(i >= 1)
        def _():
            for cp in saves((i + 1) % 2, 0):
                cp.wait()

    return pl.pallas_call(
        body, name=name, grid=(groups, nq),
        in_specs=[pl.BlockSpec((b, wide), lambda g, i: (i, g)),
                  pl.BlockSpec((s, wide), lambda g, i: (0, groups + g)),
                  pl.BlockSpec((s, wide), lambda g, i: (0, 2 * groups + g))],
        out_specs=[pl.BlockSpec((b, wide), lambda g, i: (i, g)), ANY],
        out_shape=[jax.ShapeDtypeStruct((s, heads * HEAD_DIM), F32), jax.ShapeDtypeStruct((2, heads, s, s), BF16)],
        scratch_shapes=[pltpu.VMEM((2, hp, 2, b, b), BF16), pltpu.SemaphoreType.DMA((2,))],
        compiler_params=_cparams("parallel", "arbitrary"))(qkv, qkv, qkv)


def _sb_bwd(qkv, do, saved, heads, *, name, blk=256):
    s = qkv.shape[0]
    b = _tile(s, blk)
    nq = s // b
    hp = min(SB_HEADS_PER_STEP, heads)
    assert heads % hp == 0
    groups = heads // hp
    wide = hp * HEAD_DIM
    scale = HEAD_DIM ** -0.5

    def body(q_ref, k_ref, v_ref, do_ref, w_hbm, dq_ref, dk_ref, dv_ref, dk_acc, dv_acc, stage, sems):
        g = pl.program_id(0)
        i = pl.program_id(1)

        @pl.when(i == 0)
        def _():
            dk_acc[...] = jnp.zeros_like(dk_acc)
            dv_acc[...] = jnp.zeros_like(dv_acc)

        sls = [slice(n * HEAD_DIM, (n + 1) * HEAD_DIM) for n in range(hp)]
        qs = [q_ref[:, sl] for sl in sls]
        dos = [do_ref[:, sl].astype(BF16) for sl in sls]
        tri_excl = _tri(b, lambda r, c: r < c)
        diag = _diag_mask(b)
        r0 = pl.multiple_of(i * b, b)

        def loads(slot, j):
            c0 = pl.multiple_of(j * b, b)
            return [pltpu.make_async_copy(w_hbm.at[w, g * hp + n, pl.ds(r0, b), pl.ds(c0, b)], stage.at[slot, n, w], sems.at[slot])
                    for n in range(hp) for w in range(2)]

        def tile(j, carry, masked):
            slot = j % 2

            @pl.when(j < i)
            def _():
                for cp in loads(1 - slot, j + 1):
                    cp.start()

            for cp in loads(slot, j):
                cp.wait()
            s0 = pl.multiple_of(j * b, b)
            kbs = [k_ref[pl.ds(s0, b), sl] for sl in sls]
            vbs = [v_ref[pl.ds(s0, b), sl] for sl in sls]
            ab = [stage[slot, n, 0] for n in range(hp)]
            ps = [_nt(dob, vb) for dob, vb in zip(dos, vbs)]
            gs = [a.astype(F32) * p for a, p in zip(ab, ps)]
            hs = [_nn(gg.astype(BF16), tri_excl) for gg in gs]
            dzb = []
            for n, (gg, h, (_, cg)) in enumerate(zip(gs, hs, carry)):
                dz = (gg - (gg + (h + cg)) * stage[slot, n, 1].astype(F32)) * scale
                if masked:
                    dz = jnp.where(diag, dz, 0.0)
                dzb.append(dz.astype(BF16))
            out = tuple((dq + _nn(dz, kb), cg + jnp.sum(gg, axis=1, keepdims=True))
                        for dz, kb, gg, (dq, cg) in zip(dzb, kbs, gs, carry))
            for sl, dz, a, q, dob in zip(sls, dzb, ab, qs, dos):
                dk_acc[pl.ds(s0, b), sl] += _tn(dz, q)
                dv_acc[pl.ds(s0, b), sl] += _tn(a, dob)
            return out

        for cp in loads(0, 0):
            cp.start()
        carry = tuple((jnp.zeros((b, HEAD_DIM), F32), jnp.zeros((b, 1), F32)) for _ in range(hp))
        carry = lax.fori_loop(0, i, lambda j, cr: tile(j, cr, False), carry)
        carry = tile(i, carry, True)
        for sl, (dq, _) in zip(sls, carry):
            dq_ref[:, sl] = dq.astype(dq_ref.dtype)

        @pl.when(i == nq - 1)
        def _():
            dk_ref[...] = dk_acc[...].astype(dk_ref.dtype)
            dv_ref[...] = dv_acc[...].astype(dv_ref.dtype)

    row = pl.BlockSpec((b, wide), lambda g, i: (i, g))
    col = lambda off: pl.BlockSpec((s, wide), lambda g, i: (0, off + g), pipeline_mode=pl.Buffered(1))
    shp = jax.ShapeDtypeStruct((s, heads * HEAD_DIM), BF16)
    return pl.pallas_call(
        body, name=name, grid=(groups, nq),
        in_specs=[row, col(groups), col(2 * groups), row, ANY],
        out_specs=[row, col(0), col(0)], out_shape=[shp, shp, shp],
        scratch_shapes=[pltpu.VMEM((s, wide), F32), pltpu.VMEM((s, wide), F32),
                        pltpu.VMEM((2, hp, 2, b, b), BF16), pltpu.SemaphoreType.DMA((2,))],
        compiler_params=_cparams("parallel", "arbitrary"))(qkv, qkv, qkv, do, saved)


def _shift_rows(x, n, fill, *, down):
    rows = x.shape[0]
    if n % 8 == 0:
        pad = jnp.full((n, x.shape[1]), fill, x.dtype)
        return jnp.concatenate([pad, x[:rows - n]], axis=0) if down else jnp.concatenate([x[n:], pad], axis=0)
    t = lax.broadcasted_iota(jnp.int32, x.shape, 0)
    if down:
        return jnp.where(t >= n, pltpu.roll(x, n, 0), fill)
    return jnp.where(t < rows - n, pltpu.roll(x, rows - n, 0), fill)


def _scan_rows(a, b, *, reverse):
    n = 1
    while n < a.shape[0]:
        b = a * _shift_rows(b, n, 0.0, down=not reverse) + b
        a = a * _shift_rows(a, n, 1.0, down=not reverse)
        n *= 2
    return a, b


def _neg_expm1(x):
    p = 1.0 + x * (1.0 / 7.0)
    for k in (6.0, 5.0, 4.0, 3.0, 2.0):
        p = 1.0 + x * (1.0 / k) * p
    return jnp.where(x > -0.25, -(x * p), 1.0 - jnp.exp(x))


def _softplus_neg(lam):
    z = -lam
    e = jnp.exp(-jnp.abs(z))
    u = 1.0 + e
    d = u - 1.0
    log1p_e = jnp.where(d == 0.0, e, jnp.log(u) * (e / jnp.where(d == 0.0, 1.0, d)))
    return jnp.maximum(z, 0.0) + log1p_e


def _lru_gates(xp_ref, w, bias, wa_ref, ba, wx_ref, bx, sp, taps, tm, heads):
    halo = xp_ref.shape[0] - tm
    xc = jnp.broadcast_to(bias, (tm, w.shape[1]))
    for k in range(taps):
        xc = xc + w[k:k + 1, :] * xp_ref[pl.ds(halo - (taps - 1) + k, tm), :]
    xb = xc.astype(BF16)
    pr, pi = [], []
    for n in range(heads):
        xh = xb[:, n * HEAD_DIM:(n + 1) * HEAD_DIM]
        pr.append(_nn(xh, wa_ref[n]))
        pi.append(_nn(xh, wx_ref[n]))
    r = _sigmoid(jnp.concatenate(pr, axis=1) + ba)
    ig = _sigmoid(jnp.concatenate(pi, axis=1) + bx)
    log_a = (-LRU_C) * r * sp
    a = jnp.exp(log_a)
    mult = jnp.sqrt(_neg_expm1(2.0 * log_a))
    return xc, r, ig, a, mult


def _lru_fwd(ub, x_col, conv_w, conv_b, wa, ba, wx, bx, lam, *, name):
    s = ub.shape[0]
    taps, w = conv_w.shape
    heads = w // HEAD_DIM
    tm = _tile(s, 256)
    hb = tm // LRU_HALO

    def body(x_ref, xh_ref, cw_ref, cb_ref, wa_ref, ba_ref, wx_ref, bx_ref, lam_ref, h_ref, xp_ref, carry_ref):
        i = pl.program_id(0)

        @pl.when(i == 0)
        def _():
            carry_ref[...] = jnp.zeros_like(carry_ref)

        xp_ref[0:LRU_HALO, :] = jnp.where(i == 0, 0.0, xh_ref[...])
        xp_ref[LRU_HALO:, :] = x_ref[...]
        sp = _softplus_neg(lam_ref[...])
        xc, _, ig, a, mult = _lru_gates(xp_ref, cw_ref[...], cb_ref[...], wa_ref, ba_ref[...], wx_ref, bx_ref[...],
                                        sp, taps, tm, heads)
        ac, bc = _scan_rows(a, mult * (ig * xc), reverse=False)
        h = ac * carry_ref[0:1, :] + bc
        h_ref[...] = h
        carry_ref[...] = jnp.broadcast_to(h[tm - 1:tm, :], carry_ref.shape)

    full = lambda arr: pl.BlockSpec(arr.shape, lambda i: (0,) * arr.ndim)
    vecs = [conv_w, conv_b.reshape(1, w), wa.astype(BF16), ba.reshape(1, w), wx.astype(BF16), bx.reshape(1, w), lam.reshape(1, w)]
    return pl.pallas_call(
        body, name=name, grid=(s // tm,),
        in_specs=[pl.BlockSpec((tm, w), lambda i: (i, x_col)),
                  pl.BlockSpec((LRU_HALO, w), lambda i: (jnp.maximum(i * hb - 1, 0), x_col))] + [full(v) for v in vecs],
        out_specs=pl.BlockSpec((tm, w), lambda i: (i, 0)),
        out_shape=jax.ShapeDtypeStruct((s, w), F32),
        scratch_shapes=[pltpu.VMEM((tm + LRU_HALO, w), F32), pltpu.VMEM((8, w), F32)],
        compiler_params=_cparams("arbitrary"))(ub, ub, *vecs)


def _lru_bwd(ub, x_col, h, dh, conv_w, conv_b, wa, ba, wx, bx, lam, *, name):
    s = ub.shape[0]
    taps, w = conv_w.shape
    heads = w // HEAD_DIM
    tm = _tile(s, 256)
    hb = tm // LRU_HALO
    nt = s // tm

    def body(x_ref, xh_ref, h_ref, hh_ref, dh_ref, cw_ref, cb_ref, wa_ref, ba_ref, wx_ref, bx_ref, lam_ref,
             dx_ref, dcw_ref, dcb_ref, dwa_ref, dba_ref, dwx_ref, dbx_ref, dlam_ref,
             xp_ref, dxp_ref, dlt_ref, afirst_ref, dxc_next_ref, dsp_ref):
        step = pl.program_id(0)
        i = nt - 1 - step

        @pl.when(step == 0)
        def _():
            for ref in (dcw_ref, dcb_ref, dwa_ref, dba_ref, dwx_ref, dbx_ref, dlam_ref, dlt_ref, dxc_next_ref, dsp_ref):
                ref[...] = jnp.zeros_like(ref)
            afirst_ref[...] = jnp.ones_like(afirst_ref)

        xp_ref[0:LRU_HALO, :] = jnp.where(i == 0, 0.0, xh_ref[...])
        xp_ref[LRU_HALO:, :] = x_ref[...]
        cw = cw_ref[...]
        lam_v = lam_ref[...]
        sp = _softplus_neg(lam_v)
        xc, r, ig, a, mult = _lru_gates(xp_ref, cw, cb_ref[...], wa_ref, ba_ref[...], wx_ref, bx_ref[...], sp, taps, tm, heads)
        rows = lax.broadcasted_iota(jnp.int32, (tm, w), 0)
        a_next = jnp.where(rows == tm - 1, afirst_ref[0:1, :], _shift_rows(a, 1, 1.0, down=False))
        ac, bc = _scan_rows(a_next, dh_ref[...], reverse=True)
        delta = ac * dlt_ref[0:1, :] + bc
        hv = h_ref[...]
        h_last_prev = jnp.where(i == 0, 0.0, hh_ref[LRU_HALO - 1:LRU_HALO, :])
        h_prev = jnp.where(rows == 0, h_last_prev, _shift_rows(hv, 1, 0.0, down=True))
        gated = ig * xc
        da = delta * h_prev
        dmult = delta * gated
        dgated = delta * mult
        dlog_a = da * a - dmult * (a * a) / mult
        dpr = dlog_a * ((-LRU_C) * sp) * r * (1.0 - r)
        dpi = dgated * xc * ig * (1.0 - ig)
        dxc = dgated * ig
        dsp_ref[...] += jnp.sum(dlog_a * ((-LRU_C) * r), axis=0, keepdims=True)
        dba_ref[...] += jnp.sum(dpr, axis=0, keepdims=True)
        dbx_ref[...] += jnp.sum(dpi, axis=0, keepdims=True)
        xb = xc.astype(BF16)
        dprb = dpr.astype(BF16)
        dpib = dpi.astype(BF16)
        back = []
        for n in range(heads):
            sl = slice(n * HEAD_DIM, (n + 1) * HEAD_DIM)
            dwa_ref[n] += _tn(xb[:, sl], dprb[:, sl])
            dwx_ref[n] += _tn(xb[:, sl], dpib[:, sl])
            back.append(_nt(dprb[:, sl], wa_ref[n]) + _nt(dpib[:, sl], wx_ref[n]))
        dxc = dxc + jnp.concatenate(back, axis=1)
        dxp_ref[0:tm, :] = dxc
        dxp_ref[tm:, :] = dxc_next_ref[...]
        dx = jnp.zeros((tm, w), F32)
        dws = []
        for k in range(taps):
            dx = dx + cw[k:k + 1, :] * dxp_ref[pl.ds(taps - 1 - k, tm), :]
            dws.append(jnp.sum(dxc * xp_ref[pl.ds(LRU_HALO - (taps - 1) + k, tm), :], axis=0, keepdims=True))
        dx_ref[...] = dx.astype(dx_ref.dtype)
        dcw_ref[...] += jnp.concatenate(dws, axis=0)
        dcb_ref[...] += jnp.sum(dxc, axis=0, keepdims=True)
        dlt_ref[...] = jnp.broadcast_to(delta[0:1, :], dlt_ref.shape)
        afirst_ref[...] = jnp.broadcast_to(a[0:1, :], afirst_ref.shape)
        dxc_next_ref[...] = dxc[0:LRU_HALO, :]

        @pl.when(step == nt - 1)
        def _():
            dlam_ref[...] = dsp_ref[...] * (-_sigmoid(-lam_v))

    rev = lambda col: pl.BlockSpec((tm, w), lambda st: (nt - 1 - st, col))
    prev = lambda col: pl.BlockSpec((LRU_HALO, w), lambda st: (jnp.maximum((nt - 1 - st) * hb - 1, 0), col))
    full = lambda arr: pl.BlockSpec(arr.shape, lambda st: (0,) * arr.ndim)
    vec = pl.BlockSpec((1, w), lambda st: (0, 0))
    vecs = [conv_w, conv_b.reshape(1, w), wa.astype(BF16), ba.reshape(1, w), wx.astype(BF16), bx.reshape(1, w), lam.reshape(1, w)]
    vshape = jax.ShapeDtypeStruct((1, w), F32)
    return pl.pallas_call(
        body, name=name, grid=(nt,),
        in_specs=[rev(x_col), prev(x_col), rev(0), prev(0), rev(0)] + [full(v) for v in vecs],
        out_specs=[rev(0), full(conv_w), vec, full(wa), vec, full(wx), vec, vec],
        out_shape=[jax.ShapeDtypeStruct((s, w), BF16), jax.ShapeDtypeStruct(conv_w.shape, F32), vshape,
                   jax.ShapeDtypeStruct(wa.shape, F32), vshape, jax.ShapeDtypeStruct(wx.shape, F32), vshape, vshape],
        scratch_shapes=[pltpu.VMEM((tm + LRU_HALO, w), F32), pltpu.VMEM((tm + LRU_HALO, w), F32),
                        pltpu.VMEM((8, w), F32), pltpu.VMEM((8, w), F32), pltpu.VMEM((LRU_HALO, w), F32), pltpu.VMEM((1, w), F32)],
        compiler_params=_cparams("arbitrary"))(ub, ub, h, h, dh, *vecs)


def _group_fwd(y, w, gate):
    r = lax.rsqrt(jnp.mean(y * y, axis=-1, keepdims=True) + RMS_EPS)
    return ((y * r) * w) * (gate * _sigmoid(gate))


def _mix_out_fwd(y_conv, y_attn, y_lru, ua, ub, n_conv, n_attn, n_lru, *, name):
    s, c = y_conv.shape
    wa_ = y_attn.shape[1]
    d = 2 * c + wa_
    tm = _tile(s, 256)
    assert wa_ == 2 * c

    def body(yc_ref, ya_ref, yl_ref, gc_ref, ga_ref, gl_ref, nc_ref, na_ref, nl_ref, o_ref):
        o_ref[:, 0:c] = _group_fwd(yc_ref[...], nc_ref[...], gc_ref[...]).astype(o_ref.dtype)
        o_ref[:, c:c + wa_] = _group_fwd(ya_ref[...], na_ref[...], ga_ref[...]).astype(o_ref.dtype)
        o_ref[:, c + wa_:] = _group_fwd(yl_ref[...], nl_ref[...], gl_ref[...]).astype(o_ref.dtype)

    blk = lambda width, col: pl.BlockSpec((tm, width), lambda i: (i, col))
    vec = lambda width: pl.BlockSpec((1, width), lambda i: (0, 0))
    return pl.pallas_call(
        body, name=name, grid=(s // tm,),
        in_specs=[blk(c, 0), blk(wa_, 0), blk(c, 0), blk(c, 2), blk(wa_, 0), blk(c, 3), vec(c), vec(wa_), vec(c)],
        out_specs=blk(d, 0), out_shape=jax.ShapeDtypeStruct((s, d), BF16),
        compiler_params=_cparams("parallel"))(y_conv, y_attn, y_lru, ua, ub, ub, n_conv.reshape(1, c), n_attn.reshape(1, wa_), n_lru.reshape(1, c))


def _group_bwd(dout, y, w, gate):
    r = lax.rsqrt(jnp.mean(y * y, axis=-1, keepdims=True) + RMS_EPS)
    silu, dsilu = _silu_and_grad(gate)
    dy, dwp = _rms_bwd_math(dout * silu, y, r, w)
    return dy, dout * ((y * r) * w) * dsilu, dwp


def _mix_out_bwd(dy, y_conv, y_attn, y_lru, ua, ub, n_conv, n_attn, n_lru, *, name):
    s, c = y_conv.shape
    wa_ = y_attn.shape[1]
    tm = _tile(s, 256)

    def body(dy_ref, yc_ref, ya_ref, yl_ref, gc_ref, ga_ref, gl_ref, nc_ref, na_ref, nl_ref,
             dyc_ref, dya_ref, dyl_ref, dgc_ref, dga_ref, dgl_ref, dnc_ref, dna_ref, dnl_ref):
        @pl.when(pl.program_id(0) == 0)
        def _():
            for ref in (dnc_ref, dna_ref, dnl_ref):
                ref[...] = jnp.zeros_like(ref)

        groups = ((dy_ref[:, 0:c], yc_ref, nc_ref, gc_ref, dyc_ref, dgc_ref, dnc_ref),
                  (dy_ref[:, c:c + wa_], ya_ref, na_ref, ga_ref, dya_ref, dga_ref, dna_ref),
                  (dy_ref[:, c + wa_:], yl_ref, nl_ref, gl_ref, dyl_ref, dgl_ref, dnl_ref))
        for dout, y_ref, n_ref, g_ref, dyo_ref, dgo_ref, dn_ref in groups:
            dyv, dgv, dwp = _group_bwd(dout, y_ref[...], n_ref[...], g_ref[...])
            dyo_ref[...] = dyv
            dgo_ref[...] = dgv.astype(dgo_ref.dtype)
            dn_ref[...] += jnp.sum(dwp, axis=0, keepdims=True)

    blk = lambda width, col: pl.BlockSpec((tm, width), lambda i: (i, col))
    vec = lambda width: pl.BlockSpec((1, width), lambda i: (0, 0))
    sh = lambda width, dt: jax.ShapeDtypeStruct((s, width), dt)
    vs = lambda width: jax.ShapeDtypeStruct((1, width), F32)
    return pl.pallas_call(
        body, name=name, grid=(s // tm,),
        in_specs=[blk(2 * c + wa_, 0), blk(c, 0), blk(wa_, 0), blk(c, 0), blk(c, 2), blk(wa_, 0), blk(c, 3), vec(c), vec(wa_), vec(c)],
        out_specs=[blk(c, 0), blk(wa_, 0), blk(c, 0), blk(c, 0), blk(wa_, 0), blk(c, 0), vec(c), vec(wa_), vec(c)],
        out_shape=[sh(c, F32), sh(wa_, F32), sh(c, F32), sh(c, BF16), sh(wa_, BF16), sh(c, BF16), vs(c), vs(wa_), vs(c)],
        compiler_params=_cparams("arbitrary"))(dy, y_conv, y_attn, y_lru, ua, ub, ub, n_conv.reshape(1, c), n_attn.reshape(1, wa_), n_lru.reshape(1, c))


def _xattn_probs(qh, kh):
    sc = _nt(qh, kh) * (HEAD_DIM ** -0.5)
    e = jnp.exp(sc - jnp.max(sc, axis=-1, keepdims=True))
    return e / jnp.sum(e, axis=-1, keepdims=True)


def _xattn_fwd(q, kv, *, name):
    s, w = q.shape
    heads = w // HEAD_DIM
    tm = _tile(s, 512)

    def body(q_ref, kv_ref, o_ref):
        for n in range(heads):
            sl = slice(n * HEAD_DIM, (n + 1) * HEAD_DIM)
            p = _xattn_probs(q_ref[:, sl], kv_ref[:, sl])
            o_ref[:, sl] = _nn(p.astype(BF16), kv_ref[:, w + n * HEAD_DIM:w + (n + 1) * HEAD_DIM]).astype(o_ref.dtype)

    return pl.pallas_call(
        body, name=name, grid=(s // tm,),
        in_specs=[pl.BlockSpec((tm, w), lambda i: (i, 0)), pl.BlockSpec(kv.shape, lambda i: (0, 0))],
        out_specs=pl.BlockSpec((tm, w), lambda i: (i, 0)), out_shape=jax.ShapeDtypeStruct((s, w), BF16),
        compiler_params=_cparams("parallel"))(q, kv)


def _xattn_bwd(q, kv, do, *, name):
    s, w = q.shape
    heads = w // HEAD_DIM
    tm = _tile(s, 512)
    scale = HEAD_DIM ** -0.5

    def body(q_ref, kv_ref, do_ref, dq_ref, dkv_ref):
        @pl.when(pl.program_id(0) == 0)
        def _():
            dkv_ref[...] = jnp.zeros_like(dkv_ref)

        for n in range(heads):
            sl = slice(n * HEAD_DIM, (n + 1) * HEAD_DIM)
            vsl = slice(w + n * HEAD_DIM, w + (n + 1) * HEAD_DIM)
            qh, kh, vh, doh = q_ref[:, sl], kv_ref[:, sl], kv_ref[:, vsl], do_ref[:, sl]
            p = _xattn_probs(qh, kh)
            dp = _nt(doh, vh)
            ds = (p * (dp - jnp.sum(dp * p, axis=-1, keepdims=True)) * scale).astype(BF16)
            dq_ref[:, sl] = _nn(ds, kh).astype(dq_ref.dtype)
            dkv_ref[:, sl] += _tn(ds, qh)
            dkv_ref[:, vsl] += _tn(p.astype(BF16), doh)

    row = pl.BlockSpec((tm, w), lambda i: (i, 0))
    kvs = pl.BlockSpec(kv.shape, lambda i: (0, 0))
    return pl.pallas_call(
        body, name=name, grid=(s // tm,), in_specs=[row, kvs, row], out_specs=[row, kvs],
        out_shape=[jax.ShapeDtypeStruct((s, w), BF16), jax.ShapeDtypeStruct(kv.shape, F32)],
        compiler_params=_cparams("arbitrary"))(q, kv, do)


ROW_BLOCK_BYTES = 1 << 20


def _row_tile(rows, cols):
    limit = max(8, ROW_BLOCK_BYTES // (4 * cols))
    t = 8
    while t * 2 <= limit and rows % (t * 2) == 0:
        t *= 2
    assert rows % t == 0
    return t


def _cast_bf16(w, *, name):
    rows, cols = w.shape
    tr = _row_tile(rows, cols)

    def body(w_ref, o_ref):
        o_ref[...] = w_ref[...].astype(BF16)

    blk = pl.BlockSpec((tr, cols), lambda i: (i, 0))
    return pl.pallas_call(body, name=name, grid=(rows // tr,), in_specs=[blk], out_specs=blk,
                          out_shape=jax.ShapeDtypeStruct((rows, cols), BF16), compiler_params=_cparams("parallel"))(w)


def _adamw(w, m, v, gs, *, name):
    rows, cols = w.shape
    tr = _row_tile(rows, cols * 4)
    ng = len(gs)

    def body(*refs):
        w_ref, m_ref, v_ref = refs[:3]
        g_refs = refs[3:3 + ng]
        g_out, d_out, m_out, v_out = refs[3 + ng:]
        g = g_refs[0][...]
        for r in g_refs[1:]:
            g = g + r[...]
        mn = ADAM_B1 * m_ref[...] + (1.0 - ADAM_B1) * g
        vn = ADAM_B2 * v_ref[...] + (1.0 - ADAM_B2) * (g * g)
        m_hat = mn / (1.0 - ADAM_B1 ** ADAM_STEP)
        v_hat = vn / (1.0 - ADAM_B2 ** ADAM_STEP)
        g_out[...] = g
        d_out[...] = -ADAM_LR * (m_hat / (jnp.sqrt(v_hat) + ADAM_EPS) + ADAM_WD * w_ref[...])
        m_out[...] = mn
        v_out[...] = vn

    blk = pl.BlockSpec((tr, cols), lambda i: (i, 0))
    shp = jax.ShapeDtypeStruct((rows, cols), F32)
    return pl.pallas_call(body, name=name, grid=(rows // tr,), in_specs=[blk] * (3 + ng), out_specs=[blk] * 4,
                          out_shape=[shp] * 4, compiler_params=_cparams("parallel"))(w, m, v, *gs)


def _adamw_layers(w, m, v, gs, *, name):
    layers, rows, cols = w.shape
    tr = _row_tile(rows, cols * 4)
    nblk = rows // tr
    ng = len(gs[0])

    def body(*refs):
        w_ref, m_ref, v_ref = refs[:3]
        g_refs = refs[3:3 + layers * ng]
        g_out, d_out, m_out, v_out = refs[3 + layers * ng:]
        l = pl.program_id(0)
        g = jnp.zeros((tr, cols), F32)
        for ll in range(layers):
            gl = g_refs[ll * ng][...]
            for r in g_refs[ll * ng + 1:(ll + 1) * ng]:
                gl = gl + r[...]
            g = jnp.where(l == ll, gl, g)
        mn = ADAM_B1 * m_ref[...] + (1.0 - ADAM_B1) * g
        vn = ADAM_B2 * v_ref[...] + (1.0 - ADAM_B2) * (g * g)
        m_hat = mn / (1.0 - ADAM_B1 ** ADAM_STEP)
        v_hat = vn / (1.0 - ADAM_B2 ** ADAM_STEP)
        g_out[...] = g
        d_out[...] = -ADAM_LR * (m_hat / (jnp.sqrt(v_hat) + ADAM_EPS) + ADAM_WD * w_ref[...])
        m_out[...] = mn
        v_out[...] = vn

    blk = pl.BlockSpec((None, tr, cols), lambda l, i: (l, i, 0))

    def g_spec(ll):
        return pl.BlockSpec((tr, cols), lambda l, i: (jnp.where(l == ll, i, jnp.where(l < ll, 0, nblk - 1)), 0))

    shp = jax.ShapeDtypeStruct(w.shape, F32)
    return pl.pallas_call(
        body, name=name, grid=(layers, nblk), in_specs=[blk] * 3 + [g_spec(ll) for ll in range(layers) for _ in range(ng)],
        out_specs=[blk] * 4, out_shape=[shp] * 4, compiler_params=_cparams("arbitrary", "arbitrary"),
    )(w, m, v, *[g for gl in gs for g in gl])


OTHER_CHIPS = ((1, 0), (0, 1), (1, 1))
N_CHIPS = 4
ANY = pl.BlockSpec(memory_space=pl.ANY)


def _place():
    return lax.axis_index("x"), lax.axis_index("y"), lax.axis_index("c")


def _flip(v, f):
    return 1 - v if f else v


def _part(ref, lead, axis, chip, size):
    idx = list(lead) + [slice(None)] * (len(ref.shape) - len(lead))
    idx[len(lead) + axis] = pl.ds(pl.multiple_of(chip * size, size), size)
    return ref.at[tuple(idx)]


def _allgather_chips(shards, axes, *, name):
    n = len(shards)
    sizes = [sh.shape[ax] for sh, ax in zip(shards, axes)]

    def full_shape(sh, ax):
        return tuple(d * N_CHIPS if i == ax else d for i, d in enumerate(sh.shape))

    def body(*refs):
        ins, outs = refs[:n], refs[n:2 * n]
        send_sems, recv_sems, loc_sems = refs[2 * n:]
        x, y, c = _place()
        me = 2 * x + y
        local = []
        for a in range(n):
            cp = pltpu.make_async_copy(ins[a], _part(outs[a], (), axes[a], me, sizes[a]), loc_sems.at[a])
            cp.start()
            local.append(cp)

        def remote(a, j, chip):
            fx, fy = OTHER_CHIPS[j]
            return pltpu.make_async_remote_copy(
                src_ref=ins[a], dst_ref=_part(outs[a], (), axes[a], chip, sizes[a]),
                send_sem=send_sems.at[a, j], recv_sem=recv_sems.at[a, j],
                device_id=(_flip(x, fx), _flip(y, fy), c), device_id_type=MESH)

        for a in range(n):
            for j in range(len(OTHER_CHIPS)):
                remote(a, j, me).start()
        for a in range(n):
            for j, (fx, fy) in enumerate(OTHER_CHIPS):
                remote(a, j, 2 * _flip(x, fx) + _flip(y, fy)).wait()
        for cp in local:
            cp.wait()

    return pl.pallas_call(
        body, name=name, in_specs=[ANY] * n, out_specs=[ANY] * n,
        out_shape=[jax.ShapeDtypeStruct(full_shape(sh, ax), sh.dtype) for sh, ax in zip(shards, axes)],
        scratch_shapes=[pltpu.SemaphoreType.DMA((n, 3)), pltpu.SemaphoreType.DMA((n, 3)), pltpu.SemaphoreType.DMA((n,))],
    )(*shards)


HBM = pl.BlockSpec(memory_space=pltpu.HBM)
SEM = pl.BlockSpec(memory_space=pltpu.SEMAPHORE)
SPLIT_COPY = pltpu.CompilerParams(has_side_effects=pltpu.SideEffectType.DATAFLOW_SIDE_EFFECTING)


def _cast_place(w, layer, axis, chip, *, name):
    _, rows, cols = w.shape
    tr = _row_tile(rows, cols)
    nblk = rows // tr

    def body(chip_ref, w_ref, o_ref):
        o_ref[...] = w_ref[...].astype(BF16)

    if axis == 1:
        shape = (rows, cols * N_CHIPS)
        o_spec = pl.BlockSpec((tr, cols), lambda i, chip_ref: (i, chip_ref[0]))
    else:
        shape = (rows * N_CHIPS, cols)
        o_spec = pl.BlockSpec((tr, cols), lambda i, chip_ref: (chip_ref[0] * nblk + i, 0))
    return pl.pallas_call(
        body, name=name,
        grid_spec=pltpu.PrefetchScalarGridSpec(
            num_scalar_prefetch=1, grid=(nblk,),
            in_specs=[pl.BlockSpec((None, tr, cols), lambda i, chip_ref: (layer, i, 0))], out_specs=o_spec),
        out_shape=jax.ShapeDtypeStruct(shape, BF16), compiler_params=_cparams("parallel"))(chip, w)


def _gather_copy(refs, a, j, send_sems, recv_sems, *, axes, sizes, arriving):
    x, y, c = _place()
    fx, fy = OTHER_CHIPS[j]
    px, py = _flip(x, fx), _flip(y, fy)
    part = _part(refs[a], (), axes[a], (2 * px + py) if arriving else (2 * x + y), sizes[a])
    k = a * len(OTHER_CHIPS) + j
    return pltpu.make_async_remote_copy(src_ref=part, dst_ref=part, send_sem=send_sems.at[k], recv_sem=recv_sems.at[k],
                                        device_id=(px, py, c), device_id_type=MESH)


def _scatter_copy(srcs, lands, a, j, axes, sizes, send_sems, recv_sems):
    x, y, c = _place()
    fx, fy = OTHER_CHIPS[j]
    px, py = _flip(x, fx), _flip(y, fy)
    k = a * len(OTHER_CHIPS) + j
    return pltpu.make_async_remote_copy(src_ref=_part(srcs[a], (), axes[a], 2 * px + py, sizes[a]), dst_ref=lands[a].at[j],
                                        send_sem=send_sems.at[k], recv_sem=recv_sems.at[k],
                                        device_id=(px, py, c), device_id_type=MESH)


def _split_start(arrs, make_copy, ncopies, dep, *, name):
    n = len(arrs)

    def body(*refs):
        ins = refs[:n]
        send_sems, recv_sems = refs[n + 1], refs[n + 2]
        token = refs[n + 3 + n]
        for a in range(ncopies):
            for j in range(len(OTHER_CHIPS)):
                make_copy(ins, a, j, send_sems, recv_sems).start()
        token[...] = jnp.zeros_like(token)

    sem = pltpu.SemaphoreType.DMA((ncopies * len(OTHER_CHIPS),))
    res = pl.pallas_call(
        body, name=name,
        out_shape=(sem, sem, *[pltpu.HBM(a.shape, a.dtype) for a in arrs], jax.ShapeDtypeStruct((8, LANES), F32)),
        in_specs=[HBM] * n + [pl.BlockSpec(memory_space=pl.ANY)],
        out_specs=(SEM, SEM, *[HBM] * n, pl.BlockSpec(memory_space=pltpu.VMEM)),
        input_output_aliases={a: 2 + a for a in range(n)}, compiler_params=SPLIT_COPY,
    )(*[pltpu.with_memory_space_constraint(a, pltpu.HBM) for a in arrs], dep)
    return res[0], res[1], list(res[2:2 + n]), res[2 + n]


def _split_wait(arrs, send_sems, recv_sems, make_copy, ncopies, after, *, name):
    n = len(arrs)

    def body(*refs):
        ins = refs[:n]
        send, recv = refs[n], refs[n + 1]
        for a in range(ncopies):
            for j in range(len(OTHER_CHIPS)):
                cp = make_copy(ins, a, j, send, recv)
                cp.wait_send()
                cp.wait_recv()

    res = pl.pallas_call(
        body, name=name, out_shape=tuple(pltpu.HBM(a.shape, a.dtype) for a in arrs),
        in_specs=[HBM] * n + [SEM, SEM, pl.BlockSpec(memory_space=pl.ANY)], out_specs=tuple([HBM] * n),
        input_output_aliases={a: a for a in range(n)}, compiler_params=SPLIT_COPY,
    )(*arrs, send_sems, recv_sems, after)
    return list(res)


def _sum_own_and_slots(g, land, axis, chip, *, name):
    slots, rows, cols = land.shape
    tr = _row_tile(rows, cols * 4)
    nblk = rows // tr

    def body(chip_ref, g_ref, l_ref, o_ref):
        acc = g_ref[...].astype(F32)
        for j in range(slots):
            acc = acc + l_ref[j].astype(F32)
        o_ref[...] = acc

    if axis == 1:
        g_spec = pl.BlockSpec((tr, cols), lambda i, chip_ref: (i, chip_ref[0]))
    else:
        g_spec = pl.BlockSpec((tr, cols), lambda i, chip_ref: (chip_ref[0] * nblk + i, 0))
    return pl.pallas_call(
        body, name=name,
        grid_spec=pltpu.PrefetchScalarGridSpec(
            num_scalar_prefetch=1, grid=(nblk,),
            in_specs=[g_spec, pl.BlockSpec((slots, tr, cols), lambda i, chip_ref: (0, i, 0))],
            out_specs=pl.BlockSpec((tr, cols), lambda i, chip_ref: (i, 0))),
        out_shape=jax.ShapeDtypeStruct((rows, cols), F32), compiler_params=_cparams("parallel"))(chip, g, land)


def _swap_sibling(arrs, *, name):
    n = len(arrs)

    def body(*refs):
        ins, outs = refs[:n], refs[n:2 * n]
        send_sems, recv_sems = refs[2 * n:]
        x, y, c = _place()
        copies = [pltpu.make_async_remote_copy(src_ref=ins[a], dst_ref=outs[a], send_sem=send_sems.at[a], recv_sem=recv_sems.at[a],
                                               device_id=(x, y, 1 - c), device_id_type=MESH) for a in range(n)]
        for cp in copies:
            cp.start()
        for cp in copies:
            cp.wait()

    return pl.pallas_call(
        body, name=name, in_specs=[ANY] * n, out_specs=[ANY] * n,
        out_shape=[jax.ShapeDtypeStruct(a.shape, a.dtype) for a in arrs],
        scratch_shapes=[pltpu.SemaphoreType.DMA((n,)), pltpu.SemaphoreType.DMA((n,))],
    )(*arrs)


def _allreduce_small(p, *, name):
    rows, cols = p.shape
    nrel = len(OTHER_CHIPS)

    def body(p_ref, o_ref, sib_ref, land_ref, send_sems, recv_sems):
        x, y, c = _place()
        me = 2 * x + y
        pair = pltpu.make_async_remote_copy(src_ref=p_ref, dst_ref=sib_ref, send_sem=send_sems.at[nrel], recv_sem=recv_sems.at[nrel],
                                            device_id=(x, y, 1 - c), device_id_type=MESH)
        pair.start()
        pair.wait()
        land_ref[nrel] = p_ref[...] + sib_ref[...]
        copies = []
        for j, (fx, fy) in enumerate(OTHER_CHIPS):
            copies.append(pltpu.make_async_remote_copy(src_ref=land_ref.at[nrel], dst_ref=land_ref.at[j], send_sem=send_sems.at[j],
                                                       recv_sem=recv_sems.at[j], device_id=(_flip(x, fx), _flip(y, fy), c),
                                                       device_id_type=MESH))
        for cp in copies:
            cp.start()
        for cp in copies:
            cp.wait()

        def slot_of(chip):
            r = jnp.bitwise_xor(chip, me)
            return jnp.where(r == 0, nrel, jnp.where(r == 2, 0, jnp.where(r == 1, 1, 2)))

        acc = land_ref[slot_of(0)]
        for chip in range(1, N_CHIPS):
            acc = acc + land_ref[slot_of(chip)]
        o_ref[...] = acc

    vm = pl.BlockSpec(memory_space=pltpu.VMEM)
    return pl.pallas_call(
        body, name=name, in_specs=[vm], out_specs=vm, out_shape=jax.ShapeDtypeStruct((rows, cols), F32),
        scratch_shapes=[pltpu.VMEM((rows, cols), F32), pltpu.VMEM((nrel + 1, rows, cols), F32),
                        pltpu.SemaphoreType.DMA((nrel + 1,)), pltpu.SemaphoreType.DMA((nrel + 1,))],
        compiler_params=pltpu.CompilerParams(vmem_limit_bytes=V7X_VMEM_LIMIT_BYTES))(p)


WEIGHTS = ("mix_norm_g", "w_in", "conv_dw_w", "conv_dw_b", "conv_ln_g", "conv_ln_b", "conv_pw_w", "lru_conv_w", "lru_conv_b",
           "lru_wa", "lru_ba", "lru_wx", "lru_bx", "lru_lambda", "out_norm_conv", "out_norm_attn", "out_norm_lru", "w_out",
           "xattn_norm_g", "mem_norm_g", "xattn_wq", "xattn_wkv", "xattn_wo", "final_norm_g")
BIG = {"w_in": 2, "conv_pw_w": 1, "w_out": 1, "xattn_wq": 1, "xattn_wkv": 1, "xattn_wo": 2}
SMALL_SHARDED = {"conv_dw_w": 2, "lru_conv_w": 2}


IN_GROUP = ("w_in", "conv_pw_w")
REST_GROUP = ("w_out", "xattn_wq", "xattn_wkv", "xattn_wo")


def _trunk(x, mem, target, p, fetch, grads_ready):
    depth = p["mix_norm_g"].shape[0]
    c = p["conv_dw_w"].shape[2]
    aw = p["out_norm_attn"].shape[1]
    heads = aw // HEAD_DIM
    saved = []
    for l in range(depth):
        t = f"l{l}_"
        h1, r1 = _rmsnorm_fwd(x, p["mix_norm_g"][l], name=t + "mix_norm")
        wl = dict(fetch(IN_GROUP, l, r1))
        ua = _matmul(h1, wl["w_in"], mode="nn", n=3 * c, b_off=0, name=t + "in_conv")
        qkv = _matmul(h1, wl["w_in"], mode="nn", n=3 * aw, b_off=3 * c, out_dtype=BF16, name=t + "in_qkv")
        ub = _matmul(h1, wl["w_in"], mode="nn", n=aw + 2 * c, b_off=3 * c + 3 * aw, name=t + "in_gates")
        y_conv = _conv_fwd(ua, p["conv_dw_w"][l], p["conv_dw_b"][l], p["conv_ln_g"][l], p["conv_ln_b"][l], wl["conv_pw_w"],
                           name=t + "conv_fwd")
        y_attn, sbw = _sb_fwd(qkv, heads, name=t + "sb_fwd")
        y_lru = _lru_fwd(ub, aw // c, p["lru_conv_w"][l], p["lru_conv_b"][l], p["lru_wa"][l], p["lru_ba"][l], p["lru_wx"][l],
                         p["lru_bx"][l], p["lru_lambda"][l], name=t + "lru_fwd")
        y = _mix_out_fwd(y_conv, y_attn, y_lru, ua, ub, p["out_norm_conv"][l], p["out_norm_attn"][l], p["out_norm_lru"][l],
                         name=t + "mix_out_fwd")
        wl.update(fetch(REST_GROUP, l, y))
        x2 = _matmul(y, wl["w_out"], mode="nn", add=x, name=t + "out_proj")
        h2, r2 = _rmsnorm_fwd(x2, p["xattn_norm_g"][l], name=t + "xattn_norm")
        qx = _matmul(h2, wl["xattn_wq"], mode="nn", out_dtype=BF16, name=t + "xattn_q")
        memn, rm = _rmsnorm_fwd(mem, p["mem_norm_g"][l], name=t + "mem_norm")
        kv = _matmul(memn, wl["xattn_wkv"], mode="nn", out_dtype=BF16, name=t + "xattn_kv")
        o = _xattn_fwd(qx, kv, name=t + "xattn_fwd")
        x3 = _matmul(o, wl["xattn_wo"], mode="nn", add=x2, name=t + "xattn_o")
        saved.append(dict(x=x, h1=h1, r1=r1, ua=ua, qkv=qkv, ub=ub, y_conv=y_conv, y_attn=y_attn, sbw=sbw, y_lru=y_lru, y=y,
                          x2=x2, h2=h2, r2=r2, qx=qx, memn=memn, rm=rm, kv=kv, o=o, w=wl))
        x = x3

    loss, dx, dg_final = _final_loss(x, p["final_norm_g"], target, name="final_loss")
    small = {k: [None] * depth for k in WEIGHTS if k not in BIG and k != "final_norm_g"}
    token = None
    for l in reversed(range(depth)):
        t = f"l{l}_"
        s = saved[l]
        wl = s["w"]
        do = _matmul(dx, wl["xattn_wo"], mode="nt", out_dtype=BF16, dep=token, name=t + "d_xattn_o")
        dwo = _matmul(s["o"], dx, mode="tn", out_dtype=BF16, name=t + "dw_xattn_o")
        dqx, dkv = _xattn_bwd(s["qx"], s["kv"], do, name=t + "xattn_bwd")
        dwq = _matmul(s["h2"], dqx, mode="tn", out_dtype=BF16, name=t + "dw_xattn_q")
        dh2 = _matmul(dqx, wl["xattn_wq"], mode="nt", name=t + "d_xattn_q")
        dx2, dg = _rmsnorm_bwd(dh2, s["x2"], s["r2"], p["xattn_norm_g"][l], dx, name=t + "xattn_norm_bwd")
        small["xattn_norm_g"][l] = dg[0]
        dmemn = _matmul(dkv, wl["xattn_wkv"], mode="nt", name=t + "d_xattn_kv")
        dwkv = _matmul(s["memn"], dkv, mode="tn", out_dtype=BF16, name=t + "dw_xattn_kv")
        _, dg = _rmsnorm_bwd(dmemn, mem, s["rm"], p["mem_norm_g"][l], None, name=t + "mem_norm_bwd")
        small["mem_norm_g"][l] = dg[0]
        dwout = _matmul(s["y"], dx2, mode="tn", out_dtype=BF16, name=t + "dw_out_proj")
        token = grads_ready(REST_GROUP, l, dict(w_out=dwout, xattn_wq=dwq, xattn_wkv=dwkv, xattn_wo=dwo))
        dy = _matmul(dx2, wl["w_out"], mode="nt", dep=token, name=t + "d_out_proj")
        dyc, dya, dyl, dgc, dga, dgl, dnc, dna, dnl = _mix_out_bwd(
            dy, s["y_conv"], s["y_attn"], s["y_lru"], s["ua"], s["ub"], p["out_norm_conv"][l], p["out_norm_attn"][l],
            p["out_norm_lru"][l], name=t + "mix_out_bwd")
        small["out_norm_conv"][l], small["out_norm_attn"][l], small["out_norm_lru"][l] = dnc[0], dna[0], dnl[0]
        dd, dpw, dlg, dlb = _conv_bwd_a(s["ua"], dyc, p["conv_dw_w"][l], p["conv_dw_b"][l], p["conv_ln_g"][l], p["conv_ln_b"][l],
                                        wl["conv_pw_w"], name=t + "conv_bwd_a")
        dval, dglu, ddw, ddb = _conv_bwd_b(s["ua"], dd, p["conv_dw_w"][l], name=t + "conv_bwd_b")
        small["conv_ln_g"][l], small["conv_ln_b"][l], small["conv_dw_w"][l], small["conv_dw_b"][l] = dlg[0], dlb[0], ddw, ddb[0]
        dq, dk, dv = _sb_bwd(s["qkv"], dya, s["sbw"], heads, name=t + "sb_bwd")
        drx, dcw, dcb, dwa, dba, dwx, dbx, dlam = _lru_bwd(
            s["ub"], aw // c, s["y_lru"], dyl, p["lru_conv_w"][l], p["lru_conv_b"][l], p["lru_wa"][l], p["lru_ba"][l],
            p["lru_wx"][l], p["lru_bx"][l], p["lru_lambda"][l], name=t + "lru_bwd")
        small["lru_conv_w"][l], small["lru_conv_b"][l], small["lru_wa"][l], small["lru_ba"][l] = dcw, dcb[0], dwa, dba[0]
        small["lru_wx"][l], small["lru_bx"][l], small["lru_lambda"][l] = dwx, dbx[0], dlam[0]
        du = jnp.concatenate([dval, dglu, dgc, dq, dk, dv, dga, drx, dgl], axis=1)
        dwin = _matmul(s["h1"], du, mode="tn", out_dtype=BF16, name=t + "dw_in")
        token = grads_ready(IN_GROUP, l, dict(w_in=dwin, conv_pw_w=_cast_bf16(dpw, name=t + "cast_dpw")))
        dh1 = _matmul(du, wl["w_in"], mode="nt", dep=token, name=t + "d_in")
        dx, dg = _rmsnorm_bwd(dh1, s["x"], s["r1"], p["mix_norm_g"][l], dx2, name=t + "mix_norm_bwd")
        small["mix_norm_g"][l] = dg[0]
    small = {k: jnp.stack(v) for k, v in small.items()}
    small["final_norm_g"] = dg_final[0]
    return loss, dx, small


def _pack(arrs):
    flat = jnp.concatenate([a.reshape(-1) for a in arrs])
    pad = (-flat.shape[0]) % (8 * LANES)
    return jnp.pad(flat, (0, pad)).reshape(-1, LANES)


def _unpack(packed, like):
    flat = packed.reshape(-1)
    out, at = [], 0
    for a in like:
        out.append(flat[at:at + a.size].reshape(a.shape))
        at += a.size
    return out


def _as_rows(a):
    return a.reshape(-1, a.shape[-1])


def kernel(x, mem, mix_norm_g, w_in, conv_dw_w, conv_dw_b, conv_ln_g, conv_ln_b, conv_pw_w, lru_conv_w, lru_conv_b, lru_wa, lru_ba, lru_wx, lru_bx, lru_lambda, out_norm_conv, out_norm_attn, out_norm_lru, w_out, xattn_norm_g, mem_norm_g, xattn_wq, xattn_wkv, xattn_wo, final_norm_g, loss_target, m_mix_norm_g, m_w_in, m_conv_dw_w, m_conv_dw_b, m_conv_ln_g, m_conv_ln_b, m_conv_pw_w, m_lru_conv_w, m_lru_conv_b, m_lru_wa, m_lru_ba, m_lru_wx, m_lru_bx, m_lru_lambda, m_out_norm_conv, m_out_norm_attn, m_out_norm_lru, m_w_out, m_xattn_norm_g, m_mem_norm_g, m_xattn_wq, m_xattn_wkv, m_xattn_wo, m_final_norm_g, v_mix_norm_g, v_w_in, v_conv_dw_w, v_conv_dw_b, v_conv_ln_g, v_conv_ln_b, v_conv_pw_w, v_lru_conv_w, v_lru_conv_b, v_lru_wa, v_lru_ba, v_lru_wx, v_lru_bx, v_lru_lambda, v_out_norm_conv, v_out_norm_attn, v_out_norm_lru, v_w_out, v_xattn_norm_g, v_mem_norm_g, v_xattn_wq, v_xattn_wkv, v_xattn_wo, v_final_norm_g):
    given = dict(locals())
    w = {k: given[k] for k in WEIGHTS}
    m = {k: given["m_" + k] for k in WEIGHTS}
    v = {k: given["v_" + k] for k in WEIGHTS}
    depth = mix_norm_g.shape[0]
    chip = 2 * lax.axis_index("x") + lax.axis_index("y")

    chip_arr = chip.astype(jnp.int32).reshape(1)

    p = dict(w)
    p.update(zip(SMALL_SHARDED, _allgather_chips([w[k] for k in SMALL_SHARDED], list(SMALL_SHARDED.values()), name="gather_small")))
    axis2d = {k: BIG[k] - 1 for k in BIG}
    groups = [(IN_GROUP, 0), (REST_GROUP, 0)] + [(IN_GROUP + REST_GROUP, l) for l in range(1, depth)]
    pending, token = {}, p[next(iter(SMALL_SHARDED))]
    for names, l in groups:
        arrs = [_cast_place(w[k], l, axis2d[k], chip_arr, name=f"place{l}_{k}") for k in names]
        axes = [axis2d[k] for k in names]
        sizes = [a.shape[ax] // N_CHIPS for a, ax in zip(arrs, axes)]
        start = functools.partial(_gather_copy, axes=axes, sizes=sizes, arriving=False)
        land = functools.partial(_gather_copy, axes=axes, sizes=sizes, arriving=True)
        send, recv, arrs, token = _split_start(arrs, start, len(arrs), token, name=f"gather_start{l}_{names[0]}")
        pending[(names[0], l)] = (names, arrs, send, recv, land)
    last_token = token
    have = {}

    def fetch(group, l, after):
        key = (group[0], l)
        if key in pending:
            names, arrs, send, recv, land = pending.pop(key)
            after = last_token if (group, l) == groups[0] else after
            arrs = _split_wait(arrs, send, recv, land, len(arrs), after, name=f"gather_wait{l}_{names[0]}")
            have.update({(k, l): a for k, a in zip(names, arrs)})
        return {k: have[(k, l)] for k in group}

    flying = []
    held = {}

    def grads_ready(group, l, grads):
        held.update({(k, l): g for k, g in grads.items()})
        if l > 0 and group == REST_GROUP:
            return None
        names = [k for k in (IN_GROUP + REST_GROUP if l > 0 else group)]
        srcs = [held[(k, l)] for k in names]
        axes = [axis2d[k] for k in names]
        sizes = [g.shape[ax] // N_CHIPS for g, ax in zip(srcs, axes)]
        lands = [lax.empty((len(OTHER_CHIPS),) + tuple(sz if i == ax else d for i, d in enumerate(g.shape)), g.dtype)
                 for g, ax, sz in zip(srcs, axes, sizes)]
        n = len(names)
        copy = lambda refs, a, j, ss, rs_: _scatter_copy(refs[:n], refs[n:], a, j, axes, sizes, ss, rs_)
        send, recv, arrs, token = _split_start(srcs + lands, copy, n, jnp.zeros((8, LANES), F32), name=f"scatter_start{l}_{names[0]}")
        flying.append((names, l, axes, arrs, send, recv, copy))
        return token

    loss, grad_x, small = _trunk(x[0], mem[0], loss_target[0], p, fetch, grads_ready)
    loss = lax.psum(loss[0, 0], ("x", "y", "c"))

    sums = {}
    out = {}

    def arrive(entry, after):
        names, l, axes, arrs, send, recv, copy = entry
        n = len(names)
        arrs = _split_wait(arrs, send, recv, copy, n, after, name=f"scatter_wait{l}_{names[0]}")
        for k, ax, g, ld in zip(names, axes, arrs[:n], arrs[n:]):
            ld = ld.reshape((len(OTHER_CHIPS), -1, ld.shape[-1]))
            sums[(k, l)] = _sum_own_and_slots(g, ld, ax, chip_arr, name=f"sum{l}_{k}")
        return sums[(names[-1], l)]

    def update(names):
        mine = [sums[(k, l)] for k in names for l in range(depth)]
        theirs = _swap_sibling(mine, name="swap_sums_" + names[0])
        for i, k in enumerate(names):
            gs = [[mine[i * depth + l], theirs[i * depth + l]] for l in range(depth)]
            out[k] = _adamw_layers(w[k], m[k], v[k], gs, name="adamw_" + k)

    after = grad_x
    for entry in flying[:-1]:
        after = arrive(entry, after)
    update(REST_GROUP)

    small_names = [k for k in WEIGHTS if k not in BIG]
    total = _unpack(_allreduce_small(_pack([small[k] for k in small_names]), name="allreduce_small"), [small[k] for k in small_names])
    g_small = dict(zip(small_names, total))
    for k, ax in SMALL_SHARDED.items():
        size = w[k].shape[ax]
        g_small[k] = lax.dynamic_slice_in_dim(g_small[k], chip * size, size, axis=ax)
    res = _adamw(_pack([w[k] for k in small_names]), _pack([m[k] for k in small_names]), _pack([v[k] for k in small_names]),
                 [_pack([g_small[k] for k in small_names])], name="adamw_small")
    last = res[0]
    res = [_unpack(r, [w[k] for k in small_names]) for r in res]
    for i, k in enumerate(small_names):
        out[k] = [r[i] for r in res]

    arrive(flying[-1], last)
    update(IN_GROUP)

    outs = [loss, grad_x[None]]
    for part in range(4):
        outs += [out[k][part] for k in WEIGHTS]
    return tuple(outs)
```

```python
import functools

import jax
import jax.numpy as jnp
from jax import lax
from jax.experimental import pallas as pl
from jax.experimental.pallas import tpu as pltpu

F32 = jnp.float32
BF16 = jnp.bfloat16
MESH = pl.DeviceIdType.MESH

V7X_VMEM_LIMIT_BYTES = 56 * 1024 * 1024
LANES = 128
HEAD_DIM = 128
LRU_C = 8.0
RMS_EPS = 1e-6
LN_EPS = 1e-5
CONV_HALO = 32
LRU_HALO = 8
ADAM_LR = 0.001
ADAM_B1 = 0.9
ADAM_B2 = 0.999
ADAM_EPS = 1e-08
ADAM_WD = 0.01
ADAM_STEP = 10


def _cparams(*sem):
    return pltpu.CompilerParams(dimension_semantics=sem, vmem_limit_bytes=V7X_VMEM_LIMIT_BYTES)


def _tile(n, pref):
    if n <= pref:
        return n
    for t in range(pref - pref % LANES, 0, -LANES):
        if n % t == 0:
            return t
    t = pref
    while n % t:
        t //= 2
    return t


def _dot(a, b, dims):
    return lax.dot_general(a, b, (dims, ((), ())), preferred_element_type=F32)


def _nn(a, b):
    return _dot(a, b, ((1,), (0,)))


def _nt(a, b):
    return _dot(a, b, ((1,), (1,)))


def _tn(a, b):
    return _dot(a, b, ((0,), (0,)))


def _sigmoid(x):
    return jax.nn.sigmoid(x)


def _silu_and_grad(x):
    s = _sigmoid(x)
    return x * s, s * (1.0 + x * (1.0 - s))


def _matmul(a, b, *, mode, name, layer=None, n=None, b_off=0, add=None, dep=None, out_dtype=F32, tm=1024, tn=1024, tk=2048):
    bshape = b.shape if layer is None else b.shape[1:]
    if mode == "nn":
        m, k = a.shape
        n = bshape[1] if n is None else n
    elif mode == "nt":
        m, k = a.shape
        n = bshape[0]
    else:
        k, m = a.shape
        n = bshape[1]
    tm, tk = _tile(m, tm), _tile(k, tk)
    tn = _tile(n, tn)
    while b_off % tn or n % tn:
        tn -= LANES
    nk = k // tk
    off = b_off // tn
    lead = () if layer is None else (None,)
    li = () if layer is None else (layer,)
    if mode == "nn":
        a_spec = pl.BlockSpec((tm, tk), lambda i, j, kk: (i, kk))
        b_spec = pl.BlockSpec(lead + (tk, tn), lambda i, j, kk: li + (kk, j + off))
        dot = _nn
    elif mode == "nt":
        a_spec = pl.BlockSpec((tm, tk), lambda i, j, kk: (i, kk))
        b_spec = pl.BlockSpec(lead + (tn, tk), lambda i, j, kk: li + (j, kk))
        dot = _nt
    else:
        a_spec = pl.BlockSpec((tk, tm), lambda i, j, kk: (kk, i))
        b_spec = pl.BlockSpec(lead + (tk, tn), lambda i, j, kk: li + (kk, j))
        dot = _tn
    o_spec = pl.BlockSpec((tm, tn), lambda i, j, kk: (i, j))
    has_add = add is not None

    def body(*refs):
        refs = refs[:-3] + refs[-2:] if dep is not None else refs
        if has_add:
            a_ref, b_ref, add_ref, o_ref, acc_ref = refs
        else:
            a_ref, b_ref, o_ref, acc_ref = refs
        kk = pl.program_id(2)
        part = dot(a_ref[...].astype(BF16), b_ref[...].astype(BF16))

        @pl.when(kk == 0)
        def _():
            acc_ref[...] = part

        @pl.when(kk > 0)
        def _():
            acc_ref[...] += part

        @pl.when(kk == nk - 1)
        def _():
            r = acc_ref[...]
            if has_add:
                r = r + add_ref[...]
            o_ref[...] = r.astype(o_ref.dtype)

    ins = [a, b] + ([add] if has_add else [])
    specs = [a_spec, b_spec] + ([o_spec] if has_add else [])
    if dep is not None:
        ins.append(dep)
        specs.append(pl.BlockSpec((8, LANES), lambda i, j, kk: (0, 0)))
    return pl.pallas_call(
        body, name=name, grid=(m // tm, n // tn, nk), in_specs=specs, out_specs=o_spec,
        out_shape=jax.ShapeDtypeStruct((m, n), out_dtype), scratch_shapes=[pltpu.VMEM((tm, tn), F32)],
        compiler_params=_cparams("parallel", "parallel", "arbitrary"))(*ins)


def _rmsnorm_fwd(x, g, *, name):
    s, d = x.shape
    tm = _tile(s, 256)

    def body(x_ref, g_ref, h_ref, r_ref):
        xf = x_ref[...]
        r = lax.rsqrt(jnp.mean(xf * xf, axis=-1, keepdims=True) + RMS_EPS)
        h_ref[...] = ((xf * r) * g_ref[...]).astype(h_ref.dtype)
        r_ref[...] = r

    return pl.pallas_call(
        body, name=name, grid=(s // tm,),
        in_specs=[pl.BlockSpec((tm, d), lambda i: (i, 0)), pl.BlockSpec((1, d), lambda i: (0, 0))],
        out_specs=[pl.BlockSpec((tm, d), lambda i: (i, 0)), pl.BlockSpec((tm, 1), lambda i: (i, 0))],
        out_shape=[jax.ShapeDtypeStruct((s, d), BF16), jax.ShapeDtypeStruct((s, 1), F32)],
        compiler_params=_cparams("parallel"))(x, g.reshape(1, d))


def _rms_bwd_math(dh, x, r, g):
    xr = x * r
    dyg = dh * g
    m = jnp.mean(dyg * xr, axis=-1, keepdims=True)
    return r * (dyg - xr * m), dh * xr


def _rmsnorm_bwd(dh, x, r, g, dres, *, name):
    s, d = x.shape
    tm = _tile(s, 256)
    has_res = dres is not None

    def body(*refs):
        if has_res:
            dh_ref, x_ref, r_ref, g_ref, res_ref, dx_ref, dg_ref = refs
        else:
            dh_ref, x_ref, r_ref, g_ref, dx_ref, dg_ref = refs
        dx, dgp = _rms_bwd_math(dh_ref[...].astype(F32), x_ref[...], r_ref[...], g_ref[...])
        if has_res:
            dx = dx + res_ref[...]
        dx_ref[...] = dx

        @pl.when(pl.program_id(0) == 0)
        def _():
            dg_ref[...] = jnp.zeros_like(dg_ref)

        dg_ref[...] += jnp.sum(dgp, axis=0, keepdims=True)

    row = pl.BlockSpec((tm, d), lambda i: (i, 0))
    vec = pl.BlockSpec((1, d), lambda i: (0, 0))
    ins = [dh, x, r, g.reshape(1, d)] + ([dres] if has_res else [])
    specs = [row, row, pl.BlockSpec((tm, 1), lambda i: (i, 0)), vec] + ([row] if has_res else [])
    return pl.pallas_call(
        body, name=name, grid=(s // tm,), in_specs=specs, out_specs=[row, vec],
        out_shape=[jax.ShapeDtypeStruct((s, d), F32), jax.ShapeDtypeStruct((1, d), F32)],
        compiler_params=_cparams("arbitrary"))(*ins)


def _final_loss(x, g, target, *, name):
    s, d = x.shape
    tm = _tile(s, 256)

    def body(x_ref, g_ref, t_ref, loss_ref, dx_ref, dg_ref):
        xf = x_ref[...]
        gv = g_ref[...]
        r = lax.rsqrt(jnp.mean(xf * xf, axis=-1, keepdims=True) + RMS_EPS)
        diff = (xf * r) * gv - t_ref[...]
        part = 0.5 * jnp.sum(jnp.mean(diff * diff, axis=-1, keepdims=True))
        dx, dgp = _rms_bwd_math(diff * (1.0 / d), xf, r, gv)
        dx_ref[...] = dx

        @pl.when(pl.program_id(0) == 0)
        def _():
            dg_ref[...] = jnp.zeros_like(dg_ref)
            loss_ref[...] = jnp.zeros_like(loss_ref)

        dg_ref[...] += jnp.sum(dgp, axis=0, keepdims=True)
        loss_ref[...] += part

    row = pl.BlockSpec((tm, d), lambda i: (i, 0))
    vec = pl.BlockSpec((1, d), lambda i: (0, 0))
    return pl.pallas_call(
        body, name=name, grid=(s // tm,), in_specs=[row, vec, row],
        out_specs=[pl.BlockSpec((8, LANES), lambda i: (0, 0)), row, vec],
        out_shape=[jax.ShapeDtypeStruct((8, LANES), F32), jax.ShapeDtypeStruct((s, d), F32), jax.ShapeDtypeStruct((1, d), F32)],
        compiler_params=_cparams("arbitrary"))(x, g.reshape(1, d), target)


SUBLANES = 8
TAP_GROUPS = 4


def _shift_scratch(tm, c):
    return pltpu.VMEM((SUBLANES - 1, tm + CONV_HALO - SUBLANES, c), F32)


def _shift_copies(src_ref, sh_ref):
    rows = sh_ref.shape[1]
    for r in range(1, SUBLANES):
        sh_ref[r - 1] = src_ref[pl.ds(r, rows), :]


def _read_shifted(src_ref, sh_ref, off, r0):
    r = off % SUBLANES
    base = off - r + r0
    return src_ref[pl.ds(base, SUBLANES), :] if r == 0 else sh_ref[r - 1, pl.ds(base, SUBLANES), :]


def _tap_rows(w):
    return [jnp.broadcast_to(w[k:k + 1, :], (SUBLANES, w.shape[1])) for k in range(w.shape[0])]


def _tap_sum(src_ref, sh_ref, wk, offs, init, tm):
    out = []
    for r0 in range(0, tm, SUBLANES * TAP_GROUPS):
        accs = [init] * TAP_GROUPS
        for wv, off in zip(wk, offs):
            accs = [acc + wv * _read_shifted(src_ref, sh_ref, off, r0 + SUBLANES * g) for g, acc in enumerate(accs)]
        out += accs
    return jnp.concatenate(out, axis=0)


def _conv_taps(gp_ref, sh_ref, w, bias, taps, tm):
    halo = gp_ref.shape[0] - tm
    _shift_copies(gp_ref, sh_ref)
    offs = [halo - (taps - 1) + k for k in range(taps)]
    return _tap_sum(gp_ref, sh_ref, _tap_rows(w), offs, jnp.broadcast_to(bias, (SUBLANES, w.shape[1])), tm)


def _conv_core(val, glu, valh, gluh, first, gp_ref, sh_ref, w, bias, lg, lb, taps, tm):
    sg = _sigmoid(glu)
    g = val * sg
    gh = jnp.where(first, 0.0, valh * _sigmoid(gluh))
    gp_ref[0:CONV_HALO, :] = gh
    gp_ref[CONV_HALO:, :] = g
    d = _conv_taps(gp_ref, sh_ref, w, bias, taps, tm)
    mu = jnp.mean(d, axis=-1, keepdims=True)
    dc = d - mu
    rstd = lax.rsqrt(jnp.mean(dc * dc, axis=-1, keepdims=True) + LN_EPS)
    xhat = dc * rstd
    ln = xhat * lg + lb
    return sg, xhat, rstd, ln


def _conv_fwd(ua, dw_w, dw_b, ln_g, ln_b, pw, *, name):
    s = ua.shape[0]
    taps, c = dw_w.shape
    tm = _tile(s, 512)
    hb = tm // CONV_HALO

    def body(val_ref, glu_ref, valh_ref, gluh_ref, w_ref, b_ref, lg_ref, lb_ref, pw_ref, y_ref, gp_ref, sh_ref):
        first = pl.program_id(0) == 0
        _, _, _, ln = _conv_core(val_ref[...], glu_ref[...], valh_ref[...], gluh_ref[...], first, gp_ref, sh_ref,
                                 w_ref[...], b_ref[...], lg_ref[...], lb_ref[...], taps, tm)
        sw = ln * _sigmoid(ln)
        y_ref[...] = _nn(sw.astype(BF16), pw_ref[...])

    cur = lambda col: pl.BlockSpec((tm, c), lambda i: (i, col))
    prev = lambda col: pl.BlockSpec((CONV_HALO, c), lambda i: (jnp.maximum(i * hb - 1, 0), col))
    full = lambda a: pl.BlockSpec(a.shape, lambda i: (0,) * a.ndim)
    vecs = [dw_w, dw_b.reshape(1, c), ln_g.reshape(1, c), ln_b.reshape(1, c), pw]
    return pl.pallas_call(
        body, name=name, grid=(s // tm,),
        in_specs=[cur(0), cur(1), prev(0), prev(1)] + [full(a) for a in vecs],
        out_specs=pl.BlockSpec((tm, c), lambda i: (i, 0)),
        out_shape=jax.ShapeDtypeStruct((s, c), F32),
        scratch_shapes=[pltpu.VMEM((tm + CONV_HALO, c), F32), _shift_scratch(tm, c)],
        compiler_params=_cparams("parallel"))(ua, ua, ua, ua, *vecs)


def _conv_bwd_a(ua, dy, dw_w, dw_b, ln_g, ln_b, pw, *, name):
    s = ua.shape[0]
    taps, c = dw_w.shape
    tm = _tile(s, 512)
    hb = tm // CONV_HALO

    def body(val_ref, glu_ref, valh_ref, gluh_ref, dy_ref, w_ref, b_ref, lg_ref, lb_ref, pw_ref,
             dd_ref, dpw_ref, dlg_ref, dlb_ref, gp_ref, sh_ref):
        first = pl.program_id(0) == 0
        lg = lg_ref[...]
        _, xhat, rstd, ln = _conv_core(val_ref[...], glu_ref[...], valh_ref[...], gluh_ref[...], first, gp_ref, sh_ref,
                                       w_ref[...], b_ref[...], lg, lb_ref[...], taps, tm)
        sw, dsw = _silu_and_grad(ln)
        dyb = dy_ref[...].astype(BF16)
        ds = _nt(dyb, pw_ref[...])
        dln = ds * dsw
        dxhat = dln * lg
        m1 = jnp.mean(dxhat, axis=-1, keepdims=True)
        m2 = jnp.mean(dxhat * xhat, axis=-1, keepdims=True)
        dd_ref[...] = rstd * (dxhat - m1 - xhat * m2)

        @pl.when(first)
        def _():
            dpw_ref[...] = jnp.zeros_like(dpw_ref)
            dlg_ref[...] = jnp.zeros_like(dlg_ref)
            dlb_ref[...] = jnp.zeros_like(dlb_ref)

        dpw_ref[...] += _tn(sw.astype(BF16), dyb)
        dlg_ref[...] += jnp.sum(dln * xhat, axis=0, keepdims=True)
        dlb_ref[...] += jnp.sum(dln, axis=0, keepdims=True)

    cur = lambda col: pl.BlockSpec((tm, c), lambda i: (i, col))
    prev = lambda col: pl.BlockSpec((CONV_HALO, c), lambda i: (jnp.maximum(i * hb - 1, 0), col))
    full = lambda a: pl.BlockSpec(a.shape, lambda i: (0,) * a.ndim)
    vec = pl.BlockSpec((1, c), lambda i: (0, 0))
    vecs = [dw_w, dw_b.reshape(1, c), ln_g.reshape(1, c), ln_b.reshape(1, c), pw]
    return pl.pallas_call(
        body, name=name, grid=(s // tm,),
        in_specs=[cur(0), cur(1), prev(0), prev(1), pl.BlockSpec((tm, c), lambda i: (i, 0))] + [full(a) for a in vecs],
        out_specs=[pl.BlockSpec((tm, c), lambda i: (i, 0)), pl.BlockSpec((c, c), lambda i: (0, 0)), vec, vec],
        out_shape=[jax.ShapeDtypeStruct((s, c), F32), jax.ShapeDtypeStruct((c, c), F32),
                   jax.ShapeDtypeStruct((1, c), F32), jax.ShapeDtypeStruct((1, c), F32)],
        scratch_shapes=[pltpu.VMEM((tm + CONV_HALO, c), F32), _shift_scratch(tm, c)],
        compiler_params=_cparams("arbitrary"))(ua, ua, ua, ua, dy, *vecs)


def _conv_bwd_b(ua, dd, dw_w, *, name):
    s = ua.shape[0]
    taps, c = dw_w.shape
    tm = _tile(s, 512)
    hb = tm // CONV_HALO
    nt = s // tm

    def body(val_ref, glu_ref, valh_ref, gluh_ref, dd_ref, ddn_ref, w_ref, dval_ref, dglu_ref, dw_ref, db_ref,
             gp_ref, ddp_ref, shg_ref, shd_ref):
        i = pl.program_id(0)
        val = val_ref[...]
        sg = _sigmoid(glu_ref[...])
        gp_ref[0:CONV_HALO, :] = jnp.where(i == 0, 0.0, valh_ref[...] * _sigmoid(gluh_ref[...]))
        gp_ref[CONV_HALO:, :] = val * sg
        dd = dd_ref[...]
        ddp_ref[0:tm, :] = dd
        ddp_ref[tm:, :] = jnp.where(i == nt - 1, 0.0, ddn_ref[...])
        _shift_copies(gp_ref, shg_ref)
        _shift_copies(ddp_ref, shd_ref)
        zero = jnp.zeros((SUBLANES, c), F32)
        dg = _tap_sum(ddp_ref, shd_ref, _tap_rows(w_ref[...]), [taps - 1 - k for k in range(taps)], zero, tm)
        dws = []
        for k in range(taps):
            accs = [zero] * TAP_GROUPS
            for n, r0 in enumerate(range(0, tm, SUBLANES)):
                accs[n % TAP_GROUPS] = accs[n % TAP_GROUPS] + ddp_ref[pl.ds(r0, SUBLANES), :] * _read_shifted(
                    gp_ref, shg_ref, CONV_HALO - (taps - 1) + k, r0)
            dws.append(jnp.sum(sum(accs[1:], accs[0]), axis=0, keepdims=True))
        dval_ref[...] = (dg * sg).astype(dval_ref.dtype)
        dglu_ref[...] = (dg * val * sg * (1.0 - sg)).astype(dglu_ref.dtype)

        @pl.when(i == 0)
        def _():
            dw_ref[...] = jnp.zeros_like(dw_ref)
            db_ref[...] = jnp.zeros_like(db_ref)

        dw_ref[...] += jnp.concatenate(dws, axis=0)
        db_ref[...] += jnp.sum(dd, axis=0, keepdims=True)

    cur = lambda col: pl.BlockSpec((tm, c), lambda i: (i, col))
    prev = lambda col: pl.BlockSpec((CONV_HALO, c), lambda i: (jnp.maximum(i * hb - 1, 0), col))
    nxt = pl.BlockSpec((CONV_HALO, c), lambda i: (jnp.minimum((i + 1) * hb, s // CONV_HALO - 1), 0))
    return pl.pallas_call(
        body, name=name, grid=(nt,),
        in_specs=[cur(0), cur(1), prev(0), prev(1), pl.BlockSpec((tm, c), lambda i: (i, 0)), nxt,
                  pl.BlockSpec((taps, c), lambda i: (0, 0))],
        out_specs=[pl.BlockSpec((tm, c), lambda i: (i, 0)), pl.BlockSpec((tm, c), lambda i: (i, 0)),
                   pl.BlockSpec((taps, c), lambda i: (0, 0)), pl.BlockSpec((1, c), lambda i: (0, 0))],
        out_shape=[jax.ShapeDtypeStruct((s, c), BF16), jax.ShapeDtypeStruct((s, c), BF16),
                   jax.ShapeDtypeStruct((taps, c), F32), jax.ShapeDtypeStruct((1, c), F32)],
        scratch_shapes=[pltpu.VMEM((tm + CONV_HALO, c), F32), pltpu.VMEM((tm + CONV_HALO, c), F32),
                        _shift_scratch(tm, c), _shift_scratch(tm, c)],
        compiler_params=_cparams("arbitrary"))(ua, ua, ua, ua, dd, dd, dw_w)


LOG2_E = 1.4426950408889634
SB_HEADS_PER_STEP = 4


def _sb_logs(qk, mask):
    z = qk * (HEAD_DIM ** -0.5 * LOG2_E)
    ls = jnp.minimum(z, 0.0) - jnp.log2(1.0 + jnp.exp2(-jnp.abs(z)))
    lm = ls - z
    if mask is not None:
        lm = jnp.where(mask, lm, 0.0)
    return ls, lm


def _diag_mask(b):
    return lax.broadcasted_iota(jnp.int32, (b, b), 1) < lax.broadcasted_iota(jnp.int32, (b, b), 0)


def _split_dot(x, tri):
    hi = x.astype(BF16)
    lo = (x - hi.astype(F32)).astype(BF16)
    return _nn(hi, tri) + _nn(lo, tri)


def _tri(bk, cmp):
    r = lax.broadcasted_iota(jnp.int32, (bk, bk), 0)
    c = lax.broadcasted_iota(jnp.int32, (bk, bk), 1)
    return cmp(r, c).astype(BF16)


def _sb_fwd(qkv, heads, *, name, blk=256):
    s = qkv.shape[0]
    b = _tile(s, blk)
    nq = s // b
    hp = min(SB_HEADS_PER_STEP, heads)
    assert heads % hp == 0
    groups = heads // hp
    wide = hp * HEAD_DIM

    def body(q_ref, k_ref, v_ref, o_ref, w_hbm, stage, sems):
        g = pl.program_id(0)
        i = pl.program_id(1)
        sls = [slice(n * HEAD_DIM, (n + 1) * HEAD_DIM) for n in range(hp)]
        qs = [q_ref[:, sl] for sl in sls]
        tri = _tri(b, lambda r, c: r > c)
        diag = _diag_mask(b)
        r0 = pl.multiple_of(i * b, b)

        def saves(slot, j):
            c0 = pl.multiple_of(j * b, b)
            return [pltpu.make_async_copy(stage.at[slot, n, w], w_hbm.at[w, g * hp + n, pl.ds(r0, b), pl.ds(c0, b)], sems.at[slot])
                    for n in range(hp) for w in range(2)]

        def tile(t, j, carry, masked):
            slot = t % 2
            if not masked:
                @pl.when(t >= 2)
                def _():
                    for cp in saves(slot, j):
                        cp.wait()

            s0 = pl.multiple_of(j * b, b)
            kbs = [k_ref[pl.ds(s0, b), sl] for sl in sls]
            vbs = [v_ref[pl.ds(s0, b), sl] for sl in sls]
            zs = [_nt(q, kb) for q, kb in zip(qs, kbs)]
            sc = [_sb_logs(z, diag if masked else None) for z in zs]
            after = [_split_dot(lm, tri) for _, lm in sc]
            out = []
            for n, ((ls, lm), af, (acc, c)) in enumerate(zip(sc, after, carry)):
                a = jnp.exp2(ls + (af + c))
                if masked:
                    a = jnp.where(diag, a, 0.0)
                ab = a.astype(BF16)
                stage[slot, n, 0] = ab
                stage[slot, n, 1] = jnp.exp2(ls).astype(BF16)
                out.append((ab, acc, c + jnp.sum(lm, axis=1, keepdims=True)))
            for cp in saves(slot, j):
                cp.start()
            return tuple((acc + _nn(ab, vb), c) for vb, (ab, acc, c) in zip(vbs, out))

        zero = tuple((jnp.zeros((b, HEAD_DIM), F32), jnp.zeros((b, 1), F32)) for _ in range(hp))
        carry = tile(0, i, zero, True)
        carry = lax.fori_loop(0, i, lambda jj, cr: tile(jj + 1, i - 1 - jj, cr, False), carry)
        for sl, (acc, _) in zip(sls, carry):
            o_ref[:, sl] = acc
        for cp in saves(i % 2, 0):
            cp.wait()

        @pl.when(i >= 1)
        def _():
            for cp in saves((i + 1) % 2, 0):
                cp.wait()

    return pl.pallas_call(
        body, name=name, grid=(groups, nq),
        in_specs=[pl.BlockSpec((b, wide), lambda g, i: (i, g)),
                  pl.BlockSpec((s, wide), lambda g, i: (0, groups + g)),
                  pl.BlockSpec((s, wide), lambda g, i: (0, 2 * groups + g))],
        out_specs=[pl.BlockSpec((b, wide), lambda g, i: (i, g)), ANY],
        out_shape=[jax.ShapeDtypeStruct((s, heads * HEAD_DIM), F32), jax.ShapeDtypeStruct((2, heads, s, s), BF16)],
        scratch_shapes=[pltpu.VMEM((2, hp, 2, b, b), BF16), pltpu.SemaphoreType.DMA((2,))],
        compiler_params=_cparams("parallel", "arbitrary"))(qkv, qkv, qkv)


def _sb_bwd(qkv, do, saved, heads, *, name, blk=256):
    s = qkv.shape[0]
    b = _tile(s, blk)
    nq = s // b
    hp = min(SB_HEADS_PER_STEP, heads)
    assert heads % hp == 0
    groups = heads // hp
    wide = hp * HEAD_DIM
    scale = HEAD_DIM ** -0.5

    def body(q_ref, k_ref, v_ref, do_ref, w_hbm, dq_ref, dk_ref, dv_ref, dk_acc, dv_acc, stage, sems):
        g = pl.program_id(0)
        i = pl.program_id(1)

        @pl.when(i == 0)
        def _():
            dk_acc[...] = jnp.zeros_like(dk_acc)
            dv_acc[...] = jnp.zeros_like(dv_acc)

        sls = [slice(n * HEAD_DIM, (n + 1) * HEAD_DIM) for n in range(hp)]
        qs = [q_ref[:, sl] for sl in sls]
        dos = [do_ref[:, sl].astype(BF16) for sl in sls]
        tri_excl = _tri(b, lambda r, c: r < c)
        diag = _diag_mask(b)
        r0 = pl.multiple_of(i * b, b)

        def loads(slot, j):
            c0 = pl.multiple_of(j * b, b)
            return [pltpu.make_async_copy(w_hbm.at[w, g * hp + n, pl.ds(r0, b), pl.ds(c0, b)], stage.at[slot, n, w], sems.at[slot])
                    for n in range(hp) for w in range(2)]

        def tile(j, carry, masked):
            slot = j % 2

            @pl.when(j < i)
            def _():
                for cp in loads(1 - slot, j + 1):
                    cp.start()

            for cp in loads(slot, j):
                cp.wait()
            s0 = pl.multiple_of(j * b, b)
            kbs = [k_ref[pl.ds(s0, b), sl] for sl in sls]
            vbs = [v_ref[pl.ds(s0, b), sl] for sl in sls]
            ab = [stage[slot, n, 0] for n in range(hp)]
            ps = [_nt(dob, vb) for dob, vb in zip(dos, vbs)]
            gs = [a.astype(F32) * p for a, p in zip(ab, ps)]
            hs = [_nn(gg.astype(BF16), tri_excl) for gg in gs]
            dzb = []
            for n, (gg, h, (_, cg)) in enumerate(zip(gs, hs, carry)):
                dz = (gg - (gg + (h + cg)) * stage[slot, n, 1].astype(F32)) * scale
                if masked:
                    dz = jnp.where(diag, dz, 0.0)
                dzb.append(dz.astype(BF16))
            out = tuple((dq + _nn(dz, kb), cg + jnp.sum(gg, axis=1, keepdims=True))
                        for dz, kb, gg, (dq, cg) in zip(dzb, kbs, gs, carry))
            for sl, dz, a, q, dob in zip(sls, dzb, ab, qs, dos):
                dk_acc[pl.ds(s0, b), sl] += _tn(dz, q)
                dv_acc[pl.ds(s0, b), sl] += _tn(a, dob)
            return out

        for cp in loads(0, 0):
            cp.start()
        carry = tuple((jnp.zeros((b, HEAD_DIM), F32), jnp.zeros((b, 1), F32)) for _ in range(hp))
        carry = lax.fori_loop(0, i, lambda j, cr: tile(j, cr, False), carry)
        carry = tile(i, carry, True)
        for sl, (dq, _) in zip(sls, carry):
            dq_ref[:, sl] = dq.astype(dq_ref.dtype)

        @pl.when(i == nq - 1)
        def _():
            dk_ref[...] = dk_acc[...].astype(dk_ref.dtype)
            dv_ref[...] = dv_acc[...].astype(dv_ref.dtype)

    row = pl.BlockSpec((b, wide), lambda g, i: (i, g))
    col = lambda off: pl.BlockSpec((s, wide), lambda g, i: (0, off + g), pipeline_mode=pl.Buffered(1))
    shp = jax.ShapeDtypeStruct((s, heads * HEAD_DIM), BF16)
    return pl.pallas_call(
        body, name=name, grid=(groups, nq),
        in_specs=[row, col(groups), col(2 * groups), row, ANY],
        out_specs=[row, col(0), col(0)], out_shape=[shp, shp, shp],
        scratch_shapes=[pltpu.VMEM((s, wide), F32), pltpu.VMEM((s, wide), F32),
                        pltpu.VMEM((2, hp, 2, b, b), BF16), pltpu.SemaphoreType.DMA((2,))],
        compiler_params=_cparams("parallel", "arbitrary"))(qkv, qkv, qkv, do, saved)


def _shift_rows(x, n, fill, *, down):
    rows = x.shape[0]
    if n % 8 == 0:
        pad = jnp.full((n, x.shape[1]), fill, x.dtype)
        return jnp.concatenate([pad, x[:rows - n]], axis=0) if down else jnp.concatenate([x[n:], pad], axis=0)
    t = lax.broadcasted_iota(jnp.int32, x.shape, 0)
    if down:
        return jnp.where(t >= n, pltpu.roll(x, n, 0), fill)
    return jnp.where(t < rows - n, pltpu.roll(x, rows - n, 0), fill)


def _scan_rows(a, b, *, reverse):
    n = 1
    while n < a.shape[0]:
        b = a * _shift_rows(b, n, 0.0, down=not reverse) + b
        a = a * _shift_rows(a, n, 1.0, down=not reverse)
        n *= 2
    return a, b


def _neg_expm1(x):
    p = 1.0 + x * (1.0 / 7.0)
    for k in (6.0, 5.0, 4.0, 3.0, 2.0):
        p = 1.0 + x * (1.0 / k) * p
    return jnp.where(x > -0.25, -(x * p), 1.0 - jnp.exp(x))


def _softplus_neg(lam):
    z = -lam
    e = jnp.exp(-jnp.abs(z))
    u = 1.0 + e
    d = u - 1.0
    log1p_e = jnp.where(d == 0.0, e, jnp.log(u) * (e / jnp.where(d == 0.0, 1.0, d)))
    return jnp.maximum(z, 0.0) + log1p_e


def _lru_gates(xp_ref, w, bias, wa_ref, ba, wx_ref, bx, sp, taps, tm, heads):
    halo = xp_ref.shape[0] - tm
    xc = jnp.broadcast_to(bias, (tm, w.shape[1]))
    for k in range(taps):
        xc = xc + w[k:k + 1, :] * xp_ref[pl.ds(halo - (taps - 1) + k, tm), :]
    xb = xc.astype(BF16)
    pr, pi = [], []
    for n in range(heads):
        xh = xb[:, n * HEAD_DIM:(n + 1) * HEAD_DIM]
        pr.append(_nn(xh, wa_ref[n]))
        pi.append(_nn(xh, wx_ref[n]))
    r = _sigmoid(jnp.concatenate(pr, axis=1) + ba)
    ig = _sigmoid(jnp.concatenate(pi, axis=1) + bx)
    log_a = (-LRU_C) * r * sp
    a = jnp.exp(log_a)
    mult = jnp.sqrt(_neg_expm1(2.0 * log_a))
    return xc, r, ig, a, mult


def _lru_fwd(ub, x_col, conv_w, conv_b, wa, ba, wx, bx, lam, *, name):
    s = ub.shape[0]
    taps, w = conv_w.shape
    heads = w // HEAD_DIM
    tm = _tile(s, 256)
    hb = tm // LRU_HALO

    def body(x_ref, xh_ref, cw_ref, cb_ref, wa_ref, ba_ref, wx_ref, bx_ref, lam_ref, h_ref, xp_ref, carry_ref):
        i = pl.program_id(0)

        @pl.when(i == 0)
        def _():
            carry_ref[...] = jnp.zeros_like(carry_ref)

        xp_ref[0:LRU_HALO, :] = jnp.where(i == 0, 0.0, xh_ref[...])
        xp_ref[LRU_HALO:, :] = x_ref[...]
        sp = _softplus_neg(lam_ref[...])
        xc, _, ig, a, mult = _lru_gates(xp_ref, cw_ref[...], cb_ref[...], wa_ref, ba_ref[...], wx_ref, bx_ref[...],
                                        sp, taps, tm, heads)
        ac, bc = _scan_rows(a, mult * (ig * xc), reverse=False)
        h = ac * carry_ref[0:1, :] + bc
        h_ref[...] = h
        carry_ref[...] = jnp.broadcast_to(h[tm - 1:tm, :], carry_ref.shape)

    full = lambda arr: pl.BlockSpec(arr.shape, lambda i: (0,) * arr.ndim)
    vecs = [conv_w, conv_b.reshape(1, w), wa.astype(BF16), ba.reshape(1, w), wx.astype(BF16), bx.reshape(1, w), lam.reshape(1, w)]
    return pl.pallas_call(
        body, name=name, grid=(s // tm,),
        in_specs=[pl.BlockSpec((tm, w), lambda i: (i, x_col)),
                  pl.BlockSpec((LRU_HALO, w), lambda i: (jnp.maximum(i * hb - 1, 0), x_col))] + [full(v) for v in vecs],
        out_specs=pl.BlockSpec((tm, w), lambda i: (i, 0)),
        out_shape=jax.ShapeDtypeStruct((s, w), F32),
        scratch_shapes=[pltpu.VMEM((tm + LRU_HALO, w), F32), pltpu.VMEM((8, w), F32)],
        compiler_params=_cparams("arbitrary"))(ub, ub, *vecs)


def _lru_bwd(ub, x_col, h, dh, conv_w, conv_b, wa, ba, wx, bx, lam, *, name):
    s = ub.shape[0]
    taps, w = conv_w.shape
    heads = w // HEAD_DIM
    tm = _tile(s, 256)
    hb = tm // LRU_HALO
    nt = s // tm

    def body(x_ref, xh_ref, h_ref, hh_ref, dh_ref, cw_ref, cb_ref, wa_ref, ba_ref, wx_ref, bx_ref, lam_ref,
             dx_ref, dcw_ref, dcb_ref, dwa_ref, dba_ref, dwx_ref, dbx_ref, dlam_ref,
             xp_ref, dxp_ref, dlt_ref, afirst_ref, dxc_next_ref, dsp_ref):
        step = pl.program_id(0)
        i = nt - 1 - step

        @pl.when(step == 0)
        def _():
            for ref in (dcw_ref, dcb_ref, dwa_ref, dba_ref, dwx_ref, dbx_ref, dlam_ref, dlt_ref, dxc_next_ref, dsp_ref):
                ref[...] = jnp.zeros_like(ref)
            afirst_ref[...] = jnp.ones_like(afirst_ref)

        xp_ref[0:LRU_HALO, :] = jnp.where(i == 0, 0.0, xh_ref[...])
        xp_ref[LRU_HALO:, :] = x_ref[...]
        cw = cw_ref[...]
        lam_v = lam_ref[...]
        sp = _softplus_neg(lam_v)
        xc, r, ig, a, mult = _lru_gates(xp_ref, cw, cb_ref[...], wa_ref, ba_ref[...], wx_ref, bx_ref[...], sp, taps, tm, heads)
        rows = lax.broadcasted_iota(jnp.int32, (tm, w), 0)
        a_next = jnp.where(rows == tm - 1, afirst_ref[0:1, :], _shift_rows(a, 1, 1.0, down=False))
        ac, bc = _scan_rows(a_next, dh_ref[...], reverse=True)
        delta = ac * dlt_ref[0:1, :] + bc
        hv = h_ref[...]
        h_last_prev = jnp.where(i == 0, 0.0, hh_ref[LRU_HALO - 1:LRU_HALO, :])
        h_prev = jnp.where(rows == 0, h_last_prev, _shift_rows(hv, 1, 0.0, down=True))
        gated = ig * xc
        da = delta * h_prev
        dmult = delta * gated
        dgated = delta * mult
        dlog_a = da * a - dmult * (a * a) / mult
        dpr = dlog_a * ((-LRU_C) * sp) * r * (1.0 - r)
        dpi = dgated * xc * ig * (1.0 - ig)
        dxc = dgated * ig
        dsp_ref[...] += jnp.sum(dlog_a * ((-LRU_C) * r), axis=0, keepdims=True)
        dba_ref[...] += jnp.sum(dpr, axis=0, keepdims=True)
        dbx_ref[...] += jnp.sum(dpi, axis=0, keepdims=True)
        xb = xc.astype(BF16)
        dprb = dpr.astype(BF16)
        dpib = dpi.astype(BF16)
        back = []
        for n in range(heads):
            sl = slice(n * HEAD_DIM, (n + 1) * HEAD_DIM)
            dwa_ref[n] += _tn(xb[:, sl], dprb[:, sl])
            dwx_ref[n] += _tn(xb[:, sl], dpib[:, sl])
            back.append(_nt(dprb[:, sl], wa_ref[n]) + _nt(dpib[:, sl], wx_ref[n]))
        dxc = dxc + jnp.concatenate(back, axis=1)
        dxp_ref[0:tm, :] = dxc
        dxp_ref[tm:, :] = dxc_next_ref[...]
        dx = jnp.zeros((tm, w), F32)
        dws = []
        for k in range(taps):
            dx = dx + cw[k:k + 1, :] * dxp_ref[pl.ds(taps - 1 - k, tm), :]
            dws.append(jnp.sum(dxc * xp_ref[pl.ds(LRU_HALO - (taps - 1) + k, tm), :], axis=0, keepdims=True))
        dx_ref[...] = dx.astype(dx_ref.dtype)
        dcw_ref[...] += jnp.concatenate(dws, axis=0)
        dcb_ref[...] += jnp.sum(dxc, axis=0, keepdims=True)
        dlt_ref[...] = jnp.broadcast_to(delta[0:1, :], dlt_ref.shape)
        afirst_ref[...] = jnp.broadcast_to(a[0:1, :], afirst_ref.shape)
        dxc_next_ref[...] = dxc[0:LRU_HALO, :]

        @pl.when(step == nt - 1)
        def _():
            dlam_ref[...] = dsp_ref[...] * (-_sigmoid(-lam_v))

    rev = lambda col: pl.BlockSpec((tm, w), lambda st: (nt - 1 - st, col))
    prev = lambda col: pl.BlockSpec((LRU_HALO, w), lambda st: (jnp.maximum((nt - 1 - st) * hb - 1, 0), col))
    full = lambda arr: pl.BlockSpec(arr.shape, lambda st: (0,) * arr.ndim)
    vec = pl.BlockSpec((1, w), lambda st: (0, 0))
    vecs = [conv_w, conv_b.reshape(1, w), wa.astype(BF16), ba.reshape(1, w), wx.astype(BF16), bx.reshape(1, w), lam.reshape(1, w)]
    vshape = jax.ShapeDtypeStruct((1, w), F32)
    return pl.pallas_call(
        body, name=name, grid=(nt,),
        in_specs=[rev(x_col), prev(x_col), rev(0), prev(0), rev(0)] + [full(v) for v in vecs],
        out_specs=[rev(0), full(conv_w), vec, full(wa), vec, full(wx), vec, vec],
        out_shape=[jax.ShapeDtypeStruct((s, w), BF16), jax.ShapeDtypeStruct(conv_w.shape, F32), vshape,
                   jax.ShapeDtypeStruct(wa.shape, F32), vshape, jax.ShapeDtypeStruct(wx.shape, F32), vshape, vshape],
        scratch_shapes=[pltpu.VMEM((tm + LRU_HALO, w), F32), pltpu.VMEM((tm + LRU_HALO, w), F32),
                        pltpu.VMEM((8, w), F32), pltpu.VMEM((8, w), F32), pltpu.VMEM((LRU_HALO, w), F32), pltpu.VMEM((1, w), F32)],
        compiler_params=_cparams("arbitrary"))(ub, ub, h, h, dh, *vecs)


def _group_fwd(y, w, gate):
    r = lax.rsqrt(jnp.mean(y * y, axis=-1, keepdims=True) + RMS_EPS)
    return ((y * r) * w) * (gate * _sigmoid(gate))


def _mix_out_fwd(y_conv, y_attn, y_lru, ua, ub, n_conv, n_attn, n_lru, *, name):
    s, c = y_conv.shape
    wa_ = y_attn.shape[1]
    d = 2 * c + wa_
    tm = _tile(s, 256)
    assert wa_ == 2 * c

    def body(yc_ref, ya_ref, yl_ref, gc_ref, ga_ref, gl_ref, nc_ref, na_ref, nl_ref, o_ref):
        o_ref[:, 0:c] = _group_fwd(yc_ref[...], nc_ref[...], gc_ref[...]).astype(o_ref.dtype)
        o_ref[:, c:c + wa_] = _group_fwd(ya_ref[...], na_ref[...], ga_ref[...]).astype(o_ref.dtype)
        o_ref[:, c + wa_:] = _group_fwd(yl_ref[...], nl_ref[...], gl_ref[...]).astype(o_ref.dtype)

    blk = lambda width, col: pl.BlockSpec((tm, width), lambda i: (i, col))
    vec = lambda width: pl.BlockSpec((1, width), lambda i: (0, 0))
    return pl.pallas_call(
        body, name=name, grid=(s // tm,),
        in_specs=[blk(c, 0), blk(wa_, 0), blk(c, 0), blk(c, 2), blk(wa_, 0), blk(c, 3), vec(c), vec(wa_), vec(c)],
        out_specs=blk(d, 0), out_shape=jax.ShapeDtypeStruct((s, d), BF16),
        compiler_params=_cparams("parallel"))(y_conv, y_attn, y_lru, ua, ub, ub, n_conv.reshape(1, c), n_attn.reshape(1, wa_), n_lru.reshape(1, c))


def _group_bwd(dout, y, w, gate):
    r = lax.rsqrt(jnp.mean(y * y, axis=-1, keepdims=True) + RMS_EPS)
    silu, dsilu = _silu_and_grad(gate)
    dy, dwp = _rms_bwd_math(dout * silu, y, r, w)
    return dy, dout * ((y * r) * w) * dsilu, dwp


def _mix_out_bwd(dy, y_conv, y_attn, y_lru, ua, ub, n_conv, n_attn, n_lru, *, name):
    s, c = y_conv.shape
    wa_ = y_attn.shape[1]
    tm = _tile(s, 256)

    def body(dy_ref, yc_ref, ya_ref, yl_ref, gc_ref, ga_ref, gl_ref, nc_ref, na_ref, nl_ref,
             dyc_ref, dya_ref, dyl_ref, dgc_ref, dga_ref, dgl_ref, dnc_ref, dna_ref, dnl_ref):
        @pl.when(pl.program_id(0) == 0)
        def _():
            for ref in (dnc_ref, dna_ref, dnl_ref):
                ref[...] = jnp.zeros_like(ref)

        groups = ((dy_ref[:, 0:c], yc_ref, nc_ref, gc_ref, dyc_ref, dgc_ref, dnc_ref),
                  (dy_ref[:, c:c + wa_], ya_ref, na_ref, ga_ref, dya_ref, dga_ref, dna_ref),
                  (dy_ref[:, c + wa_:], yl_ref, nl_ref, gl_ref, dyl_ref, dgl_ref, dnl_ref))
        for dout, y_ref, n_ref, g_ref, dyo_ref, dgo_ref, dn_ref in groups:
            dyv, dgv, dwp = _group_bwd(dout, y_ref[...], n_ref[...], g_ref[...])
            dyo_ref[...] = dyv
            dgo_ref[...] = dgv.astype(dgo_ref.dtype)
            dn_ref[...] += jnp.sum(dwp, axis=0, keepdims=True)

    blk = lambda width, col: pl.BlockSpec((tm, width), lambda i: (i, col))
    vec = lambda width: pl.BlockSpec((1, width), lambda i: (0, 0))
    sh = lambda width, dt: jax.ShapeDtypeStruct((s, width), dt)
    vs = lambda width: jax.ShapeDtypeStruct((1, width), F32)
    return pl.pallas_call(
        body, name=name, grid=(s // tm,),
        in_specs=[blk(2 * c + wa_, 0), blk(c, 0), blk(wa_, 0), blk(c, 0), blk(c, 2), blk(wa_, 0), blk(c, 3), vec(c), vec(wa_), vec(c)],
        out_specs=[blk(c, 0), blk(wa_, 0), blk(c, 0), blk(c, 0), blk(wa_, 0), blk(c, 0), vec(c), vec(wa_), vec(c)],
        out_shape=[sh(c, F32), sh(wa_, F32), sh(c, F32), sh(c, BF16), sh(wa_, BF16), sh(c, BF16), vs(c), vs(wa_), vs(c)],
        compiler_params=_cparams("arbitrary"))(dy, y_conv, y_attn, y_lru, ua, ub, ub, n_conv.reshape(1, c), n_attn.reshape(1, wa_), n_lru.reshape(1, c))


def _xattn_probs(qh, kh):
    sc = _nt(qh, kh) * (HEAD_DIM ** -0.5)
    e = jnp.exp(sc - jnp.max(sc, axis=-1, keepdims=True))
    return e / jnp.sum(e, axis=-1, keepdims=True)


def _xattn_fwd(q, kv, *, name):
    s, w = q.shape
    heads = w // HEAD_DIM
    tm = _tile(s, 512)

    def body(q_ref, kv_ref, o_ref):
        for n in range(heads):
            sl = slice(n * HEAD_DIM, (n + 1) * HEAD_DIM)
            p = _xattn_probs(q_ref[:, sl], kv_ref[:, sl])
            o_ref[:, sl] = _nn(p.astype(BF16), kv_ref[:, w + n * HEAD_DIM:w + (n + 1) * HEAD_DIM]).astype(o_ref.dtype)

    return pl.pallas_call(
        body, name=name, grid=(s // tm,),
        in_specs=[pl.BlockSpec((tm, w), lambda i: (i, 0)), pl.BlockSpec(kv.shape, lambda i: (0, 0))],
        out_specs=pl.BlockSpec((tm, w), lambda i: (i, 0)), out_shape=jax.ShapeDtypeStruct((s, w), BF16),
        compiler_params=_cparams("parallel"))(q, kv)


def _xattn_bwd(q, kv, do, *, name):
    s, w = q.shape
    heads = w // HEAD_DIM
    tm = _tile(s, 512)
    scale = HEAD_DIM ** -0.5

    def body(q_ref, kv_ref, do_ref, dq_ref, dkv_ref):
        @pl.when(pl.program_id(0) == 0)
        def _():
            dkv_ref[...] = jnp.zeros_like(dkv_ref)

        for n in range(heads):
            sl = slice(n * HEAD_DIM, (n + 1) * HEAD_DIM)
            vsl = slice(w + n * HEAD_DIM, w + (n + 1) * HEAD_DIM)
            qh, kh, vh, doh = q_ref[:, sl], kv_ref[:, sl], kv_ref[:, vsl], do_ref[:, sl]
            p = _xattn_probs(qh, kh)
            dp = _nt(doh, vh)
            ds = (p * (dp - jnp.sum(dp * p, axis=-1, keepdims=True)) * scale).astype(BF16)
            dq_ref[:, sl] = _nn(ds, kh).astype(dq_ref.dtype)
            dkv_ref[:, sl] += _tn(ds, qh)
            dkv_ref[:, vsl] += _tn(p.astype(BF16), doh)

    row = pl.BlockSpec((tm, w), lambda i: (i, 0))
    kvs = pl.BlockSpec(kv.shape, lambda i: (0, 0))
    return pl.pallas_call(
        body, name=name, grid=(s // tm,), in_specs=[row, kvs, row], out_specs=[row, kvs],
        out_shape=[jax.ShapeDtypeStruct((s, w), BF16), jax.ShapeDtypeStruct(kv.shape, F32)],
        compiler_params=_cparams("arbitrary"))(q, kv, do)


ROW_BLOCK_BYTES = 4 << 20
PACK_ROWS = 512


def _row_tile(rows, cols):
    limit = max(8, ROW_BLOCK_BYTES // (4 * cols))
    t = 8
    while t * 2 <= limit and rows % (t * 2) == 0:
        t *= 2
    assert rows % t == 0
    return t


def _cast_bf16(w, *, name):
    rows, cols = w.shape
    tr = _row_tile(rows, cols)

    def body(w_ref, o_ref):
        o_ref[...] = w_ref[...].astype(BF16)

    blk = pl.BlockSpec((tr, cols), lambda i: (i, 0))
    return pl.pallas_call(body, name=name, grid=(rows // tr,), in_specs=[blk], out_specs=blk,
                          out_shape=jax.ShapeDtypeStruct((rows, cols), BF16), compiler_params=_cparams("parallel"))(w)


def _adamw(w, m, v, gs, *, name):
    rows, cols = w.shape
    tr = _row_tile(rows, cols * 4)
    ng = len(gs)

    def body(*refs):
        w_ref, m_ref, v_ref = refs[:3]
        g_refs = refs[3:3 + ng]
        g_out, d_out, m_out, v_out = refs[3 + ng:]
        g = g_refs[0][...]
        for r in g_refs[1:]:
            g = g + r[...]
        mn = ADAM_B1 * m_ref[...] + (1.0 - ADAM_B1) * g
        vn = ADAM_B2 * v_ref[...] + (1.0 - ADAM_B2) * (g * g)
        m_hat = mn / (1.0 - ADAM_B1 ** ADAM_STEP)
        v_hat = vn / (1.0 - ADAM_B2 ** ADAM_STEP)
        g_out[...] = g
        d_out[...] = -ADAM_LR * (m_hat / (jnp.sqrt(v_hat) + ADAM_EPS) + ADAM_WD * w_ref[...])
        m_out[...] = mn
        v_out[...] = vn

    blk = pl.BlockSpec((tr, cols), lambda i: (i, 0))
    shp = jax.ShapeDtypeStruct((rows, cols), F32)
    return pl.pallas_call(body, name=name, grid=(rows // tr,), in_specs=[blk] * (3 + ng), out_specs=[blk] * 4,
                          out_shape=[shp] * 4, compiler_params=_cparams("parallel"))(w, m, v, *gs)


def _adamw_layers(w, m, v, gs, *, name):
    layers, rows, cols = w.shape
    tr = _row_tile(rows, cols * 4)
    nblk = rows // tr
    ng = len(gs[0])

    def body(*refs):
        w_ref, m_ref, v_ref = refs[:3]
        g_refs = refs[3:3 + layers * ng]
        g_out, d_out, m_out, v_out = refs[3 + layers * ng:]
        l = pl.program_id(0)
        g = jnp.zeros((tr, cols), F32)
        for ll in range(layers):
            gl = g_refs[ll * ng][...]
            for r in g_refs[ll * ng + 1:(ll + 1) * ng]:
                gl = gl + r[...]
            g = jnp.where(l == ll, gl, g)
        mn = ADAM_B1 * m_ref[...] + (1.0 - ADAM_B1) * g
        vn = ADAM_B2 * v_ref[...] + (1.0 - ADAM_B2) * (g * g)
        m_hat = mn / (1.0 - ADAM_B1 ** ADAM_STEP)
        v_hat = vn / (1.0 - ADAM_B2 ** ADAM_STEP)
        g_out[...] = g
        d_out[...] = -ADAM_LR * (m_hat / (jnp.sqrt(v_hat) + ADAM_EPS) + ADAM_WD * w_ref[...])
        m_out[...] = mn
        v_out[...] = vn

    blk = pl.BlockSpec((None, tr, cols), lambda l, i: (l, i, 0))

    def g_spec(ll):
        return pl.BlockSpec((tr, cols), lambda l, i: (jnp.where(l == ll, i, jnp.where(l < ll, 0, nblk - 1)), 0))

    shp = jax.ShapeDtypeStruct(w.shape, F32)
    return pl.pallas_call(
        body, name=name, grid=(layers, nblk), in_specs=[blk] * 3 + [g_spec(ll) for ll in range(layers) for _ in range(ng)],
        out_specs=[blk] * 4, out_shape=[shp] * 4, compiler_params=_cparams("arbitrary", "arbitrary"),
    )(w, m, v, *[g for gl in gs for g in gl])


OTHER_CHIPS = ((1, 0), (0, 1), (1, 1))
N_CHIPS = 4
ANY = pl.BlockSpec(memory_space=pl.ANY)


def _place():
    return lax.axis_index("x"), lax.axis_index("y"), lax.axis_index("c")


def _flip(v, f):
    return 1 - v if f else v


def _part(ref, lead, axis, chip, size):
    idx = list(lead) + [slice(None)] * (len(ref.shape) - len(lead))
    idx[len(lead) + axis] = pl.ds(pl.multiple_of(chip * size, size), size)
    return ref.at[tuple(idx)]


def _allgather_chips(shards, axes, *, name):
    n = len(shards)
    sizes = [sh.shape[ax] for sh, ax in zip(shards, axes)]

    def full_shape(sh, ax):
        return tuple(d * N_CHIPS if i == ax else d for i, d in enumerate(sh.shape))

    def body(*refs):
        ins, outs = refs[:n], refs[n:2 * n]
        send_sems, recv_sems, loc_sems = refs[2 * n:]
        x, y, c = _place()
        me = 2 * x + y
        local = []
        for a in range(n):
            cp = pltpu.make_async_copy(ins[a], _part(outs[a], (), axes[a], me, sizes[a]), loc_sems.at[a])
            cp.start()
            local.append(cp)

        def remote(a, j, chip):
            fx, fy = OTHER_CHIPS[j]
            return pltpu.make_async_remote_copy(
                src_ref=ins[a], dst_ref=_part(outs[a], (), axes[a], chip, sizes[a]),
                send_sem=send_sems.at[a, j], recv_sem=recv_sems.at[a, j],
                device_id=(_flip(x, fx), _flip(y, fy), c), device_id_type=MESH)

        for a in range(n):
            for j in range(len(OTHER_CHIPS)):
                remote(a, j, me).start()
        for a in range(n):
            for j, (fx, fy) in enumerate(OTHER_CHIPS):
                remote(a, j, 2 * _flip(x, fx) + _flip(y, fy)).wait()
        for cp in local:
            cp.wait()

    return pl.pallas_call(
        body, name=name, in_specs=[ANY] * n, out_specs=[ANY] * n,
        out_shape=[jax.ShapeDtypeStruct(full_shape(sh, ax), sh.dtype) for sh, ax in zip(shards, axes)],
        scratch_shapes=[pltpu.SemaphoreType.DMA((n, 3)), pltpu.SemaphoreType.DMA((n, 3)), pltpu.SemaphoreType.DMA((n,))],
    )(*shards)


HBM = pl.BlockSpec(memory_space=pltpu.HBM)
SEM = pl.BlockSpec(memory_space=pltpu.SEMAPHORE)
SPLIT_COPY = pltpu.CompilerParams(has_side_effects=pltpu.SideEffectType.DATAFLOW_SIDE_EFFECTING)


def _cast_place(w, layer, axis, chip, *, name):
    _, rows, cols = w.shape
    tr = _row_tile(rows, cols)
    nblk = rows // tr

    def body(chip_ref, w_ref, o_ref):
        o_ref[...] = w_ref[...].astype(BF16)

    if axis == 1:
        shape = (rows, cols * N_CHIPS)
        o_spec = pl.BlockSpec((tr, cols), lambda i, chip_ref: (i, chip_ref[0]))
    else:
        shape = (rows * N_CHIPS, cols)
        o_spec = pl.BlockSpec((tr, cols), lambda i, chip_ref: (chip_ref[0] * nblk + i, 0))
    return pl.pallas_call(
        body, name=name,
        grid_spec=pltpu.PrefetchScalarGridSpec(
            num_scalar_prefetch=1, grid=(nblk,),
            in_specs=[pl.BlockSpec((None, tr, cols), lambda i, chip_ref: (layer, i, 0))], out_specs=o_spec),
        out_shape=jax.ShapeDtypeStruct(shape, BF16), compiler_params=_cparams("parallel"))(chip, w)


def _gather_copy(refs, a, j, send_sems, recv_sems, *, axes, sizes, arriving):
    x, y, c = _place()
    fx, fy = OTHER_CHIPS[j]
    px, py = _flip(x, fx), _flip(y, fy)
    part = _part(refs[a], (), axes[a], (2 * px + py) if arriving else (2 * x + y), sizes[a])
    k = a * len(OTHER_CHIPS) + j
    return pltpu.make_async_remote_copy(src_ref=part, dst_ref=part, send_sem=send_sems.at[k], recv_sem=recv_sems.at[k],
                                        device_id=(px, py, c), device_id_type=MESH)


def _scatter_copy(srcs, lands, a, j, axes, sizes, send_sems, recv_sems):
    x, y, c = _place()
    fx, fy = OTHER_CHIPS[j]
    px, py = _flip(x, fx), _flip(y, fy)
    k = a * len(OTHER_CHIPS) + j
    return pltpu.make_async_remote_copy(src_ref=_part(srcs[a], (), axes[a], 2 * px + py, sizes[a]), dst_ref=lands[a].at[j],
                                        send_sem=send_sems.at[k], recv_sem=recv_sems.at[k],
                                        device_id=(px, py, c), device_id_type=MESH)


def _split_start(arrs, make_copy, ncopies, dep, *, name):
    n = len(arrs)

    def body(*refs):
        ins = refs[:n]
        send_sems, recv_sems = refs[n + 1], refs[n + 2]
        token = refs[n + 3 + n]
        for a in range(ncopies):
            for j in range(len(OTHER_CHIPS)):
                make_copy(ins, a, j, send_sems, recv_sems).start()
        token[...] = jnp.zeros_like(token)

    sem = pltpu.SemaphoreType.DMA((ncopies * len(OTHER_CHIPS),))
    res = pl.pallas_call(
        body, name=name,
        out_shape=(sem, sem, *[pltpu.HBM(a.shape, a.dtype) for a in arrs], jax.ShapeDtypeStruct((8, LANES), F32)),
        in_specs=[HBM] * n + [pl.BlockSpec(memory_space=pl.ANY)],
        out_specs=(SEM, SEM, *[HBM] * n, pl.BlockSpec(memory_space=pltpu.VMEM)),
        input_output_aliases={a: 2 + a for a in range(n)}, compiler_params=SPLIT_COPY,
    )(*[pltpu.with_memory_space_constraint(a, pltpu.HBM) for a in arrs], dep)
    return res[0], res[1], list(res[2:2 + n]), res[2 + n]


def _split_wait(arrs, send_sems, recv_sems, make_copy, ncopies, after, *, name):
    n = len(arrs)

    def body(*refs):
        ins = refs[:n]
        send, recv = refs[n], refs[n + 1]
        for a in range(ncopies):
            for j in range(len(OTHER_CHIPS)):
                cp = make_copy(ins, a, j, send, recv)
                cp.wait_send()
                cp.wait_recv()

    res = pl.pallas_call(
        body, name=name, out_shape=tuple(pltpu.HBM(a.shape, a.dtype) for a in arrs),
        in_specs=[HBM] * n + [SEM, SEM, pl.BlockSpec(memory_space=pl.ANY)], out_specs=tuple([HBM] * n),
        input_output_aliases={a: a for a in range(n)}, compiler_params=SPLIT_COPY,
    )(*arrs, send_sems, recv_sems, after)
    return list(res)


def _sum_own_and_slots(g, land, axis, chip, *, name):
    slots, rows, cols = land.shape
    tr = _row_tile(rows, cols * 4)
    nblk = rows // tr

    def body(chip_ref, g_ref, l_ref, o_ref):
        acc = g_ref[...].astype(F32)
        for j in range(slots):
            acc = acc + l_ref[j].astype(F32)
        o_ref[...] = acc

    if axis == 1:
        g_spec = pl.BlockSpec((tr, cols), lambda i, chip_ref: (i, chip_ref[0]))
    else:
        g_spec = pl.BlockSpec((tr, cols), lambda i, chip_ref: (chip_ref[0] * nblk + i, 0))
    return pl.pallas_call(
        body, name=name,
        grid_spec=pltpu.PrefetchScalarGridSpec(
            num_scalar_prefetch=1, grid=(nblk,),
            in_specs=[g_spec, pl.BlockSpec((slots, tr, cols), lambda i, chip_ref: (0, i, 0))],
            out_specs=pl.BlockSpec((tr, cols), lambda i, chip_ref: (i, 0))),
        out_shape=jax.ShapeDtypeStruct((rows, cols), F32), compiler_params=_cparams("parallel"))(chip, g, land)


def _swap_sibling(arrs, *, name):
    n = len(arrs)

    def body(*refs):
        ins, outs = refs[:n], refs[n:2 * n]
        send_sems, recv_sems = refs[2 * n:]
        x, y, c = _place()
        copies = [pltpu.make_async_remote_copy(src_ref=ins[a], dst_ref=outs[a], send_sem=send_sems.at[a], recv_sem=recv_sems.at[a],
                                               device_id=(x, y, 1 - c), device_id_type=MESH) for a in range(n)]
        for cp in copies:
            cp.start()
        for cp in copies:
            cp.wait()

    return pl.pallas_call(
        body, name=name, in_specs=[ANY] * n, out_specs=[ANY] * n,
        out_shape=[jax.ShapeDtypeStruct(a.shape, a.dtype) for a in arrs],
        scratch_shapes=[pltpu.SemaphoreType.DMA((n,)), pltpu.SemaphoreType.DMA((n,))],
    )(*arrs)


def _allreduce_small(p, *, name):
    rows, cols = p.shape
    nrel = len(OTHER_CHIPS)

    def body(p_ref, o_ref, sib_ref, land_ref, send_sems, recv_sems):
        x, y, c = _place()
        me = 2 * x + y
        pair = pltpu.make_async_remote_copy(src_ref=p_ref, dst_ref=sib_ref, send_sem=send_sems.at[nrel], recv_sem=recv_sems.at[nrel],
                                            device_id=(x, y, 1 - c), device_id_type=MESH)
        pair.start()
        pair.wait()
        land_ref[nrel] = p_ref[...] + sib_ref[...]
        copies = []
        for j, (fx, fy) in enumerate(OTHER_CHIPS):
            copies.append(pltpu.make_async_remote_copy(src_ref=land_ref.at[nrel], dst_ref=land_ref.at[j], send_sem=send_sems.at[j],
                                                       recv_sem=recv_sems.at[j], device_id=(_flip(x, fx), _flip(y, fy), c),
                                                       device_id_type=MESH))
        for cp in copies:
            cp.start()
        for cp in copies:
            cp.wait()

        def slot_of(chip):
            r = jnp.bitwise_xor(chip, me)
            return jnp.where(r == 0, nrel, jnp.where(r == 2, 0, jnp.where(r == 1, 1, 2)))

        acc = land_ref[slot_of(0)]
        for chip in range(1, N_CHIPS):
            acc = acc + land_ref[slot_of(chip)]
        o_ref[...] = acc

    vm = pl.BlockSpec(memory_space=pltpu.VMEM)
    return pl.pallas_call(
        body, name=name, in_specs=[vm], out_specs=vm, out_shape=jax.ShapeDtypeStruct((rows, cols), F32),
        scratch_shapes=[pltpu.VMEM((rows, cols), F32), pltpu.VMEM((nrel + 1, rows, cols), F32),
                        pltpu.SemaphoreType.DMA((nrel + 1,)), pltpu.SemaphoreType.DMA((nrel + 1,))],
        compiler_params=pltpu.CompilerParams(vmem_limit_bytes=V7X_VMEM_LIMIT_BYTES))(p)


WEIGHTS = ("mix_norm_g", "w_in", "conv_dw_w", "conv_dw_b", "conv_ln_g", "conv_ln_b", "conv_pw_w", "lru_conv_w", "lru_conv_b",
           "lru_wa", "lru_ba", "lru_wx", "lru_bx", "lru_lambda", "out_norm_conv", "out_norm_attn", "out_norm_lru", "w_out",
           "xattn_norm_g", "mem_norm_g", "xattn_wq", "xattn_wkv", "xattn_wo", "final_norm_g")
BIG = {"w_in": 2, "conv_pw_w": 1, "w_out": 1, "xattn_wq": 1, "xattn_wkv": 1, "xattn_wo": 2}
SMALL_SHARDED = {"conv_dw_w": 2, "lru_conv_w": 2}


IN_GROUP = ("w_in", "conv_pw_w")
REST_GROUP = ("w_out", "xattn_wq", "xattn_wkv", "xattn_wo")


def _trunk(x, mem, target, p, fetch, grads_ready):
    depth = p["mix_norm_g"].shape[0]
    c = p["conv_dw_w"].shape[2]
    aw = p["out_norm_attn"].shape[1]
    heads = aw // HEAD_DIM
    saved = []
    for l in range(depth):
        t = f"l{l}_"
        h1, r1 = _rmsnorm_fwd(x, p["mix_norm_g"][l], name=t + "mix_norm")
        wl = dict(fetch(IN_GROUP, l, r1))
        ua = _matmul(h1, wl["w_in"], mode="nn", n=3 * c, b_off=0, name=t + "in_conv")
        qkv = _matmul(h1, wl["w_in"], mode="nn", n=3 * aw, b_off=3 * c, out_dtype=BF16, name=t + "in_qkv")
        ub = _matmul(h1, wl["w_in"], mode="nn", n=aw + 2 * c, b_off=3 * c + 3 * aw, name=t + "in_gates")
        y_conv = _conv_fwd(ua, p["conv_dw_w"][l], p["conv_dw_b"][l], p["conv_ln_g"][l], p["conv_ln_b"][l], wl["conv_pw_w"],
                           name=t + "conv_fwd")
        y_attn, sbw = _sb_fwd(qkv, heads, name=t + "sb_fwd")
        y_lru = _lru_fwd(ub, aw // c, p["lru_conv_w"][l], p["lru_conv_b"][l], p["lru_wa"][l], p["lru_ba"][l], p["lru_wx"][l],
                         p["lru_bx"][l], p["lru_lambda"][l], name=t + "lru_fwd")
        y = _mix_out_fwd(y_conv, y_attn, y_lru, ua, ub, p["out_norm_conv"][l], p["out_norm_attn"][l], p["out_norm_lru"][l],
                         name=t + "mix_out_fwd")
        wl.update(fetch(REST_GROUP, l, y))
        x2 = _matmul(y, wl["w_out"], mode="nn", add=x, name=t + "out_proj")
        h2, r2 = _rmsnorm_fwd(x2, p["xattn_norm_g"][l], name=t + "xattn_norm")
        qx = _matmul(h2, wl["xattn_wq"], mode="nn", out_dtype=BF16, name=t + "xattn_q")
        memn, rm = _rmsnorm_fwd(mem, p["mem_norm_g"][l], name=t + "mem_norm")
        kv = _matmul(memn, wl["xattn_wkv"], mode="nn", out_dtype=BF16, name=t + "xattn_kv")
        o = _xattn_fwd(qx, kv, name=t + "xattn_fwd")
        x3 = _matmul(o, wl["xattn_wo"], mode="nn", add=x2, name=t + "xattn_o")
        saved.append(dict(x=x, h1=h1, r1=r1, ua=ua, qkv=qkv, ub=ub, y_conv=y_conv, y_attn=y_attn, sbw=sbw, y_lru=y_lru, y=y,
                          x2=x2, h2=h2, r2=r2, qx=qx, memn=memn, rm=rm, kv=kv, o=o, w=wl))
        x = x3

    loss, dx, dg_final = _final_loss(x, p["final_norm_g"], target, name="final_loss")
    small = {k: [None] * depth for k in WEIGHTS if k not in BIG and k != "final_norm_g"}
    token = None
    for l in reversed(range(depth)):
        t = f"l{l}_"
        s = saved[l]
        wl = s["w"]
        do = _matmul(dx, wl["xattn_wo"], mode="nt", out_dtype=BF16, dep=token, name=t + "d_xattn_o")
        dwo = _matmul(s["o"], dx, mode="tn", out_dtype=BF16, name=t + "dw_xattn_o")
        dqx, dkv = _xattn_bwd(s["qx"], s["kv"], do, name=t + "xattn_bwd")
        dwq = _matmul(s["h2"], dqx, mode="tn", out_dtype=BF16, name=t + "dw_xattn_q")
        dh2 = _matmul(dqx, wl["xattn_wq"], mode="nt", name=t + "d_xattn_q")
        dx2, dg = _rmsnorm_bwd(dh2, s["x2"], s["r2"], p["xattn_norm_g"][l], dx, name=t + "xattn_norm_bwd")
        small["xattn_norm_g"][l] = dg[0]
        dmemn = _matmul(dkv, wl["xattn_wkv"], mode="nt", name=t + "d_xattn_kv")
        dwkv = _matmul(s["memn"], dkv, mode="tn", out_dtype=BF16, name=t + "dw_xattn_kv")
        _, dg = _rmsnorm_bwd(dmemn, mem, s["rm"], p["mem_norm_g"][l], None, name=t + "mem_norm_bwd")
        small["mem_norm_g"][l] = dg[0]
        dwout = _matmul(s["y"], dx2, mode="tn", out_dtype=BF16, name=t + "dw_out_proj")
        token = grads_ready(REST_GROUP, l, dict(w_out=dwout, xattn_wq=dwq, xattn_wkv=dwkv, xattn_wo=dwo))
        dy = _matmul(dx2, wl["w_out"], mode="nt", dep=token, name=t + "d_out_proj")
        dyc, dya, dyl, dgc, dga, dgl, dnc, dna, dnl = _mix_out_bwd(
            dy, s["y_conv"], s["y_attn"], s["y_lru"], s["ua"], s["ub"], p["out_norm_conv"][l], p["out_norm_attn"][l],
            p["out_norm_lru"][l], name=t + "mix_out_bwd")
        small["out_norm_conv"][l], small["out_norm_attn"][l], small["out_norm_lru"][l] = dnc[0], dna[0], dnl[0]
        dd, dpw, dlg, dlb = _conv_bwd_a(s["ua"], dyc, p["conv_dw_w"][l], p["conv_dw_b"][l], p["conv_ln_g"][l], p["conv_ln_b"][l],
                                        wl["conv_pw_w"], name=t + "conv_bwd_a")
        dval, dglu, ddw, ddb = _conv_bwd_b(s["ua"], dd, p["conv_dw_w"][l], name=t + "conv_bwd_b")
        small["conv_ln_g"][l], small["conv_ln_b"][l], small["conv_dw_w"][l], small["conv_dw_b"][l] = dlg[0], dlb[0], ddw, ddb[0]
        dq, dk, dv = _sb_bwd(s["qkv"], dya, s["sbw"], heads, name=t + "sb_bwd")
        drx, dcw, dcb, dwa, dba, dwx, dbx, dlam = _lru_bwd(
            s["ub"], aw // c, s["y_lru"], dyl, p["lru_conv_w"][l], p["lru_conv_b"][l], p["lru_wa"][l], p["lru_ba"][l],
            p["lru_wx"][l], p["lru_bx"][l], p["lru_lambda"][l], name=t + "lru_bwd")
        small["lru_conv_w"][l], small["lru_conv_b"][l], small["lru_wa"][l], small["lru_ba"][l] = dcw, dcb[0], dwa, dba[0]
        small["lru_wx"][l], small["lru_bx"][l], small["lru_lambda"][l] = dwx, dbx[0], dlam[0]
        du = jnp.concatenate([dval, dglu, dgc, dq, dk, dv, dga, drx, dgl], axis=1)
        dwin = _matmul(s["h1"], du, mode="tn", out_dtype=BF16, tk=4096, name=t + "dw_in")
        token = grads_ready(IN_GROUP, l, dict(w_in=dwin, conv_pw_w=_cast_bf16(dpw, name=t + "cast_dpw")))
        dh1 = _matmul(du, wl["w_in"], mode="nt", dep=token, tk=3328, name=t + "d_in")
        dx, dg = _rmsnorm_bwd(dh1, s["x"], s["r1"], p["mix_norm_g"][l], dx2, name=t + "mix_norm_bwd")
        small["mix_norm_g"][l] = dg[0]
    small = {k: jnp.stack(v) for k, v in small.items()}
    small["final_norm_g"] = dg_final[0]
    return loss, dx, small


def _pack(arrs):
    flat = jnp.concatenate([a.reshape(-1) for a in arrs])
    pad = (-flat.shape[0]) % (PACK_ROWS * LANES)
    return jnp.pad(flat, (0, pad)).reshape(-1, LANES)


def _unpack(packed, like):
    flat = packed.reshape(-1)
    out, at = [], 0
    for a in like:
        out.append(flat[at:at + a.size].reshape(a.shape))
        at += a.size
    return out


def _as_rows(a):
    return a.reshape(-1, a.shape[-1])


def kernel(x, mem, mix_norm_g, w_in, conv_dw_w, conv_dw_b, conv_ln_g, conv_ln_b, conv_pw_w, lru_conv_w, lru_conv_b, lru_wa, lru_ba, lru_wx, lru_bx, lru_lambda, out_norm_conv, out_norm_attn, out_norm_lru, w_out, xattn_norm_g, mem_norm_g, xattn_wq, xattn_wkv, xattn_wo, final_norm_g, loss_target, m_mix_norm_g, m_w_in, m_conv_dw_w, m_conv_dw_b, m_conv_ln_g, m_conv_ln_b, m_conv_pw_w, m_lru_conv_w, m_lru_conv_b, m_lru_wa, m_lru_ba, m_lru_wx, m_lru_bx, m_lru_lambda, m_out_norm_conv, m_out_norm_attn, m_out_norm_lru, m_w_out, m_xattn_norm_g, m_mem_norm_g, m_xattn_wq, m_xattn_wkv, m_xattn_wo, m_final_norm_g, v_mix_norm_g, v_w_in, v_conv_dw_w, v_conv_dw_b, v_conv_ln_g, v_conv_ln_b, v_conv_pw_w, v_lru_conv_w, v_lru_conv_b, v_lru_wa, v_lru_ba, v_lru_wx, v_lru_bx, v_lru_lambda, v_out_norm_conv, v_out_norm_attn, v_out_norm_lru, v_w_out, v_xattn_norm_g, v_mem_norm_g, v_xattn_wq, v_xattn_wkv, v_xattn_wo, v_final_norm_g):
    given = dict(locals())
    w = {k: given[k] for k in WEIGHTS}
    m = {k: given["m_" + k] for k in WEIGHTS}
    v = {k: given["v_" + k] for k in WEIGHTS}
    depth = mix_norm_g.shape[0]
    chip = 2 * lax.axis_index("x") + lax.axis_index("y")

    chip_arr = chip.astype(jnp.int32).reshape(1)

    p = dict(w)
    p.update(zip(SMALL_SHARDED, _allgather_chips([w[k] for k in SMALL_SHARDED], list(SMALL_SHARDED.values()), name="gather_small")))
    axis2d = {k: BIG[k] - 1 for k in BIG}
    groups = [(IN_GROUP, 0), (REST_GROUP, 0)] + [(IN_GROUP + REST_GROUP, l) for l in range(1, depth)]
    pending, token = {}, p[next(iter(SMALL_SHARDED))]
    for names, l in groups:
        arrs = [_cast_place(w[k], l, axis2d[k], chip_arr, name=f"place{l}_{k}") for k in names]
        axes = [axis2d[k] for k in names]
        sizes = [a.shape[ax] // N_CHIPS for a, ax in zip(arrs, axes)]
        start = functools.partial(_gather_copy, axes=axes, sizes=sizes, arriving=False)
        land = functools.partial(_gather_copy, axes=axes, sizes=sizes, arriving=True)
        send, recv, arrs, token = _split_start(arrs, start, len(arrs), token, name=f"gather_start{l}_{names[0]}")
        pending[(names[0], l)] = (names, arrs, send, recv, land)
    last_token = token
    have = {}

    def fetch(group, l, after):
        key = (group[0], l)
        if key in pending:
            names, arrs, send, recv, land = pending.pop(key)
            after = last_token if (group, l) == groups[0] else after
            arrs = _split_wait(arrs, send, recv, land, len(arrs), after, name=f"gather_wait{l}_{names[0]}")
            have.update({(k, l): a for k, a in zip(names, arrs)})
        return {k: have[(k, l)] for k in group}

    flying = []
    held = {}

    def grads_ready(group, l, grads):
        held.update({(k, l): g for k, g in grads.items()})
        if l > 0 and group == REST_GROUP:
            return None
        names = [k for k in (IN_GROUP + REST_GROUP if l > 0 else group)]
        srcs = [held[(k, l)] for k in names]
        axes = [axis2d[k] for k in names]
        sizes = [g.shape[ax] // N_CHIPS for g, ax in zip(srcs, axes)]
        lands = [lax.empty((len(OTHER_CHIPS),) + tuple(sz if i == ax else d for i, d in enumerate(g.shape)), g.dtype)
                 for g, ax, sz in zip(srcs, axes, sizes)]
        n = len(names)
        copy = lambda refs, a, j, ss, rs_: _scatter_copy(refs[:n], refs[n:], a, j, axes, sizes, ss, rs_)
        send, recv, arrs, token = _split_start(srcs + lands, copy, n, jnp.zeros((8, LANES), F32), name=f"scatter_start{l}_{names[0]}")
        flying.append((names, l, axes, arrs, send, recv, copy))
        return token

    loss, grad_x, small = _trunk(x[0], mem[0], loss_target[0], p, fetch, grads_ready)
    loss = lax.psum(loss[0, 0], ("x", "y", "c"))

    sums = {}
    out = {}

    def arrive(entry, after):
        names, l, axes, arrs, send, recv, copy = entry
        n = len(names)
        arrs = _split_wait(arrs, send, recv, copy, n, after, name=f"scatter_wait{l}_{names[0]}")
        for k, ax, g, ld in zip(names, axes, arrs[:n], arrs[n:]):
            ld = ld.reshape((len(OTHER_CHIPS), -1, ld.shape[-1]))
            sums[(k, l)] = _sum_own_and_slots(g, ld, ax, chip_arr, name=f"sum{l}_{k}")
        return sums[(names[-1], l)]

    def update(names):
        mine = [sums[(k, l)] for k in names for l in range(depth)]
        theirs = _swap_sibling(mine, name="swap_sums_" + names[0])
        for i, k in enumerate(names):
            gs = [[mine[i * depth + l], theirs[i * depth + l]] for l in range(depth)]
            out[k] = _adamw_layers(w[k], m[k], v[k], gs, name="adamw_" + k)

    after = grad_x
    for entry in flying[:-1]:
        after = arrive(entry, after)
    update(REST_GROUP)

    small_names = [k for k in WEIGHTS if k not in BIG]
    total = _unpack(_allreduce_small(_pack([small[k] for k in small_names]), name="allreduce_small"), [small[k] for k in small_names])
    g_small = dict(zip(small_names, total))
    for k, ax in SMALL_SHARDED.items():
        size = w[k].shape[ax]
        g_small[k] = lax.dynamic_slice_in_dim(g_small[k], chip * size, size, axis=ax)
    res = _adamw(_pack([w[k] for k in small_names]), _pack([m[k] for k in small_names]), _pack([v[k] for k in small_names]),
                 [_pack([g_small[k] for k in small_names])], name="adamw_small")
    last = res[0]
    res = [_unpack(r, [w[k] for k in small_names]) for r in res]
    for i, k in enumerate(small_names):
        out[k] = [r[i] for r in res]

    arrive(flying[-1], last)
    update(IN_GROUP)

    outs = [loss, grad_x[None]]
    for part in range(4):
        outs += [out[k][part] for k in WEIGHTS]
    return tuple(outs)
```

```python
import functools

import jax
import jax.numpy as jnp
from jax import lax
from jax.experimental import pallas as pl
from jax.experimental.pallas import tpu as pltpu

F32 = jnp.float32
BF16 = jnp.bfloat16
MESH = pl.DeviceIdType.MESH

V7X_VMEM_LIMIT_BYTES = 56 * 1024 * 1024
LANES = 128
HEAD_DIM = 128
LRU_C = 8.0
RMS_EPS = 1e-6
LN_EPS = 1e-5
CONV_HALO = 32
LRU_HALO = 8
ADAM_LR = 0.001
ADAM_B1 = 0.9
ADAM_B2 = 0.999
ADAM_EPS = 1e-08
ADAM_WD = 0.01
ADAM_STEP = 10


def _cparams(*sem):
    return pltpu.CompilerParams(dimension_semantics=sem, vmem_limit_bytes=V7X_VMEM_LIMIT_BYTES)


def _tile(n, pref):
    if n <= pref:
        return n
    for t in range(pref - pref % LANES, 0, -LANES):
        if n % t == 0:
            return t
    t = pref
    while n % t:
        t //= 2
    return t


def _dot(a, b, dims):
    return lax.dot_general(a, b, (dims, ((), ())), preferred_element_type=F32)


def _nn(a, b):
    return _dot(a, b, ((1,), (0,)))


def _nt(a, b):
    return _dot(a, b, ((1,), (1,)))


def _tn(a, b):
    return _dot(a, b, ((0,), (0,)))


def _sigmoid(x):
    return jax.nn.sigmoid(x)


def _silu_and_grad(x):
    s = _sigmoid(x)
    return x * s, s * (1.0 + x * (1.0 - s))


def _matmul(a, b, *, mode, name, layer=None, n=None, b_off=0, add=None, dep=None, out_dtype=F32, tm=1024, tn=1024, tk=2048):
    bshape = b.shape if layer is None else b.shape[1:]
    if mode == "nn":
        m, k = a.shape
        n = bshape[1] if n is None else n
    elif mode == "nt":
        m, k = a.shape
        n = bshape[0]
    else:
        k, m = a.shape
        n = bshape[1]
    tm, tk = _tile(m, tm), _tile(k, tk)
    tn = _tile(n, tn)
    while b_off % tn or n % tn:
        tn -= LANES
    nk = k // tk
    off = b_off // tn
    lead = () if layer is None else (None,)
    li = () if layer is None else (layer,)
    if mode == "nn":
        a_spec = pl.BlockSpec((tm, tk), lambda i, j, kk: (i, kk))
        b_spec = pl.BlockSpec(lead + (tk, tn), lambda i, j, kk: li + (kk, j + off))
        dot = _nn
    elif mode == "nt":
        a_spec = pl.BlockSpec((tm, tk), lambda i, j, kk: (i, kk))
        b_spec = pl.BlockSpec(lead + (tn, tk), lambda i, j, kk: li + (j, kk))
        dot = _nt
    else:
        a_spec = pl.BlockSpec((tk, tm), lambda i, j, kk: (kk, i))
        b_spec = pl.BlockSpec(lead + (tk, tn), lambda i, j, kk: li + (kk, j))
        dot = _tn
    o_spec = pl.BlockSpec((tm, tn), lambda i, j, kk: (i, j))
    has_add = add is not None

    def body(*refs):
        refs = refs[:-3] + refs[-2:] if dep is not None else refs
        if has_add:
            a_ref, b_ref, add_ref, o_ref, acc_ref = refs
        else:
            a_ref, b_ref, o_ref, acc_ref = refs
        kk = pl.program_id(2)
        part = dot(a_ref[...].astype(BF16), b_ref[...].astype(BF16))

        @pl.when(kk == 0)
        def _():
            acc_ref[...] = part

        @pl.when(kk > 0)
        def _():
            acc_ref[...] += part

        @pl.when(kk == nk - 1)
        def _():
            r = acc_ref[...]
            if has_add:
                r = r + add_ref[...]
            o_ref[...] = r.astype(o_ref.dtype)

    ins = [a, b] + ([add] if has_add else [])
    specs = [a_spec, b_spec] + ([o_spec] if has_add else [])
    if dep is not None:
        ins.append(dep)
        specs.append(pl.BlockSpec((8, LANES), lambda i, j, kk: (0, 0)))
    return pl.pallas_call(
        body, name=name, grid=(m // tm, n // tn, nk), in_specs=specs, out_specs=o_spec,
        out_shape=jax.ShapeDtypeStruct((m, n), out_dtype), scratch_shapes=[pltpu.VMEM((tm, tn), F32)],
        compiler_params=_cparams("parallel", "parallel", "arbitrary"))(*ins)


def _rmsnorm_fwd(x, g, *, name):
    s, d = x.shape
    tm = _tile(s, 256)

    def body(x_ref, g_ref, h_ref, r_ref):
        xf = x_ref[...]
        r = lax.rsqrt(jnp.mean(xf * xf, axis=-1, keepdims=True) + RMS_EPS)
        h_ref[...] = ((xf * r) * g_ref[...]).astype(h_ref.dtype)
        r_ref[...] = r

    return pl.pallas_call(
        body, name=name, grid=(s // tm,),
        in_specs=[pl.BlockSpec((tm, d), lambda i: (i, 0)), pl.BlockSpec((1, d), lambda i: (0, 0))],
        out_specs=[pl.BlockSpec((tm, d), lambda i: (i, 0)), pl.BlockSpec((tm, 1), lambda i: (i, 0))],
        out_shape=[jax.ShapeDtypeStruct((s, d), BF16), jax.ShapeDtypeStruct((s, 1), F32)],
        compiler_params=_cparams("parallel"))(x, g.reshape(1, d))


def _rms_bwd_math(dh, x, r, g):
    xr = x * r
    dyg = dh * g
    m = jnp.mean(dyg * xr, axis=-1, keepdims=True)
    return r * (dyg - xr * m), dh * xr


def _rmsnorm_bwd(dh, x, r, g, dres, *, name):
    s, d = x.shape
    tm = _tile(s, 256)
    has_res = dres is not None

    def body(*refs):
        if has_res:
            dh_ref, x_ref, r_ref, g_ref, res_ref, dx_ref, dg_ref = refs
        else:
            dh_ref, x_ref, r_ref, g_ref, dx_ref, dg_ref = refs
        dx, dgp = _rms_bwd_math(dh_ref[...].astype(F32), x_ref[...], r_ref[...], g_ref[...])
        if has_res:
            dx = dx + res_ref[...]
        dx_ref[...] = dx

        @pl.when(pl.program_id(0) == 0)
        def _():
            dg_ref[...] = jnp.zeros_like(dg_ref)

        dg_ref[...] += jnp.sum(dgp, axis=0, keepdims=True)

    row = pl.BlockSpec((tm, d), lambda i: (i, 0))
    vec = pl.BlockSpec((1, d), lambda i: (0, 0))
    ins = [dh, x, r, g.reshape(1, d)] + ([dres] if has_res else [])
    specs = [row, row, pl.BlockSpec((tm, 1), lambda i: (i, 0)), vec] + ([row] if has_res else [])
    return pl.pallas_call(
        body, name=name, grid=(s // tm,), in_specs=specs, out_specs=[row, vec],
        out_shape=[jax.ShapeDtypeStruct((s, d), F32), jax.ShapeDtypeStruct((1, d), F32)],
        compiler_params=_cparams("arbitrary"))(*ins)


def _final_loss(x, g, target, *, name):
    s, d = x.shape
    tm = _tile(s, 256)

    def body(x_ref, g_ref, t_ref, loss_ref, dx_ref, dg_ref):
        xf = x_ref[...]
        gv = g_ref[...]
        r = lax.rsqrt(jnp.mean(xf * xf, axis=-1, keepdims=True) + RMS_EPS)
        diff = (xf * r) * gv - t_ref[...]
        part = 0.5 * jnp.sum(jnp.mean(diff * diff, axis=-1, keepdims=True))
        dx, dgp = _rms_bwd_math(diff * (1.0 / d), xf, r, gv)
        dx_ref[...] = dx

        @pl.when(pl.program_id(0) == 0)
        def _():
            dg_ref[...] = jnp.zeros_like(dg_ref)
            loss_ref[...] = jnp.zeros_like(loss_ref)

        dg_ref[...] += jnp.sum(dgp, axis=0, keepdims=True)
        loss_ref[...] += part

    row = pl.BlockSpec((tm, d), lambda i: (i, 0))
    vec = pl.BlockSpec((1, d), lambda i: (0, 0))
    return pl.pallas_call(
        body, name=name, grid=(s // tm,), in_specs=[row, vec, row],
        out_specs=[pl.BlockSpec((8, LANES), lambda i: (0, 0)), row, vec],
        out_shape=[jax.ShapeDtypeStruct((8, LANES), F32), jax.ShapeDtypeStruct((s, d), F32), jax.ShapeDtypeStruct((1, d), F32)],
        compiler_params=_cparams("arbitrary"))(x, g.reshape(1, d), target)


SUBLANES = 8
TAP_GROUPS = 4


def _shift_scratch(tm, c):
    return pltpu.VMEM((SUBLANES - 1, tm + CONV_HALO - SUBLANES, c), F32)


def _shift_copies(src_ref, sh_ref):
    rows = sh_ref.shape[1]
    for r in range(1, SUBLANES):
        sh_ref[r - 1] = src_ref[pl.ds(r, rows), :]


def _read_shifted(src_ref, sh_ref, off, r0):
    r = off % SUBLANES
    base = off - r + r0
    return src_ref[pl.ds(base, SUBLANES), :] if r == 0 else sh_ref[r - 1, pl.ds(base, SUBLANES), :]


def _tap_rows(w):
    return [jnp.broadcast_to(w[k:k + 1, :], (SUBLANES, w.shape[1])) for k in range(w.shape[0])]


def _tap_sum(src_ref, sh_ref, wk, offs, init, tm):
    out = []
    for r0 in range(0, tm, SUBLANES * TAP_GROUPS):
        accs = [init] * TAP_GROUPS
        for wv, off in zip(wk, offs):
            accs = [acc + wv * _read_shifted(src_ref, sh_ref, off, r0 + SUBLANES * g) for g, acc in enumerate(accs)]
        out += accs
    return jnp.concatenate(out, axis=0)


def _conv_taps(gp_ref, sh_ref, w, bias, taps, tm):
    halo = gp_ref.shape[0] - tm
    _shift_copies(gp_ref, sh_ref)
    offs = [halo - (taps - 1) + k for k in range(taps)]
    return _tap_sum(gp_ref, sh_ref, _tap_rows(w), offs, jnp.broadcast_to(bias, (SUBLANES, w.shape[1])), tm)


def _conv_core(val, glu, valh, gluh, first, gp_ref, sh_ref, w, bias, lg, lb, taps, tm):
    sg = _sigmoid(glu)
    g = val * sg
    gh = jnp.where(first, 0.0, valh * _sigmoid(gluh))
    gp_ref[0:CONV_HALO, :] = gh
    gp_ref[CONV_HALO:, :] = g
    d = _conv_taps(gp_ref, sh_ref, w, bias, taps, tm)
    mu = jnp.mean(d, axis=-1, keepdims=True)
    dc = d - mu
    rstd = lax.rsqrt(jnp.mean(dc * dc, axis=-1, keepdims=True) + LN_EPS)
    xhat = dc * rstd
    ln = xhat * lg + lb
    return sg, xhat, rstd, ln


def _conv_fwd(ua, dw_w, dw_b, ln_g, ln_b, pw, *, name):
    s = ua.shape[0]
    taps, c = dw_w.shape
    tm = _tile(s, 512)
    hb = tm // CONV_HALO

    def body(val_ref, glu_ref, valh_ref, gluh_ref, w_ref, b_ref, lg_ref, lb_ref, pw_ref, y_ref, gp_ref, sh_ref):
        first = pl.program_id(0) == 0
        _, _, _, ln = _conv_core(val_ref[...], glu_ref[...], valh_ref[...], gluh_ref[...], first, gp_ref, sh_ref,
                                 w_ref[...], b_ref[...], lg_ref[...], lb_ref[...], taps, tm)
        sw = ln * _sigmoid(ln)
        y_ref[...] = _nn(sw.astype(BF16), pw_ref[...])

    cur = lambda col: pl.BlockSpec((tm, c), lambda i: (i, col))
    prev = lambda col: pl.BlockSpec((CONV_HALO, c), lambda i: (jnp.maximum(i * hb - 1, 0), col))
    full = lambda a: pl.BlockSpec(a.shape, lambda i: (0,) * a.ndim)
    vecs = [dw_w, dw_b.reshape(1, c), ln_g.reshape(1, c), ln_b.reshape(1, c), pw]
    return pl.pallas_call(
        body, name=name, grid=(s // tm,),
        in_specs=[cur(0), cur(1), prev(0), prev(1)] + [full(a) for a in vecs],
        out_specs=pl.BlockSpec((tm, c), lambda i: (i, 0)),
        out_shape=jax.ShapeDtypeStruct((s, c), F32),
        scratch_shapes=[pltpu.VMEM((tm + CONV_HALO, c), F32), _shift_scratch(tm, c)],
        compiler_params=_cparams("parallel"))(ua, ua, ua, ua, *vecs)


def _conv_bwd_a(ua, dy, dw_w, dw_b, ln_g, ln_b, pw, *, name):
    s = ua.shape[0]
    taps, c = dw_w.shape
    tm = _tile(s, 512)
    hb = tm // CONV_HALO

    def body(val_ref, glu_ref, valh_ref, gluh_ref, dy_ref, w_ref, b_ref, lg_ref, lb_ref, pw_ref,
             dd_ref, dpw_ref, dlg_ref, dlb_ref, gp_ref, sh_ref):
        first = pl.program_id(0) == 0
        lg = lg_ref[...]
        _, xhat, rstd, ln = _conv_core(val_ref[...], glu_ref[...], valh_ref[...], gluh_ref[...], first, gp_ref, sh_ref,
                                       w_ref[...], b_ref[...], lg, lb_ref[...], taps, tm)
        sw, dsw = _silu_and_grad(ln)
        dyb = dy_ref[...].astype(BF16)
        ds = _nt(dyb, pw_ref[...])
        dln = ds * dsw
        dxhat = dln * lg
        m1 = jnp.mean(dxhat, axis=-1, keepdims=True)
        m2 = jnp.mean(dxhat * xhat, axis=-1, keepdims=True)
        dd_ref[...] = rstd * (dxhat - m1 - xhat * m2)

        @pl.when(first)
        def _():
            dpw_ref[...] = jnp.zeros_like(dpw_ref)
            dlg_ref[...] = jnp.zeros_like(dlg_ref)
            dlb_ref[...] = jnp.zeros_like(dlb_ref)

        dpw_ref[...] += _tn(sw.astype(BF16), dyb)
        dlg_ref[...] += jnp.sum(dln * xhat, axis=0, keepdims=True)
        dlb_ref[...] += jnp.sum(dln, axis=0, keepdims=True)

    cur = lambda col: pl.BlockSpec((tm, c), lambda i: (i, col))
    prev = lambda col: pl.BlockSpec((CONV_HALO, c), lambda i: (jnp.maximum(i * hb - 1, 0), col))
    full = lambda a: pl.BlockSpec(a.shape, lambda i: (0,) * a.ndim)
    vec = pl.BlockSpec((1, c), lambda i: (0, 0))
    vecs = [dw_w, dw_b.reshape(1, c), ln_g.reshape(1, c), ln_b.reshape(1, c), pw]
    return pl.pallas_call(
        body, name=name, grid=(s // tm,),
        in_specs=[cur(0), cur(1), prev(0), prev(1), pl.BlockSpec((tm, c), lambda i: (i, 0))] + [full(a) for a in vecs],
        out_specs=[pl.BlockSpec((tm, c), lambda i: (i, 0)), pl.BlockSpec((c, c), lambda i: (0, 0)), vec, vec],
        out_shape=[jax.ShapeDtypeStruct((s, c), F32), jax.ShapeDtypeStruct((c, c), F32),
                   jax.ShapeDtypeStruct((1, c), F32), jax.ShapeDtypeStruct((1, c), F32)],
        scratch_shapes=[pltpu.VMEM((tm + CONV_HALO, c), F32), _shift_scratch(tm, c)],
        compiler_params=_cparams("arbitrary"))(ua, ua, ua, ua, dy, *vecs)


def _conv_bwd_b(ua, dd, dw_w, *, name):
    s = ua.shape[0]
    taps, c = dw_w.shape
    tm = _tile(s, 512)
    hb = tm // CONV_HALO
    nt = s // tm

    def body(val_ref, glu_ref, valh_ref, gluh_ref, dd_ref, ddn_ref, w_ref, dval_ref, dglu_ref, dw_ref, db_ref,
             gp_ref, ddp_ref, shg_ref, shd_ref):
        i = pl.program_id(0)
        val = val_ref[...]
        sg = _sigmoid(glu_ref[...])
        gp_ref[0:CONV_HALO, :] = jnp.where(i == 0, 0.0, valh_ref[...] * _sigmoid(gluh_ref[...]))
        gp_ref[CONV_HALO:, :] = val * sg
        dd = dd_ref[...]
        ddp_ref[0:tm, :] = dd
        ddp_ref[tm:, :] = jnp.where(i == nt - 1, 0.0, ddn_ref[...])
        _shift_copies(gp_ref, shg_ref)
        _shift_copies(ddp_ref, shd_ref)
        zero = jnp.zeros((SUBLANES, c), F32)
        dg = _tap_sum(ddp_ref, shd_ref, _tap_rows(w_ref[...]), [taps - 1 - k for k in range(taps)], zero, tm)
        dws = []
        for k in range(taps):
            accs = [zero] * TAP_GROUPS
            for n, r0 in enumerate(range(0, tm, SUBLANES)):
                accs[n % TAP_GROUPS] = accs[n % TAP_GROUPS] + ddp_ref[pl.ds(r0, SUBLANES), :] * _read_shifted(
                    gp_ref, shg_ref, CONV_HALO - (taps - 1) + k, r0)
            dws.append(jnp.sum(sum(accs[1:], accs[0]), axis=0, keepdims=True))
        dval_ref[...] = (dg * sg).astype(dval_ref.dtype)
        dglu_ref[...] = (dg * val * sg * (1.0 - sg)).astype(dglu_ref.dtype)

        @pl.when(i == 0)
        def _():
            dw_ref[...] = jnp.zeros_like(dw_ref)
            db_ref[...] = jnp.zeros_like(db_ref)

        dw_ref[...] += jnp.concatenate(dws, axis=0)
        db_ref[...] += jnp.sum(dd, axis=0, keepdims=True)

    cur = lambda col: pl.BlockSpec((tm, c), lambda i: (i, col))
    prev = lambda col: pl.BlockSpec((CONV_HALO, c), lambda i: (jnp.maximum(i * hb - 1, 0), col))
    nxt = pl.BlockSpec((CONV_HALO, c), lambda i: (jnp.minimum((i + 1) * hb, s // CONV_HALO - 1), 0))
    return pl.pallas_call(
        body, name=name, grid=(nt,),
        in_specs=[cur(0), cur(1), prev(0), prev(1), pl.BlockSpec((tm, c), lambda i: (i, 0)), nxt,
                  pl.BlockSpec((taps, c), lambda i: (0, 0))],
        out_specs=[pl.BlockSpec((tm, c), lambda i: (i, 0)), pl.BlockSpec((tm, c), lambda i: (i, 0)),
                   pl.BlockSpec((taps, c), lambda i: (0, 0)), pl.BlockSpec((1, c), lambda i: (0, 0))],
        out_shape=[jax.ShapeDtypeStruct((s, c), BF16), jax.ShapeDtypeStruct((s, c), BF16),
                   jax.ShapeDtypeStruct((taps, c), F32), jax.ShapeDtypeStruct((1, c), F32)],
        scratch_shapes=[pltpu.VMEM((tm + CONV_HALO, c), F32), pltpu.VMEM((tm + CONV_HALO, c), F32),
                        _shift_scratch(tm, c), _shift_scratch(tm, c)],
        compiler_params=_cparams("arbitrary"))(ua, ua, ua, ua, dd, dd, dw_w)


LOG2_E = 1.4426950408889634
SB_HEADS_PER_STEP = 4


def _sb_logs(qk, mask):
    z = qk * (HEAD_DIM ** -0.5 * LOG2_E)
    ls = jnp.minimum(z, 0.0) - jnp.log2(1.0 + jnp.exp2(-jnp.abs(z)))
    lm = ls - z
    if mask is not None:
        lm = jnp.where(mask, lm, 0.0)
    return ls, lm


def _diag_mask(b):
    return lax.broadcasted_iota(jnp.int32, (b, b), 1) < lax.broadcasted_iota(jnp.int32, (b, b), 0)


def _split_dot(x, tri):
    hi = x.astype(BF16)
    lo = (x - hi.astype(F32)).astype(BF16)
    return _nn(hi, tri) + _nn(lo, tri)


def _tri(bk, cmp):
    r = lax.broadcasted_iota(jnp.int32, (bk, bk), 0)
    c = lax.broadcasted_iota(jnp.int32, (bk, bk), 1)
    return cmp(r, c).astype(BF16)


def _sb_fwd(qkv, heads, *, name, blk=256):
    s = qkv.shape[0]
    b = _tile(s, blk)
    nq = s // b
    hp = min(SB_HEADS_PER_STEP, heads)
    assert heads % hp == 0
    groups = heads // hp
    wide = hp * HEAD_DIM

    def body(q_ref, k_ref, v_ref, o_ref, w_hbm, stage, sems):
        g = pl.program_id(0)
        i = pl.program_id(1)
        sls = [slice(n * HEAD_DIM, (n + 1) * HEAD_DIM) for n in range(hp)]
        qs = [q_ref[:, sl] for sl in sls]
        tri = _tri(b, lambda r, c: r > c)
        diag = _diag_mask(b)
        r0 = pl.multiple_of(i * b, b)

        def saves(slot, j):
            c0 = pl.multiple_of(j * b, b)
            return [pltpu.make_async_copy(stage.at[slot, n, w], w_hbm.at[w, g * hp + n, pl.ds(r0, b), pl.ds(c0, b)], sems.at[slot])
                    for n in range(hp) for w in range(2)]

        def tile(t, j, carry, masked):
            slot = t % 2
            if not masked:
                @pl.when(t >= 2)
                def _():
                    for cp in saves(slot, j):
                        cp.wait()

            s0 = pl.multiple_of(j * b, b)
            kbs = [k_ref[pl.ds(s0, b), sl] for sl in sls]
            vbs = [v_ref[pl.ds(s0, b), sl] for sl in sls]
            zs = [_nt(q, kb) for q, kb in zip(qs, kbs)]
            sc = [_sb_logs(z, diag if masked else None) for z in zs]
            after = [_split_dot(lm, tri) for _, lm in sc]
            out = []
            for n, ((ls, lm), af, (acc, c)) in enumerate(zip(sc, after, carry)):
                a = jnp.exp2(ls + (af + c))
                if masked:
                    a = jnp.where(diag, a, 0.0)
                ab = a.astype(BF16)
                stage[slot, n, 0] = ab
                stage[slot, n, 1] = jnp.exp2(ls).astype(BF16)
                out.append((ab, acc, c + jnp.sum(lm, axis=1, keepdims=True)))
            for cp in saves(slot, j):
                cp.start()
            return tuple((acc + _nn(ab, vb), c) for vb, (ab, acc, c) in zip(vbs, out))

        zero = tuple((jnp.zeros((b, HEAD_DIM), F32), jnp.zeros((b, 1), F32)) for _ in range(hp))
        carry = tile(0, i, zero, True)
        carry = lax.fori_loop(0, i, lambda jj, cr: tile(jj + 1, i - 1 - jj, cr, False), carry)
        for sl, (acc, _) in zip(sls, carry):
            o_ref[:, sl] = acc
        for cp in saves(i % 2, 0):
            cp.wait()

        @pl.when(i >= 1)
        def _():
            for cp in saves((i + 1) % 2, 0):
                cp.wait()

    return pl.pallas_call(
        body, name=name, grid=(groups, nq),
        in_specs=[pl.BlockSpec((b, wide), lambda g, i: (i, g)),
                  pl.BlockSpec((s, wide), lambda g, i: (0, groups + g)),
                  pl.BlockSpec((s, wide), lambda g, i: (0, 2 * groups + g))],
        out_specs=[pl.BlockSpec((b, wide), lambda g, i: (i, g)), ANY],
        out_shape=[jax.ShapeDtypeStruct((s, heads * HEAD_DIM), F32), jax.ShapeDtypeStruct((2, heads, s, s), BF16)],
        scratch_shapes=[pltpu.VMEM((2, hp, 2, b, b), BF16), pltpu.SemaphoreType.DMA((2,))],
        compiler_params=_cparams("parallel", "arbitrary"))(qkv, qkv, qkv)


def _sb_bwd(qkv, do, saved, heads, *, name, blk=256):
    s = qkv.shape[0]
    b = _tile(s, blk)
    nq = s // b
    hp = min(SB_HEADS_PER_STEP, heads)
    assert heads % hp == 0
    groups = heads // hp
    wide = hp * HEAD_DIM
    scale = HEAD_DIM ** -0.5

    def body(q_ref, k_ref, v_ref, do_ref, w_hbm, dq_ref, dk_ref, dv_ref, dk_acc, dv_acc, stage, sems):
        g = pl.program_id(0)
        i = pl.program_id(1)

        @pl.when(i == 0)
        def _():
            dk_acc[...] = jnp.zeros_like(dk_acc)
            dv_acc[...] = jnp.zeros_like(dv_acc)

        sls = [slice(n * HEAD_DIM, (n + 1) * HEAD_DIM) for n in range(hp)]
        qs = [q_ref[:, sl] for sl in sls]
        dos = [do_ref[:, sl].astype(BF16) for sl in sls]
        tri_excl = _tri(b, lambda r, c: r < c)
        diag = _diag_mask(b)
        r0 = pl.multiple_of(i * b, b)

        def loads(slot, j):
            c0 = pl.multiple_of(j * b, b)
            return [pltpu.make_async_copy(w_hbm.at[w, g * hp + n, pl.ds(r0, b), pl.ds(c0, b)], stage.at[slot, n, w], sems.at[slot])
                    for n in range(hp) for w in range(2)]

        def tile(j, carry, masked):
            slot = j % 2

            @pl.when(j < i)
            def _():
                for cp in loads(1 - slot, j + 1):
                    cp.start()

            for cp in loads(slot, j):
                cp.wait()
            s0 = pl.multiple_of(j * b, b)
            kbs = [k_ref[pl.ds(s0, b), sl] for sl in sls]
            vbs = [v_ref[pl.ds(s0, b), sl] for sl in sls]
            ab = [stage[slot, n, 0] for n in range(hp)]
            ps = [_nt(dob, vb) for dob, vb in zip(dos, vbs)]
            gs = [a.astype(F32) * p for a, p in zip(ab, ps)]
            hs = [_nn(gg.astype(BF16), tri_excl) for gg in gs]
            dzb = []
            for n, (gg, h, (_, cg)) in enumerate(zip(gs, hs, carry)):
                dz = (gg - (gg + (h + cg)) * stage[slot, n, 1].astype(F32)) * scale
                if masked:
                    dz = jnp.where(diag, dz, 0.0)
                dzb.append(dz.astype(BF16))
            out = tuple((dq + _nn(dz, kb), cg + jnp.sum(gg, axis=1, keepdims=True))
                        for dz, kb, gg, (dq, cg) in zip(dzb, kbs, gs, carry))
            for sl, dz, a, q, dob in zip(sls, dzb, ab, qs, dos):
                dk_acc[pl.ds(s0, b), sl] += _tn(dz, q)
                dv_acc[pl.ds(s0, b), sl] += _tn(a, dob)
            return out

        for cp in loads(0, 0):
            cp.start()
        carry = tuple((jnp.zeros((b, HEAD_DIM), F32), jnp.zeros((b, 1), F32)) for _ in range(hp))
        carry = lax.fori_loop(0, i, lambda j, cr: tile(j, cr, False), carry)
        carry = tile(i, carry, True)
        for sl, (dq, _) in zip(sls, carry):
            dq_ref[:, sl] = dq.astype(dq_ref.dtype)

        @pl.when(i == nq - 1)
        def _():
            dk_ref[...] = dk_acc[...].astype(dk_ref.dtype)
            dv_ref[...] = dv_acc[...].astype(dv_ref.dtype)

    row = pl.BlockSpec((b, wide), lambda g, i: (i, g))
    col = lambda off: pl.BlockSpec((s, wide), lambda g, i: (0, off + g), pipeline_mode=pl.Buffered(1))
    shp = jax.ShapeDtypeStruct((s, heads * HEAD_DIM), BF16)
    return pl.pallas_call(
        body, name=name, grid=(groups, nq),
        in_specs=[row, col(groups), col(2 * groups), row, ANY],
        out_specs=[row, col(0), col(0)], out_shape=[shp, shp, shp],
        scratch_shapes=[pltpu.VMEM((s, wide), F32), pltpu.VMEM((s, wide), F32),
                        pltpu.VMEM((2, hp, 2, b, b), BF16), pltpu.SemaphoreType.DMA((2,))],
        compiler_params=_cparams("parallel", "arbitrary"))(qkv, qkv, qkv, do, saved)


def _shift_rows(x, n, fill, *, down):
    rows = x.shape[0]
    if n % 8 == 0:
        pad = jnp.full((n, x.shape[1]), fill, x.dtype)
        return jnp.concatenate([pad, x[:rows - n]], axis=0) if down else jnp.concatenate([x[n:], pad], axis=0)
    t = lax.broadcasted_iota(jnp.int32, x.shape, 0)
    if down:
        return jnp.where(t >= n, pltpu.roll(x, n, 0), fill)
    return jnp.where(t < rows - n, pltpu.roll(x, rows - n, 0), fill)


def _scan_rows(a, b, *, reverse):
    n = 1
    while n < a.shape[0]:
        b = a * _shift_rows(b, n, 0.0, down=not reverse) + b
        a = a * _shift_rows(a, n, 1.0, down=not reverse)
        n *= 2
    return a, b


def _neg_expm1(x):
    p = 1.0 + x * (1.0 / 7.0)
    for k in (6.0, 5.0, 4.0, 3.0, 2.0):
        p = 1.0 + x * (1.0 / k) * p
    return jnp.where(x > -0.25, -(x * p), 1.0 - jnp.exp(x))


def _softplus_neg(lam):
    z = -lam
    e = jnp.exp(-jnp.abs(z))
    u = 1.0 + e
    d = u - 1.0
    log1p_e = jnp.where(d == 0.0, e, jnp.log(u) * (e / jnp.where(d == 0.0, 1.0, d)))
    return jnp.maximum(z, 0.0) + log1p_e


def _lru_gates(xp_ref, w, bias, wa_ref, ba, wx_ref, bx, sp, taps, tm, heads):
    halo = xp_ref.shape[0] - tm
    xc = jnp.broadcast_to(bias, (tm, w.shape[1]))
    for k in range(taps):
        xc = xc + w[k:k + 1, :] * xp_ref[pl.ds(halo - (taps - 1) + k, tm), :]
    xb = xc.astype(BF16)
    pr, pi = [], []
    for n in range(heads):
        xh = xb[:, n * HEAD_DIM:(n + 1) * HEAD_DIM]
        pr.append(_nn(xh, wa_ref[n]))
        pi.append(_nn(xh, wx_ref[n]))
    r = _sigmoid(jnp.concatenate(pr, axis=1) + ba)
    ig = _sigmoid(jnp.concatenate(pi, axis=1) + bx)
    log_a = (-LRU_C) * r * sp
    a = jnp.exp(log_a)
    mult = jnp.sqrt(_neg_expm1(2.0 * log_a))
    return xc, r, ig, a, mult


def _lru_fwd(ub, x_col, conv_w, conv_b, wa, ba, wx, bx, lam, *, name):
    s = ub.shape[0]
    taps, w = conv_w.shape
    heads = w // HEAD_DIM
    tm = _tile(s, 256)
    hb = tm // LRU_HALO

    def body(x_ref, xh_ref, cw_ref, cb_ref, wa_ref, ba_ref, wx_ref, bx_ref, lam_ref, h_ref, xp_ref, carry_ref):
        i = pl.program_id(0)

        @pl.when(i == 0)
        def _():
            carry_ref[...] = jnp.zeros_like(carry_ref)

        xp_ref[0:LRU_HALO, :] = jnp.where(i == 0, 0.0, xh_ref[...])
        xp_ref[LRU_HALO:, :] = x_ref[...]
        sp = _softplus_neg(lam_ref[...])
        xc, _, ig, a, mult = _lru_gates(xp_ref, cw_ref[...], cb_ref[...], wa_ref, ba_ref[...], wx_ref, bx_ref[...],
                                        sp, taps, tm, heads)
        ac, bc = _scan_rows(a, mult * (ig * xc), reverse=False)
        h = ac * carry_ref[0:1, :] + bc
        h_ref[...] = h
        carry_ref[...] = jnp.broadcast_to(h[tm - 1:tm, :], carry_ref.shape)

    full = lambda arr: pl.BlockSpec(arr.shape, lambda i: (0,) * arr.ndim)
    vecs = [conv_w, conv_b.reshape(1, w), wa.astype(BF16), ba.reshape(1, w), wx.astype(BF16), bx.reshape(1, w), lam.reshape(1, w)]
    return pl.pallas_call(
        body, name=name, grid=(s // tm,),
        in_specs=[pl.BlockSpec((tm, w), lambda i: (i, x_col)),
                  pl.BlockSpec((LRU_HALO, w), lambda i: (jnp.maximum(i * hb - 1, 0), x_col))] + [full(v) for v in vecs],
        out_specs=pl.BlockSpec((tm, w), lambda i: (i, 0)),
        out_shape=jax.ShapeDtypeStruct((s, w), F32),
        scratch_shapes=[pltpu.VMEM((tm + LRU_HALO, w), F32), pltpu.VMEM((8, w), F32)],
        compiler_params=_cparams("arbitrary"))(ub, ub, *vecs)


def _lru_bwd(ub, x_col, h, dh, conv_w, conv_b, wa, ba, wx, bx, lam, *, name):
    s = ub.shape[0]
    taps, w = conv_w.shape
    heads = w // HEAD_DIM
    tm = _tile(s, 256)
    hb = tm // LRU_HALO
    nt = s // tm

    def body(x_ref, xh_ref, h_ref, hh_ref, dh_ref, cw_ref, cb_ref, wa_ref, ba_ref, wx_ref, bx_ref, lam_ref,
             dx_ref, dcw_ref, dcb_ref, dwa_ref, dba_ref, dwx_ref, dbx_ref, dlam_ref,
             xp_ref, dxp_ref, dlt_ref, afirst_ref, dxc_next_ref, dsp_ref):
        step = pl.program_id(0)
        i = nt - 1 - step

        @pl.when(step == 0)
        def _():
            for ref in (dcw_ref, dcb_ref, dwa_ref, dba_ref, dwx_ref, dbx_ref, dlam_ref, dlt_ref, dxc_next_ref, dsp_ref):
                ref[...] = jnp.zeros_like(ref)
            afirst_ref[...] = jnp.ones_like(afirst_ref)

        xp_ref[0:LRU_HALO, :] = jnp.where(i == 0, 0.0, xh_ref[...])
        xp_ref[LRU_HALO:, :] = x_ref[...]
        cw = cw_ref[...]
        lam_v = lam_ref[...]
        sp = _softplus_neg(lam_v)
        xc, r, ig, a, mult = _lru_gates(xp_ref, cw, cb_ref[...], wa_ref, ba_ref[...], wx_ref, bx_ref[...], sp, taps, tm, heads)
        rows = lax.broadcasted_iota(jnp.int32, (tm, w), 0)
        a_next = jnp.where(rows == tm - 1, afirst_ref[0:1, :], _shift_rows(a, 1, 1.0, down=False))
        ac, bc = _scan_rows(a_next, dh_ref[...], reverse=True)
        delta = ac * dlt_ref[0:1, :] + bc
        hv = h_ref[...]
        h_last_prev = jnp.where(i == 0, 0.0, hh_ref[LRU_HALO - 1:LRU_HALO, :])
        h_prev = jnp.where(rows == 0, h_last_prev, _shift_rows(hv, 1, 0.0, down=True))
        gated = ig * xc
        da = delta * h_prev
        dmult = delta * gated
        dgated = delta * mult
        dlog_a = da * a - dmult * (a * a) / mult
        dpr = dlog_a * ((-LRU_C) * sp) * r * (1.0 - r)
        dpi = dgated * xc * ig * (1.0 - ig)
        dxc = dgated * ig
        dsp_ref[...] += jnp.sum(dlog_a * ((-LRU_C) * r), axis=0, keepdims=True)
        dba_ref[...] += jnp.sum(dpr, axis=0, keepdims=True)
        dbx_ref[...] += jnp.sum(dpi, axis=0, keepdims=True)
        xb = xc.astype(BF16)
        dprb = dpr.astype(BF16)
        dpib = dpi.astype(BF16)
        back = []
        for n in range(heads):
            sl = slice(n * HEAD_DIM, (n + 1) * HEAD_DIM)
            dwa_ref[n] += _tn(xb[:, sl], dprb[:, sl])
            dwx_ref[n] += _tn(xb[:, sl], dpib[:, sl])
            back.append(_nt(dprb[:, sl], wa_ref[n]) + _nt(dpib[:, sl], wx_ref[n]))
        dxc = dxc + jnp.concatenate(back, axis=1)
        dxp_ref[0:tm, :] = dxc
        dxp_ref[tm:, :] = dxc_next_ref[...]
        dx = jnp.zeros((tm, w), F32)
        dws = []
        for k in range(taps):
            dx = dx + cw[k:k + 1, :] * dxp_ref[pl.ds(taps - 1 - k, tm), :]
            dws.append(jnp.sum(dxc * xp_ref[pl.ds(LRU_HALO - (taps - 1) + k, tm), :], axis=0, keepdims=True))
        dx_ref[...] = dx.astype(dx_ref.dtype)
        dcw_ref[...] += jnp.concatenate(dws, axis=0)
        dcb_ref[...] += jnp.sum(dxc, axis=0, keepdims=True)
        dlt_ref[...] = jnp.broadcast_to(delta[0:1, :], dlt_ref.shape)
        afirst_ref[...] = jnp.broadcast_to(a[0:1, :], afirst_ref.shape)
        dxc_next_ref[...] = dxc[0:LRU_HALO, :]

        @pl.when(step == nt - 1)
        def _():
            dlam_ref[...] = dsp_ref[...] * (-_sigmoid(-lam_v))

    rev = lambda col: pl.BlockSpec((tm, w), lambda st: (nt - 1 - st, col))
    prev = lambda col: pl.BlockSpec((LRU_HALO, w), lambda st: (jnp.maximum((nt - 1 - st) * hb - 1, 0), col))
    full = lambda arr: pl.BlockSpec(arr.shape, lambda st: (0,) * arr.ndim)
    vec = pl.BlockSpec((1, w), lambda st: (0, 0))
    vecs = [conv_w, conv_b.reshape(1, w), wa.astype(BF16), ba.reshape(1, w), wx.astype(BF16), bx.reshape(1, w), lam.reshape(1, w)]
    vshape = jax.ShapeDtypeStruct((1, w), F32)
    return pl.pallas_call(
        body, name=name, grid=(nt,),
        in_specs=[rev(x_col), prev(x_col), rev(0), prev(0), rev(0)] + [full(v) for v in vecs],
        out_specs=[rev(0), full(conv_w), vec, full(wa), vec, full(wx), vec, vec],
        out_shape=[jax.ShapeDtypeStruct((s, w), BF16), jax.ShapeDtypeStruct(conv_w.shape, F32), vshape,
                   jax.ShapeDtypeStruct(wa.shape, F32), vshape, jax.ShapeDtypeStruct(wx.shape, F32), vshape, vshape],
        scratch_shapes=[pltpu.VMEM((tm + LRU_HALO, w), F32), pltpu.VMEM((tm + LRU_HALO, w), F32),
                        pltpu.VMEM((8, w), F32), pltpu.VMEM((8, w), F32), pltpu.VMEM((LRU_HALO, w), F32), pltpu.VMEM((1, w), F32)],
        compiler_params=_cparams("arbitrary"))(ub, ub, h, h, dh, *vecs)


def _group_fwd(y, w, gate):
    r = lax.rsqrt(jnp.mean(y * y, axis=-1, keepdims=True) + RMS_EPS)
    return ((y * r) * w) * (gate * _sigmoid(gate))


def _mix_out_fwd(y_conv, y_attn, y_lru, ua, ub, n_conv, n_attn, n_lru, *, name):
    s, c = y_conv.shape
    wa_ = y_attn.shape[1]
    d = 2 * c + wa_
    tm = _tile(s, 256)
    assert wa_ == 2 * c

    def body(yc_ref, ya_ref, yl_ref, gc_ref, ga_ref, gl_ref, nc_ref, na_ref, nl_ref, o_ref):
        o_ref[:, 0:c] = _group_fwd(yc_ref[...], nc_ref[...], gc_ref[...]).astype(o_ref.dtype)
        o_ref[:, c:c + wa_] = _group_fwd(ya_ref[...], na_ref[...], ga_ref[...]).astype(o_ref.dtype)
        o_ref[:, c + wa_:] = _group_fwd(yl_ref[...], nl_ref[...], gl_ref[...]).astype(o_ref.dtype)

    blk = lambda width, col: pl.BlockSpec((tm, width), lambda i: (i, col))
    vec = lambda width: pl.BlockSpec((1, width), lambda i: (0, 0))
    return pl.pallas_call(
        body, name=name, grid=(s // tm,),
        in_specs=[blk(c, 0), blk(wa_, 0), blk(c, 0), blk(c, 2), blk(wa_, 0), blk(c, 3), vec(c), vec(wa_), vec(c)],
        out_specs=blk(d, 0), out_shape=jax.ShapeDtypeStruct((s, d), BF16),
        compiler_params=_cparams("parallel"))(y_conv, y_attn, y_lru, ua, ub, ub, n_conv.reshape(1, c), n_attn.reshape(1, wa_), n_lru.reshape(1, c))


def _group_bwd(dout, y, w, gate):
    r = lax.rsqrt(jnp.mean(y * y, axis=-1, keepdims=True) + RMS_EPS)
    silu, dsilu = _silu_and_grad(gate)
    dy, dwp = _rms_bwd_math(dout * silu, y, r, w)
    return dy, dout * ((y * r) * w) * dsilu, dwp


def _mix_out_bwd(dy, y_conv, y_attn, y_lru, ua, ub, n_conv, n_attn, n_lru, *, name):
    s, c = y_conv.shape
    wa_ = y_attn.shape[1]
    tm = _tile(s, 256)

    def body(dy_ref, yc_ref, ya_ref, yl_ref, gc_ref, ga_ref, gl_ref, nc_ref, na_ref, nl_ref,
             dyc_ref, dya_ref, dyl_ref, dgc_ref, dga_ref, dgl_ref, dnc_ref, dna_ref, dnl_ref):
        @pl.when(pl.program_id(0) == 0)
        def _():
            for ref in (dnc_ref, dna_ref, dnl_ref):
                ref[...] = jnp.zeros_like(ref)

        groups = ((dy_ref[:, 0:c], yc_ref, nc_ref, gc_ref, dyc_ref, dgc_ref, dnc_ref),
                  (dy_ref[:, c:c + wa_], ya_ref, na_ref, ga_ref, dya_ref, dga_ref, dna_ref),
                  (dy_ref[:, c + wa_:], yl_ref, nl_ref, gl_ref, dyl_ref, dgl_ref, dnl_ref))
        for dout, y_ref, n_ref, g_ref, dyo_ref, dgo_ref, dn_ref in groups:
            dyv, dgv, dwp = _group_bwd(dout.astype(F32), y_ref[...], n_ref[...], g_ref[...])
            dyo_ref[...] = dyv.astype(dyo_ref.dtype)
            dgo_ref[...] = dgv.astype(dgo_ref.dtype)
            dn_ref[...] += jnp.sum(dwp, axis=0, keepdims=True)

    blk = lambda width, col: pl.BlockSpec((tm, width), lambda i: (i, col))
    vec = lambda width: pl.BlockSpec((1, width), lambda i: (0, 0))
    sh = lambda width, dt: jax.ShapeDtypeStruct((s, width), dt)
    vs = lambda width: jax.ShapeDtypeStruct((1, width), F32)
    return pl.pallas_call(
        body, name=name, grid=(s // tm,),
        in_specs=[blk(2 * c + wa_, 0), blk(c, 0), blk(wa_, 0), blk(c, 0), blk(c, 2), blk(wa_, 0), blk(c, 3), vec(c), vec(wa_), vec(c)],
        out_specs=[blk(c, 0), blk(wa_, 0), blk(c, 0), blk(c, 0), blk(wa_, 0), blk(c, 0), vec(c), vec(wa_), vec(c)],
        out_shape=[sh(c, BF16), sh(wa_, BF16), sh(c, F32), sh(c, BF16), sh(wa_, BF16), sh(c, BF16), vs(c), vs(wa_), vs(c)],
        compiler_params=_cparams("arbitrary"))(dy, y_conv, y_attn, y_lru, ua, ub, ub, n_conv.reshape(1, c), n_attn.reshape(1, wa_), n_lru.reshape(1, c))


def _xattn_probs(qh, kh):
    sc = _nt(qh, kh) * (HEAD_DIM ** -0.5)
    e = jnp.exp(sc - jnp.max(sc, axis=-1, keepdims=True))
    return e / jnp.sum(e, axis=-1, keepdims=True)


def _xattn_fwd(q, kv, *, name):
    s, w = q.shape
    heads = w // HEAD_DIM
    tm = _tile(s, 512)

    def body(q_ref, kv_ref, o_ref):
        for n in range(heads):
            sl = slice(n * HEAD_DIM, (n + 1) * HEAD_DIM)
            p = _xattn_probs(q_ref[:, sl], kv_ref[:, sl])
            o_ref[:, sl] = _nn(p.astype(BF16), kv_ref[:, w + n * HEAD_DIM:w + (n + 1) * HEAD_DIM]).astype(o_ref.dtype)

    return pl.pallas_call(
        body, name=name, grid=(s // tm,),
        in_specs=[pl.BlockSpec((tm, w), lambda i: (i, 0)), pl.BlockSpec(kv.shape, lambda i: (0, 0))],
        out_specs=pl.BlockSpec((tm, w), lambda i: (i, 0)), out_shape=jax.ShapeDtypeStruct((s, w), BF16),
        compiler_params=_cparams("parallel"))(q, kv)


def _xattn_bwd(q, kv, do, *, name):
    s, w = q.shape
    heads = w // HEAD_DIM
    tm = _tile(s, 512)
    scale = HEAD_DIM ** -0.5

    def body(q_ref, kv_ref, do_ref, dq_ref, dkv_ref):
        @pl.when(pl.program_id(0) == 0)
        def _():
            dkv_ref[...] = jnp.zeros_like(dkv_ref)

        for n in range(heads):
            sl = slice(n * HEAD_DIM, (n + 1) * HEAD_DIM)
            vsl = slice(w + n * HEAD_DIM, w + (n + 1) * HEAD_DIM)
            qh, kh, vh, doh = q_ref[:, sl], kv_ref[:, sl], kv_ref[:, vsl], do_ref[:, sl]
            p = _xattn_probs(qh, kh)
            dp = _nt(doh, vh)
            ds = (p * (dp - jnp.sum(dp * p, axis=-1, keepdims=True)) * scale).astype(BF16)
            dq_ref[:, sl] = _nn(ds, kh).astype(dq_ref.dtype)
            dkv_ref[:, sl] += _tn(ds, qh)
            dkv_ref[:, vsl] += _tn(p.astype(BF16), doh)

    row = pl.BlockSpec((tm, w), lambda i: (i, 0))
    kvs = pl.BlockSpec(kv.shape, lambda i: (0, 0))
    return pl.pallas_call(
        body, name=name, grid=(s // tm,), in_specs=[row, kvs, row], out_specs=[row, kvs],
        out_shape=[jax.ShapeDtypeStruct((s, w), BF16), jax.ShapeDtypeStruct(kv.shape, F32)],
        compiler_params=_cparams("arbitrary"))(q, kv, do)


ROW_BLOCK_BYTES = 4 << 20
PACK_ROWS = 512


def _row_tile(rows, cols):
    limit = max(8, ROW_BLOCK_BYTES // (4 * cols))
    t = 8
    while t * 2 <= limit and rows % (t * 2) == 0:
        t *= 2
    assert rows % t == 0
    return t


def _cast_bf16(w, *, name):
    rows, cols = w.shape
    tr = _row_tile(rows, cols)

    def body(w_ref, o_ref):
        o_ref[...] = w_ref[...].astype(BF16)

    blk = pl.BlockSpec((tr, cols), lambda i: (i, 0))
    return pl.pallas_call(body, name=name, grid=(rows // tr,), in_specs=[blk], out_specs=blk,
                          out_shape=jax.ShapeDtypeStruct((rows, cols), BF16), compiler_params=_cparams("parallel"))(w)


def _adamw(w, m, v, gs, *, name):
    rows, cols = w.shape
    tr = _row_tile(rows, cols * 4)
    ng = len(gs)

    def body(*refs):
        w_ref, m_ref, v_ref = refs[:3]
        g_refs = refs[3:3 + ng]
        g_out, d_out, m_out, v_out = refs[3 + ng:]
        g = g_refs[0][...]
        for r in g_refs[1:]:
            g = g + r[...]
        mn = ADAM_B1 * m_ref[...] + (1.0 - ADAM_B1) * g
        vn = ADAM_B2 * v_ref[...] + (1.0 - ADAM_B2) * (g * g)
        m_hat = mn / (1.0 - ADAM_B1 ** ADAM_STEP)
        v_hat = vn / (1.0 - ADAM_B2 ** ADAM_STEP)
        g_out[...] = g
        d_out[...] = -ADAM_LR * (m_hat / (jnp.sqrt(v_hat) + ADAM_EPS) + ADAM_WD * w_ref[...])
        m_out[...] = mn
        v_out[...] = vn

    blk = pl.BlockSpec((tr, cols), lambda i: (i, 0))
    shp = jax.ShapeDtypeStruct((rows, cols), F32)
    return pl.pallas_call(body, name=name, grid=(rows // tr,), in_specs=[blk] * (3 + ng), out_specs=[blk] * 4,
                          out_shape=[shp] * 4, compiler_params=_cparams("parallel"))(w, m, v, *gs)


def _adamw_layers(w, m, v, gs, *, name):
    layers, rows, cols = w.shape
    tr = _row_tile(rows, cols * 4)
    nblk = rows // tr
    ng = len(gs[0])

    def body(*refs):
        w_ref, m_ref, v_ref = refs[:3]
        g_refs = refs[3:3 + layers * ng]
        g_out, d_out, m_out, v_out = refs[3 + layers * ng:]
        l = pl.program_id(0)
        g = jnp.zeros((tr, cols), F32)
        for ll in range(layers):
            gl = g_refs[ll * ng][...]
            for r in g_refs[ll * ng + 1:(ll + 1) * ng]:
                gl = gl + r[...]
            g = jnp.where(l == ll, gl, g)
        mn = ADAM_B1 * m_ref[...] + (1.0 - ADAM_B1) * g
        vn = ADAM_B2 * v_ref[...] + (1.0 - ADAM_B2) * (g * g)
        m_hat = mn / (1.0 - ADAM_B1 ** ADAM_STEP)
        v_hat = vn / (1.0 - ADAM_B2 ** ADAM_STEP)
        g_out[...] = g
        d_out[...] = -ADAM_LR * (m_hat / (jnp.sqrt(v_hat) + ADAM_EPS) + ADAM_WD * w_ref[...])
        m_out[...] = mn
        v_out[...] = vn

    blk = pl.BlockSpec((None, tr, cols), lambda l, i: (l, i, 0))

    def g_spec(ll):
        return pl.BlockSpec((tr, cols), lambda l, i: (jnp.where(l == ll, i, jnp.where(l < ll, 0, nblk - 1)), 0))

    shp = jax.ShapeDtypeStruct(w.shape, F32)
    return pl.pallas_call(
        body, name=name, grid=(layers, nblk), in_specs=[blk] * 3 + [g_spec(ll) for ll in range(layers) for _ in range(ng)],
        out_specs=[blk] * 4, out_shape=[shp] * 4, compiler_params=_cparams("arbitrary", "arbitrary"),
    )(w, m, v, *[g for gl in gs for g in gl])


OTHER_CHIPS = ((1, 0), (0, 1), (1, 1))
N_CHIPS = 4
ANY = pl.BlockSpec(memory_space=pl.ANY)


def _place():
    return lax.axis_index("x"), lax.axis_index("y"), lax.axis_index("c")


def _flip(v, f):
    return 1 - v if f else v


def _part(ref, lead, axis, chip, size):
    idx = list(lead) + [slice(None)] * (len(ref.shape) - len(lead))
    idx[len(lead) + axis] = pl.ds(pl.multiple_of(chip * size, size), size)
    return ref.at[tuple(idx)]


def _allgather_chips(shards, axes, *, name):
    n = len(shards)
    sizes = [sh.shape[ax] for sh, ax in zip(shards, axes)]

    def full_shape(sh, ax):
        return tuple(d * N_CHIPS if i == ax else d for i, d in enumerate(sh.shape))

    def body(*refs):
        ins, outs = refs[:n], refs[n:2 * n]
        send_sems, recv_sems, loc_sems = refs[2 * n:]
        x, y, c = _place()
        me = 2 * x + y
        local = []
        for a in range(n):
            cp = pltpu.make_async_copy(ins[a], _part(outs[a], (), axes[a], me, sizes[a]), loc_sems.at[a])
            cp.start()
            local.append(cp)

        def remote(a, j, chip):
            fx, fy = OTHER_CHIPS[j]
            return pltpu.make_async_remote_copy(
                src_ref=ins[a], dst_ref=_part(outs[a], (), axes[a], chip, sizes[a]),
                send_sem=send_sems.at[a, j], recv_sem=recv_sems.at[a, j],
                device_id=(_flip(x, fx), _flip(y, fy), c), device_id_type=MESH)

        for a in range(n):
            for j in range(len(OTHER_CHIPS)):
                remote(a, j, me).start()
        for a in range(n):
            for j, (fx, fy) in enumerate(OTHER_CHIPS):
                remote(a, j, 2 * _flip(x, fx) + _flip(y, fy)).wait()
        for cp in local:
            cp.wait()

    return pl.pallas_call(
        body, name=name, in_specs=[ANY] * n, out_specs=[ANY] * n,
        out_shape=[jax.ShapeDtypeStruct(full_shape(sh, ax), sh.dtype) for sh, ax in zip(shards, axes)],
        scratch_shapes=[pltpu.SemaphoreType.DMA((n, 3)), pltpu.SemaphoreType.DMA((n, 3)), pltpu.SemaphoreType.DMA((n,))],
    )(*shards)


HBM = pl.BlockSpec(memory_space=pltpu.HBM)
SEM = pl.BlockSpec(memory_space=pltpu.SEMAPHORE)
SPLIT_COPY = pltpu.CompilerParams(has_side_effects=pltpu.SideEffectType.DATAFLOW_SIDE_EFFECTING)


def _cast_place(w, layer, axis, chip, *, name):
    _, rows, cols = w.shape
    tr = _row_tile(rows, cols)
    nblk = rows // tr

    def body(chip_ref, w_ref, o_ref):
        o_ref[...] = w_ref[...].astype(BF16)

    if axis == 1:
        shape = (rows, cols * N_CHIPS)
        o_spec = pl.BlockSpec((tr, cols), lambda i, chip_ref: (i, chip_ref[0]))
    else:
        shape = (rows * N_CHIPS, cols)
        o_spec = pl.BlockSpec((tr, cols), lambda i, chip_ref: (chip_ref[0] * nblk + i, 0))
    return pl.pallas_call(
        body, name=name,
        grid_spec=pltpu.PrefetchScalarGridSpec(
            num_scalar_prefetch=1, grid=(nblk,),
            in_specs=[pl.BlockSpec((None, tr, cols), lambda i, chip_ref: (layer, i, 0))], out_specs=o_spec),
        out_shape=jax.ShapeDtypeStruct(shape, BF16), compiler_params=_cparams("parallel"))(chip, w)


def _half_rows(ref, axis, chip, size, half):
    rows = ref.shape[0] // (N_CHIPS if axis == 0 else 1)
    r0 = chip * rows if axis == 0 else 0
    if half is not None:
        rows //= 2
        r0 = r0 + half * rows
    rsl = pl.ds(r0 if isinstance(r0, int) else pl.multiple_of(r0, 16), rows)
    return ref.at[rsl, :] if axis == 0 else ref.at[rsl, pl.ds(pl.multiple_of(chip * size, LANES), size)]


def _gather_copy(refs, a, j, send_sems, recv_sems, *, axes, sizes, arriving, halves=False):
    x, y, c = _place()
    fx, fy = OTHER_CHIPS[j]
    px, py = _flip(x, fx), _flip(y, fy)
    part = _half_rows(refs[a], axes[a], (2 * px + py) if arriving else (2 * x + y), sizes[a], c if halves else None)
    k = a * len(OTHER_CHIPS) + j
    return pltpu.make_async_remote_copy(src_ref=part, dst_ref=part, send_sem=send_sems.at[k], recv_sem=recv_sems.at[k],
                                        device_id=(px, py, c), device_id_type=MESH)


def _swap_halves(arrs, axes, *, name):
    n = len(arrs)
    sizes = [a.shape[ax] // N_CHIPS for a, ax in zip(arrs, axes)]

    def body(*refs):
        ins = refs[:n]
        send_sems, recv_sems = refs[2 * n:]
        x, y, c = _place()

        def copy(a, j, half):
            fx, fy = OTHER_CHIPS[j]
            part = _half_rows(ins[a], axes[a], 2 * _flip(x, fx) + _flip(y, fy), sizes[a], half)
            k = a * len(OTHER_CHIPS) + j
            return pltpu.make_async_remote_copy(src_ref=part, dst_ref=part, send_sem=send_sems.at[k], recv_sem=recv_sems.at[k],
                                                device_id=(x, y, 1 - c), device_id_type=MESH)

        todo = [(a, j) for a in range(n) for j in range(len(OTHER_CHIPS))]
        for a, j in todo:
            copy(a, j, c).start()
        for a, j in todo:
            copy(a, j, c).wait_send()
            copy(a, j, 1 - c).wait_recv()

    res = pl.pallas_call(
        body, name=name, in_specs=[ANY] * n, out_specs=[ANY] * n, out_shape=[jax.ShapeDtypeStruct(a.shape, a.dtype) for a in arrs],
        input_output_aliases={a: a for a in range(n)},
        scratch_shapes=[pltpu.SemaphoreType.DMA((n * len(OTHER_CHIPS),)), pltpu.SemaphoreType.DMA((n * len(OTHER_CHIPS),))],
    )(*arrs)
    return list(res)


def _scatter_copy(srcs, lands, a, j, axes, sizes, send_sems, recv_sems):
    x, y, c = _place()
    fx, fy = OTHER_CHIPS[j]
    px, py = _flip(x, fx), _flip(y, fy)
    k = a * len(OTHER_CHIPS) + j
    return pltpu.make_async_remote_copy(src_ref=_part(srcs[a], (), axes[a], 2 * px + py, sizes[a]), dst_ref=lands[a].at[j],
                                        send_sem=send_sems.at[k], recv_sem=recv_sems.at[k],
                                        device_id=(px, py, c), device_id_type=MESH)


def _split_start(arrs, make_copy, ncopies, dep, *, name):
    n = len(arrs)

    def body(*refs):
        ins = refs[:n]
        send_sems, recv_sems = refs[n + 1], refs[n + 2]
        token = refs[n + 3 + n]
        for a in range(ncopies):
            for j in range(len(OTHER_CHIPS)):
                make_copy(ins, a, j, send_sems, recv_sems).start()
        token[...] = jnp.zeros_like(token)

    sem = pltpu.SemaphoreType.DMA((ncopies * len(OTHER_CHIPS),))
    res = pl.pallas_call(
        body, name=name,
        out_shape=(sem, sem, *[pltpu.HBM(a.shape, a.dtype) for a in arrs], jax.ShapeDtypeStruct((8, LANES), F32)),
        in_specs=[HBM] * n + [pl.BlockSpec(memory_space=pl.ANY)],
        out_specs=(SEM, SEM, *[HBM] * n, pl.BlockSpec(memory_space=pltpu.VMEM)),
        input_output_aliases={a: 2 + a for a in range(n)}, compiler_params=SPLIT_COPY,
    )(*[pltpu.with_memory_space_constraint(a, pltpu.HBM) for a in arrs], dep)
    return res[0], res[1], list(res[2:2 + n]), res[2 + n]


def _split_wait(arrs, send_sems, recv_sems, make_copy, ncopies, after, *, name):
    n = len(arrs)

    def body(*refs):
        ins = refs[:n]
        send, recv = refs[n], refs[n + 1]
        for a in range(ncopies):
            for j in range(len(OTHER_CHIPS)):
                cp = make_copy(ins, a, j, send, recv)
                cp.wait_send()
                cp.wait_recv()

    res = pl.pallas_call(
        body, name=name, out_shape=tuple(pltpu.HBM(a.shape, a.dtype) for a in arrs),
        in_specs=[HBM] * n + [SEM, SEM, pl.BlockSpec(memory_space=pl.ANY)], out_specs=tuple([HBM] * n),
        input_output_aliases={a: a for a in range(n)}, compiler_params=SPLIT_COPY,
    )(*arrs, send_sems, recv_sems, after)
    return list(res)


def _sum_own_and_slots(g, land, axis, chip, *, name):
    slots, rows, cols = land.shape
    tr = _row_tile(rows, cols * 4)
    nblk = rows // tr

    def body(chip_ref, g_ref, l_ref, o_ref):
        acc = g_ref[...].astype(F32)
        for j in range(slots):
            acc = acc + l_ref[j].astype(F32)
        o_ref[...] = acc

    if axis == 1:
        g_spec = pl.BlockSpec((tr, cols), lambda i, chip_ref: (i, chip_ref[0]))
    else:
        g_spec = pl.BlockSpec((tr, cols), lambda i, chip_ref: (chip_ref[0] * nblk + i, 0))
    return pl.pallas_call(
        body, name=name,
        grid_spec=pltpu.PrefetchScalarGridSpec(
            num_scalar_prefetch=1, grid=(nblk,),
            in_specs=[g_spec, pl.BlockSpec((slots, tr, cols), lambda i, chip_ref: (0, i, 0))],
            out_specs=pl.BlockSpec((tr, cols), lambda i, chip_ref: (i, 0))),
        out_shape=jax.ShapeDtypeStruct((rows, cols), F32), compiler_params=_cparams("parallel"))(chip, g, land)


def _swap_sibling(arrs, *, name):
    n = len(arrs)

    def body(*refs):
        ins, outs = refs[:n], refs[n:2 * n]
        send_sems, recv_sems = refs[2 * n:]
        x, y, c = _place()
        copies = [pltpu.make_async_remote_copy(src_ref=ins[a], dst_ref=outs[a], send_sem=send_sems.at[a], recv_sem=recv_sems.at[a],
                                               device_id=(x, y, 1 - c), device_id_type=MESH) for a in range(n)]
        for cp in copies:
            cp.start()
        for cp in copies:
            cp.wait()

    return pl.pallas_call(
        body, name=name, in_specs=[ANY] * n, out_specs=[ANY] * n,
        out_shape=[jax.ShapeDtypeStruct(a.shape, a.dtype) for a in arrs],
        scratch_shapes=[pltpu.SemaphoreType.DMA((n,)), pltpu.SemaphoreType.DMA((n,))],
    )(*arrs)


def _allreduce_small(p, *, name):
    rows, cols = p.shape
    nrel = len(OTHER_CHIPS)

    def body(p_ref, o_ref, sib_ref, land_ref, send_sems, recv_sems):
        x, y, c = _place()
        me = 2 * x + y
        pair = pltpu.make_async_remote_copy(src_ref=p_ref, dst_ref=sib_ref, send_sem=send_sems.at[nrel], recv_sem=recv_sems.at[nrel],
                                            device_id=(x, y, 1 - c), device_id_type=MESH)
        pair.start()
        pair.wait()
        land_ref[nrel] = p_ref[...] + sib_ref[...]
        copies = []
        for j, (fx, fy) in enumerate(OTHER_CHIPS):
            copies.append(pltpu.make_async_remote_copy(src_ref=land_ref.at[nrel], dst_ref=land_ref.at[j], send_sem=send_sems.at[j],
                                                       recv_sem=recv_sems.at[j], device_id=(_flip(x, fx), _flip(y, fy), c),
                                                       device_id_type=MESH))
        for cp in copies:
            cp.start()
        for cp in copies:
            cp.wait()

        def slot_of(chip):
            r = jnp.bitwise_xor(chip, me)
            return jnp.where(r == 0, nrel, jnp.where(r == 2, 0, jnp.where(r == 1, 1, 2)))

        acc = land_ref[slot_of(0)]
        for chip in range(1, N_CHIPS):
            acc = acc + land_ref[slot_of(chip)]
        o_ref[...] = acc

    vm = pl.BlockSpec(memory_space=pltpu.VMEM)
    return pl.pallas_call(
        body, name=name, in_specs=[vm], out_specs=vm, out_shape=jax.ShapeDtypeStruct((rows, cols), F32),
        scratch_shapes=[pltpu.VMEM((rows, cols), F32), pltpu.VMEM((nrel + 1, rows, cols), F32),
                        pltpu.SemaphoreType.DMA((nrel + 1,)), pltpu.SemaphoreType.DMA((nrel + 1,))],
        compiler_params=pltpu.CompilerParams(vmem_limit_bytes=V7X_VMEM_LIMIT_BYTES))(p)


WEIGHTS = ("mix_norm_g", "w_in", "conv_dw_w", "conv_dw_b", "conv_ln_g", "conv_ln_b", "conv_pw_w", "lru_conv_w", "lru_conv_b",
           "lru_wa", "lru_ba", "lru_wx", "lru_bx", "lru_lambda", "out_norm_conv", "out_norm_attn", "out_norm_lru", "w_out",
           "xattn_norm_g", "mem_norm_g", "xattn_wq", "xattn_wkv", "xattn_wo", "final_norm_g")
BIG = {"w_in": 2, "conv_pw_w": 1, "w_out": 1, "xattn_wq": 1, "xattn_wkv": 1, "xattn_wo": 2}
SMALL_SHARDED = {"conv_dw_w": 2, "lru_conv_w": 2}


IN_GROUP = ("w_in", "conv_pw_w")
REST_GROUP = ("w_out", "xattn_wq", "xattn_wkv", "xattn_wo")


def _trunk(x, mem, target, p, fetch, grads_ready):
    depth = p["mix_norm_g"].shape[0]
    c = p["conv_dw_w"].shape[2]
    aw = p["out_norm_attn"].shape[1]
    heads = aw // HEAD_DIM
    saved = []
    for l in range(depth):
        t = f"l{l}_"
        h1, r1 = _rmsnorm_fwd(x, p["mix_norm_g"][l], name=t + "mix_norm")
        wl = dict(fetch(IN_GROUP, l, r1))
        ua = _matmul(h1, wl["w_in"], mode="nn", n=3 * c, b_off=0, name=t + "in_conv")
        qkv = _matmul(h1, wl["w_in"], mode="nn", n=3 * aw, b_off=3 * c, out_dtype=BF16, name=t + "in_qkv")
        ub = _matmul(h1, wl["w_in"], mode="nn", n=aw + 2 * c, b_off=3 * c + 3 * aw, name=t + "in_gates")
        y_conv = _conv_fwd(ua, p["conv_dw_w"][l], p["conv_dw_b"][l], p["conv_ln_g"][l], p["conv_ln_b"][l], wl["conv_pw_w"],
                           name=t + "conv_fwd")
        y_attn, sbw = _sb_fwd(qkv, heads, name=t + "sb_fwd")
        y_lru = _lru_fwd(ub, aw // c, p["lru_conv_w"][l], p["lru_conv_b"][l], p["lru_wa"][l], p["lru_ba"][l], p["lru_wx"][l],
                         p["lru_bx"][l], p["lru_lambda"][l], name=t + "lru_fwd")
        y = _mix_out_fwd(y_conv, y_attn, y_lru, ua, ub, p["out_norm_conv"][l], p["out_norm_attn"][l], p["out_norm_lru"][l],
                         name=t + "mix_out_fwd")
        wl.update(fetch(REST_GROUP, l, y))
        x2 = _matmul(y, wl["w_out"], mode="nn", add=x, name=t + "out_proj")
        h2, r2 = _rmsnorm_fwd(x2, p["xattn_norm_g"][l], name=t + "xattn_norm")
        qx = _matmul(h2, wl["xattn_wq"], mode="nn", out_dtype=BF16, name=t + "xattn_q")
        memn, rm = _rmsnorm_fwd(mem, p["mem_norm_g"][l], name=t + "mem_norm")
        kv = _matmul(memn, wl["xattn_wkv"], mode="nn", out_dtype=BF16, name=t + "xattn_kv")
        o = _xattn_fwd(qx, kv, name=t + "xattn_fwd")
        x3 = _matmul(o, wl["xattn_wo"], mode="nn", add=x2, name=t + "xattn_o")
        saved.append(dict(x=x, h1=h1, r1=r1, ua=ua, qkv=qkv, ub=ub, y_conv=y_conv, y_attn=y_attn, sbw=sbw, y_lru=y_lru, y=y,
                          x2=x2, h2=h2, r2=r2, qx=qx, memn=memn, rm=rm, kv=kv, o=o, w=wl))
        x = x3

    loss, dx, dg_final = _final_loss(x, p["final_norm_g"], target, name="final_loss")
    small = {k: [None] * depth for k in WEIGHTS if k not in BIG and k != "final_norm_g"}
    token = None
    for l in reversed(range(depth)):
        t = f"l{l}_"
        s = saved[l]
        wl = s["w"]
        do = _matmul(dx, wl["xattn_wo"], mode="nt", out_dtype=BF16, dep=token, name=t + "d_xattn_o")
        dwo = _matmul(s["o"], dx, mode="tn", out_dtype=BF16, name=t + "dw_xattn_o")
        dqx, dkv = _xattn_bwd(s["qx"], s["kv"], do, name=t + "xattn_bwd")
        dwq = _matmul(s["h2"], dqx, mode="tn", out_dtype=BF16, name=t + "dw_xattn_q")
        dh2 = _matmul(dqx, wl["xattn_wq"], mode="nt", out_dtype=BF16, name=t + "d_xattn_q")
        dx2, dg = _rmsnorm_bwd(dh2, s["x2"], s["r2"], p["xattn_norm_g"][l], dx, name=t + "xattn_norm_bwd")
        small["xattn_norm_g"][l] = dg[0]
        dmemn = _matmul(dkv, wl["xattn_wkv"], mode="nt", name=t + "d_xattn_kv")
        dwkv = _matmul(s["memn"], dkv, mode="tn", out_dtype=BF16, name=t + "dw_xattn_kv")
        _, dg = _rmsnorm_bwd(dmemn, mem, s["rm"], p["mem_norm_g"][l], None, name=t + "mem_norm_bwd")
        small["mem_norm_g"][l] = dg[0]
        dwout = _matmul(s["y"], dx2, mode="tn", out_dtype=BF16, name=t + "dw_out_proj")
        token = grads_ready(REST_GROUP, l, dict(w_out=dwout, xattn_wq=dwq, xattn_wkv=dwkv, xattn_wo=dwo))
        dy = _matmul(dx2, wl["w_out"], mode="nt", dep=token, out_dtype=BF16, name=t + "d_out_proj")
        dyc, dya, dyl, dgc, dga, dgl, dnc, dna, dnl = _mix_out_bwd(
            dy, s["y_conv"], s["y_attn"], s["y_lru"], s["ua"], s["ub"], p["out_norm_conv"][l], p["out_norm_attn"][l],
            p["out_norm_lru"][l], name=t + "mix_out_bwd")
        small["out_norm_conv"][l], small["out_norm_attn"][l], small["out_norm_lru"][l] = dnc[0], dna[0], dnl[0]
        dd, dpw, dlg, dlb = _conv_bwd_a(s["ua"], dyc, p["conv_dw_w"][l], p["conv_dw_b"][l], p["conv_ln_g"][l], p["conv_ln_b"][l],
                                        wl["conv_pw_w"], name=t + "conv_bwd_a")
        dval, dglu, ddw, ddb = _conv_bwd_b(s["ua"], dd, p["conv_dw_w"][l], name=t + "conv_bwd_b")
        small["conv_ln_g"][l], small["conv_ln_b"][l], small["conv_dw_w"][l], small["conv_dw_b"][l] = dlg[0], dlb[0], ddw, ddb[0]
        dq, dk, dv = _sb_bwd(s["qkv"], dya, s["sbw"], heads, name=t + "sb_bwd")
        drx, dcw, dcb, dwa, dba, dwx, dbx, dlam = _lru_bwd(
            s["ub"], aw // c, s["y_lru"], dyl, p["lru_conv_w"][l], p["lru_conv_b"][l], p["lru_wa"][l], p["lru_ba"][l],
            p["lru_wx"][l], p["lru_bx"][l], p["lru_lambda"][l], name=t + "lru_bwd")
        small["lru_conv_w"][l], small["lru_conv_b"][l], small["lru_wa"][l], small["lru_ba"][l] = dcw, dcb[0], dwa, dba[0]
        small["lru_wx"][l], small["lru_bx"][l], small["lru_lambda"][l] = dwx, dbx[0], dlam[0]
        du = jnp.concatenate([dval, dglu, dgc, dq, dk, dv, dga, drx, dgl], axis=1)
        dwin = _matmul(s["h1"], du, mode="tn", out_dtype=BF16, tk=4096, name=t + "dw_in")
        token = grads_ready(IN_GROUP, l, dict(w_in=dwin, conv_pw_w=_cast_bf16(dpw, name=t + "cast_dpw")))
        dh1 = _matmul(du, wl["w_in"], mode="nt", dep=token, tk=3328, out_dtype=BF16, name=t + "d_in")
        dx, dg = _rmsnorm_bwd(dh1, s["x"], s["r1"], p["mix_norm_g"][l], dx2, name=t + "mix_norm_bwd")
        small["mix_norm_g"][l] = dg[0]
    small = {k: jnp.stack(v) for k, v in small.items()}
    small["final_norm_g"] = dg_final[0]
    return loss, dx, small


def _pack(arrs):
    flat = jnp.concatenate([a.reshape(-1) for a in arrs])
    pad = (-flat.shape[0]) % (PACK_ROWS * LANES)
    return jnp.pad(flat, (0, pad)).reshape(-1, LANES)


def _unpack(packed, like):
    flat = packed.reshape(-1)
    out, at = [], 0
    for a in like:
        out.append(flat[at:at + a.size].reshape(a.shape))
        at += a.size
    return out


def _as_rows(a):
    return a.reshape(-1, a.shape[-1])


def kernel(x, mem, mix_norm_g, w_in, conv_dw_w, conv_dw_b, conv_ln_g, conv_ln_b, conv_pw_w, lru_conv_w, lru_conv_b, lru_wa, lru_ba, lru_wx, lru_bx, lru_lambda, out_norm_conv, out_norm_attn, out_norm_lru, w_out, xattn_norm_g, mem_norm_g, xattn_wq, xattn_wkv, xattn_wo, final_norm_g, loss_target, m_mix_norm_g, m_w_in, m_conv_dw_w, m_conv_dw_b, m_conv_ln_g, m_conv_ln_b, m_conv_pw_w, m_lru_conv_w, m_lru_conv_b, m_lru_wa, m_lru_ba, m_lru_wx, m_lru_bx, m_lru_lambda, m_out_norm_conv, m_out_norm_attn, m_out_norm_lru, m_w_out, m_xattn_norm_g, m_mem_norm_g, m_xattn_wq, m_xattn_wkv, m_xattn_wo, m_final_norm_g, v_mix_norm_g, v_w_in, v_conv_dw_w, v_conv_dw_b, v_conv_ln_g, v_conv_ln_b, v_conv_pw_w, v_lru_conv_w, v_lru_conv_b, v_lru_wa, v_lru_ba, v_lru_wx, v_lru_bx, v_lru_lambda, v_out_norm_conv, v_out_norm_attn, v_out_norm_lru, v_w_out, v_xattn_norm_g, v_mem_norm_g, v_xattn_wq, v_xattn_wkv, v_xattn_wo, v_final_norm_g):
    given = dict(locals())
    w = {k: given[k] for k in WEIGHTS}
    m = {k: given["m_" + k] for k in WEIGHTS}
    v = {k: given["v_" + k] for k in WEIGHTS}
    depth = mix_norm_g.shape[0]
    chip = 2 * lax.axis_index("x") + lax.axis_index("y")

    chip_arr = chip.astype(jnp.int32).reshape(1)

    p = dict(w)
    p.update(zip(SMALL_SHARDED, _allgather_chips([w[k] for k in SMALL_SHARDED], list(SMALL_SHARDED.values()), name="gather_small")))
    axis2d = {k: BIG[k] - 1 for k in BIG}
    groups = [(IN_GROUP, 0), (REST_GROUP, 0)] + [(IN_GROUP + REST_GROUP, l) for l in range(1, depth)]
    pending, token = {}, p[next(iter(SMALL_SHARDED))]
    for names, l in groups:
        arrs = [_cast_place(w[k], l, axis2d[k], chip_arr, name=f"place{l}_{k}") for k in names]
        axes = [axis2d[k] for k in names]
        sizes = [a.shape[ax] // N_CHIPS for a, ax in zip(arrs, axes)]
        halves = (names, l) == groups[0]
        start = functools.partial(_gather_copy, axes=axes, sizes=sizes, arriving=False, halves=halves)
        land = functools.partial(_gather_copy, axes=axes, sizes=sizes, arriving=True, halves=halves)
        send, recv, arrs, token = _split_start(arrs, start, len(arrs), token, name=f"gather_start{l}_{names[0]}")
        pending[(names[0], l)] = (names, arrs, send, recv, land, axes if halves else None)
    last_token = token
    have = {}

    def fetch(group, l, after):
        key = (group[0], l)
        if key in pending:
            names, arrs, send, recv, land, swap_axes = pending.pop(key)
            after = last_token if (group, l) == groups[0] else after
            arrs = _split_wait(arrs, send, recv, land, len(arrs), after, name=f"gather_wait{l}_{names[0]}")
            if swap_axes is not None:
                arrs = _swap_halves(arrs, swap_axes, name=f"gather_swap{l}_{names[0]}")
            have.update({(k, l): a for k, a in zip(names, arrs)})
        return {k: have[(k, l)] for k in group}

    flying = []
    held = {}

    def grads_ready(group, l, grads):
        held.update({(k, l): g for k, g in grads.items()})
        if l > 0 and group == REST_GROUP:
            return None
        names = [k for k in (IN_GROUP + REST_GROUP if l > 0 else group)]
        srcs = [held[(k, l)] for k in names]
        axes = [axis2d[k] for k in names]
        sizes = [g.shape[ax] // N_CHIPS for g, ax in zip(srcs, axes)]
        lands = [lax.empty((len(OTHER_CHIPS),) + tuple(sz if i == ax else d for i, d in enumerate(g.shape)), g.dtype)
                 for g, ax, sz in zip(srcs, axes, sizes)]
        n = len(names)
        copy = lambda refs, a, j, ss, rs_: _scatter_copy(refs[:n], refs[n:], a, j, axes, sizes, ss, rs_)
        send, recv, arrs, token = _split_start(srcs + lands, copy, n, jnp.zeros((8, LANES), F32), name=f"scatter_start{l}_{names[0]}")
        flying.append((names, l, axes, arrs, send, recv, copy))
        return token

    loss, grad_x, small = _trunk(x[0], mem[0], loss_target[0], p, fetch, grads_ready)
    loss = lax.psum(loss[0, 0], ("x", "y", "c"))

    sums = {}
    out = {}

    def arrive(entry, after):
        names, l, axes, arrs, send, recv, copy = entry
        n = len(names)
        arrs = _split_wait(arrs, send, recv, copy, n, after, name=f"scatter_wait{l}_{names[0]}")
        for k, ax, g, ld in zip(names, axes, arrs[:n], arrs[n:]):
            ld = ld.reshape((len(OTHER_CHIPS), -1, ld.shape[-1]))
            sums[(k, l)] = _sum_own_and_slots(g, ld, ax, chip_arr, name=f"sum{l}_{k}")
        return sums[(names[-1], l)]

    def update(names):
        mine = [sums[(k, l)] for k in names for l in range(depth)]
        theirs = _swap_sibling(mine, name="swap_sums_" + names[0])
        for i, k in enumerate(names):
            gs = [[mine[i * depth + l], theirs[i * depth + l]] for l in range(depth)]
            out[k] = _adamw_layers(w[k], m[k], v[k], gs, name="adamw_" + k)

    after = grad_x
    for entry in flying[:-1]:
        after = arrive(entry, after)
    update(REST_GROUP)

    small_names = [k for k in WEIGHTS if k not in BIG]
    total = _unpack(_allreduce_small(_pack([small[k] for k in small_names]), name="allreduce_small"), [small[k] for k in small_names])
    g_small = dict(zip(small_names, total))
    for k, ax in SMALL_SHARDED.items():
        size = w[k].shape[ax]
        g_small[k] = lax.dynamic_slice_in_dim(g_small[k], chip * size, size, axis=ax)
    res = _adamw(_pack([w[k] for k in small_names]), _pack([m[k] for k in small_names]), _pack([v[k] for k in small_names]),
                 [_pack([g_small[k] for k in small_names])], name="adamw_small")
    last = res[0]
    res = [_unpack(r, [w[k] for k in small_names]) for r in res]
    for i, k in enumerate(small_names):
        out[k] = [r[i] for r in res]

    arrive(flying[-1], last)
    update(IN_GROUP)

    outs = [loss, grad_x[None]]
    for part in range(4):
        outs += [out[k][part] for k in WEIGHTS]
    return tuple(outs)
```

```python
import functools

import jax
import jax.numpy as jnp
from jax import lax
from jax.experimental import pallas as pl
from jax.experimental.pallas import tpu as pltpu

F32 = jnp.float32
BF16 = jnp.bfloat16
MESH = pl.DeviceIdType.MESH

V7X_VMEM_LIMIT_BYTES = 56 * 1024 * 1024
LANES = 128
HEAD_DIM = 128
LRU_C = 8.0
RMS_EPS = 1e-6
LN_EPS = 1e-5
CONV_HALO = 32
LRU_HALO = 8
ADAM_LR = 0.001
ADAM_B1 = 0.9
ADAM_B2 = 0.999
ADAM_EPS = 1e-08
ADAM_WD = 0.01
ADAM_STEP = 10


def _cparams(*sem):
    return pltpu.CompilerParams(dimension_semantics=sem, vmem_limit_bytes=V7X_VMEM_LIMIT_BYTES)


def _tile(n, pref):
    if n <= pref:
        return n
    for t in range(pref - pref % LANES, 0, -LANES):
        if n % t == 0:
            return t
    t = pref
    while n % t:
        t //= 2
    return t


def _dot(a, b, dims):
    return lax.dot_general(a, b, (dims, ((), ())), preferred_element_type=F32)


def _nn(a, b):
    return _dot(a, b, ((1,), (0,)))


def _nt(a, b):
    return _dot(a, b, ((1,), (1,)))


def _tn(a, b):
    return _dot(a, b, ((0,), (0,)))


def _sigmoid(x):
    return jax.nn.sigmoid(x)


def _silu_and_grad(x):
    s = _sigmoid(x)
    return x * s, s * (1.0 + x * (1.0 - s))


def _matmul(a, b, *, mode, name, layer=None, n=None, b_off=0, add=None, dep=None, out_dtype=F32, tm=1024, tn=1024, tk=2048):
    bshape = b.shape if layer is None else b.shape[1:]
    if mode == "nn":
        m, k = a.shape
        n = bshape[1] if n is None else n
    elif mode == "nt":
        m, k = a.shape
        n = bshape[0]
    else:
        k, m = a.shape
        n = bshape[1]
    tm, tk = _tile(m, tm), _tile(k, tk)
    tn = _tile(n, tn)
    while b_off % tn or n % tn:
        tn -= LANES
    nk = k // tk
    off = b_off // tn
    lead = () if layer is None else (None,)
    li = () if layer is None else (layer,)
    if mode == "nn":
        a_spec = pl.BlockSpec((tm, tk), lambda i, j, kk: (i, kk))
        b_spec = pl.BlockSpec(lead + (tk, tn), lambda i, j, kk: li + (kk, j + off))
        dot = _nn
    elif mode == "nt":
        a_spec = pl.BlockSpec((tm, tk), lambda i, j, kk: (i, kk))
        b_spec = pl.BlockSpec(lead + (tn, tk), lambda i, j, kk: li + (j, kk))
        dot = _nt
    else:
        a_spec = pl.BlockSpec((tk, tm), lambda i, j, kk: (kk, i))
        b_spec = pl.BlockSpec(lead + (tk, tn), lambda i, j, kk: li + (kk, j))
        dot = _tn
    o_spec = pl.BlockSpec((tm, tn), lambda i, j, kk: (i, j))
    has_add = add is not None

    def body(*refs):
        refs = refs[:-3] + refs[-2:] if dep is not None else refs
        if has_add:
            a_ref, b_ref, add_ref, o_ref, acc_ref = refs
        else:
            a_ref, b_ref, o_ref, acc_ref = refs
        kk = pl.program_id(2)
        part = dot(a_ref[...].astype(BF16), b_ref[...].astype(BF16))

        @pl.when(kk == 0)
        def _():
            acc_ref[...] = part

        @pl.when(kk > 0)
        def _():
            acc_ref[...] += part

        @pl.when(kk == nk - 1)
        def _():
            r = acc_ref[...]
            if has_add:
                r = r + add_ref[...]
            o_ref[...] = r.astype(o_ref.dtype)

    ins = [a, b] + ([add] if has_add else [])
    specs = [a_spec, b_spec] + ([o_spec] if has_add else [])
    if dep is not None:
        ins.append(dep)
        specs.append(pl.BlockSpec((8, LANES), lambda i, j, kk: (0, 0)))
    return pl.pallas_call(
        body, name=name, grid=(m // tm, n // tn, nk), in_specs=specs, out_specs=o_spec,
        out_shape=jax.ShapeDtypeStruct((m, n), out_dtype), scratch_shapes=[pltpu.VMEM((tm, tn), F32)],
        compiler_params=_cparams("parallel", "parallel", "arbitrary"))(*ins)


def _rmsnorm_fwd(x, g, *, name):
    s, d = x.shape
    tm = _tile(s, 256)

    def body(x_ref, g_ref, h_ref, r_ref):
        xf = x_ref[...]
        r = lax.rsqrt(jnp.mean(xf * xf, axis=-1, keepdims=True) + RMS_EPS)
        h_ref[...] = ((xf * r) * g_ref[...]).astype(h_ref.dtype)
        r_ref[...] = r

    return pl.pallas_call(
        body, name=name, grid=(s // tm,),
        in_specs=[pl.BlockSpec((tm, d), lambda i: (i, 0)), pl.BlockSpec((1, d), lambda i: (0, 0))],
        out_specs=[pl.BlockSpec((tm, d), lambda i: (i, 0)), pl.BlockSpec((tm, 1), lambda i: (i, 0))],
        out_shape=[jax.ShapeDtypeStruct((s, d), BF16), jax.ShapeDtypeStruct((s, 1), F32)],
        compiler_params=_cparams("parallel"))(x, g.reshape(1, d))


def _rms_bwd_math(dh, x, r, g):
    xr = x * r
    dyg = dh * g
    m = jnp.mean(dyg * xr, axis=-1, keepdims=True)
    return r * (dyg - xr * m), dh * xr


def _rmsnorm_bwd(dh, x, r, g, dres, *, name):
    s, d = x.shape
    tm = _tile(s, 256)
    has_res = dres is not None

    def body(*refs):
        if has_res:
            dh_ref, x_ref, r_ref, g_ref, res_ref, dx_ref, dg_ref = refs
        else:
            dh_ref, x_ref, r_ref, g_ref, dx_ref, dg_ref = refs
        dx, dgp = _rms_bwd_math(dh_ref[...].astype(F32), x_ref[...], r_ref[...], g_ref[...])
        if has_res:
            dx = dx + res_ref[...]
        dx_ref[...] = dx

        @pl.when(pl.program_id(0) == 0)
        def _():
            dg_ref[...] = jnp.zeros_like(dg_ref)

        dg_ref[...] += jnp.sum(dgp, axis=0, keepdims=True)

    row = pl.BlockSpec((tm, d), lambda i: (i, 0))
    vec = pl.BlockSpec((1, d), lambda i: (0, 0))
    ins = [dh, x, r, g.reshape(1, d)] + ([dres] if has_res else [])
    specs = [row, row, pl.BlockSpec((tm, 1), lambda i: (i, 0)), vec] + ([row] if has_res else [])
    return pl.pallas_call(
        body, name=name, grid=(s // tm,), in_specs=specs, out_specs=[row, vec],
        out_shape=[jax.ShapeDtypeStruct((s, d), F32), jax.ShapeDtypeStruct((1, d), F32)],
        compiler_params=_cparams("arbitrary"))(*ins)


def _final_loss(x, g, target, *, name):
    s, d = x.shape
    tm = _tile(s, 256)

    def body(x_ref, g_ref, t_ref, loss_ref, dx_ref, dg_ref):
        xf = x_ref[...]
        gv = g_ref[...]
        r = lax.rsqrt(jnp.mean(xf * xf, axis=-1, keepdims=True) + RMS_EPS)
        diff = (xf * r) * gv - t_ref[...]
        part = 0.5 * jnp.sum(jnp.mean(diff * diff, axis=-1, keepdims=True))
        dx, dgp = _rms_bwd_math(diff * (1.0 / d), xf, r, gv)
        dx_ref[...] = dx

        @pl.when(pl.program_id(0) == 0)
        def _():
            dg_ref[...] = jnp.zeros_like(dg_ref)
            loss_ref[...] = jnp.zeros_like(loss_ref)

        dg_ref[...] += jnp.sum(dgp, axis=0, keepdims=True)
        loss_ref[...] += part

    row = pl.BlockSpec((tm, d), lambda i: (i, 0))
    vec = pl.BlockSpec((1, d), lambda i: (0, 0))
    return pl.pallas_call(
        body, name=name, grid=(s // tm,), in_specs=[row, vec, row],
        out_specs=[pl.BlockSpec((8, LANES), lambda i: (0, 0)), row, vec],
        out_shape=[jax.ShapeDtypeStruct((8, LANES), F32), jax.ShapeDtypeStruct((s, d), F32), jax.ShapeDtypeStruct((1, d), F32)],
        compiler_params=_cparams("arbitrary"))(x, g.reshape(1, d), target)


SUBLANES = 8
TAP_GROUPS = 4


def _shift_scratch(tm, c):
    return pltpu.VMEM((SUBLANES - 1, tm + CONV_HALO - SUBLANES, c), F32)


def _shift_copies(src_ref, sh_ref):
    rows = sh_ref.shape[1]
    for r in range(1, SUBLANES):
        sh_ref[r - 1] = src_ref[pl.ds(r, rows), :]


def _read_shifted(src_ref, sh_ref, off, r0):
    r = off % SUBLANES
    base = off - r + r0
    return src_ref[pl.ds(base, SUBLANES), :] if r == 0 else sh_ref[r - 1, pl.ds(base, SUBLANES), :]


def _tap_rows(w):
    return [jnp.broadcast_to(w[k:k + 1, :], (SUBLANES, w.shape[1])) for k in range(w.shape[0])]


def _tap_sum(src_ref, sh_ref, wk, offs, init, tm):
    out = []
    for r0 in range(0, tm, SUBLANES * TAP_GROUPS):
        accs = [init] * TAP_GROUPS
        for wv, off in zip(wk, offs):
            accs = [acc + wv * _read_shifted(src_ref, sh_ref, off, r0 + SUBLANES * g) for g, acc in enumerate(accs)]
        out += accs
    return jnp.concatenate(out, axis=0)


def _conv_taps(gp_ref, sh_ref, w, bias, taps, tm):
    halo = gp_ref.shape[0] - tm
    _shift_copies(gp_ref, sh_ref)
    offs = [halo - (taps - 1) + k for k in range(taps)]
    return _tap_sum(gp_ref, sh_ref, _tap_rows(w), offs, jnp.broadcast_to(bias, (SUBLANES, w.shape[1])), tm)


def _conv_core(val, glu, valh, gluh, first, gp_ref, sh_ref, w, bias, lg, lb, taps, tm):
    sg = _sigmoid(glu)
    g = val * sg
    gh = jnp.where(first, 0.0, valh * _sigmoid(gluh))
    gp_ref[0:CONV_HALO, :] = gh
    gp_ref[CONV_HALO:, :] = g
    d = _conv_taps(gp_ref, sh_ref, w, bias, taps, tm)
    mu = jnp.mean(d, axis=-1, keepdims=True)
    dc = d - mu
    rstd = lax.rsqrt(jnp.mean(dc * dc, axis=-1, keepdims=True) + LN_EPS)
    xhat = dc * rstd
    ln = xhat * lg + lb
    return sg, xhat, rstd, ln


def _conv_fwd(ua, dw_w, dw_b, ln_g, ln_b, pw, *, name):
    s = ua.shape[0]
    taps, c = dw_w.shape
    tm = _tile(s, 512)
    hb = tm // CONV_HALO

    def body(val_ref, glu_ref, valh_ref, gluh_ref, w_ref, b_ref, lg_ref, lb_ref, pw_ref, y_ref, gp_ref, sh_ref):
        first = pl.program_id(0) == 0
        _, _, _, ln = _conv_core(val_ref[...], glu_ref[...], valh_ref[...], gluh_ref[...], first, gp_ref, sh_ref,
                                 w_ref[...], b_ref[...], lg_ref[...], lb_ref[...], taps, tm)
        sw = ln * _sigmoid(ln)
        y_ref[...] = _nn(sw.astype(BF16), pw_ref[...])

    cur = lambda col: pl.BlockSpec((tm, c), lambda i: (i, col))
    prev = lambda col: pl.BlockSpec((CONV_HALO, c), lambda i: (jnp.maximum(i * hb - 1, 0), col))
    full = lambda a: pl.BlockSpec(a.shape, lambda i: (0,) * a.ndim)
    vecs = [dw_w, dw_b.reshape(1, c), ln_g.reshape(1, c), ln_b.reshape(1, c), pw]
    return pl.pallas_call(
        body, name=name, grid=(s // tm,),
        in_specs=[cur(0), cur(1), prev(0), prev(1)] + [full(a) for a in vecs],
        out_specs=pl.BlockSpec((tm, c), lambda i: (i, 0)),
        out_shape=jax.ShapeDtypeStruct((s, c), F32),
        scratch_shapes=[pltpu.VMEM((tm + CONV_HALO, c), F32), _shift_scratch(tm, c)],
        compiler_params=_cparams("parallel"))(ua, ua, ua, ua, *vecs)


def _conv_bwd_a(ua, dy, dw_w, dw_b, ln_g, ln_b, pw, *, name):
    s = ua.shape[0]
    taps, c = dw_w.shape
    tm = _tile(s, 512)
    hb = tm // CONV_HALO

    def body(val_ref, glu_ref, valh_ref, gluh_ref, dy_ref, w_ref, b_ref, lg_ref, lb_ref, pw_ref,
             dd_ref, dpw_ref, dlg_ref, dlb_ref, gp_ref, sh_ref):
        first = pl.program_id(0) == 0
        lg = lg_ref[...]
        _, xhat, rstd, ln = _conv_core(val_ref[...], glu_ref[...], valh_ref[...], gluh_ref[...], first, gp_ref, sh_ref,
                                       w_ref[...], b_ref[...], lg, lb_ref[...], taps, tm)
        sw, dsw = _silu_and_grad(ln)
        dyb = dy_ref[...].astype(BF16)
        ds = _nt(dyb, pw_ref[...])
        dln = ds * dsw
        dxhat = dln * lg
        m1 = jnp.mean(dxhat, axis=-1, keepdims=True)
        m2 = jnp.mean(dxhat * xhat, axis=-1, keepdims=True)
        dd_ref[...] = rstd * (dxhat - m1 - xhat * m2)

        @pl.when(first)
        def _():
            dpw_ref[...] = jnp.zeros_like(dpw_ref)
            dlg_ref[...] = jnp.zeros_like(dlg_ref)
            dlb_ref[...] = jnp.zeros_like(dlb_ref)

        dpw_ref[...] += _tn(sw.astype(BF16), dyb)
        dlg_ref[...] += jnp.sum(dln * xhat, axis=0, keepdims=True)
        dlb_ref[...] += jnp.sum(dln, axis=0, keepdims=True)

    cur = lambda col: pl.BlockSpec((tm, c), lambda i: (i, col))
    prev = lambda col: pl.BlockSpec((CONV_HALO, c), lambda i: (jnp.maximum(i * hb - 1, 0), col))
    full = lambda a: pl.BlockSpec(a.shape, lambda i: (0,) * a.ndim)
    vec = pl.BlockSpec((1, c), lambda i: (0, 0))
    vecs = [dw_w, dw_b.reshape(1, c), ln_g.reshape(1, c), ln_b.reshape(1, c), pw]
    return pl.pallas_call(
        body, name=name, grid=(s // tm,),
        in_specs=[cur(0), cur(1), prev(0), prev(1), pl.BlockSpec((tm, c), lambda i: (i, 0))] + [full(a) for a in vecs],
        out_specs=[pl.BlockSpec((tm, c), lambda i: (i, 0)), pl.BlockSpec((c, c), lambda i: (0, 0)), vec, vec],
        out_shape=[jax.ShapeDtypeStruct((s, c), F32), jax.ShapeDtypeStruct((c, c), F32),
                   jax.ShapeDtypeStruct((1, c), F32), jax.ShapeDtypeStruct((1, c), F32)],
        scratch_shapes=[pltpu.VMEM((tm + CONV_HALO, c), F32), _shift_scratch(tm, c)],
        compiler_params=_cparams("arbitrary"))(ua, ua, ua, ua, dy, *vecs)


def _conv_bwd_b(ua, dd, dw_w, du, *, name):
    s = ua.shape[0]
    taps, c = dw_w.shape
    tm = _tile(s, 512)
    hb = tm // CONV_HALO
    nt = s // tm

    def body(val_ref, glu_ref, valh_ref, gluh_ref, dd_ref, ddn_ref, w_ref, du_in, du_ref, dw_ref, db_ref,
             gp_ref, ddp_ref, shg_ref, shd_ref):
        i = pl.program_id(0)
        val = val_ref[...]
        sg = _sigmoid(glu_ref[...])
        gp_ref[0:CONV_HALO, :] = jnp.where(i == 0, 0.0, valh_ref[...] * _sigmoid(gluh_ref[...]))
        gp_ref[CONV_HALO:, :] = val * sg
        dd = dd_ref[...]
        ddp_ref[0:tm, :] = dd
        ddp_ref[tm:, :] = jnp.where(i == nt - 1, 0.0, ddn_ref[...])
        _shift_copies(gp_ref, shg_ref)
        _shift_copies(ddp_ref, shd_ref)
        zero = jnp.zeros((SUBLANES, c), F32)
        dg = _tap_sum(ddp_ref, shd_ref, _tap_rows(w_ref[...]), [taps - 1 - k for k in range(taps)], zero, tm)
        dws = []
        for k in range(taps):
            accs = [zero] * TAP_GROUPS
            for n, r0 in enumerate(range(0, tm, SUBLANES)):
                accs[n % TAP_GROUPS] = accs[n % TAP_GROUPS] + ddp_ref[pl.ds(r0, SUBLANES), :] * _read_shifted(
                    gp_ref, shg_ref, CONV_HALO - (taps - 1) + k, r0)
            dws.append(jnp.sum(sum(accs[1:], accs[0]), axis=0, keepdims=True))
        du_ref[:, 0:c] = (dg * sg).astype(du_ref.dtype)
        du_ref[:, c:] = (dg * val * sg * (1.0 - sg)).astype(du_ref.dtype)

        @pl.when(i == 0)
        def _():
            dw_ref[...] = jnp.zeros_like(dw_ref)
            db_ref[...] = jnp.zeros_like(db_ref)

        dw_ref[...] += jnp.concatenate(dws, axis=0)
        db_ref[...] += jnp.sum(dd, axis=0, keepdims=True)

    cur = lambda col: pl.BlockSpec((tm, c), lambda i: (i, col))
    prev = lambda col: pl.BlockSpec((CONV_HALO, c), lambda i: (jnp.maximum(i * hb - 1, 0), col))
    nxt = pl.BlockSpec((CONV_HALO, c), lambda i: (jnp.minimum((i + 1) * hb, s // CONV_HALO - 1), 0))
    return pl.pallas_call(
        body, name=name, grid=(nt,),
        in_specs=[cur(0), cur(1), prev(0), prev(1), pl.BlockSpec((tm, c), lambda i: (i, 0)), nxt,
                  pl.BlockSpec((taps, c), lambda i: (0, 0)), ANY],
        out_specs=[pl.BlockSpec((tm, 2 * c), lambda i: (i, 0)),
                   pl.BlockSpec((taps, c), lambda i: (0, 0)), pl.BlockSpec((1, c), lambda i: (0, 0))],
        out_shape=[jax.ShapeDtypeStruct(du.shape, du.dtype), jax.ShapeDtypeStruct((taps, c), F32), jax.ShapeDtypeStruct((1, c), F32)],
        input_output_aliases={7: 0},
        scratch_shapes=[pltpu.VMEM((tm + CONV_HALO, c), F32), pltpu.VMEM((tm + CONV_HALO, c), F32),
                        _shift_scratch(tm, c), _shift_scratch(tm, c)],
        compiler_params=_cparams("arbitrary"))(ua, ua, ua, ua, dd, dd, dw_w, du)


LOG2_E = 1.4426950408889634
SB_HEADS_PER_STEP = 4


def _sb_logs(qk, mask):
    z = qk * (HEAD_DIM ** -0.5 * LOG2_E)
    ls = jnp.minimum(z, 0.0) - jnp.log2(1.0 + jnp.exp2(-jnp.abs(z)))
    lm = ls - z
    if mask is not None:
        lm = jnp.where(mask, lm, 0.0)
    return ls, lm


def _diag_mask(b):
    return lax.broadcasted_iota(jnp.int32, (b, b), 1) < lax.broadcasted_iota(jnp.int32, (b, b), 0)


def _split_dot(x, tri):
    hi = x.astype(BF16)
    lo = (x - hi.astype(F32)).astype(BF16)
    return _nn(hi, tri) + _nn(lo, tri)


def _tri(bk, cmp):
    r = lax.broadcasted_iota(jnp.int32, (bk, bk), 0)
    c = lax.broadcasted_iota(jnp.int32, (bk, bk), 1)
    return cmp(r, c).astype(BF16)


def _sb_fwd(qkv, heads, *, name, blk=256):
    s = qkv.shape[0]
    b = _tile(s, blk)
    nq = s // b
    hp = min(SB_HEADS_PER_STEP, heads)
    assert heads % hp == 0
    groups = heads // hp
    wide = hp * HEAD_DIM

    def body(q_ref, k_ref, v_ref, o_ref, w_hbm, stage, sems):
        g = pl.program_id(0)
        i = pl.program_id(1)
        sls = [slice(n * HEAD_DIM, (n + 1) * HEAD_DIM) for n in range(hp)]
        qs = [q_ref[:, sl] for sl in sls]
        tri = _tri(b, lambda r, c: r > c)
        diag = _diag_mask(b)
        r0 = pl.multiple_of(i * b, b)

        def saves(slot, j):
            c0 = pl.multiple_of(j * b, b)
            return [pltpu.make_async_copy(stage.at[slot, n, w], w_hbm.at[w, g * hp + n, pl.ds(r0, b), pl.ds(c0, b)], sems.at[slot])
                    for n in range(hp) for w in range(2)]

        def tile(t, j, carry, masked):
            slot = t % 2
            if not masked:
                @pl.when(t >= 2)
                def _():
                    for cp in saves(slot, j):
                        cp.wait()

            s0 = pl.multiple_of(j * b, b)
            kbs = [k_ref[pl.ds(s0, b), sl] for sl in sls]
            vbs = [v_ref[pl.ds(s0, b), sl] for sl in sls]
            zs = [_nt(q, kb) for q, kb in zip(qs, kbs)]
            sc = [_sb_logs(z, diag if masked else None) for z in zs]
            after = [_split_dot(lm, tri) for _, lm in sc]
            out = []
            for n, ((ls, lm), af, (acc, c)) in enumerate(zip(sc, after, carry)):
                a = jnp.exp2(ls + (af + c))
                if masked:
                    a = jnp.where(diag, a, 0.0)
                ab = a.astype(BF16)
                stage[slot, n, 0] = ab
                stage[slot, n, 1] = jnp.exp2(ls).astype(BF16)
                out.append((ab, acc, c + jnp.sum(lm, axis=1, keepdims=True)))
            for cp in saves(slot, j):
                cp.start()
            return tuple((acc + _nn(ab, vb), c) for vb, (ab, acc, c) in zip(vbs, out))

        zero = tuple((jnp.zeros((b, HEAD_DIM), F32), jnp.zeros((b, 1), F32)) for _ in range(hp))
        carry = tile(0, i, zero, True)
        carry = lax.fori_loop(0, i, lambda jj, cr: tile(jj + 1, i - 1 - jj, cr, False), carry)
        for sl, (acc, _) in zip(sls, carry):
            o_ref[:, sl] = acc
        for cp in saves(i % 2, 0):
            cp.wait()

        @pl.when(i >= 1)
        def _():
            for cp in saves((i + 1) % 2, 0):
                cp.wait()

    return pl.pallas_call(
        body, name=name, grid=(groups, nq),
        in_specs=[pl.BlockSpec((b, wide), lambda g, i: (i, g)),
                  pl.BlockSpec((s, wide), lambda g, i: (0, groups + g)),
                  pl.BlockSpec((s, wide), lambda g, i: (0, 2 * groups + g))],
        out_specs=[pl.BlockSpec((b, wide), lambda g, i: (i, g)), ANY],
        out_shape=[jax.ShapeDtypeStruct((s, heads * HEAD_DIM), F32), jax.ShapeDtypeStruct((2, heads, s, s), BF16)],
        scratch_shapes=[pltpu.VMEM((2, hp, 2, b, b), BF16), pltpu.SemaphoreType.DMA((2,))],
        compiler_params=_cparams("parallel", "arbitrary"))(qkv, qkv, qkv)


def _sb_bwd(qkv, do, saved, du, du_col, heads, *, name, blk=256):
    s = qkv.shape[0]
    b = _tile(s, blk)
    nq = s // b
    hp = min(SB_HEADS_PER_STEP, heads)
    assert heads % hp == 0
    groups = heads // hp
    wide = hp * HEAD_DIM
    scale = HEAD_DIM ** -0.5

    def body(q_ref, k_ref, v_ref, do_ref, w_hbm, du_in, du_ref, dk_acc, dv_acc, stage, sems, dq_out, dkv_out, out_sems):
        g = pl.program_id(0)
        i = pl.program_id(1)

        @pl.when(i == 0)
        def _():
            dk_acc[...] = jnp.zeros_like(dk_acc)
            dv_acc[...] = jnp.zeros_like(dv_acc)

        sls = [slice(n * HEAD_DIM, (n + 1) * HEAD_DIM) for n in range(hp)]
        qs = [q_ref[:, sl] for sl in sls]
        dos = [do_ref[:, sl].astype(BF16) for sl in sls]
        tri_excl = _tri(b, lambda r, c: r < c)
        diag = _diag_mask(b)
        r0 = pl.multiple_of(i * b, b)

        def loads(slot, j):
            c0 = pl.multiple_of(j * b, b)
            return [pltpu.make_async_copy(w_hbm.at[w, g * hp + n, pl.ds(r0, b), pl.ds(c0, b)], stage.at[slot, n, w], sems.at[slot])
                    for n in range(hp) for w in range(2)]

        def tile(j, carry, masked):
            slot = j % 2

            @pl.when(j < i)
            def _():
                for cp in loads(1 - slot, j + 1):
                    cp.start()

            for cp in loads(slot, j):
                cp.wait()
            s0 = pl.multiple_of(j * b, b)
            kbs = [k_ref[pl.ds(s0, b), sl] for sl in sls]
            vbs = [v_ref[pl.ds(s0, b), sl] for sl in sls]
            ab = [stage[slot, n, 0] for n in range(hp)]
            ps = [_nt(dob, vb) for dob, vb in zip(dos, vbs)]
            gs = [a.astype(F32) * p for a, p in zip(ab, ps)]
            hs = [_nn(gg.astype(BF16), tri_excl) for gg in gs]
            dzb = []
            for n, (gg, h, (_, cg)) in enumerate(zip(gs, hs, carry)):
                dz = (gg - (gg + (h + cg)) * stage[slot, n, 1].astype(F32)) * scale
                if masked:
                    dz = jnp.where(diag, dz, 0.0)
                dzb.append(dz.astype(BF16))
            out = tuple((dq + _nn(dz, kb), cg + jnp.sum(gg, axis=1, keepdims=True))
                        for dz, kb, gg, (dq, cg) in zip(dzb, kbs, gs, carry))
            for sl, dz, a, q, dob in zip(sls, dzb, ab, qs, dos):
                dk_acc[pl.ds(s0, b), sl] += _tn(dz, q)
                dv_acc[pl.ds(s0, b), sl] += _tn(a, dob)
            return out

        for cp in loads(0, 0):
            cp.start()
        carry = tuple((jnp.zeros((b, HEAD_DIM), F32), jnp.zeros((b, 1), F32)) for _ in range(hp))
        carry = lax.fori_loop(0, i, lambda j, cr: tile(j, cr, False), carry)
        carry = tile(i, carry, True)
        cols = [pl.multiple_of(du_col + n * heads * HEAD_DIM + g * wide, LANES) for n in range(3)]
        for sl, (dq, _) in zip(sls, carry):
            dq_out[:, sl] = dq.astype(dq_out.dtype)
        put = pltpu.make_async_copy(dq_out, du_ref.at[pl.ds(r0, b), pl.ds(cols[0], wide)], out_sems.at[0])
        put.start()
        put.wait()

        @pl.when(i == nq - 1)
        def _():
            dkv_out[0] = dk_acc[...].astype(dkv_out.dtype)
            dkv_out[1] = dv_acc[...].astype(dkv_out.dtype)
            puts = [pltpu.make_async_copy(dkv_out.at[n], du_ref.at[:, pl.ds(cols[1 + n], wide)], out_sems.at[1 + n]) for n in range(2)]
            for cp in puts:
                cp.start()
            for cp in puts:
                cp.wait()

    row = pl.BlockSpec((b, wide), lambda g, i: (i, g))
    col = lambda off: pl.BlockSpec((s, wide), lambda g, i: (0, off + g), pipeline_mode=pl.Buffered(1))
    return pl.pallas_call(
        body, name=name, grid=(groups, nq),
        in_specs=[row, col(groups), col(2 * groups), row, ANY, ANY],
        out_specs=ANY, out_shape=jax.ShapeDtypeStruct(du.shape, du.dtype), input_output_aliases={5: 0},
        scratch_shapes=[pltpu.VMEM((s, wide), F32), pltpu.VMEM((s, wide), F32),
                        pltpu.VMEM((2, hp, 2, b, b), BF16), pltpu.SemaphoreType.DMA((2,)),
                        pltpu.VMEM((b, wide), BF16), pltpu.VMEM((2, s, wide), BF16), pltpu.SemaphoreType.DMA((3,))],
        compiler_params=_cparams("arbitrary", "arbitrary"))(qkv, qkv, qkv, do, saved, du)


def _shift_rows(x, n, fill, *, down):
    rows = x.shape[0]
    if n % 8 == 0:
        pad = jnp.full((n, x.shape[1]), fill, x.dtype)
        return jnp.concatenate([pad, x[:rows - n]], axis=0) if down else jnp.concatenate([x[n:], pad], axis=0)
    t = lax.broadcasted_iota(jnp.int32, x.shape, 0)
    if down:
        return jnp.where(t >= n, pltpu.roll(x, n, 0), fill)
    return jnp.where(t < rows - n, pltpu.roll(x, rows - n, 0), fill)


def _scan_rows(a, b, *, reverse):
    n = 1
    while n < a.shape[0]:
        b = a * _shift_rows(b, n, 0.0, down=not reverse) + b
        a = a * _shift_rows(a, n, 1.0, down=not reverse)
        n *= 2
    return a, b


def _neg_expm1(x):
    p = 1.0 + x * (1.0 / 7.0)
    for k in (6.0, 5.0, 4.0, 3.0, 2.0):
        p = 1.0 + x * (1.0 / k) * p
    return jnp.where(x > -0.25, -(x * p), 1.0 - jnp.exp(x))


def _softplus_neg(lam):
    z = -lam
    e = jnp.exp(-jnp.abs(z))
    u = 1.0 + e
    d = u - 1.0
    log1p_e = jnp.where(d == 0.0, e, jnp.log(u) * (e / jnp.where(d == 0.0, 1.0, d)))
    return jnp.maximum(z, 0.0) + log1p_e


def _lru_gates(xp_ref, w, bias, wa_ref, ba, wx_ref, bx, sp, taps, tm, heads):
    halo = xp_ref.shape[0] - tm
    xc = jnp.broadcast_to(bias, (tm, w.shape[1]))
    for k in range(taps):
        xc = xc + w[k:k + 1, :] * xp_ref[pl.ds(halo - (taps - 1) + k, tm), :]
    xb = xc.astype(BF16)
    pr, pi = [], []
    for n in range(heads):
        xh = xb[:, n * HEAD_DIM:(n + 1) * HEAD_DIM]
        pr.append(_nn(xh, wa_ref[n]))
        pi.append(_nn(xh, wx_ref[n]))
    r = _sigmoid(jnp.concatenate(pr, axis=1) + ba)
    ig = _sigmoid(jnp.concatenate(pi, axis=1) + bx)
    log_a = (-LRU_C) * r * sp
    a = jnp.exp(log_a)
    mult = jnp.sqrt(_neg_expm1(2.0 * log_a))
    return xc, r, ig, a, mult


def _lru_fwd(ub, x_col, conv_w, conv_b, wa, ba, wx, bx, lam, *, name):
    s = ub.shape[0]
    taps, w = conv_w.shape
    heads = w // HEAD_DIM
    tm = _tile(s, 256)
    hb = tm // LRU_HALO

    def body(x_ref, xh_ref, cw_ref, cb_ref, wa_ref, ba_ref, wx_ref, bx_ref, lam_ref, h_ref, xp_ref, carry_ref):
        i = pl.program_id(0)

        @pl.when(i == 0)
        def _():
            carry_ref[...] = jnp.zeros_like(carry_ref)

        xp_ref[0:LRU_HALO, :] = jnp.where(i == 0, 0.0, xh_ref[...])
        xp_ref[LRU_HALO:, :] = x_ref[...]
        sp = _softplus_neg(lam_ref[...])
        xc, _, ig, a, mult = _lru_gates(xp_ref, cw_ref[...], cb_ref[...], wa_ref, ba_ref[...], wx_ref, bx_ref[...],
                                        sp, taps, tm, heads)
        ac, bc = _scan_rows(a, mult * (ig * xc), reverse=False)
        h = ac * carry_ref[0:1, :] + bc
        h_ref[...] = h
        carry_ref[...] = jnp.broadcast_to(h[tm - 1:tm, :], carry_ref.shape)

    full = lambda arr: pl.BlockSpec(arr.shape, lambda i: (0,) * arr.ndim)
    vecs = [conv_w, conv_b.reshape(1, w), wa.astype(BF16), ba.reshape(1, w), wx.astype(BF16), bx.reshape(1, w), lam.reshape(1, w)]
    return pl.pallas_call(
        body, name=name, grid=(s // tm,),
        in_specs=[pl.BlockSpec((tm, w), lambda i: (i, x_col)),
                  pl.BlockSpec((LRU_HALO, w), lambda i: (jnp.maximum(i * hb - 1, 0), x_col))] + [full(v) for v in vecs],
        out_specs=pl.BlockSpec((tm, w), lambda i: (i, 0)),
        out_shape=jax.ShapeDtypeStruct((s, w), F32),
        scratch_shapes=[pltpu.VMEM((tm + LRU_HALO, w), F32), pltpu.VMEM((8, w), F32)],
        compiler_params=_cparams("arbitrary"))(ub, ub, *vecs)


def _lru_bwd(ub, x_col, h, dh, conv_w, conv_b, wa, ba, wx, bx, lam, du, du_col, *, name):
    s = ub.shape[0]
    taps, w = conv_w.shape
    heads = w // HEAD_DIM
    tm = _tile(s, 256)
    hb = tm // LRU_HALO
    nt = s // tm

    def body(x_ref, xh_ref, h_ref, hh_ref, dh_ref, cw_ref, cb_ref, wa_ref, ba_ref, wx_ref, bx_ref, lam_ref, du_in,
             dx_ref, dcw_ref, dcb_ref, dwa_ref, dba_ref, dwx_ref, dbx_ref, dlam_ref,
             xp_ref, dxp_ref, dlt_ref, afirst_ref, dxc_next_ref, dsp_ref):
        step = pl.program_id(0)
        i = nt - 1 - step

        @pl.when(step == 0)
        def _():
            for ref in (dcw_ref, dcb_ref, dwa_ref, dba_ref, dwx_ref, dbx_ref, dlam_ref, dlt_ref, dxc_next_ref, dsp_ref):
                ref[...] = jnp.zeros_like(ref)
            afirst_ref[...] = jnp.ones_like(afirst_ref)

        xp_ref[0:LRU_HALO, :] = jnp.where(i == 0, 0.0, xh_ref[...])
        xp_ref[LRU_HALO:, :] = x_ref[...]
        cw = cw_ref[...]
        lam_v = lam_ref[...]
        sp = _softplus_neg(lam_v)
        xc, r, ig, a, mult = _lru_gates(xp_ref, cw, cb_ref[...], wa_ref, ba_ref[...], wx_ref, bx_ref[...], sp, taps, tm, heads)
        rows = lax.broadcasted_iota(jnp.int32, (tm, w), 0)
        a_next = jnp.where(rows == tm - 1, afirst_ref[0:1, :], _shift_rows(a, 1, 1.0, down=False))
        ac, bc = _scan_rows(a_next, dh_ref[...], reverse=True)
        delta = ac * dlt_ref[0:1, :] + bc
        hv = h_ref[...]
        h_last_prev = jnp.where(i == 0, 0.0, hh_ref[LRU_HALO - 1:LRU_HALO, :])
        h_prev = jnp.where(rows == 0, h_last_prev, _shift_rows(hv, 1, 0.0, down=True))
        gated = ig * xc
        da = delta * h_prev
        dmult = delta * gated
        dgated = delta * mult
        dlog_a = da * a - dmult * (a * a) / mult
        dpr = dlog_a * ((-LRU_C) * sp) * r * (1.0 - r)
        dpi = dgated * xc * ig * (1.0 - ig)
        dxc = dgated * ig
        dsp_ref[...] += jnp.sum(dlog_a * ((-LRU_C) * r), axis=0, keepdims=True)
        dba_ref[...] += jnp.sum(dpr, axis=0, keepdims=True)
        dbx_ref[...] += jnp.sum(dpi, axis=0, keepdims=True)
        xb = xc.astype(BF16)
        dprb = dpr.astype(BF16)
        dpib = dpi.astype(BF16)
        back = []
        for n in range(heads):
            sl = slice(n * HEAD_DIM, (n + 1) * HEAD_DIM)
            dwa_ref[n] += _tn(xb[:, sl], dprb[:, sl])
            dwx_ref[n] += _tn(xb[:, sl], dpib[:, sl])
            back.append(_nt(dprb[:, sl], wa_ref[n]) + _nt(dpib[:, sl], wx_ref[n]))
        dxc = dxc + jnp.concatenate(back, axis=1)
        dxp_ref[0:tm, :] = dxc
        dxp_ref[tm:, :] = dxc_next_ref[...]
        dx = jnp.zeros((tm, w), F32)
        dws = []
        for k in range(taps):
            dx = dx + cw[k:k + 1, :] * dxp_ref[pl.ds(taps - 1 - k, tm), :]
            dws.append(jnp.sum(dxc * xp_ref[pl.ds(LRU_HALO - (taps - 1) + k, tm), :], axis=0, keepdims=True))
        dx_ref[...] = dx.astype(dx_ref.dtype)
        dcw_ref[...] += jnp.concatenate(dws, axis=0)
        dcb_ref[...] += jnp.sum(dxc, axis=0, keepdims=True)
        dlt_ref[...] = jnp.broadcast_to(delta[0:1, :], dlt_ref.shape)
        afirst_ref[...] = jnp.broadcast_to(a[0:1, :], afirst_ref.shape)
        dxc_next_ref[...] = dxc[0:LRU_HALO, :]

        @pl.when(step == nt - 1)
        def _():
            dlam_ref[...] = dsp_ref[...] * (-_sigmoid(-lam_v))

    rev = lambda col: pl.BlockSpec((tm, w), lambda st: (nt - 1 - st, col))
    prev = lambda col: pl.BlockSpec((LRU_HALO, w), lambda st: (jnp.maximum((nt - 1 - st) * hb - 1, 0), col))
    full = lambda arr: pl.BlockSpec(arr.shape, lambda st: (0,) * arr.ndim)
    vec = pl.BlockSpec((1, w), lambda st: (0, 0))
    vecs = [conv_w, conv_b.reshape(1, w), wa.astype(BF16), ba.reshape(1, w), wx.astype(BF16), bx.reshape(1, w), lam.reshape(1, w)]
    vshape = jax.ShapeDtypeStruct((1, w), F32)
    return pl.pallas_call(
        body, name=name, grid=(nt,),
        in_specs=[rev(x_col), prev(x_col), rev(0), prev(0), rev(0)] + [full(v) for v in vecs] + [ANY],
        out_specs=[rev(du_col), full(conv_w), vec, full(wa), vec, full(wx), vec, vec],
        out_shape=[jax.ShapeDtypeStruct(du.shape, du.dtype), jax.ShapeDtypeStruct(conv_w.shape, F32), vshape,
                   jax.ShapeDtypeStruct(wa.shape, F32), vshape, jax.ShapeDtypeStruct(wx.shape, F32), vshape, vshape],
        input_output_aliases={5 + len(vecs): 0},
        scratch_shapes=[pltpu.VMEM((tm + LRU_HALO, w), F32), pltpu.VMEM((tm + LRU_HALO, w), F32),
                        pltpu.VMEM((8, w), F32), pltpu.VMEM((8, w), F32), pltpu.VMEM((LRU_HALO, w), F32), pltpu.VMEM((1, w), F32)],
        compiler_params=_cparams("arbitrary"))(ub, ub, h, h, dh, *vecs, du)


def _group_fwd(y, w, gate):
    r = lax.rsqrt(jnp.mean(y * y, axis=-1, keepdims=True) + RMS_EPS)
    return ((y * r) * w) * (gate * _sigmoid(gate))


def _mix_out_fwd(y_conv, y_attn, y_lru, ua, ub, n_conv, n_attn, n_lru, *, name):
    s, c = y_conv.shape
    wa_ = y_attn.shape[1]
    d = 2 * c + wa_
    tm = _tile(s, 256)
    assert wa_ == 2 * c

    def body(yc_ref, ya_ref, yl_ref, gc_ref, ga_ref, gl_ref, nc_ref, na_ref, nl_ref, o_ref):
        o_ref[:, 0:c] = _group_fwd(yc_ref[...], nc_ref[...], gc_ref[...]).astype(o_ref.dtype)
        o_ref[:, c:c + wa_] = _group_fwd(ya_ref[...], na_ref[...], ga_ref[...]).astype(o_ref.dtype)
        o_ref[:, c + wa_:] = _group_fwd(yl_ref[...], nl_ref[...], gl_ref[...]).astype(o_ref.dtype)

    blk = lambda width, col: pl.BlockSpec((tm, width), lambda i: (i, col))
    vec = lambda width: pl.BlockSpec((1, width), lambda i: (0, 0))
    return pl.pallas_call(
        body, name=name, grid=(s // tm,),
        in_specs=[blk(c, 0), blk(wa_, 0), blk(c, 0), blk(c, 2), blk(wa_, 0), blk(c, 3), vec(c), vec(wa_), vec(c)],
        out_specs=blk(d, 0), out_shape=jax.ShapeDtypeStruct((s, d), BF16),
        compiler_params=_cparams("parallel"))(y_conv, y_attn, y_lru, ua, ub, ub, n_conv.reshape(1, c), n_attn.reshape(1, wa_), n_lru.reshape(1, c))


def _group_bwd(dout, y, w, gate):
    r = lax.rsqrt(jnp.mean(y * y, axis=-1, keepdims=True) + RMS_EPS)
    silu, dsilu = _silu_and_grad(gate)
    dy, dwp = _rms_bwd_math(dout * silu, y, r, w)
    return dy, dout * ((y * r) * w) * dsilu, dwp


def _mix_out_bwd(dy, y_conv, y_attn, y_lru, ua, ub, n_conv, n_attn, n_lru, *, name):
    s, c = y_conv.shape
    wa_ = y_attn.shape[1]
    tm = _tile(s, 256)

    du_width = 5 * c + 4 * wa_
    gate_cols = (2 * c, 3 * c + 3 * wa_, 4 * c + 4 * wa_)

    def body(dy_ref, yc_ref, ya_ref, yl_ref, gc_ref, ga_ref, gl_ref, nc_ref, na_ref, nl_ref,
             dyc_ref, dya_ref, dyl_ref, du_ref, dnc_ref, dna_ref, dnl_ref):
        @pl.when(pl.program_id(0) == 0)
        def _():
            for ref in (dnc_ref, dna_ref, dnl_ref):
                ref[...] = jnp.zeros_like(ref)

        groups = ((dy_ref[:, 0:c], yc_ref, nc_ref, gc_ref, dyc_ref, dnc_ref),
                  (dy_ref[:, c:c + wa_], ya_ref, na_ref, ga_ref, dya_ref, dna_ref),
                  (dy_ref[:, c + wa_:], yl_ref, nl_ref, gl_ref, dyl_ref, dnl_ref))
        for col, (dout, y_ref, n_ref, g_ref, dyo_ref, dn_ref) in zip(gate_cols, groups):
            dyv, dgv, dwp = _group_bwd(dout.astype(F32), y_ref[...], n_ref[...], g_ref[...])
            dyo_ref[...] = dyv.astype(dyo_ref.dtype)
            du_ref[:, col:col + dgv.shape[1]] = dgv.astype(du_ref.dtype)
            dn_ref[...] += jnp.sum(dwp, axis=0, keepdims=True)

    blk = lambda width, col: pl.BlockSpec((tm, width), lambda i: (i, col))
    vec = lambda width: pl.BlockSpec((1, width), lambda i: (0, 0))
    sh = lambda width, dt: jax.ShapeDtypeStruct((s, width), dt)
    vs = lambda width: jax.ShapeDtypeStruct((1, width), F32)
    return pl.pallas_call(
        body, name=name, grid=(s // tm,),
        in_specs=[blk(2 * c + wa_, 0), blk(c, 0), blk(wa_, 0), blk(c, 0), blk(c, 2), blk(wa_, 0), blk(c, 3), vec(c), vec(wa_), vec(c)],
        out_specs=[blk(c, 0), blk(wa_, 0), blk(c, 0), blk(du_width, 0), vec(c), vec(wa_), vec(c)],
        out_shape=[sh(c, BF16), sh(wa_, BF16), sh(c, F32), sh(du_width, BF16), vs(c), vs(wa_), vs(c)],
        compiler_params=_cparams("arbitrary"))(dy, y_conv, y_attn, y_lru, ua, ub, ub, n_conv.reshape(1, c), n_attn.reshape(1, wa_), n_lru.reshape(1, c))


def _xattn_probs(qh, kh):
    sc = _nt(qh, kh) * (HEAD_DIM ** -0.5)
    e = jnp.exp(sc - jnp.max(sc, axis=-1, keepdims=True))
    return e / jnp.sum(e, axis=-1, keepdims=True)


def _xattn_fwd(q, kv, *, name):
    s, w = q.shape
    heads = w // HEAD_DIM
    tm = _tile(s, 512)

    def body(q_ref, kv_ref, o_ref):
        for n in range(heads):
            sl = slice(n * HEAD_DIM, (n + 1) * HEAD_DIM)
            p = _xattn_probs(q_ref[:, sl], kv_ref[:, sl])
            o_ref[:, sl] = _nn(p.astype(BF16), kv_ref[:, w + n * HEAD_DIM:w + (n + 1) * HEAD_DIM]).astype(o_ref.dtype)

    return pl.pallas_call(
        body, name=name, grid=(s // tm,),
        in_specs=[pl.BlockSpec((tm, w), lambda i: (i, 0)), pl.BlockSpec(kv.shape, lambda i: (0, 0))],
        out_specs=pl.BlockSpec((tm, w), lambda i: (i, 0)), out_shape=jax.ShapeDtypeStruct((s, w), BF16),
        compiler_params=_cparams("parallel"))(q, kv)


def _xattn_bwd(q, kv, do, *, name):
    s, w = q.shape
    heads = w // HEAD_DIM
    tm = _tile(s, 512)
    scale = HEAD_DIM ** -0.5

    def body(q_ref, kv_ref, do_ref, dq_ref, dkv_ref):
        @pl.when(pl.program_id(0) == 0)
        def _():
            dkv_ref[...] = jnp.zeros_like(dkv_ref)

        for n in range(heads):
            sl = slice(n * HEAD_DIM, (n + 1) * HEAD_DIM)
            vsl = slice(w + n * HEAD_DIM, w + (n + 1) * HEAD_DIM)
            qh, kh, vh, doh = q_ref[:, sl], kv_ref[:, sl], kv_ref[:, vsl], do_ref[:, sl]
            p = _xattn_probs(qh, kh)
            dp = _nt(doh, vh)
            ds = (p * (dp - jnp.sum(dp * p, axis=-1, keepdims=True)) * scale).astype(BF16)
            dq_ref[:, sl] = _nn(ds, kh).astype(dq_ref.dtype)
            dkv_ref[:, sl] += _tn(ds, qh)
            dkv_ref[:, vsl] += _tn(p.astype(BF16), doh)

    row = pl.BlockSpec((tm, w), lambda i: (i, 0))
    kvs = pl.BlockSpec(kv.shape, lambda i: (0, 0))
    return pl.pallas_call(
        body, name=name, grid=(s // tm,), in_specs=[row, kvs, row], out_specs=[row, kvs],
        out_shape=[jax.ShapeDtypeStruct((s, w), BF16), jax.ShapeDtypeStruct(kv.shape, F32)],
        compiler_params=_cparams("arbitrary"))(q, kv, do)


ROW_BLOCK_BYTES = 4 << 20
PACK_ROWS = 512


def _row_tile(rows, cols):
    limit = max(8, ROW_BLOCK_BYTES // (4 * cols))
    t = 8
    while t * 2 <= limit and rows % (t * 2) == 0:
        t *= 2
    assert rows % t == 0
    return t


def _cast_bf16(w, *, name):
    rows, cols = w.shape
    tr = _row_tile(rows, cols)

    def body(w_ref, o_ref):
        o_ref[...] = w_ref[...].astype(BF16)

    blk = pl.BlockSpec((tr, cols), lambda i: (i, 0))
    return pl.pallas_call(body, name=name, grid=(rows // tr,), in_specs=[blk], out_specs=blk,
                          out_shape=jax.ShapeDtypeStruct((rows, cols), BF16), compiler_params=_cparams("parallel"))(w)


def _adamw(w, m, v, gs, *, name):
    rows, cols = w.shape
    tr = _row_tile(rows, cols * 4)
    ng = len(gs)

    def body(*refs):
        w_ref, m_ref, v_ref = refs[:3]
        g_refs = refs[3:3 + ng]
        g_out, d_out, m_out, v_out = refs[3 + ng:]
        g = g_refs[0][...]
        for r in g_refs[1:]:
            g = g + r[...]
        mn = ADAM_B1 * m_ref[...] + (1.0 - ADAM_B1) * g
        vn = ADAM_B2 * v_ref[...] + (1.0 - ADAM_B2) * (g * g)
        m_hat = mn / (1.0 - ADAM_B1 ** ADAM_STEP)
        v_hat = vn / (1.0 - ADAM_B2 ** ADAM_STEP)
        g_out[...] = g
        d_out[...] = -ADAM_LR * (m_hat / (jnp.sqrt(v_hat) + ADAM_EPS) + ADAM_WD * w_ref[...])
        m_out[...] = mn
        v_out[...] = vn

    blk = pl.BlockSpec((tr, cols), lambda i: (i, 0))
    shp = jax.ShapeDtypeStruct((rows, cols), F32)
    return pl.pallas_call(body, name=name, grid=(rows // tr,), in_specs=[blk] * (3 + ng), out_specs=[blk] * 4,
                          out_shape=[shp] * 4, compiler_params=_cparams("parallel"))(w, m, v, *gs)


def _adamw_layers(w, m, v, gs, *, name):
    layers, rows, cols = w.shape
    tr = _row_tile(rows, cols * 4)
    nblk = rows // tr
    ng = len(gs[0])

    def body(*refs):
        w_ref, m_ref, v_ref = refs[:3]
        g_refs = refs[3:3 + layers * ng]
        g_out, d_out, m_out, v_out = refs[3 + layers * ng:]
        l = pl.program_id(0)
        g = jnp.zeros((tr, cols), F32)
        for ll in range(layers):
            gl = g_refs[ll * ng][...]
            for r in g_refs[ll * ng + 1:(ll + 1) * ng]:
                gl = gl + r[...]
            g = jnp.where(l == ll, gl, g)
        mn = ADAM_B1 * m_ref[...] + (1.0 - ADAM_B1) * g
        vn = ADAM_B2 * v_ref[...] + (1.0 - ADAM_B2) * (g * g)
        m_hat = mn / (1.0 - ADAM_B1 ** ADAM_STEP)
        v_hat = vn / (1.0 - ADAM_B2 ** ADAM_STEP)
        g_out[...] = g
        d_out[...] = -ADAM_LR * (m_hat / (jnp.sqrt(v_hat) + ADAM_EPS) + ADAM_WD * w_ref[...])
        m_out[...] = mn
        v_out[...] = vn

    blk = pl.BlockSpec((None, tr, cols), lambda l, i: (l, i, 0))

    def g_spec(ll):
        return pl.BlockSpec((tr, cols), lambda l, i: (jnp.where(l == ll, i, jnp.where(l < ll, 0, nblk - 1)), 0))

    shp = jax.ShapeDtypeStruct(w.shape, F32)
    return pl.pallas_call(
        body, name=name, grid=(layers, nblk), in_specs=[blk] * 3 + [g_spec(ll) for ll in range(layers) for _ in range(ng)],
        out_specs=[blk] * 4, out_shape=[shp] * 4, compiler_params=_cparams("arbitrary", "arbitrary"),
    )(w, m, v, *[g for gl in gs for g in gl])


OTHER_CHIPS = ((1, 0), (0, 1), (1, 1))
N_CHIPS = 4
ANY = pl.BlockSpec(memory_space=pl.ANY)


def _place():
    return lax.axis_index("x"), lax.axis_index("y"), lax.axis_index("c")


def _flip(v, f):
    return 1 - v if f else v


def _part(ref, lead, axis, chip, size):
    idx = list(lead) + [slice(None)] * (len(ref.shape) - len(lead))
    idx[len(lead) + axis] = pl.ds(pl.multiple_of(chip * size, size), size)
    return ref.at[tuple(idx)]


def _allgather_chips(shards, axes, *, name):
    n = len(shards)
    sizes = [sh.shape[ax] for sh, ax in zip(shards, axes)]

    def full_shape(sh, ax):
        return tuple(d * N_CHIPS if i == ax else d for i, d in enumerate(sh.shape))

    def body(*refs):
        ins, outs = refs[:n], refs[n:2 * n]
        send_sems, recv_sems, loc_sems = refs[2 * n:]
        x, y, c = _place()
        me = 2 * x + y
        local = []
        for a in range(n):
            cp = pltpu.make_async_copy(ins[a], _part(outs[a], (), axes[a], me, sizes[a]), loc_sems.at[a])
            cp.start()
            local.append(cp)

        def remote(a, j, chip):
            fx, fy = OTHER_CHIPS[j]
            return pltpu.make_async_remote_copy(
                src_ref=ins[a], dst_ref=_part(outs[a], (), axes[a], chip, sizes[a]),
                send_sem=send_sems.at[a, j], recv_sem=recv_sems.at[a, j],
                device_id=(_flip(x, fx), _flip(y, fy), c), device_id_type=MESH)

        for a in range(n):
            for j in range(len(OTHER_CHIPS)):
                remote(a, j, me).start()
        for a in range(n):
            for j, (fx, fy) in enumerate(OTHER_CHIPS):
                remote(a, j, 2 * _flip(x, fx) + _flip(y, fy)).wait()
        for cp in local:
            cp.wait()

    return pl.pallas_call(
        body, name=name, in_specs=[ANY] * n, out_specs=[ANY] * n,
        out_shape=[jax.ShapeDtypeStruct(full_shape(sh, ax), sh.dtype) for sh, ax in zip(shards, axes)],
        scratch_shapes=[pltpu.SemaphoreType.DMA((n, 3)), pltpu.SemaphoreType.DMA((n, 3)), pltpu.SemaphoreType.DMA((n,))],
    )(*shards)


HBM = pl.BlockSpec(memory_space=pltpu.HBM)
SEM = pl.BlockSpec(memory_space=pltpu.SEMAPHORE)
SPLIT_COPY = pltpu.CompilerParams(has_side_effects=pltpu.SideEffectType.DATAFLOW_SIDE_EFFECTING)


def _cast_place(w, layer, axis, chip, *, name):
    _, rows, cols = w.shape
    tr = _row_tile(rows, cols)
    nblk = rows // tr

    def body(chip_ref, w_ref, o_ref):
        o_ref[...] = w_ref[...].astype(BF16)

    if axis == 1:
        shape = (rows, cols * N_CHIPS)
        o_spec = pl.BlockSpec((tr, cols), lambda i, chip_ref: (i, chip_ref[0]))
    else:
        shape = (rows * N_CHIPS, cols)
        o_spec = pl.BlockSpec((tr, cols), lambda i, chip_ref: (chip_ref[0] * nblk + i, 0))
    return pl.pallas_call(
        body, name=name,
        grid_spec=pltpu.PrefetchScalarGridSpec(
            num_scalar_prefetch=1, grid=(nblk,),
            in_specs=[pl.BlockSpec((None, tr, cols), lambda i, chip_ref: (layer, i, 0))], out_specs=o_spec),
        out_shape=jax.ShapeDtypeStruct(shape, BF16), compiler_params=_cparams("parallel"))(chip, w)


def _half_rows(ref, axis, chip, size, half):
    rows = ref.shape[0] // (N_CHIPS if axis == 0 else 1)
    r0 = chip * rows if axis == 0 else 0
    if half is not None:
        rows //= 2
        r0 = r0 + half * rows
    rsl = pl.ds(r0 if isinstance(r0, int) else pl.multiple_of(r0, 16), rows)
    return ref.at[rsl, :] if axis == 0 else ref.at[rsl, pl.ds(pl.multiple_of(chip * size, LANES), size)]


def _gather_copy(refs, a, j, send_sems, recv_sems, *, axes, sizes, arriving, halves=False):
    x, y, c = _place()
    fx, fy = OTHER_CHIPS[j]
    px, py = _flip(x, fx), _flip(y, fy)
    part = _half_rows(refs[a], axes[a], (2 * px + py) if arriving else (2 * x + y), sizes[a], c if halves else None)
    k = a * len(OTHER_CHIPS) + j
    return pltpu.make_async_remote_copy(src_ref=part, dst_ref=part, send_sem=send_sems.at[k], recv_sem=recv_sems.at[k],
                                        device_id=(px, py, c), device_id_type=MESH)


def _swap_halves(arrs, axes, *, name):
    n = len(arrs)
    sizes = [a.shape[ax] // N_CHIPS for a, ax in zip(arrs, axes)]

    def body(*refs):
        ins = refs[:n]
        send_sems, recv_sems = refs[2 * n:]
        x, y, c = _place()

        def copy(a, j, half):
            fx, fy = OTHER_CHIPS[j]
            part = _half_rows(ins[a], axes[a], 2 * _flip(x, fx) + _flip(y, fy), sizes[a], half)
            k = a * len(OTHER_CHIPS) + j
            return pltpu.make_async_remote_copy(src_ref=part, dst_ref=part, send_sem=send_sems.at[k], recv_sem=recv_sems.at[k],
                                                device_id=(x, y, 1 - c), device_id_type=MESH)

        todo = [(a, j) for a in range(n) for j in range(len(OTHER_CHIPS))]
        for a, j in todo:
            copy(a, j, c).start()
        for a, j in todo:
            copy(a, j, c).wait_send()
            copy(a, j, 1 - c).wait_recv()

    res = pl.pallas_call(
        body, name=name, in_specs=[ANY] * n, out_specs=[ANY] * n, out_shape=[jax.ShapeDtypeStruct(a.shape, a.dtype) for a in arrs],
        input_output_aliases={a: a for a in range(n)},
        scratch_shapes=[pltpu.SemaphoreType.DMA((n * len(OTHER_CHIPS),)), pltpu.SemaphoreType.DMA((n * len(OTHER_CHIPS),))],
    )(*arrs)
    return list(res)


def _scatter_copy(srcs, lands, a, j, axes, sizes, send_sems, recv_sems):
    x, y, c = _place()
    fx, fy = OTHER_CHIPS[j]
    px, py = _flip(x, fx), _flip(y, fy)
    k = a * len(OTHER_CHIPS) + j
    return pltpu.make_async_remote_copy(src_ref=_part(srcs[a], (), axes[a], 2 * px + py, sizes[a]), dst_ref=lands[a].at[j],
                                        send_sem=send_sems.at[k], recv_sem=recv_sems.at[k],
                                        device_id=(px, py, c), device_id_type=MESH)


def _split_start(arrs, make_copy, ncopies, dep, *, name):
    n = len(arrs)

    def body(*refs):
        ins = refs[:n]
        send_sems, recv_sems = refs[n + 1], refs[n + 2]
        token = refs[n + 3 + n]
        for a in range(ncopies):
            for j in range(len(OTHER_CHIPS)):
                make_copy(ins, a, j, send_sems, recv_sems).start()
        token[...] = jnp.zeros_like(token)

    sem = pltpu.SemaphoreType.DMA((ncopies * len(OTHER_CHIPS),))
    res = pl.pallas_call(
        body, name=name,
        out_shape=(sem, sem, *[pltpu.HBM(a.shape, a.dtype) for a in arrs], jax.ShapeDtypeStruct((8, LANES), F32)),
        in_specs=[HBM] * n + [pl.BlockSpec(memory_space=pl.ANY)],
        out_specs=(SEM, SEM, *[HBM] * n, pl.BlockSpec(memory_space=pltpu.VMEM)),
        input_output_aliases={a: 2 + a for a in range(n)}, compiler_params=SPLIT_COPY,
    )(*[pltpu.with_memory_space_constraint(a, pltpu.HBM) for a in arrs], dep)
    return res[0], res[1], list(res[2:2 + n]), res[2 + n]


def _split_wait(arrs, send_sems, recv_sems, make_copy, ncopies, after, *, name):
    n = len(arrs)

    def body(*refs):
        ins = refs[:n]
        send, recv = refs[n], refs[n + 1]
        for a in range(ncopies):
            for j in range(len(OTHER_CHIPS)):
                cp = make_copy(ins, a, j, send, recv)
                cp.wait_send()
                cp.wait_recv()

    res = pl.pallas_call(
        body, name=name, out_shape=tuple(pltpu.HBM(a.shape, a.dtype) for a in arrs),
        in_specs=[HBM] * n + [SEM, SEM, pl.BlockSpec(memory_space=pl.ANY)], out_specs=tuple([HBM] * n),
        input_output_aliases={a: a for a in range(n)}, compiler_params=SPLIT_COPY,
    )(*arrs, send_sems, recv_sems, after)
    return list(res)


def _sum_own_and_slots(g, land, axis, chip, *, name):
    slots, rows, cols = land.shape
    tr = _row_tile(rows, cols * 4)
    nblk = rows // tr

    def body(chip_ref, g_ref, l_ref, o_ref):
        acc = g_ref[...].astype(F32)
        for j in range(slots):
            acc = acc + l_ref[j].astype(F32)
        o_ref[...] = acc

    if axis == 1:
        g_spec = pl.BlockSpec((tr, cols), lambda i, chip_ref: (i, chip_ref[0]))
    else:
        g_spec = pl.BlockSpec((tr, cols), lambda i, chip_ref: (chip_ref[0] * nblk + i, 0))
    return pl.pallas_call(
        body, name=name,
        grid_spec=pltpu.PrefetchScalarGridSpec(
            num_scalar_prefetch=1, grid=(nblk,),
            in_specs=[g_spec, pl.BlockSpec((slots, tr, cols), lambda i, chip_ref: (0, i, 0))],
            out_specs=pl.BlockSpec((tr, cols), lambda i, chip_ref: (i, 0))),
        out_shape=jax.ShapeDtypeStruct((rows, cols), F32), compiler_params=_cparams("parallel"))(chip, g, land)


def _swap_sibling(arrs, *, name):
    n = len(arrs)

    def body(*refs):
        ins, outs = refs[:n], refs[n:2 * n]
        send_sems, recv_sems = refs[2 * n:]
        x, y, c = _place()
        copies = [pltpu.make_async_remote_copy(src_ref=ins[a], dst_ref=outs[a], send_sem=send_sems.at[a], recv_sem=recv_sems.at[a],
                                               device_id=(x, y, 1 - c), device_id_type=MESH) for a in range(n)]
        for cp in copies:
            cp.start()
        for cp in copies:
            cp.wait()

    return pl.pallas_call(
        body, name=name, in_specs=[ANY] * n, out_specs=[ANY] * n,
        out_shape=[jax.ShapeDtypeStruct(a.shape, a.dtype) for a in arrs],
        scratch_shapes=[pltpu.SemaphoreType.DMA((n,)), pltpu.SemaphoreType.DMA((n,))],
    )(*arrs)


def _allreduce_small(p, *, name):
    rows, cols = p.shape
    nrel = len(OTHER_CHIPS)

    def body(p_ref, o_ref, sib_ref, land_ref, send_sems, recv_sems):
        x, y, c = _place()
        me = 2 * x + y
        pair = pltpu.make_async_remote_copy(src_ref=p_ref, dst_ref=sib_ref, send_sem=send_sems.at[nrel], recv_sem=recv_sems.at[nrel],
                                            device_id=(x, y, 1 - c), device_id_type=MESH)
        pair.start()
        pair.wait()
        land_ref[nrel] = p_ref[...] + sib_ref[...]
        copies = []
        for j, (fx, fy) in enumerate(OTHER_CHIPS):
            copies.append(pltpu.make_async_remote_copy(src_ref=land_ref.at[nrel], dst_ref=land_ref.at[j], send_sem=send_sems.at[j],
                                                       recv_sem=recv_sems.at[j], device_id=(_flip(x, fx), _flip(y, fy), c),
                                                       device_id_type=MESH))
        for cp in copies:
            cp.start()
        for cp in copies:
            cp.wait()

        def slot_of(chip):
            r = jnp.bitwise_xor(chip, me)
            return jnp.where(r == 0, nrel, jnp.where(r == 2, 0, jnp.where(r == 1, 1, 2)))

        acc = land_ref[slot_of(0)]
        for chip in range(1, N_CHIPS):
            acc = acc + land_ref[slot_of(chip)]
        o_ref[...] = acc

    vm = pl.BlockSpec(memory_space=pltpu.VMEM)
    return pl.pallas_call(
        body, name=name, in_specs=[vm], out_specs=vm, out_shape=jax.ShapeDtypeStruct((rows, cols), F32),
        scratch_shapes=[pltpu.VMEM((rows, cols), F32), pltpu.VMEM((nrel + 1, rows, cols), F32),
                        pltpu.SemaphoreType.DMA((nrel + 1,)), pltpu.SemaphoreType.DMA((nrel + 1,))],
        compiler_params=pltpu.CompilerParams(vmem_limit_bytes=V7X_VMEM_LIMIT_BYTES))(p)


WEIGHTS = ("mix_norm_g", "w_in", "conv_dw_w", "conv_dw_b", "conv_ln_g", "conv_ln_b", "conv_pw_w", "lru_conv_w", "lru_conv_b",
           "lru_wa", "lru_ba", "lru_wx", "lru_bx", "lru_lambda", "out_norm_conv", "out_norm_attn", "out_norm_lru", "w_out",
           "xattn_norm_g", "mem_norm_g", "xattn_wq", "xattn_wkv", "xattn_wo", "final_norm_g")
BIG = {"w_in": 2, "conv_pw_w": 1, "w_out": 1, "xattn_wq": 1, "xattn_wkv": 1, "xattn_wo": 2}
SMALL_SHARDED = {"conv_dw_w": 2, "lru_conv_w": 2}


IN_GROUP = ("w_in", "conv_pw_w")
REST_GROUP = ("w_out", "xattn_wq", "xattn_wkv", "xattn_wo")


def _trunk(x, mem, target, p, fetch, grads_ready):
    depth = p["mix_norm_g"].shape[0]
    c = p["conv_dw_w"].shape[2]
    aw = p["out_norm_attn"].shape[1]
    heads = aw // HEAD_DIM
    saved = []
    for l in range(depth):
        t = f"l{l}_"
        h1, r1 = _rmsnorm_fwd(x, p["mix_norm_g"][l], name=t + "mix_norm")
        wl = dict(fetch(IN_GROUP, l, r1))
        ua = _matmul(h1, wl["w_in"], mode="nn", n=3 * c, b_off=0, name=t + "in_conv")
        qkv = _matmul(h1, wl["w_in"], mode="nn", n=3 * aw, b_off=3 * c, out_dtype=BF16, name=t + "in_qkv")
        ub = _matmul(h1, wl["w_in"], mode="nn", n=aw + 2 * c, b_off=3 * c + 3 * aw, name=t + "in_gates")
        y_conv = _conv_fwd(ua, p["conv_dw_w"][l], p["conv_dw_b"][l], p["conv_ln_g"][l], p["conv_ln_b"][l], wl["conv_pw_w"],
                           name=t + "conv_fwd")
        y_attn, sbw = _sb_fwd(qkv, heads, name=t + "sb_fwd")
        y_lru = _lru_fwd(ub, aw // c, p["lru_conv_w"][l], p["lru_conv_b"][l], p["lru_wa"][l], p["lru_ba"][l], p["lru_wx"][l],
                         p["lru_bx"][l], p["lru_lambda"][l], name=t + "lru_fwd")
        y = _mix_out_fwd(y_conv, y_attn, y_lru, ua, ub, p["out_norm_conv"][l], p["out_norm_attn"][l], p["out_norm_lru"][l],
                         name=t + "mix_out_fwd")
        wl.update(fetch(REST_GROUP, l, y))
        x2 = _matmul(y, wl["w_out"], mode="nn", add=x, name=t + "out_proj")
        h2, r2 = _rmsnorm_fwd(x2, p["xattn_norm_g"][l], name=t + "xattn_norm")
        qx = _matmul(h2, wl["xattn_wq"], mode="nn", out_dtype=BF16, name=t + "xattn_q")
        memn, rm = _rmsnorm_fwd(mem, p["mem_norm_g"][l], name=t + "mem_norm")
        kv = _matmul(memn, wl["xattn_wkv"], mode="nn", out_dtype=BF16, name=t + "xattn_kv")
        o = _xattn_fwd(qx, kv, name=t + "xattn_fwd")
        x3 = _matmul(o, wl["xattn_wo"], mode="nn", add=x2, name=t + "xattn_o")
        saved.append(dict(x=x, h1=h1, r1=r1, ua=ua, qkv=qkv, ub=ub, y_conv=y_conv, y_attn=y_attn, sbw=sbw, y_lru=y_lru, y=y,
                          x2=x2, h2=h2, r2=r2, qx=qx, memn=memn, rm=rm, kv=kv, o=o, w=wl))
        x = x3

    loss, dx, dg_final = _final_loss(x, p["final_norm_g"], target, name="final_loss")
    small = {k: [None] * depth for k in WEIGHTS if k not in BIG and k != "final_norm_g"}
    token = None
    for l in reversed(range(depth)):
        t = f"l{l}_"
        s = saved[l]
        wl = s["w"]
        do = _matmul(dx, wl["xattn_wo"], mode="nt", out_dtype=BF16, dep=token, name=t + "d_xattn_o")
        dwo = _matmul(s["o"], dx, mode="tn", out_dtype=BF16, name=t + "dw_xattn_o")
        dqx, dkv = _xattn_bwd(s["qx"], s["kv"], do, name=t + "xattn_bwd")
        dwq = _matmul(s["h2"], dqx, mode="tn", out_dtype=BF16, name=t + "dw_xattn_q")
        dh2 = _matmul(dqx, wl["xattn_wq"], mode="nt", out_dtype=BF16, name=t + "d_xattn_q")
        dx2, dg = _rmsnorm_bwd(dh2, s["x2"], s["r2"], p["xattn_norm_g"][l], dx, name=t + "xattn_norm_bwd")
        small["xattn_norm_g"][l] = dg[0]
        dmemn = _matmul(dkv, wl["xattn_wkv"], mode="nt", name=t + "d_xattn_kv")
        dwkv = _matmul(s["memn"], dkv, mode="tn", out_dtype=BF16, name=t + "dw_xattn_kv")
        _, dg = _rmsnorm_bwd(dmemn, mem, s["rm"], p["mem_norm_g"][l], None, name=t + "mem_norm_bwd")
        small["mem_norm_g"][l] = dg[0]
        dwout = _matmul(s["y"], dx2, mode="tn", out_dtype=BF16, name=t + "dw_out_proj")
        token = grads_ready(REST_GROUP, l, dict(w_out=dwout, xattn_wq=dwq, xattn_wkv=dwkv, xattn_wo=dwo))
        dy = _matmul(dx2, wl["w_out"], mode="nt", dep=token, out_dtype=BF16, name=t + "d_out_proj")
        dyc, dya, dyl, du, dnc, dna, dnl = _mix_out_bwd(
            dy, s["y_conv"], s["y_attn"], s["y_lru"], s["ua"], s["ub"], p["out_norm_conv"][l], p["out_norm_attn"][l],
            p["out_norm_lru"][l], name=t + "mix_out_bwd")
        small["out_norm_conv"][l], small["out_norm_attn"][l], small["out_norm_lru"][l] = dnc[0], dna[0], dnl[0]
        dd, dpw, dlg, dlb = _conv_bwd_a(s["ua"], dyc, p["conv_dw_w"][l], p["conv_dw_b"][l], p["conv_ln_g"][l], p["conv_ln_b"][l],
                                        wl["conv_pw_w"], name=t + "conv_bwd_a")
        du, ddw, ddb = _conv_bwd_b(s["ua"], dd, p["conv_dw_w"][l], du, name=t + "conv_bwd_b")
        small["conv_ln_g"][l], small["conv_ln_b"][l], small["conv_dw_w"][l], small["conv_dw_b"][l] = dlg[0], dlb[0], ddw, ddb[0]
        du = _sb_bwd(s["qkv"], dya, s["sbw"], du, 3 * c, heads, name=t + "sb_bwd")
        du, dcw, dcb, dwa, dba, dwx, dbx, dlam = _lru_bwd(
            s["ub"], aw // c, s["y_lru"], dyl, p["lru_conv_w"][l], p["lru_conv_b"][l], p["lru_wa"][l], p["lru_ba"][l],
            p["lru_wx"][l], p["lru_bx"][l], p["lru_lambda"][l], du, (3 * c + 4 * aw) // c, name=t + "lru_bwd")
        small["lru_conv_w"][l], small["lru_conv_b"][l], small["lru_wa"][l], small["lru_ba"][l] = dcw, dcb[0], dwa, dba[0]
        small["lru_wx"][l], small["lru_bx"][l], small["lru_lambda"][l] = dwx, dbx[0], dlam[0]
        dwin = _matmul(s["h1"], du, mode="tn", out_dtype=BF16, tk=4096, name=t + "dw_in")
        token = grads_ready(IN_GROUP, l, dict(w_in=dwin, conv_pw_w=_cast_bf16(dpw, name=t + "cast_dpw")))
        dh1 = _matmul(du, wl["w_in"], mode="nt", dep=token, tk=3328, out_dtype=BF16, name=t + "d_in")
        dx, dg = _rmsnorm_bwd(dh1, s["x"], s["r1"], p["mix_norm_g"][l], dx2, name=t + "mix_norm_bwd")
        small["mix_norm_g"][l] = dg[0]
    small = {k: jnp.stack(v) for k, v in small.items()}
    small["final_norm_g"] = dg_final[0]
    return loss, dx, small


def _pack(arrs):
    flat = jnp.concatenate([a.reshape(-1) for a in arrs])
    pad = (-flat.shape[0]) % (PACK_ROWS * LANES)
    return jnp.pad(flat, (0, pad)).reshape(-1, LANES)


def _unpack(packed, like):
    flat = packed.reshape(-1)
    out, at = [], 0
    for a in like:
        out.append(flat[at:at + a.size].reshape(a.shape))
        at += a.size
    return out


def _as_rows(a):
    return a.reshape(-1, a.shape[-1])


def kernel(x, mem, mix_norm_g, w_in, conv_dw_w, conv_dw_b, conv_ln_g, conv_ln_b, conv_pw_w, lru_conv_w, lru_conv_b, lru_wa, lru_ba, lru_wx, lru_bx, lru_lambda, out_norm_conv, out_norm_attn, out_norm_lru, w_out, xattn_norm_g, mem_norm_g, xattn_wq, xattn_wkv, xattn_wo, final_norm_g, loss_target, m_mix_norm_g, m_w_in, m_conv_dw_w, m_conv_dw_b, m_conv_ln_g, m_conv_ln_b, m_conv_pw_w, m_lru_conv_w, m_lru_conv_b, m_lru_wa, m_lru_ba, m_lru_wx, m_lru_bx, m_lru_lambda, m_out_norm_conv, m_out_norm_attn, m_out_norm_lru, m_w_out, m_xattn_norm_g, m_mem_norm_g, m_xattn_wq, m_xattn_wkv, m_xattn_wo, m_final_norm_g, v_mix_norm_g, v_w_in, v_conv_dw_w, v_conv_dw_b, v_conv_ln_g, v_conv_ln_b, v_conv_pw_w, v_lru_conv_w, v_lru_conv_b, v_lru_wa, v_lru_ba, v_lru_wx, v_lru_bx, v_lru_lambda, v_out_norm_conv, v_out_norm_attn, v_out_norm_lru, v_w_out, v_xattn_norm_g, v_mem_norm_g, v_xattn_wq, v_xattn_wkv, v_xattn_wo, v_final_norm_g):
    given = dict(locals())
    w = {k: given[k] for k in WEIGHTS}
    m = {k: given["m_" + k] for k in WEIGHTS}
    v = {k: given["v_" + k] for k in WEIGHTS}
    depth = mix_norm_g.shape[0]
    chip = 2 * lax.axis_index("x") + lax.axis_index("y")

    chip_arr = chip.astype(jnp.int32).reshape(1)

    p = dict(w)
    p.update(zip(SMALL_SHARDED, _allgather_chips([w[k] for k in SMALL_SHARDED], list(SMALL_SHARDED.values()), name="gather_small")))
    axis2d = {k: BIG[k] - 1 for k in BIG}
    groups = [(IN_GROUP, 0), (REST_GROUP, 0)] + [(IN_GROUP + REST_GROUP, l) for l in range(1, depth)]
    pending, token = {}, p[next(iter(SMALL_SHARDED))]
    for names, l in groups:
        arrs = [_cast_place(w[k], l, axis2d[k], chip_arr, name=f"place{l}_{k}") for k in names]
        axes = [axis2d[k] for k in names]
        sizes = [a.shape[ax] // N_CHIPS for a, ax in zip(arrs, axes)]
        halves = (names, l) == groups[0]
        start = functools.partial(_gather_copy, axes=axes, sizes=sizes, arriving=False, halves=halves)
        land = functools.partial(_gather_copy, axes=axes, sizes=sizes, arriving=True, halves=halves)
        send, recv, arrs, token = _split_start(arrs, start, len(arrs), token, name=f"gather_start{l}_{names[0]}")
        pending[(names[0], l)] = (names, arrs, send, recv, land, axes if halves else None)
    last_token = token
    have = {}

    def fetch(group, l, after):
        key = (group[0], l)
        if key in pending:
            names, arrs, send, recv, land, swap_axes = pending.pop(key)
            after = last_token if (group, l) == groups[0] else after
            arrs = _split_wait(arrs, send, recv, land, len(arrs), after, name=f"gather_wait{l}_{names[0]}")
            if swap_axes is not None:
                arrs = _swap_halves(arrs, swap_axes, name=f"gather_swap{l}_{names[0]}")
            have.update({(k, l): a for k, a in zip(names, arrs)})
        return {k: have[(k, l)] for k in group}

    flying = []
    held = {}

    def grads_ready(group, l, grads):
        held.update({(k, l): g for k, g in grads.items()})
        if l > 0 and group == REST_GROUP:
            return None
        names = [k for k in (IN_GROUP + REST_GROUP if l > 0 else group)]
        srcs = [held[(k, l)] for k in names]
        axes = [axis2d[k] for k in names]
        sizes = [g.shape[ax] // N_CHIPS for g, ax in zip(srcs, axes)]
        lands = [lax.empty((len(OTHER_CHIPS),) + tuple(sz if i == ax else d for i, d in enumerate(g.shape)), g.dtype)
                 for g, ax, sz in zip(srcs, axes, sizes)]
        n = len(names)
        copy = lambda refs, a, j, ss, rs_: _scatter_copy(refs[:n], refs[n:], a, j, axes, sizes, ss, rs_)
        send, recv, arrs, token = _split_start(srcs + lands, copy, n, jnp.zeros((8, LANES), F32), name=f"scatter_start{l}_{names[0]}")
        flying.append((names, l, axes, arrs, send, recv, copy))
        return token

    loss, grad_x, small = _trunk(x[0], mem[0], loss_target[0], p, fetch, grads_ready)
    loss = lax.psum(loss[0, 0], ("x", "y", "c"))

    sums = {}
    out = {}

    def arrive(entry, after):
        names, l, axes, arrs, send, recv, copy = entry
        n = len(names)
        arrs = _split_wait(arrs, send, recv, copy, n, after, name=f"scatter_wait{l}_{names[0]}")
        for k, ax, g, ld in zip(names, axes, arrs[:n], arrs[n:]):
            ld = ld.reshape((len(OTHER_CHIPS), -1, ld.shape[-1]))
            sums[(k, l)] = _sum_own_and_slots(g, ld, ax, chip_arr, name=f"sum{l}_{k}")
        return sums[(names[-1], l)]

    def update(names):
        mine = [sums[(k, l)] for k in names for l in range(depth)]
        theirs = _swap_sibling(mine, name="swap_sums_" + names[0])
        for i, k in enumerate(names):
            gs = [[mine[i * depth + l], theirs[i * depth + l]] for l in range(depth)]
            out[k] = _adamw_layers(w[k], m[k], v[k], gs, name="adamw_" + k)

    after = grad_x
    for entry in flying[:-1]:
        after = arrive(entry, after)
    update(REST_GROUP)

    small_names = [k for k in WEIGHTS if k not in BIG]
    total = _unpack(_allreduce_small(_pack([small[k] for k in small_names]), name="allreduce_small"), [small[k] for k in small_names])
    g_small = dict(zip(small_names, total))
    for k, ax in SMALL_SHARDED.items():
        size = w[k].shape[ax]
        g_small[k] = lax.dynamic_slice_in_dim(g_small[k], chip * size, size, axis=ax)
    res = _adamw(_pack([w[k] for k in small_names]), _pack([m[k] for k in small_names]), _pack([v[k] for k in small_names]),
                 [_pack([g_small[k] for k in small_names])], name="adamw_small")
    last = res[0]
    res = [_unpack(r, [w[k] for k in small_names]) for r in res]
    for i, k in enumerate(small_names):
        out[k] = [r[i] for r in res]

    arrive(flying[-1], last)
    update(IN_GROUP)

    outs = [loss, grad_x[None]]
    for part in range(4):
        outs += [out[k][part] for k in WEIGHTS]
    return tuple(outs)
```

```python
import functools

import jax
import jax.numpy as jnp
from jax import lax
from jax.experimental import pallas as pl
from jax.experimental.pallas import tpu as pltpu

F32 = jnp.float32
BF16 = jnp.bfloat16
MESH = pl.DeviceIdType.MESH

V7X_VMEM_LIMIT_BYTES = 56 * 1024 * 1024
LANES = 128
HEAD_DIM = 128
LRU_C = 8.0
RMS_EPS = 1e-6
LN_EPS = 1e-5
CONV_HALO = 32
LRU_HALO = 8
ADAM_LR = 0.001
ADAM_B1 = 0.9
ADAM_B2 = 0.999
ADAM_EPS = 1e-08
ADAM_WD = 0.01
ADAM_STEP = 10


def _cparams(*sem):
    return pltpu.CompilerParams(dimension_semantics=sem, vmem_limit_bytes=V7X_VMEM_LIMIT_BYTES)


def _tile(n, pref):
    if n <= pref:
        return n
    for t in range(pref - pref % LANES, 0, -LANES):
        if n % t == 0:
            return t
    t = pref
    while n % t:
        t //= 2
    return t


def _dot(a, b, dims):
    return lax.dot_general(a, b, (dims, ((), ())), preferred_element_type=F32)


def _nn(a, b):
    return _dot(a, b, ((1,), (0,)))


def _nt(a, b):
    return _dot(a, b, ((1,), (1,)))


def _tn(a, b):
    return _dot(a, b, ((0,), (0,)))


def _sigmoid(x):
    return jax.nn.sigmoid(x)


def _silu_and_grad(x):
    s = _sigmoid(x)
    return x * s, s * (1.0 + x * (1.0 - s))


def _matmul(a, b, *, mode, name, layer=None, n=None, b_off=0, add=None, dep=None, out_dtype=F32, tm=1024, tn=1024, tk=2048):
    bshape = b.shape if layer is None else b.shape[1:]
    if mode == "nn":
        m, k = a.shape
        n = bshape[1] if n is None else n
    elif mode == "nt":
        m, k = a.shape
        n = bshape[0]
    else:
        k, m = a.shape
        n = bshape[1]
    tm, tk = _tile(m, tm), _tile(k, tk)
    tn = _tile(n, tn)
    while b_off % tn or n % tn:
        tn -= LANES
    nk = k // tk
    off = b_off // tn
    lead = () if layer is None else (None,)
    li = () if layer is None else (layer,)
    if mode == "nn":
        a_spec = pl.BlockSpec((tm, tk), lambda i, j, kk: (i, kk))
        b_spec = pl.BlockSpec(lead + (tk, tn), lambda i, j, kk: li + (kk, j + off))
        dot = _nn
    elif mode == "nt":
        a_spec = pl.BlockSpec((tm, tk), lambda i, j, kk: (i, kk))
        b_spec = pl.BlockSpec(lead + (tn, tk), lambda i, j, kk: li + (j, kk))
        dot = _nt
    else:
        a_spec = pl.BlockSpec((tk, tm), lambda i, j, kk: (kk, i))
        b_spec = pl.BlockSpec(lead + (tk, tn), lambda i, j, kk: li + (kk, j))
        dot = _tn
    o_spec = pl.BlockSpec((tm, tn), lambda i, j, kk: (i, j))
    has_add = add is not None

    def body(*refs):
        refs = refs[:-3] + refs[-2:] if dep is not None else refs
        if has_add:
            a_ref, b_ref, add_ref, o_ref, acc_ref = refs
        else:
            a_ref, b_ref, o_ref, acc_ref = refs
        kk = pl.program_id(2)
        part = dot(a_ref[...].astype(BF16), b_ref[...].astype(BF16))

        @pl.when(kk == 0)
        def _():
            acc_ref[...] = part

        @pl.when(kk > 0)
        def _():
            acc_ref[...] += part

        @pl.when(kk == nk - 1)
        def _():
            r = acc_ref[...]
            if has_add:
                r = r + add_ref[...]
            o_ref[...] = r.astype(o_ref.dtype)

    ins = [a, b] + ([add] if has_add else [])
    specs = [a_spec, b_spec] + ([o_spec] if has_add else [])
    if dep is not None:
        ins.append(dep)
        specs.append(pl.BlockSpec((8, LANES), lambda i, j, kk: (0, 0)))
    return pl.pallas_call(
        body, name=name, grid=(m // tm, n // tn, nk), in_specs=specs, out_specs=o_spec,
        out_shape=jax.ShapeDtypeStruct((m, n), out_dtype), scratch_shapes=[pltpu.VMEM((tm, tn), F32)],
        compiler_params=_cparams("parallel", "parallel", "arbitrary"))(*ins)


def _rmsnorm_fwd(x, g, *, name):
    s, d = x.shape
    tm = _tile(s, 256)

    def body(x_ref, g_ref, h_ref, r_ref):
        xf = x_ref[...]
        r = lax.rsqrt(jnp.mean(xf * xf, axis=-1, keepdims=True) + RMS_EPS)
        h_ref[...] = ((xf * r) * g_ref[...]).astype(h_ref.dtype)
        r_ref[...] = r

    return pl.pallas_call(
        body, name=name, grid=(s // tm,),
        in_specs=[pl.BlockSpec((tm, d), lambda i: (i, 0)), pl.BlockSpec((1, d), lambda i: (0, 0))],
        out_specs=[pl.BlockSpec((tm, d), lambda i: (i, 0)), pl.BlockSpec((tm, 1), lambda i: (i, 0))],
        out_shape=[jax.ShapeDtypeStruct((s, d), BF16), jax.ShapeDtypeStruct((s, 1), F32)],
        compiler_params=_cparams("parallel"))(x, g.reshape(1, d))


def _rms_bwd_math(dh, x, r, g):
    xr = x * r
    dyg = dh * g
    m = jnp.mean(dyg * xr, axis=-1, keepdims=True)
    return r * (dyg - xr * m), dh * xr


def _rmsnorm_bwd(dh, x, r, g, dres, *, name):
    s, d = x.shape
    tm = _tile(s, 256)
    has_res = dres is not None

    def body(*refs):
        if has_res:
            dh_ref, x_ref, r_ref, g_ref, res_ref, dx_ref, dg_ref = refs
        else:
            dh_ref, x_ref, r_ref, g_ref, dx_ref, dg_ref = refs
        dx, dgp = _rms_bwd_math(dh_ref[...].astype(F32), x_ref[...], r_ref[...], g_ref[...])
        if has_res:
            dx = dx + res_ref[...]
        dx_ref[...] = dx

        @pl.when(pl.program_id(0) == 0)
        def _():
            dg_ref[...] = jnp.zeros_like(dg_ref)

        dg_ref[...] += jnp.sum(dgp, axis=0, keepdims=True)

    row = pl.BlockSpec((tm, d), lambda i: (i, 0))
    vec = pl.BlockSpec((1, d), lambda i: (0, 0))
    ins = [dh, x, r, g.reshape(1, d)] + ([dres] if has_res else [])
    specs = [row, row, pl.BlockSpec((tm, 1), lambda i: (i, 0)), vec] + ([row] if has_res else [])
    return pl.pallas_call(
        body, name=name, grid=(s // tm,), in_specs=specs, out_specs=[row, vec],
        out_shape=[jax.ShapeDtypeStruct((s, d), F32), jax.ShapeDtypeStruct((1, d), F32)],
        compiler_params=_cparams("arbitrary"))(*ins)


def _final_loss(x, g, target, *, name):
    s, d = x.shape
    tm = _tile(s, 256)

    def body(x_ref, g_ref, t_ref, loss_ref, dx_ref, dg_ref):
        xf = x_ref[...]
        gv = g_ref[...]
        r = lax.rsqrt(jnp.mean(xf * xf, axis=-1, keepdims=True) + RMS_EPS)
        diff = (xf * r) * gv - t_ref[...]
        part = 0.5 * jnp.sum(jnp.mean(diff * diff, axis=-1, keepdims=True))
        dx, dgp = _rms_bwd_math(diff * (1.0 / d), xf, r, gv)
        dx_ref[...] = dx

        @pl.when(pl.program_id(0) == 0)
        def _():
            dg_ref[...] = jnp.zeros_like(dg_ref)
            loss_ref[...] = jnp.zeros_like(loss_ref)

        dg_ref[...] += jnp.sum(dgp, axis=0, keepdims=True)
        loss_ref[...] += part

    row = pl.BlockSpec((tm, d), lambda i: (i, 0))
    vec = pl.BlockSpec((1, d), lambda i: (0, 0))
    return pl.pallas_call(
        body, name=name, grid=(s // tm,), in_specs=[row, vec, row],
        out_specs=[pl.BlockSpec((8, LANES), lambda i: (0, 0)), row, vec],
        out_shape=[jax.ShapeDtypeStruct((8, LANES), F32), jax.ShapeDtypeStruct((s, d), F32), jax.ShapeDtypeStruct((1, d), F32)],
        compiler_params=_cparams("arbitrary"))(x, g.reshape(1, d), target)


SUBLANES = 8
TAP_GROUPS = 4


def _shift_scratch(tm, c):
    return pltpu.VMEM((SUBLANES - 1, tm + CONV_HALO - SUBLANES, c), F32)


def _shift_copies(src_ref, sh_ref):
    rows = sh_ref.shape[1]
    for r in range(1, SUBLANES):
        sh_ref[r - 1] = src_ref[pl.ds(r, rows), :]


def _read_shifted(src_ref, sh_ref, off, r0):
    r = off % SUBLANES
    base = off - r + r0
    return src_ref[pl.ds(base, SUBLANES), :] if r == 0 else sh_ref[r - 1, pl.ds(base, SUBLANES), :]


def _tap_rows(w):
    return [jnp.broadcast_to(w[k:k + 1, :], (SUBLANES, w.shape[1])) for k in range(w.shape[0])]


def _tap_sum(src_ref, sh_ref, wk, offs, init, tm):
    out = []
    for r0 in range(0, tm, SUBLANES * TAP_GROUPS):
        accs = [init] * TAP_GROUPS
        for wv, off in zip(wk, offs):
            accs = [acc + wv * _read_shifted(src_ref, sh_ref, off, r0 + SUBLANES * g) for g, acc in enumerate(accs)]
        out += accs
    return jnp.concatenate(out, axis=0)


def _conv_taps(gp_ref, sh_ref, w, bias, taps, tm):
    halo = gp_ref.shape[0] - tm
    _shift_copies(gp_ref, sh_ref)
    offs = [halo - (taps - 1) + k for k in range(taps)]
    return _tap_sum(gp_ref, sh_ref, _tap_rows(w), offs, jnp.broadcast_to(bias, (SUBLANES, w.shape[1])), tm)


def _conv_core(val, glu, valh, gluh, first, gp_ref, sh_ref, w, bias, lg, lb, taps, tm):
    sg = _sigmoid(glu)
    g = val * sg
    gh = jnp.where(first, 0.0, valh * _sigmoid(gluh))
    gp_ref[0:CONV_HALO, :] = gh
    gp_ref[CONV_HALO:, :] = g
    d = _conv_taps(gp_ref, sh_ref, w, bias, taps, tm)
    mu = jnp.mean(d, axis=-1, keepdims=True)
    dc = d - mu
    rstd = lax.rsqrt(jnp.mean(dc * dc, axis=-1, keepdims=True) + LN_EPS)
    xhat = dc * rstd
    ln = xhat * lg + lb
    return sg, xhat, rstd, ln


def _conv_fwd(ua, dw_w, dw_b, ln_g, ln_b, pw, *, name):
    s = ua.shape[0]
    taps, c = dw_w.shape
    tm = _tile(s, 512)
    hb = tm // CONV_HALO

    def body(val_ref, glu_ref, valh_ref, gluh_ref, w_ref, b_ref, lg_ref, lb_ref, pw_ref, y_ref, gp_ref, sh_ref):
        first = pl.program_id(0) == 0
        _, _, _, ln = _conv_core(val_ref[...], glu_ref[...], valh_ref[...], gluh_ref[...], first, gp_ref, sh_ref,
                                 w_ref[...], b_ref[...], lg_ref[...], lb_ref[...], taps, tm)
        sw = ln * _sigmoid(ln)
        y_ref[...] = _nn(sw.astype(BF16), pw_ref[...])

    cur = lambda col: pl.BlockSpec((tm, c), lambda i: (i, col))
    prev = lambda col: pl.BlockSpec((CONV_HALO, c), lambda i: (jnp.maximum(i * hb - 1, 0), col))
    full = lambda a: pl.BlockSpec(a.shape, lambda i: (0,) * a.ndim)
    vecs = [dw_w, dw_b.reshape(1, c), ln_g.reshape(1, c), ln_b.reshape(1, c), pw]
    return pl.pallas_call(
        body, name=name, grid=(s // tm,),
        in_specs=[cur(0), cur(1), prev(0), prev(1)] + [full(a) for a in vecs],
        out_specs=pl.BlockSpec((tm, c), lambda i: (i, 0)),
        out_shape=jax.ShapeDtypeStruct((s, c), F32),
        scratch_shapes=[pltpu.VMEM((tm + CONV_HALO, c), F32), _shift_scratch(tm, c)],
        compiler_params=_cparams("parallel"))(ua, ua, ua, ua, *vecs)


def _conv_bwd_a(ua, dy, dw_w, dw_b, ln_g, ln_b, pw, *, name):
    s = ua.shape[0]
    taps, c = dw_w.shape
    tm = _tile(s, 512)
    hb = tm // CONV_HALO

    def body(val_ref, glu_ref, valh_ref, gluh_ref, dy_ref, w_ref, b_ref, lg_ref, lb_ref, pw_ref,
             dd_ref, dpw_ref, dlg_ref, dlb_ref, gp_ref, sh_ref):
        first = pl.program_id(0) == 0
        lg = lg_ref[...]
        _, xhat, rstd, ln = _conv_core(val_ref[...], glu_ref[...], valh_ref[...], gluh_ref[...], first, gp_ref, sh_ref,
                                       w_ref[...], b_ref[...], lg, lb_ref[...], taps, tm)
        sw, dsw = _silu_and_grad(ln)
        dyb = dy_ref[...].astype(BF16)
        ds = _nt(dyb, pw_ref[...])
        dln = ds * dsw
        dxhat = dln * lg
        m1 = jnp.mean(dxhat, axis=-1, keepdims=True)
        m2 = jnp.mean(dxhat * xhat, axis=-1, keepdims=True)
        dd_ref[...] = rstd * (dxhat - m1 - xhat * m2)

        @pl.when(first)
        def _():
            dpw_ref[...] = jnp.zeros_like(dpw_ref)
            dlg_ref[...] = jnp.zeros_like(dlg_ref)
            dlb_ref[...] = jnp.zeros_like(dlb_ref)

        dpw_ref[...] += _tn(sw.astype(BF16), dyb)
        dlg_ref[...] += jnp.sum(dln * xhat, axis=0, keepdims=True)
        dlb_ref[...] += jnp.sum(dln, axis=0, keepdims=True)

    cur = lambda col: pl.BlockSpec((tm, c), lambda i: (i, col))
    prev = lambda col: pl.BlockSpec((CONV_HALO, c), lambda i: (jnp.maximum(i * hb - 1, 0), col))
    full = lambda a: pl.BlockSpec(a.shape, lambda i: (0,) * a.ndim)
    vec = pl.BlockSpec((1, c), lambda i: (0, 0))
    vecs = [dw_w, dw_b.reshape(1, c), ln_g.reshape(1, c), ln_b.reshape(1, c), pw]
    return pl.pallas_call(
        body, name=name, grid=(s // tm,),
        in_specs=[cur(0), cur(1), prev(0), prev(1), pl.BlockSpec((tm, c), lambda i: (i, 0))] + [full(a) for a in vecs],
        out_specs=[pl.BlockSpec((tm, c), lambda i: (i, 0)), pl.BlockSpec((c, c), lambda i: (0, 0)), vec, vec],
        out_shape=[jax.ShapeDtypeStruct((s, c), F32), jax.ShapeDtypeStruct((c, c), F32),
                   jax.ShapeDtypeStruct((1, c), F32), jax.ShapeDtypeStruct((1, c), F32)],
        scratch_shapes=[pltpu.VMEM((tm + CONV_HALO, c), F32), _shift_scratch(tm, c)],
        compiler_params=_cparams("arbitrary"))(ua, ua, ua, ua, dy, *vecs)


def _conv_bwd_b(ua, dd, dw_w, du, *, name):
    s = ua.shape[0]
    taps, c = dw_w.shape
    tm = _tile(s, 512)
    hb = tm // CONV_HALO
    nt = s // tm

    def body(val_ref, glu_ref, valh_ref, gluh_ref, dd_ref, ddn_ref, w_ref, du_in, du_ref, dw_ref, db_ref,
             gp_ref, ddp_ref, shg_ref, shd_ref):
        i = pl.program_id(0)
        val = val_ref[...]
        sg = _sigmoid(glu_ref[...])
        gp_ref[0:CONV_HALO, :] = jnp.where(i == 0, 0.0, valh_ref[...] * _sigmoid(gluh_ref[...]))
        gp_ref[CONV_HALO:, :] = val * sg
        dd = dd_ref[...]
        ddp_ref[0:tm, :] = dd
        ddp_ref[tm:, :] = jnp.where(i == nt - 1, 0.0, ddn_ref[...])
        _shift_copies(gp_ref, shg_ref)
        _shift_copies(ddp_ref, shd_ref)
        zero = jnp.zeros((SUBLANES, c), F32)
        dg = _tap_sum(ddp_ref, shd_ref, _tap_rows(w_ref[...]), [taps - 1 - k for k in range(taps)], zero, tm)
        dws = []
        for k in range(taps):
            accs = [zero] * TAP_GROUPS
            for n, r0 in enumerate(range(0, tm, SUBLANES)):
                accs[n % TAP_GROUPS] = accs[n % TAP_GROUPS] + ddp_ref[pl.ds(r0, SUBLANES), :] * _read_shifted(
                    gp_ref, shg_ref, CONV_HALO - (taps - 1) + k, r0)
            dws.append(jnp.sum(sum(accs[1:], accs[0]), axis=0, keepdims=True))
        du_ref[:, 0:c] = (dg * sg).astype(du_ref.dtype)
        du_ref[:, c:] = (dg * val * sg * (1.0 - sg)).astype(du_ref.dtype)

        @pl.when(i == 0)
        def _():
            dw_ref[...] = jnp.zeros_like(dw_ref)
            db_ref[...] = jnp.zeros_like(db_ref)

        dw_ref[...] += jnp.concatenate(dws, axis=0)
        db_ref[...] += jnp.sum(dd, axis=0, keepdims=True)

    cur = lambda col: pl.BlockSpec((tm, c), lambda i: (i, col))
    prev = lambda col: pl.BlockSpec((CONV_HALO, c), lambda i: (jnp.maximum(i * hb - 1, 0), col))
    nxt = pl.BlockSpec((CONV_HALO, c), lambda i: (jnp.minimum((i + 1) * hb, s // CONV_HALO - 1), 0))
    return pl.pallas_call(
        body, name=name, grid=(nt,),
        in_specs=[cur(0), cur(1), prev(0), prev(1), pl.BlockSpec((tm, c), lambda i: (i, 0)), nxt,
                  pl.BlockSpec((taps, c), lambda i: (0, 0)), ANY],
        out_specs=[pl.BlockSpec((tm, 2 * c), lambda i: (i, 0)),
                   pl.BlockSpec((taps, c), lambda i: (0, 0)), pl.BlockSpec((1, c), lambda i: (0, 0))],
        out_shape=[jax.ShapeDtypeStruct(du.shape, du.dtype), jax.ShapeDtypeStruct((taps, c), F32), jax.ShapeDtypeStruct((1, c), F32)],
        input_output_aliases={7: 0},
        scratch_shapes=[pltpu.VMEM((tm + CONV_HALO, c), F32), pltpu.VMEM((tm + CONV_HALO, c), F32),
                        _shift_scratch(tm, c), _shift_scratch(tm, c)],
        compiler_params=_cparams("arbitrary"))(ua, ua, ua, ua, dd, dd, dw_w, du)


LOG2_E = 1.4426950408889634
SB_HEADS_PER_STEP = 4


def _sb_logs(qk, mask):
    z = qk * (HEAD_DIM ** -0.5 * LOG2_E)
    ls = jnp.minimum(z, 0.0) - jnp.log2(1.0 + jnp.exp2(-jnp.abs(z)))
    lm = ls - z
    if mask is not None:
        lm = jnp.where(mask, lm, 0.0)
    return ls, lm


def _diag_mask(b):
    return lax.broadcasted_iota(jnp.int32, (b, b), 1) < lax.broadcasted_iota(jnp.int32, (b, b), 0)


def _split_dot(x, tri):
    hi = x.astype(BF16)
    lo = (x - hi.astype(F32)).astype(BF16)
    return _nn(hi, tri) + _nn(lo, tri)


def _tri(bk, cmp):
    r = lax.broadcasted_iota(jnp.int32, (bk, bk), 0)
    c = lax.broadcasted_iota(jnp.int32, (bk, bk), 1)
    return cmp(r, c).astype(BF16)


def _sb_fwd(qkv, heads, *, name, blk=256):
    s = qkv.shape[0]
    b = _tile(s, blk)
    nq = s // b
    hp = min(SB_HEADS_PER_STEP, heads)
    assert heads % hp == 0
    groups = heads // hp
    wide = hp * HEAD_DIM

    def body(q_ref, k_ref, v_ref, o_ref, w_hbm, stage, sems):
        g = pl.program_id(0)
        i = pl.program_id(1)
        sls = [slice(n * HEAD_DIM, (n + 1) * HEAD_DIM) for n in range(hp)]
        qs = [q_ref[:, sl] for sl in sls]
        tri = _tri(b, lambda r, c: r > c)
        diag = _diag_mask(b)
        r0 = pl.multiple_of(i * b, b)

        def saves(slot, j):
            c0 = pl.multiple_of(j * b, b)
            return [pltpu.make_async_copy(stage.at[slot, n, w], w_hbm.at[w, g * hp + n, pl.ds(r0, b), pl.ds(c0, b)], sems.at[slot])
                    for n in range(hp) for w in range(2)]

        def tile(t, j, carry, masked):
            slot = t % 2
            if not masked:
                @pl.when(t >= 2)
                def _():
                    for cp in saves(slot, j):
                        cp.wait()

            s0 = pl.multiple_of(j * b, b)
            kbs = [k_ref[pl.ds(s0, b), sl] for sl in sls]
            vbs = [v_ref[pl.ds(s0, b), sl] for sl in sls]
            zs = [_nt(q, kb) for q, kb in zip(qs, kbs)]
            sc = [_sb_logs(z, diag if masked else None) for z in zs]
            after = [_split_dot(lm, tri) for _, lm in sc]
            out = []
            for n, ((ls, lm), af, (acc, c)) in enumerate(zip(sc, after, carry)):
                a = jnp.exp2(ls + (af + c))
                if masked:
                    a = jnp.where(diag, a, 0.0)
                ab = a.astype(BF16)
                stage[slot, n, 0] = ab
                stage[slot, n, 1] = jnp.exp2(ls).astype(BF16)
                out.append((ab, acc, c + jnp.sum(lm, axis=1, keepdims=True)))
            for cp in saves(slot, j):
                cp.start()
            return tuple((acc + _nn(ab, vb), c) for vb, (ab, acc, c) in zip(vbs, out))

        zero = tuple((jnp.zeros((b, HEAD_DIM), F32), jnp.zeros((b, 1), F32)) for _ in range(hp))
        carry = tile(0, i, zero, True)
        carry = lax.fori_loop(0, i, lambda jj, cr: tile(jj + 1, i - 1 - jj, cr, False), carry)
        for sl, (acc, _) in zip(sls, carry):
            o_ref[:, sl] = acc
        for cp in saves(i % 2, 0):
            cp.wait()

        @pl.when(i >= 1)
        def _():
            for cp in saves((i + 1) % 2, 0):
                cp.wait()

    return pl.pallas_call(
        body, name=name, grid=(groups, nq),
        in_specs=[pl.BlockSpec((b, wide), lambda g, i: (i, g)),
                  pl.BlockSpec((s, wide), lambda g, i: (0, groups + g)),
                  pl.BlockSpec((s, wide), lambda g, i: (0, 2 * groups + g))],
        out_specs=[pl.BlockSpec((b, wide), lambda g, i: (i, g)), ANY],
        out_shape=[jax.ShapeDtypeStruct((s, heads * HEAD_DIM), F32), jax.ShapeDtypeStruct((2, heads, s, s), BF16)],
        scratch_shapes=[pltpu.VMEM((2, hp, 2, b, b), BF16), pltpu.SemaphoreType.DMA((2,))],
        compiler_params=_cparams("parallel", "arbitrary"))(qkv, qkv, qkv)


def _sb_bwd(qkv, do, saved, du, du_col, heads, *, name, blk=256):
    s = qkv.shape[0]
    b = _tile(s, blk)
    nq = s // b
    hp = min(SB_HEADS_PER_STEP, heads)
    assert heads % hp == 0
    groups = heads // hp
    wide = hp * HEAD_DIM
    scale = HEAD_DIM ** -0.5

    def body(q_ref, k_ref, v_ref, do_ref, w_hbm, du_in, du_ref, dk_acc, dv_acc, stage, sems, dq_out, dkv_out, out_sems):
        g = pl.program_id(0)
        i = pl.program_id(1)

        @pl.when(i == 0)
        def _():
            dk_acc[...] = jnp.zeros_like(dk_acc)
            dv_acc[...] = jnp.zeros_like(dv_acc)

        sls = [slice(n * HEAD_DIM, (n + 1) * HEAD_DIM) for n in range(hp)]
        qs = [q_ref[:, sl] for sl in sls]
        dos = [do_ref[:, sl].astype(BF16) for sl in sls]
        tri_excl = _tri(b, lambda r, c: r < c)
        diag = _diag_mask(b)
        r0 = pl.multiple_of(i * b, b)

        def loads(slot, j):
            c0 = pl.multiple_of(j * b, b)
            return [pltpu.make_async_copy(w_hbm.at[w, g * hp + n, pl.ds(r0, b), pl.ds(c0, b)], stage.at[slot, n, w], sems.at[slot])
                    for n in range(hp) for w in range(2)]

        def tile(j, carry, masked):
            slot = j % 2

            @pl.when(j < i)
            def _():
                for cp in loads(1 - slot, j + 1):
                    cp.start()

            for cp in loads(slot, j):
                cp.wait()
            s0 = pl.multiple_of(j * b, b)
            kbs = [k_ref[pl.ds(s0, b), sl] for sl in sls]
            vbs = [v_ref[pl.ds(s0, b), sl] for sl in sls]
            ab = [stage[slot, n, 0] for n in range(hp)]
            ps = [_nt(dob, vb) for dob, vb in zip(dos, vbs)]
            gs = [a.astype(F32) * p for a, p in zip(ab, ps)]
            hs = [_nn(gg.astype(BF16), tri_excl) for gg in gs]
            dzb = []
            for n, (gg, h, (_, cg)) in enumerate(zip(gs, hs, carry)):
                dz = (gg - (gg + (h + cg)) * stage[slot, n, 1].astype(F32)) * scale
                if masked:
                    dz = jnp.where(diag, dz, 0.0)
                dzb.append(dz.astype(BF16))
            out = tuple((dq + _nn(dz, kb), cg + jnp.sum(gg, axis=1, keepdims=True))
                        for dz, kb, gg, (dq, cg) in zip(dzb, kbs, gs, carry))
            for sl, dz, a, q, dob in zip(sls, dzb, ab, qs, dos):
                dk_acc[pl.ds(s0, b), sl] += _tn(dz, q)
                dv_acc[pl.ds(s0, b), sl] += _tn(a, dob)
            return out

        for cp in loads(0, 0):
            cp.start()
        carry = tuple((jnp.zeros((b, HEAD_DIM), F32), jnp.zeros((b, 1), F32)) for _ in range(hp))
        carry = lax.fori_loop(0, i, lambda j, cr: tile(j, cr, False), carry)
        carry = tile(i, carry, True)
        cols = [pl.multiple_of(du_col + n * heads * HEAD_DIM + g * wide, LANES) for n in range(3)]
        for sl, (dq, _) in zip(sls, carry):
            dq_out[:, sl] = dq.astype(dq_out.dtype)
        put = pltpu.make_async_copy(dq_out, du_ref.at[pl.ds(r0, b), pl.ds(cols[0], wide)], out_sems.at[0])
        put.start()
        put.wait()

        @pl.when(i == nq - 1)
        def _():
            dkv_out[0] = dk_acc[...].astype(dkv_out.dtype)
            dkv_out[1] = dv_acc[...].astype(dkv_out.dtype)
            puts = [pltpu.make_async_copy(dkv_out.at[n], du_ref.at[:, pl.ds(cols[1 + n], wide)], out_sems.at[1 + n]) for n in range(2)]
            for cp in puts:
                cp.start()
            for cp in puts:
                cp.wait()

    row = pl.BlockSpec((b, wide), lambda g, i: (i, g))
    col = lambda off: pl.BlockSpec((s, wide), lambda g, i: (0, off + g), pipeline_mode=pl.Buffered(1))
    return pl.pallas_call(
        body, name=name, grid=(groups, nq),
        in_specs=[row, col(groups), col(2 * groups), row, ANY, ANY],
        out_specs=ANY, out_shape=jax.ShapeDtypeStruct(du.shape, du.dtype), input_output_aliases={5: 0},
        scratch_shapes=[pltpu.VMEM((s, wide), F32), pltpu.VMEM((s, wide), F32),
                        pltpu.VMEM((2, hp, 2, b, b), BF16), pltpu.SemaphoreType.DMA((2,)),
                        pltpu.VMEM((b, wide), BF16), pltpu.VMEM((2, s, wide), BF16), pltpu.SemaphoreType.DMA((3,))],
        compiler_params=_cparams("arbitrary", "arbitrary"))(qkv, qkv, qkv, do, saved, du)


def _shift_rows(x, n, fill, *, down):
    rows = x.shape[0]
    if n % 8 == 0:
        pad = jnp.full((n, x.shape[1]), fill, x.dtype)
        return jnp.concatenate([pad, x[:rows - n]], axis=0) if down else jnp.concatenate([x[n:], pad], axis=0)
    t = lax.broadcasted_iota(jnp.int32, x.shape, 0)
    if down:
        return jnp.where(t >= n, pltpu.roll(x, n, 0), fill)
    return jnp.where(t < rows - n, pltpu.roll(x, rows - n, 0), fill)


def _scan_rows(a, b, *, reverse):
    n = 1
    while n < a.shape[0]:
        b = a * _shift_rows(b, n, 0.0, down=not reverse) + b
        a = a * _shift_rows(a, n, 1.0, down=not reverse)
        n *= 2
    return a, b


def _neg_expm1(x):
    p = 1.0 + x * (1.0 / 7.0)
    for k in (6.0, 5.0, 4.0, 3.0, 2.0):
        p = 1.0 + x * (1.0 / k) * p
    return jnp.where(x > -0.25, -(x * p), 1.0 - jnp.exp(x))


def _softplus_neg(lam):
    z = -lam
    e = jnp.exp(-jnp.abs(z))
    u = 1.0 + e
    d = u - 1.0
    log1p_e = jnp.where(d == 0.0, e, jnp.log(u) * (e / jnp.where(d == 0.0, 1.0, d)))
    return jnp.maximum(z, 0.0) + log1p_e


def _lru_gates(xp_ref, w, bias, wa_ref, ba, wx_ref, bx, sp, taps, tm, heads):
    halo = xp_ref.shape[0] - tm
    xc = jnp.broadcast_to(bias, (tm, w.shape[1]))
    for k in range(taps):
        xc = xc + w[k:k + 1, :] * xp_ref[pl.ds(halo - (taps - 1) + k, tm), :]
    xb = xc.astype(BF16)
    pr, pi = [], []
    for n in range(heads):
        xh = xb[:, n * HEAD_DIM:(n + 1) * HEAD_DIM]
        pr.append(_nn(xh, wa_ref[n]))
        pi.append(_nn(xh, wx_ref[n]))
    r = _sigmoid(jnp.concatenate(pr, axis=1) + ba)
    ig = _sigmoid(jnp.concatenate(pi, axis=1) + bx)
    log_a = (-LRU_C) * r * sp
    a = jnp.exp(log_a)
    mult = jnp.sqrt(_neg_expm1(2.0 * log_a))
    return xc, r, ig, a, mult


def _lru_fwd(ub, x_col, conv_w, conv_b, wa, ba, wx, bx, lam, *, name):
    s = ub.shape[0]
    taps, w = conv_w.shape
    heads = w // HEAD_DIM
    tm = _tile(s, 256)
    hb = tm // LRU_HALO

    def body(x_ref, xh_ref, cw_ref, cb_ref, wa_ref, ba_ref, wx_ref, bx_ref, lam_ref, h_ref, xp_ref, carry_ref):
        i = pl.program_id(0)

        @pl.when(i == 0)
        def _():
            carry_ref[...] = jnp.zeros_like(carry_ref)

        xp_ref[0:LRU_HALO, :] = jnp.where(i == 0, 0.0, xh_ref[...])
        xp_ref[LRU_HALO:, :] = x_ref[...]
        sp = _softplus_neg(lam_ref[...])
        xc, _, ig, a, mult = _lru_gates(xp_ref, cw_ref[...], cb_ref[...], wa_ref, ba_ref[...], wx_ref, bx_ref[...],
                                        sp, taps, tm, heads)
        ac, bc = _scan_rows(a, mult * (ig * xc), reverse=False)
        h = ac * carry_ref[0:1, :] + bc
        h_ref[...] = h
        carry_ref[...] = jnp.broadcast_to(h[tm - 1:tm, :], carry_ref.shape)

    full = lambda arr: pl.BlockSpec(arr.shape, lambda i: (0,) * arr.ndim)
    vecs = [conv_w, conv_b.reshape(1, w), wa.astype(BF16), ba.reshape(1, w), wx.astype(BF16), bx.reshape(1, w), lam.reshape(1, w)]
    return pl.pallas_call(
        body, name=name, grid=(s // tm,),
        in_specs=[pl.BlockSpec((tm, w), lambda i: (i, x_col)),
                  pl.BlockSpec((LRU_HALO, w), lambda i: (jnp.maximum(i * hb - 1, 0), x_col))] + [full(v) for v in vecs],
        out_specs=pl.BlockSpec((tm, w), lambda i: (i, 0)),
        out_shape=jax.ShapeDtypeStruct((s, w), F32),
        scratch_shapes=[pltpu.VMEM((tm + LRU_HALO, w), F32), pltpu.VMEM((8, w), F32)],
        compiler_params=_cparams("arbitrary"))(ub, ub, *vecs)


def _lru_bwd(ub, x_col, h, dh, conv_w, conv_b, wa, ba, wx, bx, lam, du, du_col, *, name):
    s = ub.shape[0]
    taps, w = conv_w.shape
    heads = w // HEAD_DIM
    tm = _tile(s, 256)
    hb = tm // LRU_HALO
    nt = s // tm

    def body(x_ref, xh_ref, h_ref, hh_ref, dh_ref, cw_ref, cb_ref, wa_ref, ba_ref, wx_ref, bx_ref, lam_ref, du_in,
             dx_ref, dcw_ref, dcb_ref, dwa_ref, dba_ref, dwx_ref, dbx_ref, dlam_ref,
             xp_ref, dxp_ref, dlt_ref, afirst_ref, dxc_next_ref, dsp_ref):
        step = pl.program_id(0)
        i = nt - 1 - step

        @pl.when(step == 0)
        def _():
            for ref in (dcw_ref, dcb_ref, dwa_ref, dba_ref, dwx_ref, dbx_ref, dlam_ref, dlt_ref, dxc_next_ref, dsp_ref):
                ref[...] = jnp.zeros_like(ref)
            afirst_ref[...] = jnp.ones_like(afirst_ref)

        xp_ref[0:LRU_HALO, :] = jnp.where(i == 0, 0.0, xh_ref[...])
        xp_ref[LRU_HALO:, :] = x_ref[...]
        cw = cw_ref[...]
        lam_v = lam_ref[...]
        sp = _softplus_neg(lam_v)
        xc, r, ig, a, mult = _lru_gates(xp_ref, cw, cb_ref[...], wa_ref, ba_ref[...], wx_ref, bx_ref[...], sp, taps, tm, heads)
        rows = lax.broadcasted_iota(jnp.int32, (tm, w), 0)
        a_next = jnp.where(rows == tm - 1, afirst_ref[0:1, :], _shift_rows(a, 1, 1.0, down=False))
        ac, bc = _scan_rows(a_next, dh_ref[...], reverse=True)
        delta = ac * dlt_ref[0:1, :] + bc
        hv = h_ref[...]
        h_last_prev = jnp.where(i == 0, 0.0, hh_ref[LRU_HALO - 1:LRU_HALO, :])
        h_prev = jnp.where(rows == 0, h_last_prev, _shift_rows(hv, 1, 0.0, down=True))
        gated = ig * xc
        da = delta * h_prev
        dmult = delta * gated
        dgated = delta * mult
        dlog_a = da * a - dmult * (a * a) / mult
        dpr = dlog_a * ((-LRU_C) * sp) * r * (1.0 - r)
        dpi = dgated * xc * ig * (1.0 - ig)
        dxc = dgated * ig
        dsp_ref[...] += jnp.sum(dlog_a * ((-LRU_C) * r), axis=0, keepdims=True)
        dba_ref[...] += jnp.sum(dpr, axis=0, keepdims=True)
        dbx_ref[...] += jnp.sum(dpi, axis=0, keepdims=True)
        xb = xc.astype(BF16)
        dprb = dpr.astype(BF16)
        dpib = dpi.astype(BF16)
        back = []
        for n in range(heads):
            sl = slice(n * HEAD_DIM, (n + 1) * HEAD_DIM)
            dwa_ref[n] += _tn(xb[:, sl], dprb[:, sl])
            dwx_ref[n] += _tn(xb[:, sl], dpib[:, sl])
            back.append(_nt(dprb[:, sl], wa_ref[n]) + _nt(dpib[:, sl], wx_ref[n]))
        dxc = dxc + jnp.concatenate(back, axis=1)
        dxp_ref[0:tm, :] = dxc
        dxp_ref[tm:, :] = dxc_next_ref[...]
        dx = jnp.zeros((tm, w), F32)
        dws = []
        for k in range(taps):
            dx = dx + cw[k:k + 1, :] * dxp_ref[pl.ds(taps - 1 - k, tm), :]
            dws.append(jnp.sum(dxc * xp_ref[pl.ds(LRU_HALO - (taps - 1) + k, tm), :], axis=0, keepdims=True))
        dx_ref[...] = dx.astype(dx_ref.dtype)
        dcw_ref[...] += jnp.concatenate(dws, axis=0)
        dcb_ref[...] += jnp.sum(dxc, axis=0, keepdims=True)
        dlt_ref[...] = jnp.broadcast_to(delta[0:1, :], dlt_ref.shape)
        afirst_ref[...] = jnp.broadcast_to(a[0:1, :], afirst_ref.shape)
        dxc_next_ref[...] = dxc[0:LRU_HALO, :]

        @pl.when(step == nt - 1)
        def _():
            dlam_ref[...] = dsp_ref[...] * (-_sigmoid(-lam_v))

    rev = lambda col: pl.BlockSpec((tm, w), lambda st: (nt - 1 - st, col))
    prev = lambda col: pl.BlockSpec((LRU_HALO, w), lambda st: (jnp.maximum((nt - 1 - st) * hb - 1, 0), col))
    full = lambda arr: pl.BlockSpec(arr.shape, lambda st: (0,) * arr.ndim)
    vec = pl.BlockSpec((1, w), lambda st: (0, 0))
    vecs = [conv_w, conv_b.reshape(1, w), wa.astype(BF16), ba.reshape(1, w), wx.astype(BF16), bx.reshape(1, w), lam.reshape(1, w)]
    vshape = jax.ShapeDtypeStruct((1, w), F32)
    return pl.pallas_call(
        body, name=name, grid=(nt,),
        in_specs=[rev(x_col), prev(x_col), rev(0), prev(0), rev(0)] + [full(v) for v in vecs] + [ANY],
        out_specs=[rev(du_col), full(conv_w), vec, full(wa), vec, full(wx), vec, vec],
        out_shape=[jax.ShapeDtypeStruct(du.shape, du.dtype), jax.ShapeDtypeStruct(conv_w.shape, F32), vshape,
                   jax.ShapeDtypeStruct(wa.shape, F32), vshape, jax.ShapeDtypeStruct(wx.shape, F32), vshape, vshape],
        input_output_aliases={5 + len(vecs): 0},
        scratch_shapes=[pltpu.VMEM((tm + LRU_HALO, w), F32), pltpu.VMEM((tm + LRU_HALO, w), F32),
                        pltpu.VMEM((8, w), F32), pltpu.VMEM((8, w), F32), pltpu.VMEM((LRU_HALO, w), F32), pltpu.VMEM((1, w), F32)],
        compiler_params=_cparams("arbitrary"))(ub, ub, h, h, dh, *vecs, du)


def _group_fwd(y, w, gate):
    r = lax.rsqrt(jnp.mean(y * y, axis=-1, keepdims=True) + RMS_EPS)
    return ((y * r) * w) * (gate * _sigmoid(gate))


def _mix_out_fwd(y_conv, y_attn, y_lru, ua, ub, n_conv, n_attn, n_lru, *, name):
    s, c = y_conv.shape
    wa_ = y_attn.shape[1]
    d = 2 * c + wa_
    tm = _tile(s, 256)
    assert wa_ == 2 * c

    def body(yc_ref, ya_ref, yl_ref, gc_ref, ga_ref, gl_ref, nc_ref, na_ref, nl_ref, o_ref):
        o_ref[:, 0:c] = _group_fwd(yc_ref[...], nc_ref[...], gc_ref[...]).astype(o_ref.dtype)
        o_ref[:, c:c + wa_] = _group_fwd(ya_ref[...], na_ref[...], ga_ref[...]).astype(o_ref.dtype)
        o_ref[:, c + wa_:] = _group_fwd(yl_ref[...], nl_ref[...], gl_ref[...]).astype(o_ref.dtype)

    blk = lambda width, col: pl.BlockSpec((tm, width), lambda i: (i, col))
    vec = lambda width: pl.BlockSpec((1, width), lambda i: (0, 0))
    return pl.pallas_call(
        body, name=name, grid=(s // tm,),
        in_specs=[blk(c, 0), blk(wa_, 0), blk(c, 0), blk(c, 2), blk(wa_, 0), blk(c, 3), vec(c), vec(wa_), vec(c)],
        out_specs=blk(d, 0), out_shape=jax.ShapeDtypeStruct((s, d), BF16),
        compiler_params=_cparams("parallel"))(y_conv, y_attn, y_lru, ua, ub, ub, n_conv.reshape(1, c), n_attn.reshape(1, wa_), n_lru.reshape(1, c))


def _group_bwd(dout, y, w, gate):
    r = lax.rsqrt(jnp.mean(y * y, axis=-1, keepdims=True) + RMS_EPS)
    silu, dsilu = _silu_and_grad(gate)
    dy, dwp = _rms_bwd_math(dout * silu, y, r, w)
    return dy, dout * ((y * r) * w) * dsilu, dwp


def _mix_out_bwd(dy, y_conv, y_attn, y_lru, ua, ub, n_conv, n_attn, n_lru, *, name):
    s, c = y_conv.shape
    wa_ = y_attn.shape[1]
    tm = _tile(s, 256)

    du_width = 5 * c + 4 * wa_
    gate_cols = (2 * c, 3 * c + 3 * wa_, 4 * c + 4 * wa_)

    def body(dy_ref, yc_ref, ya_ref, yl_ref, gc_ref, ga_ref, gl_ref, nc_ref, na_ref, nl_ref,
             dyc_ref, dya_ref, dyl_ref, du_ref, dnc_ref, dna_ref, dnl_ref):
        @pl.when(pl.program_id(0) == 0)
        def _():
            for ref in (dnc_ref, dna_ref, dnl_ref):
                ref[...] = jnp.zeros_like(ref)

        groups = ((dy_ref[:, 0:c], yc_ref, nc_ref, gc_ref, dyc_ref, dnc_ref),
                  (dy_ref[:, c:c + wa_], ya_ref, na_ref, ga_ref, dya_ref, dna_ref),
                  (dy_ref[:, c + wa_:], yl_ref, nl_ref, gl_ref, dyl_ref, dnl_ref))
        for col, (dout, y_ref, n_ref, g_ref, dyo_ref, dn_ref) in zip(gate_cols, groups):
            dyv, dgv, dwp = _group_bwd(dout.astype(F32), y_ref[...], n_ref[...], g_ref[...])
            dyo_ref[...] = dyv.astype(dyo_ref.dtype)
            du_ref[:, col:col + dgv.shape[1]] = dgv.astype(du_ref.dtype)
            dn_ref[...] += jnp.sum(dwp, axis=0, keepdims=True)

    blk = lambda width, col: pl.BlockSpec((tm, width), lambda i: (i, col))
    vec = lambda width: pl.BlockSpec((1, width), lambda i: (0, 0))
    sh = lambda width, dt: jax.ShapeDtypeStruct((s, width), dt)
    vs = lambda width: jax.ShapeDtypeStruct((1, width), F32)
    return pl.pallas_call(
        body, name=name, grid=(s // tm,),
        in_specs=[blk(2 * c + wa_, 0), blk(c, 0), blk(wa_, 0), blk(c, 0), blk(c, 2), blk(wa_, 0), blk(c, 3), vec(c), vec(wa_), vec(c)],
        out_specs=[blk(c, 0), blk(wa_, 0), blk(c, 0), blk(du_width, 0), vec(c), vec(wa_), vec(c)],
        out_shape=[sh(c, BF16), sh(wa_, BF16), sh(c, F32), sh(du_width, BF16), vs(c), vs(wa_), vs(c)],
        compiler_params=_cparams("arbitrary"))(dy, y_conv, y_attn, y_lru, ua, ub, ub, n_conv.reshape(1, c), n_attn.reshape(1, wa_), n_lru.reshape(1, c))


def _xattn_probs(qh, kh):
    sc = _nt(qh, kh) * (HEAD_DIM ** -0.5)
    e = jnp.exp(sc - jnp.max(sc, axis=-1, keepdims=True))
    return e / jnp.sum(e, axis=-1, keepdims=True)


def _xattn_fwd(q, kv, *, name):
    s, w = q.shape
    heads = w // HEAD_DIM
    tm = _tile(s, 512)

    def body(q_ref, kv_ref, o_ref):
        for n in range(heads):
            sl = slice(n * HEAD_DIM, (n + 1) * HEAD_DIM)
            p = _xattn_probs(q_ref[:, sl], kv_ref[:, sl])
            o_ref[:, sl] = _nn(p.astype(BF16), kv_ref[:, w + n * HEAD_DIM:w + (n + 1) * HEAD_DIM]).astype(o_ref.dtype)

    return pl.pallas_call(
        body, name=name, grid=(s // tm,),
        in_specs=[pl.BlockSpec((tm, w), lambda i: (i, 0)), pl.BlockSpec(kv.shape, lambda i: (0, 0))],
        out_specs=pl.BlockSpec((tm, w), lambda i: (i, 0)), out_shape=jax.ShapeDtypeStruct((s, w), BF16),
        compiler_params=_cparams("parallel"))(q, kv)


def _xattn_bwd(q, kv, do, *, name):
    s, w = q.shape
    heads = w // HEAD_DIM
    tm = _tile(s, 512)
    scale = HEAD_DIM ** -0.5

    def body(q_ref, kv_ref, do_ref, dq_ref, dkv_ref):
        @pl.when(pl.program_id(0) == 0)
        def _():
            dkv_ref[...] = jnp.zeros_like(dkv_ref)

        for n in range(heads):
            sl = slice(n * HEAD_DIM, (n + 1) * HEAD_DIM)
            vsl = slice(w + n * HEAD_DIM, w + (n + 1) * HEAD_DIM)
            qh, kh, vh, doh = q_ref[:, sl], kv_ref[:, sl], kv_ref[:, vsl], do_ref[:, sl]
            p = _xattn_probs(qh, kh)
            dp = _nt(doh, vh)
            ds = (p * (dp - jnp.sum(dp * p, axis=-1, keepdims=True)) * scale).astype(BF16)
            dq_ref[:, sl] = _nn(ds, kh).astype(dq_ref.dtype)
            dkv_ref[:, sl] += _tn(ds, qh)
            dkv_ref[:, vsl] += _tn(p.astype(BF16), doh)

    row = pl.BlockSpec((tm, w), lambda i: (i, 0))
    kvs = pl.BlockSpec(kv.shape, lambda i: (0, 0))
    return pl.pallas_call(
        body, name=name, grid=(s // tm,), in_specs=[row, kvs, row], out_specs=[row, kvs],
        out_shape=[jax.ShapeDtypeStruct((s, w), BF16), jax.ShapeDtypeStruct(kv.shape, F32)],
        compiler_params=_cparams("arbitrary"))(q, kv, do)


ROW_BLOCK_BYTES = 4 << 20
PACK_ROWS = 512


def _row_tile(rows, cols):
    limit = max(8, ROW_BLOCK_BYTES // (4 * cols))
    t = 8
    while t * 2 <= limit and rows % (t * 2) == 0:
        t *= 2
    assert rows % t == 0
    return t


def _cast_bf16(w, *, name):
    rows, cols = w.shape
    tr = _row_tile(rows, cols)

    def body(w_ref, o_ref):
        o_ref[...] = w_ref[...].astype(BF16)

    blk = pl.BlockSpec((tr, cols), lambda i: (i, 0))
    return pl.pallas_call(body, name=name, grid=(rows // tr,), in_specs=[blk], out_specs=blk,
                          out_shape=jax.ShapeDtypeStruct((rows, cols), BF16), compiler_params=_cparams("parallel"))(w)


def _adamw(w, m, v, gs, *, name):
    rows, cols = w.shape
    tr = _row_tile(rows, cols * 4)
    ng = len(gs)

    def body(*refs):
        w_ref, m_ref, v_ref = refs[:3]
        g_refs = refs[3:3 + ng]
        g_out, d_out, m_out, v_out = refs[3 + ng:]
        g = g_refs[0][...]
        for r in g_refs[1:]:
            g = g + r[...]
        mn = ADAM_B1 * m_ref[...] + (1.0 - ADAM_B1) * g
        vn = ADAM_B2 * v_ref[...] + (1.0 - ADAM_B2) * (g * g)
        m_hat = mn / (1.0 - ADAM_B1 ** ADAM_STEP)
        v_hat = vn / (1.0 - ADAM_B2 ** ADAM_STEP)
        g_out[...] = g
        d_out[...] = -ADAM_LR * (m_hat / (jnp.sqrt(v_hat) + ADAM_EPS) + ADAM_WD * w_ref[...])
        m_out[...] = mn
        v_out[...] = vn

    blk = pl.BlockSpec((tr, cols), lambda i: (i, 0))
    shp = jax.ShapeDtypeStruct((rows, cols), F32)
    return pl.pallas_call(body, name=name, grid=(rows // tr,), in_specs=[blk] * (3 + ng), out_specs=[blk] * 4,
                          out_shape=[shp] * 4, compiler_params=_cparams("parallel"))(w, m, v, *gs)


def _adamw_layers(w, m, v, gs, *, name):
    layers, rows, cols = w.shape
    tr = _row_tile(rows, cols * 4)
    nblk = rows // tr
    ng = len(gs[0])

    def body(*refs):
        w_ref, m_ref, v_ref = refs[:3]
        g_refs = refs[3:3 + layers * ng]
        g_out, d_out, m_out, v_out = refs[3 + layers * ng:]
        l = pl.program_id(0)
        g = jnp.zeros((tr, cols), F32)
        for ll in range(layers):
            gl = g_refs[ll * ng][...]
            for r in g_refs[ll * ng + 1:(ll + 1) * ng]:
                gl = gl + r[...]
            g = jnp.where(l == ll, gl, g)
        mn = ADAM_B1 * m_ref[...] + (1.0 - ADAM_B1) * g
        vn = ADAM_B2 * v_ref[...] + (1.0 - ADAM_B2) * (g * g)
        m_hat = mn / (1.0 - ADAM_B1 ** ADAM_STEP)
        v_hat = vn / (1.0 - ADAM_B2 ** ADAM_STEP)
        g_out[...] = g
        d_out[...] = -ADAM_LR * (m_hat / (jnp.sqrt(v_hat) + ADAM_EPS) + ADAM_WD * w_ref[...])
        m_out[...] = mn
        v_out[...] = vn

    blk = pl.BlockSpec((None, tr, cols), lambda l, i: (l, i, 0))

    def g_spec(ll):
        return pl.BlockSpec((tr, cols), lambda l, i: (jnp.where(l == ll, i, jnp.where(l < ll, 0, nblk - 1)), 0))

    shp = jax.ShapeDtypeStruct(w.shape, F32)
    return pl.pallas_call(
        body, name=name, grid=(layers, nblk), in_specs=[blk] * 3 + [g_spec(ll) for ll in range(layers) for _ in range(ng)],
        out_specs=[blk] * 4, out_shape=[shp] * 4, compiler_params=_cparams("arbitrary", "arbitrary"),
    )(w, m, v, *[g for gl in gs for g in gl])


OTHER_CHIPS = ((1, 0), (0, 1), (1, 1))
N_CHIPS = 4
ANY = pl.BlockSpec(memory_space=pl.ANY)


def _place():
    return lax.axis_index("x"), lax.axis_index("y"), lax.axis_index("c")


def _flip(v, f):
    return 1 - v if f else v


def _part(ref, lead, axis, chip, size):
    idx = list(lead) + [slice(None)] * (len(ref.shape) - len(lead))
    idx[len(lead) + axis] = pl.ds(pl.multiple_of(chip * size, size), size)
    return ref.at[tuple(idx)]


def _allgather_chips(shards, axes, *, name):
    n = len(shards)
    sizes = [sh.shape[ax] for sh, ax in zip(shards, axes)]

    def full_shape(sh, ax):
        return tuple(d * N_CHIPS if i == ax else d for i, d in enumerate(sh.shape))

    def body(*refs):
        ins, outs = refs[:n], refs[n:2 * n]
        send_sems, recv_sems, loc_sems = refs[2 * n:]
        x, y, c = _place()
        me = 2 * x + y
        local = []
        for a in range(n):
            cp = pltpu.make_async_copy(ins[a], _part(outs[a], (), axes[a], me, sizes[a]), loc_sems.at[a])
            cp.start()
            local.append(cp)

        def remote(a, j, chip):
            fx, fy = OTHER_CHIPS[j]
            return pltpu.make_async_remote_copy(
                src_ref=ins[a], dst_ref=_part(outs[a], (), axes[a], chip, sizes[a]),
                send_sem=send_sems.at[a, j], recv_sem=recv_sems.at[a, j],
                device_id=(_flip(x, fx), _flip(y, fy), c), device_id_type=MESH)

        for a in range(n):
            for j in range(len(OTHER_CHIPS)):
                remote(a, j, me).start()
        for a in range(n):
            for j, (fx, fy) in enumerate(OTHER_CHIPS):
                remote(a, j, 2 * _flip(x, fx) + _flip(y, fy)).wait()
        for cp in local:
            cp.wait()

    return pl.pallas_call(
        body, name=name, in_specs=[ANY] * n, out_specs=[ANY] * n,
        out_shape=[jax.ShapeDtypeStruct(full_shape(sh, ax), sh.dtype) for sh, ax in zip(shards, axes)],
        scratch_shapes=[pltpu.SemaphoreType.DMA((n, 3)), pltpu.SemaphoreType.DMA((n, 3)), pltpu.SemaphoreType.DMA((n,))],
    )(*shards)


HBM = pl.BlockSpec(memory_space=pltpu.HBM)
SEM = pl.BlockSpec(memory_space=pltpu.SEMAPHORE)
SPLIT_COPY = pltpu.CompilerParams(has_side_effects=pltpu.SideEffectType.DATAFLOW_SIDE_EFFECTING)


def _cast_place(w, layer, axis, chip, dep, *, name):
    _, rows, cols = w.shape
    tr = _row_tile(rows, cols)
    nblk = rows // tr

    def body(chip_ref, w_ref, dep_ref, o_ref):
        o_ref[...] = w_ref[...].astype(BF16)

    if axis == 1:
        shape = (rows, cols * N_CHIPS)
        o_spec = pl.BlockSpec((tr, cols), lambda i, chip_ref: (i, chip_ref[0]))
    else:
        shape = (rows * N_CHIPS, cols)
        o_spec = pl.BlockSpec((tr, cols), lambda i, chip_ref: (chip_ref[0] * nblk + i, 0))
    return pl.pallas_call(
        body, name=name,
        grid_spec=pltpu.PrefetchScalarGridSpec(
            num_scalar_prefetch=1, grid=(nblk,),
            in_specs=[pl.BlockSpec((None, tr, cols), lambda i, chip_ref: (layer, i, 0)), ANY], out_specs=o_spec),
        out_shape=jax.ShapeDtypeStruct(shape, BF16), compiler_params=_cparams("parallel"))(chip, w, dep)


def _half_rows(ref, axis, chip, size, half):
    rows = ref.shape[0] // (N_CHIPS if axis == 0 else 1)
    r0 = chip * rows if axis == 0 else 0
    if half is not None:
        rows //= 2
        r0 = r0 + half * rows
    rsl = pl.ds(r0 if isinstance(r0, int) else pl.multiple_of(r0, 16), rows)
    return ref.at[rsl, :] if axis == 0 else ref.at[rsl, pl.ds(pl.multiple_of(chip * size, LANES), size)]


def _gather_copy(refs, a, j, send_sems, recv_sems, *, axes, sizes, arriving, halves=False):
    x, y, c = _place()
    fx, fy = OTHER_CHIPS[j]
    px, py = _flip(x, fx), _flip(y, fy)
    part = _half_rows(refs[a], axes[a], (2 * px + py) if arriving else (2 * x + y), sizes[a], c if halves else None)
    k = a * len(OTHER_CHIPS) + j
    return pltpu.make_async_remote_copy(src_ref=part, dst_ref=part, send_sem=send_sems.at[k], recv_sem=recv_sems.at[k],
                                        device_id=(px, py, c), device_id_type=MESH)


def _swap_halves(arrs, axes, *, name):
    n = len(arrs)
    sizes = [a.shape[ax] // N_CHIPS for a, ax in zip(arrs, axes)]

    def body(*refs):
        ins = refs[:n]
        send_sems, recv_sems = refs[2 * n:]
        x, y, c = _place()

        def copy(a, j, half):
            fx, fy = OTHER_CHIPS[j]
            part = _half_rows(ins[a], axes[a], 2 * _flip(x, fx) + _flip(y, fy), sizes[a], half)
            k = a * len(OTHER_CHIPS) + j
            return pltpu.make_async_remote_copy(src_ref=part, dst_ref=part, send_sem=send_sems.at[k], recv_sem=recv_sems.at[k],
                                                device_id=(x, y, 1 - c), device_id_type=MESH)

        todo = [(a, j) for a in range(n) for j in range(len(OTHER_CHIPS))]
        for a, j in todo:
            copy(a, j, c).start()
        for a, j in todo:
            copy(a, j, c).wait_send()
            copy(a, j, 1 - c).wait_recv()

    res = pl.pallas_call(
        body, name=name, in_specs=[ANY] * n, out_specs=[ANY] * n, out_shape=[jax.ShapeDtypeStruct(a.shape, a.dtype) for a in arrs],
        input_output_aliases={a: a for a in range(n)},
        scratch_shapes=[pltpu.SemaphoreType.DMA((n * len(OTHER_CHIPS),)), pltpu.SemaphoreType.DMA((n * len(OTHER_CHIPS),))],
    )(*arrs)
    return list(res)


def _scatter_copy(srcs, lands, a, j, axes, sizes, send_sems, recv_sems):
    x, y, c = _place()
    fx, fy = OTHER_CHIPS[j]
    px, py = _flip(x, fx), _flip(y, fy)
    k = a * len(OTHER_CHIPS) + j
    return pltpu.make_async_remote_copy(src_ref=_part(srcs[a], (), axes[a], 2 * px + py, sizes[a]), dst_ref=lands[a].at[j],
                                        send_sem=send_sems.at[k], recv_sem=recv_sems.at[k],
                                        device_id=(px, py, c), device_id_type=MESH)


def _split_start(arrs, make_copy, ncopies, dep, *, name):
    n = len(arrs)

    def body(*refs):
        ins = refs[:n]
        send_sems, recv_sems = refs[n + 1], refs[n + 2]
        token = refs[n + 3 + n]
        for a in range(ncopies):
            for j in range(len(OTHER_CHIPS)):
                make_copy(ins, a, j, send_sems, recv_sems).start()
        token[...] = jnp.zeros_like(token)

    sem = pltpu.SemaphoreType.DMA((ncopies * len(OTHER_CHIPS),))
    res = pl.pallas_call(
        body, name=name,
        out_shape=(sem, sem, *[pltpu.HBM(a.shape, a.dtype) for a in arrs], jax.ShapeDtypeStruct((8, LANES), F32)),
        in_specs=[HBM] * n + [pl.BlockSpec(memory_space=pl.ANY)],
        out_specs=(SEM, SEM, *[HBM] * n, pl.BlockSpec(memory_space=pltpu.VMEM)),
        input_output_aliases={a: 2 + a for a in range(n)}, compiler_params=SPLIT_COPY,
    )(*[pltpu.with_memory_space_constraint(a, pltpu.HBM) for a in arrs], dep)
    return res[0], res[1], list(res[2:2 + n]), res[2 + n]


def _split_wait(arrs, send_sems, recv_sems, make_copy, ncopies, after, *, name):
    n = len(arrs)

    def body(*refs):
        ins = refs[:n]
        send, recv = refs[n], refs[n + 1]
        for a in range(ncopies):
            for j in range(len(OTHER_CHIPS)):
                cp = make_copy(ins, a, j, send, recv)
                cp.wait_send()
                cp.wait_recv()

    res = pl.pallas_call(
        body, name=name, out_shape=tuple(pltpu.HBM(a.shape, a.dtype) for a in arrs),
        in_specs=[HBM] * n + [SEM, SEM, pl.BlockSpec(memory_space=pl.ANY)], out_specs=tuple([HBM] * n),
        input_output_aliases={a: a for a in range(n)}, compiler_params=SPLIT_COPY,
    )(*arrs, send_sems, recv_sems, after)
    return list(res)


def _sum_own_and_slots(g, land, axis, chip, *, name):
    slots, rows, cols = land.shape
    tr = _row_tile(rows, cols * 4)
    nblk = rows // tr

    def body(chip_ref, g_ref, l_ref, o_ref):
        acc = g_ref[...].astype(F32)
        for j in range(slots):
            acc = acc + l_ref[j].astype(F32)
        o_ref[...] = acc

    if axis == 1:
        g_spec = pl.BlockSpec((tr, cols), lambda i, chip_ref: (i, chip_ref[0]))
    else:
        g_spec = pl.BlockSpec((tr, cols), lambda i, chip_ref: (chip_ref[0] * nblk + i, 0))
    return pl.pallas_call(
        body, name=name,
        grid_spec=pltpu.PrefetchScalarGridSpec(
            num_scalar_prefetch=1, grid=(nblk,),
            in_specs=[g_spec, pl.BlockSpec((slots, tr, cols), lambda i, chip_ref: (0, i, 0))],
            out_specs=pl.BlockSpec((tr, cols), lambda i, chip_ref: (i, 0))),
        out_shape=jax.ShapeDtypeStruct((rows, cols), F32), compiler_params=_cparams("parallel"))(chip, g, land)


def _swap_sibling(arrs, *, name):
    n = len(arrs)

    def body(*refs):
        ins, outs = refs[:n], refs[n:2 * n]
        send_sems, recv_sems = refs[2 * n:]
        x, y, c = _place()
        copies = [pltpu.make_async_remote_copy(src_ref=ins[a], dst_ref=outs[a], send_sem=send_sems.at[a], recv_sem=recv_sems.at[a],
                                               device_id=(x, y, 1 - c), device_id_type=MESH) for a in range(n)]
        for cp in copies:
            cp.start()
        for cp in copies:
            cp.wait()

    return pl.pallas_call(
        body, name=name, in_specs=[ANY] * n, out_specs=[ANY] * n,
        out_shape=[jax.ShapeDtypeStruct(a.shape, a.dtype) for a in arrs],
        scratch_shapes=[pltpu.SemaphoreType.DMA((n,)), pltpu.SemaphoreType.DMA((n,))],
    )(*arrs)


def _allreduce_small(p, *, name):
    rows, cols = p.shape
    nrel = len(OTHER_CHIPS)

    def body(p_ref, o_ref, sib_ref, land_ref, send_sems, recv_sems):
        x, y, c = _place()
        me = 2 * x + y
        pair = pltpu.make_async_remote_copy(src_ref=p_ref, dst_ref=sib_ref, send_sem=send_sems.at[nrel], recv_sem=recv_sems.at[nrel],
                                            device_id=(x, y, 1 - c), device_id_type=MESH)
        pair.start()
        pair.wait()
        land_ref[nrel] = p_ref[...] + sib_ref[...]
        copies = []
        for j, (fx, fy) in enumerate(OTHER_CHIPS):
            copies.append(pltpu.make_async_remote_copy(src_ref=land_ref.at[nrel], dst_ref=land_ref.at[j], send_sem=send_sems.at[j],
                                                       recv_sem=recv_sems.at[j], device_id=(_flip(x, fx), _flip(y, fy), c),
                                                       device_id_type=MESH))
        for cp in copies:
            cp.start()
        for cp in copies:
            cp.wait()

        def slot_of(chip):
            r = jnp.bitwise_xor(chip, me)
            return jnp.where(r == 0, nrel, jnp.where(r == 2, 0, jnp.where(r == 1, 1, 2)))

        acc = land_ref[slot_of(0)]
        for chip in range(1, N_CHIPS):
            acc = acc + land_ref[slot_of(chip)]
        o_ref[...] = acc

    vm = pl.BlockSpec(memory_space=pltpu.VMEM)
    return pl.pallas_call(
        body, name=name, in_specs=[vm], out_specs=vm, out_shape=jax.ShapeDtypeStruct((rows, cols), F32),
        scratch_shapes=[pltpu.VMEM((rows, cols), F32), pltpu.VMEM((nrel + 1, rows, cols), F32),
                        pltpu.SemaphoreType.DMA((nrel + 1,)), pltpu.SemaphoreType.DMA((nrel + 1,))],
        compiler_params=pltpu.CompilerParams(vmem_limit_bytes=V7X_VMEM_LIMIT_BYTES))(p)


WEIGHTS = ("mix_norm_g", "w_in", "conv_dw_w", "conv_dw_b", "conv_ln_g", "conv_ln_b", "conv_pw_w", "lru_conv_w", "lru_conv_b",
           "lru_wa", "lru_ba", "lru_wx", "lru_bx", "lru_lambda", "out_norm_conv", "out_norm_attn", "out_norm_lru", "w_out",
           "xattn_norm_g", "mem_norm_g", "xattn_wq", "xattn_wkv", "xattn_wo", "final_norm_g")
BIG = {"w_in": 2, "conv_pw_w": 1, "w_out": 1, "xattn_wq": 1, "xattn_wkv": 1, "xattn_wo": 2}
SMALL_SHARDED = {"conv_dw_w": 2, "lru_conv_w": 2}


IN_GROUP = ("w_in", "conv_pw_w")
REST_GROUP = ("w_out", "xattn_wq", "xattn_wkv", "xattn_wo")


def _trunk(x, mem, target, p, fetch, grads_ready):
    depth = p["mix_norm_g"].shape[0]
    c = p["conv_dw_w"].shape[2]
    aw = p["out_norm_attn"].shape[1]
    heads = aw // HEAD_DIM
    saved = []
    for l in range(depth):
        t = f"l{l}_"
        h1, r1 = _rmsnorm_fwd(x, p["mix_norm_g"][l], name=t + "mix_norm")
        wl = dict(fetch(IN_GROUP, l, r1))
        ua = _matmul(h1, wl["w_in"], mode="nn", n=3 * c, b_off=0, name=t + "in_conv")
        qkv = _matmul(h1, wl["w_in"], mode="nn", n=3 * aw, b_off=3 * c, out_dtype=BF16, name=t + "in_qkv")
        ub = _matmul(h1, wl["w_in"], mode="nn", n=aw + 2 * c, b_off=3 * c + 3 * aw, name=t + "in_gates")
        y_conv = _conv_fwd(ua, p["conv_dw_w"][l], p["conv_dw_b"][l], p["conv_ln_g"][l], p["conv_ln_b"][l], wl["conv_pw_w"],
                           name=t + "conv_fwd")
        y_attn, sbw = _sb_fwd(qkv, heads, name=t + "sb_fwd")
        y_lru = _lru_fwd(ub, aw // c, p["lru_conv_w"][l], p["lru_conv_b"][l], p["lru_wa"][l], p["lru_ba"][l], p["lru_wx"][l],
                         p["lru_bx"][l], p["lru_lambda"][l], name=t + "lru_fwd")
        y = _mix_out_fwd(y_conv, y_attn, y_lru, ua, ub, p["out_norm_conv"][l], p["out_norm_attn"][l], p["out_norm_lru"][l],
                         name=t + "mix_out_fwd")
        wl.update(fetch(REST_GROUP, l, y))
        x2 = _matmul(y, wl["w_out"], mode="nn", add=x, name=t + "out_proj")
        h2, r2 = _rmsnorm_fwd(x2, p["xattn_norm_g"][l], name=t + "xattn_norm")
        qx = _matmul(h2, wl["xattn_wq"], mode="nn", out_dtype=BF16, name=t + "xattn_q")
        memn, rm = _rmsnorm_fwd(mem, p["mem_norm_g"][l], name=t + "mem_norm")
        kv = _matmul(memn, wl["xattn_wkv"], mode="nn", out_dtype=BF16, name=t + "xattn_kv")
        o = _xattn_fwd(qx, kv, name=t + "xattn_fwd")
        x3 = _matmul(o, wl["xattn_wo"], mode="nn", add=x2, name=t + "xattn_o")
        saved.append(dict(x=x, h1=h1, r1=r1, ua=ua, qkv=qkv, ub=ub, y_conv=y_conv, y_attn=y_attn, sbw=sbw, y_lru=y_lru, y=y,
                          x2=x2, h2=h2, r2=r2, qx=qx, memn=memn, rm=rm, kv=kv, o=o, w=wl))
        x = x3

    loss, dx, dg_final = _final_loss(x, p["final_norm_g"], target, name="final_loss")
    small = {k: [None] * depth for k in WEIGHTS if k not in BIG and k != "final_norm_g"}
    token = None
    for l in reversed(range(depth)):
        t = f"l{l}_"
        s = saved[l]
        wl = s["w"]
        do = _matmul(dx, wl["xattn_wo"], mode="nt", out_dtype=BF16, dep=token, name=t + "d_xattn_o")
        dwo = _matmul(s["o"], dx, mode="tn", out_dtype=BF16, name=t + "dw_xattn_o")
        dqx, dkv = _xattn_bwd(s["qx"], s["kv"], do, name=t + "xattn_bwd")
        dwq = _matmul(s["h2"], dqx, mode="tn", out_dtype=BF16, name=t + "dw_xattn_q")
        dh2 = _matmul(dqx, wl["xattn_wq"], mode="nt", out_dtype=BF16, name=t + "d_xattn_q")
        dx2, dg = _rmsnorm_bwd(dh2, s["x2"], s["r2"], p["xattn_norm_g"][l], dx, name=t + "xattn_norm_bwd")
        small["xattn_norm_g"][l] = dg[0]
        dmemn = _matmul(dkv, wl["xattn_wkv"], mode="nt", name=t + "d_xattn_kv")
        dwkv = _matmul(s["memn"], dkv, mode="tn", out_dtype=BF16, name=t + "dw_xattn_kv")
        _, dg = _rmsnorm_bwd(dmemn, mem, s["rm"], p["mem_norm_g"][l], None, name=t + "mem_norm_bwd")
        small["mem_norm_g"][l] = dg[0]
        dwout = _matmul(s["y"], dx2, mode="tn", out_dtype=BF16, name=t + "dw_out_proj")
        token = grads_ready(REST_GROUP, l, dict(w_out=dwout, xattn_wq=dwq, xattn_wkv=dwkv, xattn_wo=dwo))
        dy = _matmul(dx2, wl["w_out"], mode="nt", dep=token, out_dtype=BF16, name=t + "d_out_proj")
        dyc, dya, dyl, du, dnc, dna, dnl = _mix_out_bwd(
            dy, s["y_conv"], s["y_attn"], s["y_lru"], s["ua"], s["ub"], p["out_norm_conv"][l], p["out_norm_attn"][l],
            p["out_norm_lru"][l], name=t + "mix_out_bwd")
        small["out_norm_conv"][l], small["out_norm_attn"][l], small["out_norm_lru"][l] = dnc[0], dna[0], dnl[0]
        dd, dpw, dlg, dlb = _conv_bwd_a(s["ua"], dyc, p["conv_dw_w"][l], p["conv_dw_b"][l], p["conv_ln_g"][l], p["conv_ln_b"][l],
                                        wl["conv_pw_w"], name=t + "conv_bwd_a")
        du, ddw, ddb = _conv_bwd_b(s["ua"], dd, p["conv_dw_w"][l], du, name=t + "conv_bwd_b")
        small["conv_ln_g"][l], small["conv_ln_b"][l], small["conv_dw_w"][l], small["conv_dw_b"][l] = dlg[0], dlb[0], ddw, ddb[0]
        du = _sb_bwd(s["qkv"], dya, s["sbw"], du, 3 * c, heads, name=t + "sb_bwd")
        du, dcw, dcb, dwa, dba, dwx, dbx, dlam = _lru_bwd(
            s["ub"], aw // c, s["y_lru"], dyl, p["lru_conv_w"][l], p["lru_conv_b"][l], p["lru_wa"][l], p["lru_ba"][l],
            p["lru_wx"][l], p["lru_bx"][l], p["lru_lambda"][l], du, (3 * c + 4 * aw) // c, name=t + "lru_bwd")
        small["lru_conv_w"][l], small["lru_conv_b"][l], small["lru_wa"][l], small["lru_ba"][l] = dcw, dcb[0], dwa, dba[0]
        small["lru_wx"][l], small["lru_bx"][l], small["lru_lambda"][l] = dwx, dbx[0], dlam[0]
        dwin = _matmul(s["h1"], du, mode="tn", out_dtype=BF16, tk=4096, name=t + "dw_in")
        token = grads_ready(IN_GROUP, l, dict(w_in=dwin, conv_pw_w=_cast_bf16(dpw, name=t + "cast_dpw")))
        dh1 = _matmul(du, wl["w_in"], mode="nt", dep=token, tk=3328, out_dtype=BF16, name=t + "d_in")
        dx, dg = _rmsnorm_bwd(dh1, s["x"], s["r1"], p["mix_norm_g"][l], dx2, name=t + "mix_norm_bwd")
        small["mix_norm_g"][l] = dg[0]
    small = {k: jnp.stack(v) for k, v in small.items()}
    small["final_norm_g"] = dg_final[0]
    return loss, dx, small


def _pack(arrs):
    flat = jnp.concatenate([a.reshape(-1) for a in arrs])
    pad = (-flat.shape[0]) % (PACK_ROWS * LANES)
    return jnp.pad(flat, (0, pad)).reshape(-1, LANES)


def _unpack(packed, like):
    flat = packed.reshape(-1)
    out, at = [], 0
    for a in like:
        out.append(flat[at:at + a.size].reshape(a.shape))
        at += a.size
    return out


def _as_rows(a):
    return a.reshape(-1, a.shape[-1])


def kernel(x, mem, mix_norm_g, w_in, conv_dw_w, conv_dw_b, conv_ln_g, conv_ln_b, conv_pw_w, lru_conv_w, lru_conv_b, lru_wa, lru_ba, lru_wx, lru_bx, lru_lambda, out_norm_conv, out_norm_attn, out_norm_lru, w_out, xattn_norm_g, mem_norm_g, xattn_wq, xattn_wkv, xattn_wo, final_norm_g, loss_target, m_mix_norm_g, m_w_in, m_conv_dw_w, m_conv_dw_b, m_conv_ln_g, m_conv_ln_b, m_conv_pw_w, m_lru_conv_w, m_lru_conv_b, m_lru_wa, m_lru_ba, m_lru_wx, m_lru_bx, m_lru_lambda, m_out_norm_conv, m_out_norm_attn, m_out_norm_lru, m_w_out, m_xattn_norm_g, m_mem_norm_g, m_xattn_wq, m_xattn_wkv, m_xattn_wo, m_final_norm_g, v_mix_norm_g, v_w_in, v_conv_dw_w, v_conv_dw_b, v_conv_ln_g, v_conv_ln_b, v_conv_pw_w, v_lru_conv_w, v_lru_conv_b, v_lru_wa, v_lru_ba, v_lru_wx, v_lru_bx, v_lru_lambda, v_out_norm_conv, v_out_norm_attn, v_out_norm_lru, v_w_out, v_xattn_norm_g, v_mem_norm_g, v_xattn_wq, v_xattn_wkv, v_xattn_wo, v_final_norm_g):
    given = dict(locals())
    w = {k: given[k] for k in WEIGHTS}
    m = {k: given["m_" + k] for k in WEIGHTS}
    v = {k: given["v_" + k] for k in WEIGHTS}
    depth = mix_norm_g.shape[0]
    chip = 2 * lax.axis_index("x") + lax.axis_index("y")

    chip_arr = chip.astype(jnp.int32).reshape(1)

    p = dict(w)
    p.update(zip(SMALL_SHARDED, _allgather_chips([w[k] for k in SMALL_SHARDED], list(SMALL_SHARDED.values()), name="gather_small")))
    axis2d = {k: BIG[k] - 1 for k in BIG}
    groups = [(IN_GROUP, 0), (REST_GROUP, 0)] + [(IN_GROUP + REST_GROUP, l) for l in range(1, depth)]
    pending, token = {}, p[next(iter(SMALL_SHARDED))]
    for names, l in groups:
        arrs = [_cast_place(w[k], l, axis2d[k], chip_arr, token, name=f"place{l}_{k}") for k in names]
        axes = [axis2d[k] for k in names]
        sizes = [a.shape[ax] // N_CHIPS for a, ax in zip(arrs, axes)]
        halves = (names, l) == groups[0]
        start = functools.partial(_gather_copy, axes=axes, sizes=sizes, arriving=False, halves=halves)
        land = functools.partial(_gather_copy, axes=axes, sizes=sizes, arriving=True, halves=halves)
        send, recv, arrs, token = _split_start(arrs, start, len(arrs), token, name=f"gather_start{l}_{names[0]}")
        pending[(names[0], l)] = (names, arrs, send, recv, land, axes if halves else None)
    last_token = token
    have = {}

    def fetch(group, l, after):
        key = (group[0], l)
        if key in pending:
            names, arrs, send, recv, land, swap_axes = pending.pop(key)
            after = last_token if (group, l) == groups[0] else after
            arrs = _split_wait(arrs, send, recv, land, len(arrs), after, name=f"gather_wait{l}_{names[0]}")
            if swap_axes is not None:
                arrs = _swap_halves(arrs, swap_axes, name=f"gather_swap{l}_{names[0]}")
            have.update({(k, l): a for k, a in zip(names, arrs)})
        return {k: have[(k, l)] for k in group}

    flying = []
    held = {}

    def grads_ready(group, l, grads):
        held.update({(k, l): g for k, g in grads.items()})
        if l > 0 and group == REST_GROUP:
            return None
        names = [k for k in (IN_GROUP + REST_GROUP if l > 0 else group)]
        srcs = [held[(k, l)] for k in names]
        axes = [axis2d[k] for k in names]
        sizes = [g.shape[ax] // N_CHIPS for g, ax in zip(srcs, axes)]
        lands = [lax.empty((len(OTHER_CHIPS),) + tuple(sz if i == ax else d for i, d in enumerate(g.shape)), g.dtype)
                 for g, ax, sz in zip(srcs, axes, sizes)]
        n = len(names)
        copy = lambda refs, a, j, ss, rs_: _scatter_copy(refs[:n], refs[n:], a, j, axes, sizes, ss, rs_)
        send, recv, arrs, token = _split_start(srcs + lands, copy, n, jnp.zeros((8, LANES), F32), name=f"scatter_start{l}_{names[0]}")
        flying.append((names, l, axes, arrs, send, recv, copy))
        return token

    loss, grad_x, small = _trunk(x[0], mem[0], loss_target[0], p, fetch, grads_ready)

    sums = {}
    out = {}

    def arrive(entry, after):
        names, l, axes, arrs, send, recv, copy = entry
        n = len(names)
        arrs = _split_wait(arrs, send, recv, copy, n, after, name=f"scatter_wait{l}_{names[0]}")
        for k, ax, g, ld in zip(names, axes, arrs[:n], arrs[n:]):
            ld = ld.reshape((len(OTHER_CHIPS), -1, ld.shape[-1]))
            sums[(k, l)] = _sum_own_and_slots(g, ld, ax, chip_arr, name=f"sum{l}_{k}")
        return sums[(names[-1], l)]

    def update(names):
        mine = [sums[(k, l)] for k in names for l in range(depth)]
        theirs = _swap_sibling(mine, name="swap_sums_" + names[0])
        for i, k in enumerate(names):
            gs = [[mine[i * depth + l], theirs[i * depth + l]] for l in range(depth)]
            out[k] = _adamw_layers(w[k], m[k], v[k], gs, name="adamw_" + k)

    after = grad_x
    for entry in flying[:-1]:
        after = arrive(entry, after)
    update(REST_GROUP)

    small_names = [k for k in WEIGHTS if k not in BIG]
    parts = [small[k] for k in small_names] + [loss[0, :1]]
    total = _unpack(_allreduce_small(_pack(parts), name="allreduce_small"), parts)
    loss = total[-1][0]
    g_small = dict(zip(small_names, total[:-1]))
    for k, ax in SMALL_SHARDED.items():
        size = w[k].shape[ax]
        g_small[k] = lax.dynamic_slice_in_dim(g_small[k], chip * size, size, axis=ax)
    res = _adamw(_pack([w[k] for k in small_names]), _pack([m[k] for k in small_names]), _pack([v[k] for k in small_names]),
                 [_pack([g_small[k] for k in small_names])], name="adamw_small")
    last = res[0]
    res = [_unpack(r, [w[k] for k in small_names]) for r in res]
    for i, k in enumerate(small_names):
        out[k] = [r[i] for r in res]

    arrive(flying[-1], last)
    update(IN_GROUP)

    outs = [loss, grad_x[None]]
    for part in range(4):
        outs += [out[k][part] for k in WEIGHTS]
    return tuple(outs)
```

```python
import functools

import jax
import jax.numpy as jnp
from jax import lax
from jax.experimental import pallas as pl
from jax.experimental.pallas import tpu as pltpu

F32 = jnp.float32
BF16 = jnp.bfloat16
MESH = pl.DeviceIdType.MESH

V7X_VMEM_LIMIT_BYTES = 56 * 1024 * 1024
LANES = 128
HEAD_DIM = 128
LRU_C = 8.0
RMS_EPS = 1e-6
LN_EPS = 1e-5
CONV_HALO = 32
LRU_HALO = 8
ADAM_LR = 0.001
ADAM_B1 = 0.9
ADAM_B2 = 0.999
ADAM_EPS = 1e-08
ADAM_WD = 0.01
ADAM_STEP = 10


def _cparams(*sem):
    return pltpu.CompilerParams(dimension_semantics=sem, vmem_limit_bytes=V7X_VMEM_LIMIT_BYTES)


def _tile(n, pref):
    if n <= pref:
        return n
    for t in range(pref - pref % LANES, 0, -LANES):
        if n % t == 0:
            return t
    t = pref
    while n % t:
        t //= 2
    return t


def _dot(a, b, dims):
    return lax.dot_general(a, b, (dims, ((), ())), preferred_element_type=F32)


def _nn(a, b):
    return _dot(a, b, ((1,), (0,)))


def _nt(a, b):
    return _dot(a, b, ((1,), (1,)))


def _tn(a, b):
    return _dot(a, b, ((0,), (0,)))


def _sigmoid(x):
    return jax.nn.sigmoid(x)


def _silu_and_grad(x):
    s = _sigmoid(x)
    return x * s, s * (1.0 + x * (1.0 - s))


def _matmul(a, b, *, mode, name, layer=None, n=None, b_off=0, add=None, dep=None, out_dtype=F32, tm=1024, tn=1024, tk=2048):
    bshape = b.shape if layer is None else b.shape[1:]
    if mode == "nn":
        m, k = a.shape
        n = bshape[1] if n is None else n
    elif mode == "nt":
        m, k = a.shape
        n = bshape[0]
    else:
        k, m = a.shape
        n = bshape[1]
    tm, tk = _tile(m, tm), _tile(k, tk)
    tn = _tile(n, tn)
    while b_off % tn or n % tn:
        tn -= LANES
    nk = k // tk
    off = b_off // tn
    lead = () if layer is None else (None,)
    li = () if layer is None else (layer,)
    if mode == "nn":
        a_spec = pl.BlockSpec((tm, tk), lambda i, j, kk: (i, kk))
        b_spec = pl.BlockSpec(lead + (tk, tn), lambda i, j, kk: li + (kk, j + off))
        dot = _nn
    elif mode == "nt":
        a_spec = pl.BlockSpec((tm, tk), lambda i, j, kk: (i, kk))
        b_spec = pl.BlockSpec(lead + (tn, tk), lambda i, j, kk: li + (j, kk))
        dot = _nt
    else:
        a_spec = pl.BlockSpec((tk, tm), lambda i, j, kk: (kk, i))
        b_spec = pl.BlockSpec(lead + (tk, tn), lambda i, j, kk: li + (kk, j))
        dot = _tn
    o_spec = pl.BlockSpec((tm, tn), lambda i, j, kk: (i, j))
    has_add = add is not None

    def body(*refs):
        refs = refs[:-3] + refs[-2:] if dep is not None else refs
        if has_add:
            a_ref, b_ref, add_ref, o_ref, acc_ref = refs
        else:
            a_ref, b_ref, o_ref, acc_ref = refs
        kk = pl.program_id(2)
        part = dot(a_ref[...].astype(BF16), b_ref[...].astype(BF16))

        @pl.when(kk == 0)
        def _():
            acc_ref[...] = part

        @pl.when(kk > 0)
        def _():
            acc_ref[...] += part

        @pl.when(kk == nk - 1)
        def _():
            r = acc_ref[...]
            if has_add:
                r = r + add_ref[...]
            o_ref[...] = r.astype(o_ref.dtype)

    ins = [a, b] + ([add] if has_add else [])
    specs = [a_spec, b_spec] + ([o_spec] if has_add else [])
    if dep is not None:
        ins.append(dep)
        specs.append(pl.BlockSpec((8, LANES), lambda i, j, kk: (0, 0)))
    return pl.pallas_call(
        body, name=name, grid=(m // tm, n // tn, nk), in_specs=specs, out_specs=o_spec,
        out_shape=jax.ShapeDtypeStruct((m, n), out_dtype), scratch_shapes=[pltpu.VMEM((tm, tn), F32)],
        compiler_params=_cparams("parallel", "parallel", "arbitrary"))(*ins)


def _rmsnorm_fwd(x, g, *, name):
    s, d = x.shape
    tm = _tile(s, 256)

    def body(x_ref, g_ref, h_ref, r_ref):
        xf = x_ref[...]
        r = lax.rsqrt(jnp.mean(xf * xf, axis=-1, keepdims=True) + RMS_EPS)
        h_ref[...] = ((xf * r) * g_ref[...]).astype(h_ref.dtype)
        r_ref[...] = r

    return pl.pallas_call(
        body, name=name, grid=(s // tm,),
        in_specs=[pl.BlockSpec((tm, d), lambda i: (i, 0)), pl.BlockSpec((1, d), lambda i: (0, 0))],
        out_specs=[pl.BlockSpec((tm, d), lambda i: (i, 0)), pl.BlockSpec((tm, 1), lambda i: (i, 0))],
        out_shape=[jax.ShapeDtypeStruct((s, d), BF16), jax.ShapeDtypeStruct((s, 1), F32)],
        compiler_params=_cparams("parallel"))(x, g.reshape(1, d))


def _rms_bwd_math(dh, x, r, g):
    xr = x * r
    dyg = dh * g
    m = jnp.mean(dyg * xr, axis=-1, keepdims=True)
    return r * (dyg - xr * m), dh * xr


def _rmsnorm_bwd(dh, x, r, g, dres, *, name):
    s, d = x.shape
    tm = _tile(s, 256)
    has_res = dres is not None

    def body(*refs):
        if has_res:
            dh_ref, x_ref, r_ref, g_ref, res_ref, dx_ref, dg_ref = refs
        else:
            dh_ref, x_ref, r_ref, g_ref, dx_ref, dg_ref = refs
        dx, dgp = _rms_bwd_math(dh_ref[...].astype(F32), x_ref[...], r_ref[...], g_ref[...])
        if has_res:
            dx = dx + res_ref[...]
        dx_ref[...] = dx

        @pl.when(pl.program_id(0) == 0)
        def _():
            dg_ref[...] = jnp.zeros_like(dg_ref)

        dg_ref[...] += jnp.sum(dgp, axis=0, keepdims=True)

    row = pl.BlockSpec((tm, d), lambda i: (i, 0))
    vec = pl.BlockSpec((1, d), lambda i: (0, 0))
    ins = [dh, x, r, g.reshape(1, d)] + ([dres] if has_res else [])
    specs = [row, row, pl.BlockSpec((tm, 1), lambda i: (i, 0)), vec] + ([row] if has_res else [])
    return pl.pallas_call(
        body, name=name, grid=(s // tm,), in_specs=specs, out_specs=[row, vec],
        out_shape=[jax.ShapeDtypeStruct((s, d), F32), jax.ShapeDtypeStruct((1, d), F32)],
        compiler_params=_cparams("arbitrary"))(*ins)


def _final_loss(x, g, target, *, name):
    s, d = x.shape
    tm = _tile(s, 256)

    def body(x_ref, g_ref, t_ref, loss_ref, dx_ref, dg_ref):
        xf = x_ref[...]
        gv = g_ref[...]
        r = lax.rsqrt(jnp.mean(xf * xf, axis=-1, keepdims=True) + RMS_EPS)
        diff = (xf * r) * gv - t_ref[...]
        part = 0.5 * jnp.sum(jnp.mean(diff * diff, axis=-1, keepdims=True))
        dx, dgp = _rms_bwd_math(diff * (1.0 / d), xf, r, gv)
        dx_ref[...] = dx

        @pl.when(pl.program_id(0) == 0)
        def _():
            dg_ref[...] = jnp.zeros_like(dg_ref)
            loss_ref[...] = jnp.zeros_like(loss_ref)

        dg_ref[...] += jnp.sum(dgp, axis=0, keepdims=True)
        loss_ref[...] += part

    row = pl.BlockSpec((tm, d), lambda i: (i, 0))
    vec = pl.BlockSpec((1, d), lambda i: (0, 0))
    return pl.pallas_call(
        body, name=name, grid=(s // tm,), in_specs=[row, vec, row],
        out_specs=[pl.BlockSpec((8, LANES), lambda i: (0, 0)), row, vec],
        out_shape=[jax.ShapeDtypeStruct((8, LANES), F32), jax.ShapeDtypeStruct((s, d), F32), jax.ShapeDtypeStruct((1, d), F32)],
        compiler_params=_cparams("arbitrary"))(x, g.reshape(1, d), target)


SUBLANES = 8
TAP_GROUPS = 4


def _shift_scratch(tm, c):
    return pltpu.VMEM((SUBLANES - 1, tm + CONV_HALO - SUBLANES, c), F32)


def _shift_copies(src_ref, sh_ref):
    rows = sh_ref.shape[1]
    for r in range(1, SUBLANES):
        sh_ref[r - 1] = src_ref[pl.ds(r, rows), :]


def _read_shifted(src_ref, sh_ref, off, r0):
    r = off % SUBLANES
    base = off - r + r0
    return src_ref[pl.ds(base, SUBLANES), :] if r == 0 else sh_ref[r - 1, pl.ds(base, SUBLANES), :]


def _tap_rows(w):
    return [jnp.broadcast_to(w[k:k + 1, :], (SUBLANES, w.shape[1])) for k in range(w.shape[0])]


def _tap_sum(src_ref, sh_ref, wk, offs, init, tm):
    out = []
    for r0 in range(0, tm, SUBLANES * TAP_GROUPS):
        accs = [init] * TAP_GROUPS
        for wv, off in zip(wk, offs):
            accs = [acc + wv * _read_shifted(src_ref, sh_ref, off, r0 + SUBLANES * g) for g, acc in enumerate(accs)]
        out += accs
    return jnp.concatenate(out, axis=0)


def _conv_taps(gp_ref, sh_ref, w, bias, taps, tm):
    halo = gp_ref.shape[0] - tm
    _shift_copies(gp_ref, sh_ref)
    offs = [halo - (taps - 1) + k for k in range(taps)]
    return _tap_sum(gp_ref, sh_ref, _tap_rows(w), offs, jnp.broadcast_to(bias, (SUBLANES, w.shape[1])), tm)


def _conv_core(val, glu, valh, gluh, first, gp_ref, sh_ref, w, bias, lg, lb, taps, tm):
    sg = _sigmoid(glu)
    g = val * sg
    gh = jnp.where(first, 0.0, valh * _sigmoid(gluh))
    gp_ref[0:CONV_HALO, :] = gh
    gp_ref[CONV_HALO:, :] = g
    d = _conv_taps(gp_ref, sh_ref, w, bias, taps, tm)
    mu = jnp.mean(d, axis=-1, keepdims=True)
    dc = d - mu
    rstd = lax.rsqrt(jnp.mean(dc * dc, axis=-1, keepdims=True) + LN_EPS)
    xhat = dc * rstd
    ln = xhat * lg + lb
    return sg, xhat, rstd, ln


def _conv_fwd(ua, dw_w, dw_b, ln_g, ln_b, pw, *, name):
    s = ua.shape[0]
    taps, c = dw_w.shape
    tm = _tile(s, 512)
    hb = tm // CONV_HALO

    def body(val_ref, glu_ref, valh_ref, gluh_ref, w_ref, b_ref, lg_ref, lb_ref, pw_ref, y_ref, gp_ref, sh_ref):
        first = pl.program_id(0) == 0
        _, _, _, ln = _conv_core(val_ref[...], glu_ref[...], valh_ref[...], gluh_ref[...], first, gp_ref, sh_ref,
                                 w_ref[...], b_ref[...], lg_ref[...], lb_ref[...], taps, tm)
        sw = ln * _sigmoid(ln)
        y_ref[...] = _nn(sw.astype(BF16), pw_ref[...])

    cur = lambda col: pl.BlockSpec((tm, c), lambda i: (i, col))
    prev = lambda col: pl.BlockSpec((CONV_HALO, c), lambda i: (jnp.maximum(i * hb - 1, 0), col))
    full = lambda a: pl.BlockSpec(a.shape, lambda i: (0,) * a.ndim)
    vecs = [dw_w, dw_b.reshape(1, c), ln_g.reshape(1, c), ln_b.reshape(1, c), pw]
    return pl.pallas_call(
        body, name=name, grid=(s // tm,),
        in_specs=[cur(0), cur(1), prev(0), prev(1)] + [full(a) for a in vecs],
        out_specs=pl.BlockSpec((tm, c), lambda i: (i, 0)),
        out_shape=jax.ShapeDtypeStruct((s, c), F32),
        scratch_shapes=[pltpu.VMEM((tm + CONV_HALO, c), F32), _shift_scratch(tm, c)],
        compiler_params=_cparams("parallel"))(ua, ua, ua, ua, *vecs)


def _conv_bwd_a(ua, dy, dw_w, dw_b, ln_g, ln_b, pw, *, name):
    s = ua.shape[0]
    taps, c = dw_w.shape
    tm = _tile(s, 512)
    hb = tm // CONV_HALO

    def body(val_ref, glu_ref, valh_ref, gluh_ref, dy_ref, w_ref, b_ref, lg_ref, lb_ref, pw_ref,
             dd_ref, dpw_ref, dlg_ref, dlb_ref, gp_ref, sh_ref):
        first = pl.program_id(0) == 0
        lg = lg_ref[...]
        _, xhat, rstd, ln = _conv_core(val_ref[...], glu_ref[...], valh_ref[...], gluh_ref[...], first, gp_ref, sh_ref,
                                       w_ref[...], b_ref[...], lg, lb_ref[...], taps, tm)
        sw, dsw = _silu_and_grad(ln)
        dyb = dy_ref[...].astype(BF16)
        ds = _nt(dyb, pw_ref[...])
        dln = ds * dsw
        dxhat = dln * lg
        m1 = jnp.mean(dxhat, axis=-1, keepdims=True)
        m2 = jnp.mean(dxhat * xhat, axis=-1, keepdims=True)
        dd_ref[...] = rstd * (dxhat - m1 - xhat * m2)

        @pl.when(first)
        def _():
            dpw_ref[...] = jnp.zeros_like(dpw_ref)
            dlg_ref[...] = jnp.zeros_like(dlg_ref)
            dlb_ref[...] = jnp.zeros_like(dlb_ref)

        dpw_ref[...] += _tn(sw.astype(BF16), dyb)
        dlg_ref[...] += jnp.sum(dln * xhat, axis=0, keepdims=True)
        dlb_ref[...] += jnp.sum(dln, axis=0, keepdims=True)

    cur = lambda col: pl.BlockSpec((tm, c), lambda i: (i, col))
    prev = lambda col: pl.BlockSpec((CONV_HALO, c), lambda i: (jnp.maximum(i * hb - 1, 0), col))
    full = lambda a: pl.BlockSpec(a.shape, lambda i: (0,) * a.ndim)
    vec = pl.BlockSpec((1, c), lambda i: (0, 0))
    vecs = [dw_w, dw_b.reshape(1, c), ln_g.reshape(1, c), ln_b.reshape(1, c), pw]
    return pl.pallas_call(
        body, name=name, grid=(s // tm,),
        in_specs=[cur(0), cur(1), prev(0), prev(1), pl.BlockSpec((tm, c), lambda i: (i, 0))] + [full(a) for a in vecs],
        out_specs=[pl.BlockSpec((tm, c), lambda i: (i, 0)), pl.BlockSpec((c, c), lambda i: (0, 0)), vec, vec],
        out_shape=[jax.ShapeDtypeStruct((s, c), F32), jax.ShapeDtypeStruct((c, c), F32),
                   jax.ShapeDtypeStruct((1, c), F32), jax.ShapeDtypeStruct((1, c), F32)],
        scratch_shapes=[pltpu.VMEM((tm + CONV_HALO, c), F32), _shift_scratch(tm, c)],
        compiler_params=_cparams("arbitrary"))(ua, ua, ua, ua, dy, *vecs)


def _conv_bwd_b(ua, dd, dw_w, du, *, name):
    s = ua.shape[0]
    taps, c = dw_w.shape
    tm = _tile(s, 512)
    hb = tm // CONV_HALO
    nt = s // tm

    def body(val_ref, glu_ref, valh_ref, gluh_ref, dd_ref, ddn_ref, w_ref, du_in, du_ref, dw_ref, db_ref,
             gp_ref, ddp_ref, shg_ref, shd_ref):
        i = pl.program_id(0)
        val = val_ref[...]
        sg = _sigmoid(glu_ref[...])
        gp_ref[0:CONV_HALO, :] = jnp.where(i == 0, 0.0, valh_ref[...] * _sigmoid(gluh_ref[...]))
        gp_ref[CONV_HALO:, :] = val * sg
        dd = dd_ref[...]
        ddp_ref[0:tm, :] = dd
        ddp_ref[tm:, :] = jnp.where(i == nt - 1, 0.0, ddn_ref[...])
        _shift_copies(gp_ref, shg_ref)
        _shift_copies(ddp_ref, shd_ref)
        zero = jnp.zeros((SUBLANES, c), F32)
        dg = _tap_sum(ddp_ref, shd_ref, _tap_rows(w_ref[...]), [taps - 1 - k for k in range(taps)], zero, tm)
        dws = []
        for k in range(taps):
            accs = [zero] * TAP_GROUPS
            for n, r0 in enumerate(range(0, tm, SUBLANES)):
                accs[n % TAP_GROUPS] = accs[n % TAP_GROUPS] + ddp_ref[pl.ds(r0, SUBLANES), :] * _read_shifted(
                    gp_ref, shg_ref, CONV_HALO - (taps - 1) + k, r0)
            dws.append(jnp.sum(sum(accs[1:], accs[0]), axis=0, keepdims=True))
        du_ref[:, 0:c] = (dg * sg).astype(du_ref.dtype)
        du_ref[:, c:] = (dg * val * sg * (1.0 - sg)).astype(du_ref.dtype)

        @pl.when(i == 0)
        def _():
            dw_ref[...] = jnp.zeros_like(dw_ref)
            db_ref[...] = jnp.zeros_like(db_ref)

        dw_ref[...] += jnp.concatenate(dws, axis=0)
        db_ref[...] += jnp.sum(dd, axis=0, keepdims=True)

    cur = lambda col: pl.BlockSpec((tm, c), lambda i: (i, col))
    prev = lambda col: pl.BlockSpec((CONV_HALO, c), lambda i: (jnp.maximum(i * hb - 1, 0), col))
    nxt = pl.BlockSpec((CONV_HALO, c), lambda i: (jnp.minimum((i + 1) * hb, s // CONV_HALO - 1), 0))
    return pl.pallas_call(
        body, name=name, grid=(nt,),
        in_specs=[cur(0), cur(1), prev(0), prev(1), pl.BlockSpec((tm, c), lambda i: (i, 0)), nxt,
                  pl.BlockSpec((taps, c), lambda i: (0, 0)), ANY],
        out_specs=[pl.BlockSpec((tm, 2 * c), lambda i: (i, 0)),
                   pl.BlockSpec((taps, c), lambda i: (0, 0)), pl.BlockSpec((1, c), lambda i: (0, 0))],
        out_shape=[jax.ShapeDtypeStruct(du.shape, du.dtype), jax.ShapeDtypeStruct((taps, c), F32), jax.ShapeDtypeStruct((1, c), F32)],
        input_output_aliases={7: 0},
        scratch_shapes=[pltpu.VMEM((tm + CONV_HALO, c), F32), pltpu.VMEM((tm + CONV_HALO, c), F32),
                        _shift_scratch(tm, c), _shift_scratch(tm, c)],
        compiler_params=_cparams("arbitrary"))(ua, ua, ua, ua, dd, dd, dw_w, du)


LOG2_E = 1.4426950408889634
SB_HEADS_PER_STEP = 4


def _sb_logs(qk, mask):
    z = qk * (HEAD_DIM ** -0.5 * LOG2_E)
    ls = jnp.minimum(z, 0.0) - jnp.log2(1.0 + jnp.exp2(-jnp.abs(z)))
    lm = ls - z
    if mask is not None:
        lm = jnp.where(mask, lm, 0.0)
    return ls, lm


def _diag_mask(b):
    return lax.broadcasted_iota(jnp.int32, (b, b), 1) < lax.broadcasted_iota(jnp.int32, (b, b), 0)


def _split_dot(x, tri):
    hi = x.astype(BF16)
    lo = (x - hi.astype(F32)).astype(BF16)
    return _nn(hi, tri) + _nn(lo, tri)


def _tri(bk, cmp):
    r = lax.broadcasted_iota(jnp.int32, (bk, bk), 0)
    c = lax.broadcasted_iota(jnp.int32, (bk, bk), 1)
    return cmp(r, c).astype(BF16)


def _sb_fwd(qkv, heads, *, name, blk=256):
    s = qkv.shape[0]
    b = _tile(s, blk)
    nq = s // b
    hp = min(SB_HEADS_PER_STEP, heads)
    assert heads % hp == 0
    groups = heads // hp
    wide = hp * HEAD_DIM

    def body(q_ref, k_ref, v_ref, o_ref, w_hbm, stage, sems):
        g = pl.program_id(0)
        i = pl.program_id(1)
        sls = [slice(n * HEAD_DIM, (n + 1) * HEAD_DIM) for n in range(hp)]
        qs = [q_ref[:, sl] for sl in sls]
        tri = _tri(b, lambda r, c: r > c)
        diag = _diag_mask(b)
        r0 = pl.multiple_of(i * b, b)

        def saves(slot, j):
            c0 = pl.multiple_of(j * b, b)
            return [pltpu.make_async_copy(stage.at[slot, n, w], w_hbm.at[w, g * hp + n, pl.ds(r0, b), pl.ds(c0, b)], sems.at[slot])
                    for n in range(hp) for w in range(2)]

        def tile(t, j, carry, masked):
            slot = t % 2
            if not masked:
                @pl.when(t >= 2)
                def _():
                    for cp in saves(slot, j):
                        cp.wait()

            s0 = pl.multiple_of(j * b, b)
            kbs = [k_ref[pl.ds(s0, b), sl] for sl in sls]
            vbs = [v_ref[pl.ds(s0, b), sl] for sl in sls]
            zs = [_nt(q, kb) for q, kb in zip(qs, kbs)]
            sc = [_sb_logs(z, diag if masked else None) for z in zs]
            after = [_split_dot(lm, tri) for _, lm in sc]
            out = []
            for n, ((ls, lm), af, (acc, c)) in enumerate(zip(sc, after, carry)):
                a = jnp.exp2(ls + (af + c))
                if masked:
                    a = jnp.where(diag, a, 0.0)
                ab = a.astype(BF16)
                stage[slot, n, 0] = ab
                stage[slot, n, 1] = jnp.exp2(ls).astype(BF16)
                out.append((ab, acc, c + jnp.sum(lm, axis=1, keepdims=True)))
            for cp in saves(slot, j):
                cp.start()
            return tuple((acc + _nn(ab, vb), c) for vb, (ab, acc, c) in zip(vbs, out))

        zero = tuple((jnp.zeros((b, HEAD_DIM), F32), jnp.zeros((b, 1), F32)) for _ in range(hp))
        carry = tile(0, i, zero, True)
        carry = lax.fori_loop(0, i, lambda jj, cr: tile(jj + 1, i - 1 - jj, cr, False), carry)
        for sl, (acc, _) in zip(sls, carry):
            o_ref[:, sl] = acc
        for cp in saves(i % 2, 0):
            cp.wait()

        @pl.when(i >= 1)
        def _():
            for cp in saves((i + 1) % 2, 0):
                cp.wait()

    return pl.pallas_call(
        body, name=name, grid=(groups, nq),
        in_specs=[pl.BlockSpec((b, wide), lambda g, i: (i, g)),
                  pl.BlockSpec((s, wide), lambda g, i: (0, groups + g)),
                  pl.BlockSpec((s, wide), lambda g, i: (0, 2 * groups + g))],
        out_specs=[pl.BlockSpec((b, wide), lambda g, i: (i, g)), ANY],
        out_shape=[jax.ShapeDtypeStruct((s, heads * HEAD_DIM), F32), jax.ShapeDtypeStruct((2, heads, s, s), BF16)],
        scratch_shapes=[pltpu.VMEM((2, hp, 2, b, b), BF16), pltpu.SemaphoreType.DMA((2,))],
        compiler_params=_cparams("parallel", "arbitrary"))(qkv, qkv, qkv)


def _sb_bwd(qkv, do, saved, du, du_col, heads, *, name, blk=256):
    s = qkv.shape[0]
    b = _tile(s, blk)
    nq = s // b
    hp = min(SB_HEADS_PER_STEP, heads)
    assert heads % hp == 0
    groups = heads // hp
    wide = hp * HEAD_DIM
    scale = HEAD_DIM ** -0.5

    def body(q_ref, k_ref, v_ref, do_ref, w_hbm, du_in, du_ref, dk_acc, dv_acc, stage, sems, dq_out, dkv_out, out_sems):
        g = pl.program_id(0)
        i = pl.program_id(1)

        @pl.when(i == 0)
        def _():
            dk_acc[...] = jnp.zeros_like(dk_acc)
            dv_acc[...] = jnp.zeros_like(dv_acc)

        sls = [slice(n * HEAD_DIM, (n + 1) * HEAD_DIM) for n in range(hp)]
        qs = [q_ref[:, sl] for sl in sls]
        dos = [do_ref[:, sl].astype(BF16) for sl in sls]
        tri_excl = _tri(b, lambda r, c: r < c)
        diag = _diag_mask(b)
        r0 = pl.multiple_of(i * b, b)

        def loads(slot, j):
            c0 = pl.multiple_of(j * b, b)
            return [pltpu.make_async_copy(w_hbm.at[w, g * hp + n, pl.ds(r0, b), pl.ds(c0, b)], stage.at[slot, n, w], sems.at[slot])
                    for n in range(hp) for w in range(2)]

        def tile(j, carry, masked):
            slot = j % 2

            @pl.when(j < i)
            def _():
                for cp in loads(1 - slot, j + 1):
                    cp.start()

            for cp in loads(slot, j):
                cp.wait()
            s0 = pl.multiple_of(j * b, b)
            kbs = [k_ref[pl.ds(s0, b), sl] for sl in sls]
            vbs = [v_ref[pl.ds(s0, b), sl] for sl in sls]
            ab = [stage[slot, n, 0] for n in range(hp)]
            ps = [_nt(dob, vb) for dob, vb in zip(dos, vbs)]
            gs = [a.astype(F32) * p for a, p in zip(ab, ps)]
            hs = [_nn(gg.astype(BF16), tri_excl) for gg in gs]
            dzb = []
            for n, (gg, h, (_, cg)) in enumerate(zip(gs, hs, carry)):
                dz = (gg - (gg + (h + cg)) * stage[slot, n, 1].astype(F32)) * scale
                if masked:
                    dz = jnp.where(diag, dz, 0.0)
                dzb.append(dz.astype(BF16))
            out = tuple((dq + _nn(dz, kb), cg + jnp.sum(gg, axis=1, keepdims=True))
                        for dz, kb, gg, (dq, cg) in zip(dzb, kbs, gs, carry))
            for sl, dz, a, q, dob in zip(sls, dzb, ab, qs, dos):
                dk_acc[pl.ds(s0, b), sl] += _tn(dz, q)
                dv_acc[pl.ds(s0, b), sl] += _tn(a, dob)
            return out

        for cp in loads(0, 0):
            cp.start()
        carry = tuple((jnp.zeros((b, HEAD_DIM), F32), jnp.zeros((b, 1), F32)) for _ in range(hp))
        carry = lax.fori_loop(0, i, lambda j, cr: tile(j, cr, False), carry)
        carry = tile(i, carry, True)
        cols = [pl.multiple_of(du_col + n * heads * HEAD_DIM + g * wide, LANES) for n in range(3)]
        for sl, (dq, _) in zip(sls, carry):
            dq_out[:, sl] = dq.astype(dq_out.dtype)
        put = pltpu.make_async_copy(dq_out, du_ref.at[pl.ds(r0, b), pl.ds(cols[0], wide)], out_sems.at[0])
        put.start()
        put.wait()

        @pl.when(i == nq - 1)
        def _():
            dkv_out[0] = dk_acc[...].astype(dkv_out.dtype)
            dkv_out[1] = dv_acc[...].astype(dkv_out.dtype)
            puts = [pltpu.make_async_copy(dkv_out.at[n], du_ref.at[:, pl.ds(cols[1 + n], wide)], out_sems.at[1 + n]) for n in range(2)]
            for cp in puts:
                cp.start()
            for cp in puts:
                cp.wait()

    row = pl.BlockSpec((b, wide), lambda g, i: (i, g))
    col = lambda off: pl.BlockSpec((s, wide), lambda g, i: (0, off + g), pipeline_mode=pl.Buffered(1))
    return pl.pallas_call(
        body, name=name, grid=(groups, nq),
        in_specs=[row, col(groups), col(2 * groups), row, ANY, ANY],
        out_specs=ANY, out_shape=jax.ShapeDtypeStruct(du.shape, du.dtype), input_output_aliases={5: 0},
        scratch_shapes=[pltpu.VMEM((s, wide), F32), pltpu.VMEM((s, wide), F32),
                        pltpu.VMEM((2, hp, 2, b, b), BF16), pltpu.SemaphoreType.DMA((2,)),
                        pltpu.VMEM((b, wide), BF16), pltpu.VMEM((2, s, wide), BF16), pltpu.SemaphoreType.DMA((3,))],
        compiler_params=_cparams("arbitrary", "arbitrary"))(qkv, qkv, qkv, do, saved, du)


def _shift_rows(x, n, fill, *, down):
    rows = x.shape[0]
    if n % 8 == 0:
        pad = jnp.full((n, x.shape[1]), fill, x.dtype)
        return jnp.concatenate([pad, x[:rows - n]], axis=0) if down else jnp.concatenate([x[n:], pad], axis=0)
    t = lax.broadcasted_iota(jnp.int32, x.shape, 0)
    if down:
        return jnp.where(t >= n, pltpu.roll(x, n, 0), fill)
    return jnp.where(t < rows - n, pltpu.roll(x, rows - n, 0), fill)


def _scan_rows(a, b, *, reverse):
    n = 1
    while n < a.shape[0]:
        b = a * _shift_rows(b, n, 0.0, down=not reverse) + b
        a = a * _shift_rows(a, n, 1.0, down=not reverse)
        n *= 2
    return a, b


def _neg_expm1(x):
    p = 1.0 + x * (1.0 / 7.0)
    for k in (6.0, 5.0, 4.0, 3.0, 2.0):
        p = 1.0 + x * (1.0 / k) * p
    return jnp.where(x > -0.25, -(x * p), 1.0 - jnp.exp(x))


def _softplus_neg(lam):
    z = -lam
    e = jnp.exp(-jnp.abs(z))
    u = 1.0 + e
    d = u - 1.0
    log1p_e = jnp.where(d == 0.0, e, jnp.log(u) * (e / jnp.where(d == 0.0, 1.0, d)))
    return jnp.maximum(z, 0.0) + log1p_e


def _lru_gates(xp_ref, w, bias, wa_ref, ba, wx_ref, bx, sp, taps, tm, heads):
    halo = xp_ref.shape[0] - tm
    xc = jnp.broadcast_to(bias, (tm, w.shape[1]))
    for k in range(taps):
        xc = xc + w[k:k + 1, :] * xp_ref[pl.ds(halo - (taps - 1) + k, tm), :]
    xb = xc.astype(BF16)
    pr, pi = [], []
    for n in range(heads):
        xh = xb[:, n * HEAD_DIM:(n + 1) * HEAD_DIM]
        pr.append(_nn(xh, wa_ref[n]))
        pi.append(_nn(xh, wx_ref[n]))
    r = _sigmoid(jnp.concatenate(pr, axis=1) + ba)
    ig = _sigmoid(jnp.concatenate(pi, axis=1) + bx)
    log_a = (-LRU_C) * r * sp
    a = jnp.exp(log_a)
    mult = jnp.sqrt(_neg_expm1(2.0 * log_a))
    return xc, r, ig, a, mult


def _lru_fwd(ub, x_col, conv_w, conv_b, wa, ba, wx, bx, lam, *, name):
    s = ub.shape[0]
    taps, w = conv_w.shape
    heads = w // HEAD_DIM
    tm = _tile(s, 256)
    hb = tm // LRU_HALO

    def body(x_ref, xh_ref, cw_ref, cb_ref, wa_ref, ba_ref, wx_ref, bx_ref, lam_ref, h_ref, xp_ref, carry_ref):
        i = pl.program_id(0)

        @pl.when(i == 0)
        def _():
            carry_ref[...] = jnp.zeros_like(carry_ref)

        xp_ref[0:LRU_HALO, :] = jnp.where(i == 0, 0.0, xh_ref[...])
        xp_ref[LRU_HALO:, :] = x_ref[...]
        sp = _softplus_neg(lam_ref[...])
        xc, _, ig, a, mult = _lru_gates(xp_ref, cw_ref[...], cb_ref[...], wa_ref, ba_ref[...], wx_ref, bx_ref[...],
                                        sp, taps, tm, heads)
        ac, bc = _scan_rows(a, mult * (ig * xc), reverse=False)
        h = ac * carry_ref[0:1, :] + bc
        h_ref[...] = h
        carry_ref[...] = jnp.broadcast_to(h[tm - 1:tm, :], carry_ref.shape)

    full = lambda arr: pl.BlockSpec(arr.shape, lambda i: (0,) * arr.ndim)
    vecs = [conv_w, conv_b.reshape(1, w), wa.astype(BF16), ba.reshape(1, w), wx.astype(BF16), bx.reshape(1, w), lam.reshape(1, w)]
    return pl.pallas_call(
        body, name=name, grid=(s // tm,),
        in_specs=[pl.BlockSpec((tm, w), lambda i: (i, x_col)),
                  pl.BlockSpec((LRU_HALO, w), lambda i: (jnp.maximum(i * hb - 1, 0), x_col))] + [full(v) for v in vecs],
        out_specs=pl.BlockSpec((tm, w), lambda i: (i, 0)),
        out_shape=jax.ShapeDtypeStruct((s, w), F32),
        scratch_shapes=[pltpu.VMEM((tm + LRU_HALO, w), F32), pltpu.VMEM((8, w), F32)],
        compiler_params=_cparams("arbitrary"))(ub, ub, *vecs)


def _lru_bwd(ub, x_col, h, dh, conv_w, conv_b, wa, ba, wx, bx, lam, du, du_col, *, name):
    s = ub.shape[0]
    taps, w = conv_w.shape
    heads = w // HEAD_DIM
    tm = _tile(s, 256)
    hb = tm // LRU_HALO
    nt = s // tm

    def body(x_ref, xh_ref, h_ref, hh_ref, dh_ref, cw_ref, cb_ref, wa_ref, ba_ref, wx_ref, bx_ref, lam_ref, du_in,
             dx_ref, dcw_ref, dcb_ref, dwa_ref, dba_ref, dwx_ref, dbx_ref, dlam_ref,
             xp_ref, dxp_ref, dlt_ref, afirst_ref, dxc_next_ref, dsp_ref):
        step = pl.program_id(0)
        i = nt - 1 - step

        @pl.when(step == 0)
        def _():
            for ref in (dcw_ref, dcb_ref, dwa_ref, dba_ref, dwx_ref, dbx_ref, dlam_ref, dlt_ref, dxc_next_ref, dsp_ref):
                ref[...] = jnp.zeros_like(ref)
            afirst_ref[...] = jnp.ones_like(afirst_ref)

        xp_ref[0:LRU_HALO, :] = jnp.where(i == 0, 0.0, xh_ref[...])
        xp_ref[LRU_HALO:, :] = x_ref[...]
        cw = cw_ref[...]
        lam_v = lam_ref[...]
        sp = _softplus_neg(lam_v)
        xc, r, ig, a, mult = _lru_gates(xp_ref, cw, cb_ref[...], wa_ref, ba_ref[...], wx_ref, bx_ref[...], sp, taps, tm, heads)
        rows = lax.broadcasted_iota(jnp.int32, (tm, w), 0)
        a_next = jnp.where(rows == tm - 1, afirst_ref[0:1, :], _shift_rows(a, 1, 1.0, down=False))
        ac, bc = _scan_rows(a_next, dh_ref[...], reverse=True)
        delta = ac * dlt_ref[0:1, :] + bc
        hv = h_ref[...]
        h_last_prev = jnp.where(i == 0, 0.0, hh_ref[LRU_HALO - 1:LRU_HALO, :])
        h_prev = jnp.where(rows == 0, h_last_prev, _shift_rows(hv, 1, 0.0, down=True))
        gated = ig * xc
        da = delta * h_prev
        dmult = delta * gated
        dgated = delta * mult
        dlog_a = da * a - dmult * (a * a) / mult
        dpr = dlog_a * ((-LRU_C) * sp) * r * (1.0 - r)
        dpi = dgated * xc * ig * (1.0 - ig)
        dxc = dgated * ig
        dsp_ref[...] += jnp.sum(dlog_a * ((-LRU_C) * r), axis=0, keepdims=True)
        dba_ref[...] += jnp.sum(dpr, axis=0, keepdims=True)
        dbx_ref[...] += jnp.sum(dpi, axis=0, keepdims=True)
        xb = xc.astype(BF16)
        dprb = dpr.astype(BF16)
        dpib = dpi.astype(BF16)
        back = []
        for n in range(heads):
            sl = slice(n * HEAD_DIM, (n + 1) * HEAD_DIM)
            dwa_ref[n] += _tn(xb[:, sl], dprb[:, sl])
            dwx_ref[n] += _tn(xb[:, sl], dpib[:, sl])
            back.append(_nt(dprb[:, sl], wa_ref[n]) + _nt(dpib[:, sl], wx_ref[n]))
        dxc = dxc + jnp.concatenate(back, axis=1)
        dxp_ref[0:tm, :] = dxc
        dxp_ref[tm:, :] = dxc_next_ref[...]
        dx = jnp.zeros((tm, w), F32)
        dws = []
        for k in range(taps):
            dx = dx + cw[k:k + 1, :] * dxp_ref[pl.ds(taps - 1 - k, tm), :]
            dws.append(jnp.sum(dxc * xp_ref[pl.ds(LRU_HALO - (taps - 1) + k, tm), :], axis=0, keepdims=True))
        dx_ref[...] = dx.astype(dx_ref.dtype)
        dcw_ref[...] += jnp.concatenate(dws, axis=0)
        dcb_ref[...] += jnp.sum(dxc, axis=0, keepdims=True)
        dlt_ref[...] = jnp.broadcast_to(delta[0:1, :], dlt_ref.shape)
        afirst_ref[...] = jnp.broadcast_to(a[0:1, :], afirst_ref.shape)
        dxc_next_ref[...] = dxc[0:LRU_HALO, :]

        @pl.when(step == nt - 1)
        def _():
            dlam_ref[...] = dsp_ref[...] * (-_sigmoid(-lam_v))

    rev = lambda col: pl.BlockSpec((tm, w), lambda st: (nt - 1 - st, col))
    prev = lambda col: pl.BlockSpec((LRU_HALO, w), lambda st: (jnp.maximum((nt - 1 - st) * hb - 1, 0), col))
    full = lambda arr: pl.BlockSpec(arr.shape, lambda st: (0,) * arr.ndim)
    vec = pl.BlockSpec((1, w), lambda st: (0, 0))
    vecs = [conv_w, conv_b.reshape(1, w), wa.astype(BF16), ba.reshape(1, w), wx.astype(BF16), bx.reshape(1, w), lam.reshape(1, w)]
    vshape = jax.ShapeDtypeStruct((1, w), F32)
    return pl.pallas_call(
        body, name=name, grid=(nt,),
        in_specs=[rev(x_col), prev(x_col), rev(0), prev(0), rev(0)] + [full(v) for v in vecs] + [ANY],
        out_specs=[rev(du_col), full(conv_w), vec, full(wa), vec, full(wx), vec, vec],
        out_shape=[jax.ShapeDtypeStruct(du.shape, du.dtype), jax.ShapeDtypeStruct(conv_w.shape, F32), vshape,
                   jax.ShapeDtypeStruct(wa.shape, F32), vshape, jax.ShapeDtypeStruct(wx.shape, F32), vshape, vshape],
        input_output_aliases={5 + len(vecs): 0},
        scratch_shapes=[pltpu.VMEM((tm + LRU_HALO, w), F32), pltpu.VMEM((tm + LRU_HALO, w), F32),
                        pltpu.VMEM((8, w), F32), pltpu.VMEM((8, w), F32), pltpu.VMEM((LRU_HALO, w), F32), pltpu.VMEM((1, w), F32)],
        compiler_params=_cparams("arbitrary"))(ub, ub, h, h, dh, *vecs, du)


def _group_fwd(y, w, gate):
    r = lax.rsqrt(jnp.mean(y * y, axis=-1, keepdims=True) + RMS_EPS)
    return ((y * r) * w) * (gate * _sigmoid(gate))


def _mix_out_fwd(y_conv, y_attn, y_lru, ua, ub, n_conv, n_attn, n_lru, *, name):
    s, c = y_conv.shape
    wa_ = y_attn.shape[1]
    d = 2 * c + wa_
    tm = _tile(s, 256)
    assert wa_ == 2 * c

    def body(yc_ref, ya_ref, yl_ref, gc_ref, ga_ref, gl_ref, nc_ref, na_ref, nl_ref, o_ref):
        o_ref[:, 0:c] = _group_fwd(yc_ref[...], nc_ref[...], gc_ref[...]).astype(o_ref.dtype)
        o_ref[:, c:c + wa_] = _group_fwd(ya_ref[...], na_ref[...], ga_ref[...]).astype(o_ref.dtype)
        o_ref[:, c + wa_:] = _group_fwd(yl_ref[...], nl_ref[...], gl_ref[...]).astype(o_ref.dtype)

    blk = lambda width, col: pl.BlockSpec((tm, width), lambda i: (i, col))
    vec = lambda width: pl.BlockSpec((1, width), lambda i: (0, 0))
    return pl.pallas_call(
        body, name=name, grid=(s // tm,),
        in_specs=[blk(c, 0), blk(wa_, 0), blk(c, 0), blk(c, 2), blk(wa_, 0), blk(c, 3), vec(c), vec(wa_), vec(c)],
        out_specs=blk(d, 0), out_shape=jax.ShapeDtypeStruct((s, d), BF16),
        compiler_params=_cparams("parallel"))(y_conv, y_attn, y_lru, ua, ub, ub, n_conv.reshape(1, c), n_attn.reshape(1, wa_), n_lru.reshape(1, c))


def _group_bwd(dout, y, w, gate):
    r = lax.rsqrt(jnp.mean(y * y, axis=-1, keepdims=True) + RMS_EPS)
    silu, dsilu = _silu_and_grad(gate)
    dy, dwp = _rms_bwd_math(dout * silu, y, r, w)
    return dy, dout * ((y * r) * w) * dsilu, dwp


def _mix_out_bwd(dy, y_conv, y_attn, y_lru, ua, ub, n_conv, n_attn, n_lru, *, name):
    s, c = y_conv.shape
    wa_ = y_attn.shape[1]
    tm = _tile(s, 256)

    du_width = 5 * c + 4 * wa_
    gate_cols = (2 * c, 3 * c + 3 * wa_, 4 * c + 4 * wa_)

    def body(dy_ref, yc_ref, ya_ref, yl_ref, gc_ref, ga_ref, gl_ref, nc_ref, na_ref, nl_ref,
             dyc_ref, dya_ref, dyl_ref, du_ref, dnc_ref, dna_ref, dnl_ref):
        @pl.when(pl.program_id(0) == 0)
        def _():
            for ref in (dnc_ref, dna_ref, dnl_ref):
                ref[...] = jnp.zeros_like(ref)

        groups = ((dy_ref[:, 0:c], yc_ref, nc_ref, gc_ref, dyc_ref, dnc_ref),
                  (dy_ref[:, c:c + wa_], ya_ref, na_ref, ga_ref, dya_ref, dna_ref),
                  (dy_ref[:, c + wa_:], yl_ref, nl_ref, gl_ref, dyl_ref, dnl_ref))
        for col, (dout, y_ref, n_ref, g_ref, dyo_ref, dn_ref) in zip(gate_cols, groups):
            dyv, dgv, dwp = _group_bwd(dout.astype(F32), y_ref[...], n_ref[...], g_ref[...])
            dyo_ref[...] = dyv.astype(dyo_ref.dtype)
            du_ref[:, col:col + dgv.shape[1]] = dgv.astype(du_ref.dtype)
            dn_ref[...] += jnp.sum(dwp, axis=0, keepdims=True)

    blk = lambda width, col: pl.BlockSpec((tm, width), lambda i: (i, col))
    vec = lambda width: pl.BlockSpec((1, width), lambda i: (0, 0))
    sh = lambda width, dt: jax.ShapeDtypeStruct((s, width), dt)
    vs = lambda width: jax.ShapeDtypeStruct((1, width), F32)
    return pl.pallas_call(
        body, name=name, grid=(s // tm,),
        in_specs=[blk(2 * c + wa_, 0), blk(c, 0), blk(wa_, 0), blk(c, 0), blk(c, 2), blk(wa_, 0), blk(c, 3), vec(c), vec(wa_), vec(c)],
        out_specs=[blk(c, 0), blk(wa_, 0), blk(c, 0), blk(du_width, 0), vec(c), vec(wa_), vec(c)],
        out_shape=[sh(c, BF16), sh(wa_, BF16), sh(c, F32), sh(du_width, BF16), vs(c), vs(wa_), vs(c)],
        compiler_params=_cparams("arbitrary"))(dy, y_conv, y_attn, y_lru, ua, ub, ub, n_conv.reshape(1, c), n_attn.reshape(1, wa_), n_lru.reshape(1, c))


def _xattn_probs(qh, kh):
    sc = _nt(qh, kh) * (HEAD_DIM ** -0.5)
    e = jnp.exp(sc - jnp.max(sc, axis=-1, keepdims=True))
    return e / jnp.sum(e, axis=-1, keepdims=True)


def _xattn_fwd(q, kv, *, name):
    s, w = q.shape
    heads = w // HEAD_DIM
    tm = _tile(s, 512)

    def body(q_ref, kv_ref, o_ref):
        for n in range(heads):
            sl = slice(n * HEAD_DIM, (n + 1) * HEAD_DIM)
            p = _xattn_probs(q_ref[:, sl], kv_ref[:, sl])
            o_ref[:, sl] = _nn(p.astype(BF16), kv_ref[:, w + n * HEAD_DIM:w + (n + 1) * HEAD_DIM]).astype(o_ref.dtype)

    return pl.pallas_call(
        body, name=name, grid=(s // tm,),
        in_specs=[pl.BlockSpec((tm, w), lambda i: (i, 0)), pl.BlockSpec(kv.shape, lambda i: (0, 0))],
        out_specs=pl.BlockSpec((tm, w), lambda i: (i, 0)), out_shape=jax.ShapeDtypeStruct((s, w), BF16),
        compiler_params=_cparams("parallel"))(q, kv)


def _xattn_bwd(q, kv, do, *, name):
    s, w = q.shape
    heads = w // HEAD_DIM
    tm = _tile(s, 512)
    scale = HEAD_DIM ** -0.5

    def body(q_ref, kv_ref, do_ref, dq_ref, dkv_ref):
        @pl.when(pl.program_id(0) == 0)
        def _():
            dkv_ref[...] = jnp.zeros_like(dkv_ref)

        for n in range(heads):
            sl = slice(n * HEAD_DIM, (n + 1) * HEAD_DIM)
            vsl = slice(w + n * HEAD_DIM, w + (n + 1) * HEAD_DIM)
            qh, kh, vh, doh = q_ref[:, sl], kv_ref[:, sl], kv_ref[:, vsl], do_ref[:, sl]
            p = _xattn_probs(qh, kh)
            dp = _nt(doh, vh)
            ds = (p * (dp - jnp.sum(dp * p, axis=-1, keepdims=True)) * scale).astype(BF16)
            dq_ref[:, sl] = _nn(ds, kh).astype(dq_ref.dtype)
            dkv_ref[:, sl] += _tn(ds, qh)
            dkv_ref[:, vsl] += _tn(p.astype(BF16), doh)

    row = pl.BlockSpec((tm, w), lambda i: (i, 0))
    kvs = pl.BlockSpec(kv.shape, lambda i: (0, 0))
    return pl.pallas_call(
        body, name=name, grid=(s // tm,), in_specs=[row, kvs, row], out_specs=[row, kvs],
        out_shape=[jax.ShapeDtypeStruct((s, w), BF16), jax.ShapeDtypeStruct(kv.shape, F32)],
        compiler_params=_cparams("arbitrary"))(q, kv, do)


ROW_BLOCK_BYTES = 4 << 20
PACK_ROWS = 512


def _row_tile(rows, cols):
    limit = max(8, ROW_BLOCK_BYTES // (4 * cols))
    t = 8
    while t * 2 <= limit and rows % (t * 2) == 0:
        t *= 2
    assert rows % t == 0
    return t


def _cast_bf16(w, *, name):
    rows, cols = w.shape
    tr = _row_tile(rows, cols)

    def body(w_ref, o_ref):
        o_ref[...] = w_ref[...].astype(BF16)

    blk = pl.BlockSpec((tr, cols), lambda i: (i, 0))
    return pl.pallas_call(body, name=name, grid=(rows // tr,), in_specs=[blk], out_specs=blk,
                          out_shape=jax.ShapeDtypeStruct((rows, cols), BF16), compiler_params=_cparams("parallel"))(w)


def _adamw(w, m, v, gs, *, name):
    rows, cols = w.shape
    tr = _row_tile(rows, cols * 4)
    ng = len(gs)

    def body(*refs):
        w_ref, m_ref, v_ref = refs[:3]
        g_refs = refs[3:3 + ng]
        g_out, d_out, m_out, v_out = refs[3 + ng:]
        g = g_refs[0][...]
        for r in g_refs[1:]:
            g = g + r[...]
        mn = ADAM_B1 * m_ref[...] + (1.0 - ADAM_B1) * g
        vn = ADAM_B2 * v_ref[...] + (1.0 - ADAM_B2) * (g * g)
        m_hat = mn / (1.0 - ADAM_B1 ** ADAM_STEP)
        v_hat = vn / (1.0 - ADAM_B2 ** ADAM_STEP)
        g_out[...] = g
        d_out[...] = -ADAM_LR * (m_hat / (jnp.sqrt(v_hat) + ADAM_EPS) + ADAM_WD * w_ref[...])
        m_out[...] = mn
        v_out[...] = vn

    blk = pl.BlockSpec((tr, cols), lambda i: (i, 0))
    shp = jax.ShapeDtypeStruct((rows, cols), F32)
    return pl.pallas_call(body, name=name, grid=(rows // tr,), in_specs=[blk] * (3 + ng), out_specs=[blk] * 4,
                          out_shape=[shp] * 4, compiler_params=_cparams("parallel"))(w, m, v, *gs)


def _adamw_layers(w, m, v, gs, *, name):
    layers, rows, cols = w.shape
    tr = _row_tile(rows, cols * 4)
    nblk = rows // tr
    ng = len(gs[0])

    def body(*refs):
        w_ref, m_ref, v_ref = refs[:3]
        g_refs = refs[3:3 + layers * ng]
        g_out, d_out, m_out, v_out = refs[3 + layers * ng:]
        l = pl.program_id(0)
        g = jnp.zeros((tr, cols), F32)
        for ll in range(layers):
            gl = g_refs[ll * ng][...].astype(F32)
            for r in g_refs[ll * ng + 1:(ll + 1) * ng]:
                gl = gl + r[...].astype(F32)
            g = jnp.where(l == ll, gl, g)
        mn = ADAM_B1 * m_ref[...] + (1.0 - ADAM_B1) * g
        vn = ADAM_B2 * v_ref[...] + (1.0 - ADAM_B2) * (g * g)
        m_hat = mn / (1.0 - ADAM_B1 ** ADAM_STEP)
        v_hat = vn / (1.0 - ADAM_B2 ** ADAM_STEP)
        g_out[...] = g
        d_out[...] = -ADAM_LR * (m_hat / (jnp.sqrt(v_hat) + ADAM_EPS) + ADAM_WD * w_ref[...])
        m_out[...] = mn
        v_out[...] = vn

    blk = pl.BlockSpec((None, tr, cols), lambda l, i: (l, i, 0))

    def g_spec(ll):
        return pl.BlockSpec((tr, cols), lambda l, i: (jnp.where(l == ll, i, jnp.where(l < ll, 0, nblk - 1)), 0))

    shp = jax.ShapeDtypeStruct(w.shape, F32)
    return pl.pallas_call(
        body, name=name, grid=(layers, nblk), in_specs=[blk] * 3 + [g_spec(ll) for ll in range(layers) for _ in range(ng)],
        out_specs=[blk] * 4, out_shape=[shp] * 4, compiler_params=_cparams("arbitrary", "arbitrary"),
    )(w, m, v, *[g for gl in gs for g in gl])


OTHER_CHIPS = ((1, 0), (0, 1), (1, 1))
N_CHIPS = 4
ANY = pl.BlockSpec(memory_space=pl.ANY)


def _place():
    return lax.axis_index("x"), lax.axis_index("y"), lax.axis_index("c")


def _flip(v, f):
    return 1 - v if f else v


def _part(ref, lead, axis, chip, size):
    idx = list(lead) + [slice(None)] * (len(ref.shape) - len(lead))
    idx[len(lead) + axis] = pl.ds(pl.multiple_of(chip * size, size), size)
    return ref.at[tuple(idx)]


def _allgather_chips(shards, axes, *, name):
    n = len(shards)
    sizes = [sh.shape[ax] for sh, ax in zip(shards, axes)]

    def full_shape(sh, ax):
        return tuple(d * N_CHIPS if i == ax else d for i, d in enumerate(sh.shape))

    def body(*refs):
        ins, outs = refs[:n], refs[n:2 * n]
        send_sems, recv_sems, loc_sems = refs[2 * n:]
        x, y, c = _place()
        me = 2 * x + y
        local = []
        for a in range(n):
            cp = pltpu.make_async_copy(ins[a], _part(outs[a], (), axes[a], me, sizes[a]), loc_sems.at[a])
            cp.start()
            local.append(cp)

        def remote(a, j, chip):
            fx, fy = OTHER_CHIPS[j]
            return pltpu.make_async_remote_copy(
                src_ref=ins[a], dst_ref=_part(outs[a], (), axes[a], chip, sizes[a]),
                send_sem=send_sems.at[a, j], recv_sem=recv_sems.at[a, j],
                device_id=(_flip(x, fx), _flip(y, fy), c), device_id_type=MESH)

        for a in range(n):
            for j in range(len(OTHER_CHIPS)):
                remote(a, j, me).start()
        for a in range(n):
            for j, (fx, fy) in enumerate(OTHER_CHIPS):
                remote(a, j, 2 * _flip(x, fx) + _flip(y, fy)).wait()
        for cp in local:
            cp.wait()

    return pl.pallas_call(
        body, name=name, in_specs=[ANY] * n, out_specs=[ANY] * n,
        out_shape=[jax.ShapeDtypeStruct(full_shape(sh, ax), sh.dtype) for sh, ax in zip(shards, axes)],
        scratch_shapes=[pltpu.SemaphoreType.DMA((n, 3)), pltpu.SemaphoreType.DMA((n, 3)), pltpu.SemaphoreType.DMA((n,))],
    )(*shards)


HBM = pl.BlockSpec(memory_space=pltpu.HBM)
SEM = pl.BlockSpec(memory_space=pltpu.SEMAPHORE)
SPLIT_COPY = pltpu.CompilerParams(has_side_effects=pltpu.SideEffectType.DATAFLOW_SIDE_EFFECTING)


def _cast_place(w, layer, axis, chip, dep, *, name):
    _, rows, cols = w.shape
    tr = _row_tile(rows, cols)
    nblk = rows // tr

    def body(chip_ref, w_ref, dep_ref, o_ref):
        o_ref[...] = w_ref[...].astype(BF16)

    if axis == 1:
        shape = (rows, cols * N_CHIPS)
        o_spec = pl.BlockSpec((tr, cols), lambda i, chip_ref: (i, chip_ref[0]))
    else:
        shape = (rows * N_CHIPS, cols)
        o_spec = pl.BlockSpec((tr, cols), lambda i, chip_ref: (chip_ref[0] * nblk + i, 0))
    return pl.pallas_call(
        body, name=name,
        grid_spec=pltpu.PrefetchScalarGridSpec(
            num_scalar_prefetch=1, grid=(nblk,),
            in_specs=[pl.BlockSpec((None, tr, cols), lambda i, chip_ref: (layer, i, 0)), ANY], out_specs=o_spec),
        out_shape=jax.ShapeDtypeStruct(shape, BF16), compiler_params=_cparams("parallel"))(chip, w, dep)


def _half_rows(ref, axis, chip, size, half):
    rows = ref.shape[0] // (N_CHIPS if axis == 0 else 1)
    r0 = chip * rows if axis == 0 else 0
    if half is not None:
        rows //= 2
        r0 = r0 + half * rows
    rsl = pl.ds(r0 if isinstance(r0, int) else pl.multiple_of(r0, 16), rows)
    return ref.at[rsl, :] if axis == 0 else ref.at[rsl, pl.ds(pl.multiple_of(chip * size, LANES), size)]


def _gather_copy(refs, a, j, send_sems, recv_sems, *, axes, sizes, arriving, halves=False):
    x, y, c = _place()
    fx, fy = OTHER_CHIPS[j]
    px, py = _flip(x, fx), _flip(y, fy)
    part = _half_rows(refs[a], axes[a], (2 * px + py) if arriving else (2 * x + y), sizes[a], c if halves else None)
    k = a * len(OTHER_CHIPS) + j
    return pltpu.make_async_remote_copy(src_ref=part, dst_ref=part, send_sem=send_sems.at[k], recv_sem=recv_sems.at[k],
                                        device_id=(px, py, c), device_id_type=MESH)


def _swap_halves(arrs, axes, *, name):
    n = len(arrs)
    sizes = [a.shape[ax] // N_CHIPS for a, ax in zip(arrs, axes)]

    def body(*refs):
        ins = refs[:n]
        send_sems, recv_sems = refs[2 * n:]
        x, y, c = _place()

        def copy(a, j, half):
            fx, fy = OTHER_CHIPS[j]
            part = _half_rows(ins[a], axes[a], 2 * _flip(x, fx) + _flip(y, fy), sizes[a], half)
            k = a * len(OTHER_CHIPS) + j
            return pltpu.make_async_remote_copy(src_ref=part, dst_ref=part, send_sem=send_sems.at[k], recv_sem=recv_sems.at[k],
                                                device_id=(x, y, 1 - c), device_id_type=MESH)

        todo = [(a, j) for a in range(n) for j in range(len(OTHER_CHIPS))]
        for a, j in todo:
            copy(a, j, c).start()
        for a, j in todo:
            copy(a, j, c).wait_send()
            copy(a, j, 1 - c).wait_recv()

    res = pl.pallas_call(
        body, name=name, in_specs=[ANY] * n, out_specs=[ANY] * n, out_shape=[jax.ShapeDtypeStruct(a.shape, a.dtype) for a in arrs],
        input_output_aliases={a: a for a in range(n)},
        scratch_shapes=[pltpu.SemaphoreType.DMA((n * len(OTHER_CHIPS),)), pltpu.SemaphoreType.DMA((n * len(OTHER_CHIPS),))],
    )(*arrs)
    return list(res)


def _scatter_copy(srcs, lands, a, j, axes, sizes, send_sems, recv_sems):
    x, y, c = _place()
    fx, fy = OTHER_CHIPS[j]
    px, py = _flip(x, fx), _flip(y, fy)
    k = a * len(OTHER_CHIPS) + j
    return pltpu.make_async_remote_copy(src_ref=_part(srcs[a], (), axes[a], 2 * px + py, sizes[a]), dst_ref=lands[a].at[j],
                                        send_sem=send_sems.at[k], recv_sem=recv_sems.at[k],
                                        device_id=(px, py, c), device_id_type=MESH)


def _split_start(arrs, make_copy, ncopies, dep, *, name):
    n = len(arrs)

    def body(*refs):
        ins = refs[:n]
        send_sems, recv_sems = refs[n + 1], refs[n + 2]
        token = refs[n + 3 + n]
        for a in range(ncopies):
            for j in range(len(OTHER_CHIPS)):
                make_copy(ins, a, j, send_sems, recv_sems).start()
        token[...] = jnp.zeros_like(token)

    sem = pltpu.SemaphoreType.DMA((ncopies * len(OTHER_CHIPS),))
    res = pl.pallas_call(
        body, name=name,
        out_shape=(sem, sem, *[pltpu.HBM(a.shape, a.dtype) for a in arrs], jax.ShapeDtypeStruct((8, LANES), F32)),
        in_specs=[HBM] * n + [pl.BlockSpec(memory_space=pl.ANY)],
        out_specs=(SEM, SEM, *[HBM] * n, pl.BlockSpec(memory_space=pltpu.VMEM)),
        input_output_aliases={a: 2 + a for a in range(n)}, compiler_params=SPLIT_COPY,
    )(*[pltpu.with_memory_space_constraint(a, pltpu.HBM) for a in arrs], dep)
    return res[0], res[1], list(res[2:2 + n]), res[2 + n]


def _split_wait(arrs, send_sems, recv_sems, make_copy, ncopies, after, *, name):
    n = len(arrs)

    def body(*refs):
        ins = refs[:n]
        send, recv = refs[n], refs[n + 1]
        for a in range(ncopies):
            for j in range(len(OTHER_CHIPS)):
                cp = make_copy(ins, a, j, send, recv)
                cp.wait_send()
                cp.wait_recv()

    res = pl.pallas_call(
        body, name=name, out_shape=tuple(pltpu.HBM(a.shape, a.dtype) for a in arrs),
        in_specs=[HBM] * n + [SEM, SEM, pl.BlockSpec(memory_space=pl.ANY)], out_specs=tuple([HBM] * n),
        input_output_aliases={a: a for a in range(n)}, compiler_params=SPLIT_COPY,
    )(*arrs, send_sems, recv_sems, after)
    return list(res)


def _sum_own_and_slots(g, land, axis, chip, *, name):
    slots, rows, cols = land.shape
    tr = _row_tile(rows, cols * 4)
    nblk = rows // tr

    def body(chip_ref, g_ref, l_ref, o_ref):
        acc = g_ref[...].astype(F32)
        for j in range(slots):
            acc = acc + l_ref[j].astype(F32)
        o_ref[...] = acc.astype(o_ref.dtype)

    if axis == 1:
        g_spec = pl.BlockSpec((tr, cols), lambda i, chip_ref: (i, chip_ref[0]))
    else:
        g_spec = pl.BlockSpec((tr, cols), lambda i, chip_ref: (chip_ref[0] * nblk + i, 0))
    return pl.pallas_call(
        body, name=name,
        grid_spec=pltpu.PrefetchScalarGridSpec(
            num_scalar_prefetch=1, grid=(nblk,),
            in_specs=[g_spec, pl.BlockSpec((slots, tr, cols), lambda i, chip_ref: (0, i, 0))],
            out_specs=pl.BlockSpec((tr, cols), lambda i, chip_ref: (i, 0))),
        out_shape=jax.ShapeDtypeStruct((rows, cols), BF16), compiler_params=_cparams("parallel"))(chip, g, land)


def _swap_sibling(arrs, *, name):
    n = len(arrs)

    def body(*refs):
        ins, outs = refs[:n], refs[n:2 * n]
        send_sems, recv_sems = refs[2 * n:]
        x, y, c = _place()
        copies = [pltpu.make_async_remote_copy(src_ref=ins[a], dst_ref=outs[a], send_sem=send_sems.at[a], recv_sem=recv_sems.at[a],
                                               device_id=(x, y, 1 - c), device_id_type=MESH) for a in range(n)]
        for cp in copies:
            cp.start()
        for cp in copies:
            cp.wait()

    return pl.pallas_call(
        body, name=name, in_specs=[ANY] * n, out_specs=[ANY] * n,
        out_shape=[jax.ShapeDtypeStruct(a.shape, a.dtype) for a in arrs],
        scratch_shapes=[pltpu.SemaphoreType.DMA((n,)), pltpu.SemaphoreType.DMA((n,))],
    )(*arrs)


def _allreduce_small(p, *, name):
    rows, cols = p.shape
    nrel = len(OTHER_CHIPS)

    def body(p_ref, o_ref, sib_ref, land_ref, send_sems, recv_sems):
        x, y, c = _place()
        me = 2 * x + y
        pair = pltpu.make_async_remote_copy(src_ref=p_ref, dst_ref=sib_ref, send_sem=send_sems.at[nrel], recv_sem=recv_sems.at[nrel],
                                            device_id=(x, y, 1 - c), device_id_type=MESH)
        pair.start()
        pair.wait()
        land_ref[nrel] = p_ref[...] + sib_ref[...]
        copies = []
        for j, (fx, fy) in enumerate(OTHER_CHIPS):
            copies.append(pltpu.make_async_remote_copy(src_ref=land_ref.at[nrel], dst_ref=land_ref.at[j], send_sem=send_sems.at[j],
                                                       recv_sem=recv_sems.at[j], device_id=(_flip(x, fx), _flip(y, fy), c),
                                                       device_id_type=MESH))
        for cp in copies:
            cp.start()
        for cp in copies:
            cp.wait()

        def slot_of(chip):
            r = jnp.bitwise_xor(chip, me)
            return jnp.where(r == 0, nrel, jnp.where(r == 2, 0, jnp.where(r == 1, 1, 2)))

        acc = land_ref[slot_of(0)]
        for chip in range(1, N_CHIPS):
            acc = acc + land_ref[slot_of(chip)]
        o_ref[...] = acc

    vm = pl.BlockSpec(memory_space=pltpu.VMEM)
    return pl.pallas_call(
        body, name=name, in_specs=[vm], out_specs=vm, out_shape=jax.ShapeDtypeStruct((rows, cols), F32),
        scratch_shapes=[pltpu.VMEM((rows, cols), F32), pltpu.VMEM((nrel + 1, rows, cols), F32),
                        pltpu.SemaphoreType.DMA((nrel + 1,)), pltpu.SemaphoreType.DMA((nrel + 1,))],
        compiler_params=pltpu.CompilerParams(vmem_limit_bytes=V7X_VMEM_LIMIT_BYTES))(p)


WEIGHTS = ("mix_norm_g", "w_in", "conv_dw_w", "conv_dw_b", "conv_ln_g", "conv_ln_b", "conv_pw_w", "lru_conv_w", "lru_conv_b",
           "lru_wa", "lru_ba", "lru_wx", "lru_bx", "lru_lambda", "out_norm_conv", "out_norm_attn", "out_norm_lru", "w_out",
           "xattn_norm_g", "mem_norm_g", "xattn_wq", "xattn_wkv", "xattn_wo", "final_norm_g")
BIG = {"w_in": 2, "conv_pw_w": 1, "w_out": 1, "xattn_wq": 1, "xattn_wkv": 1, "xattn_wo": 2}
SMALL_SHARDED = {"conv_dw_w": 2, "lru_conv_w": 2}


IN_GROUP = ("w_in", "conv_pw_w")
REST_GROUP = ("w_out", "xattn_wq", "xattn_wkv", "xattn_wo")


def _trunk(x, mem, target, p, fetch, grads_ready):
    depth = p["mix_norm_g"].shape[0]
    c = p["conv_dw_w"].shape[2]
    aw = p["out_norm_attn"].shape[1]
    heads = aw // HEAD_DIM
    saved = []
    for l in range(depth):
        t = f"l{l}_"
        h1, r1 = _rmsnorm_fwd(x, p["mix_norm_g"][l], name=t + "mix_norm")
        wl = dict(fetch(IN_GROUP, l, r1))
        ua = _matmul(h1, wl["w_in"], mode="nn", n=3 * c, b_off=0, name=t + "in_conv")
        qkv = _matmul(h1, wl["w_in"], mode="nn", n=3 * aw, b_off=3 * c, out_dtype=BF16, name=t + "in_qkv")
        ub = _matmul(h1, wl["w_in"], mode="nn", n=aw + 2 * c, b_off=3 * c + 3 * aw, name=t + "in_gates")
        y_conv = _conv_fwd(ua, p["conv_dw_w"][l], p["conv_dw_b"][l], p["conv_ln_g"][l], p["conv_ln_b"][l], wl["conv_pw_w"],
                           name=t + "conv_fwd")
        y_attn, sbw = _sb_fwd(qkv, heads, name=t + "sb_fwd")
        y_lru = _lru_fwd(ub, aw // c, p["lru_conv_w"][l], p["lru_conv_b"][l], p["lru_wa"][l], p["lru_ba"][l], p["lru_wx"][l],
                         p["lru_bx"][l], p["lru_lambda"][l], name=t + "lru_fwd")
        y = _mix_out_fwd(y_conv, y_attn, y_lru, ua, ub, p["out_norm_conv"][l], p["out_norm_attn"][l], p["out_norm_lru"][l],
                         name=t + "mix_out_fwd")
        wl.update(fetch(REST_GROUP, l, y))
        x2 = _matmul(y, wl["w_out"], mode="nn", add=x, name=t + "out_proj")
        h2, r2 = _rmsnorm_fwd(x2, p["xattn_norm_g"][l], name=t + "xattn_norm")
        qx = _matmul(h2, wl["xattn_wq"], mode="nn", out_dtype=BF16, name=t + "xattn_q")
        memn, rm = _rmsnorm_fwd(mem, p["mem_norm_g"][l], name=t + "mem_norm")
        kv = _matmul(memn, wl["xattn_wkv"], mode="nn", out_dtype=BF16, name=t + "xattn_kv")
        o = _xattn_fwd(qx, kv, name=t + "xattn_fwd")
        x3 = _matmul(o, wl["xattn_wo"], mode="nn", add=x2, name=t + "xattn_o")
        saved.append(dict(x=x, h1=h1, r1=r1, ua=ua, qkv=qkv, ub=ub, y_conv=y_conv, y_attn=y_attn, sbw=sbw, y_lru=y_lru, y=y,
                          x2=x2, h2=h2, r2=r2, qx=qx, memn=memn, rm=rm, kv=kv, o=o, w=wl))
        x = x3

    loss, dx, dg_final = _final_loss(x, p["final_norm_g"], target, name="final_loss")
    small = {k: [None] * depth for k in WEIGHTS if k not in BIG and k != "final_norm_g"}
    token = None
    for l in reversed(range(depth)):
        t = f"l{l}_"
        s = saved[l]
        wl = s["w"]
        do = _matmul(dx, wl["xattn_wo"], mode="nt", out_dtype=BF16, dep=token, name=t + "d_xattn_o")
        dwo = _matmul(s["o"], dx, mode="tn", out_dtype=BF16, name=t + "dw_xattn_o")
        dqx, dkv = _xattn_bwd(s["qx"], s["kv"], do, name=t + "xattn_bwd")
        dwq = _matmul(s["h2"], dqx, mode="tn", out_dtype=BF16, name=t + "dw_xattn_q")
        dh2 = _matmul(dqx, wl["xattn_wq"], mode="nt", out_dtype=BF16, name=t + "d_xattn_q")
        dx2, dg = _rmsnorm_bwd(dh2, s["x2"], s["r2"], p["xattn_norm_g"][l], dx, name=t + "xattn_norm_bwd")
        small["xattn_norm_g"][l] = dg[0]
        dmemn = _matmul(dkv, wl["xattn_wkv"], mode="nt", name=t + "d_xattn_kv")
        dwkv = _matmul(s["memn"], dkv, mode="tn", out_dtype=BF16, name=t + "dw_xattn_kv")
        _, dg = _rmsnorm_bwd(dmemn, mem, s["rm"], p["mem_norm_g"][l], None, name=t + "mem_norm_bwd")
        small["mem_norm_g"][l] = dg[0]
        dwout = _matmul(s["y"], dx2, mode="tn", out_dtype=BF16, name=t + "dw_out_proj")
        token = grads_ready(REST_GROUP, l, dict(w_out=dwout, xattn_wq=dwq, xattn_wkv=dwkv, xattn_wo=dwo))
        dy = _matmul(dx2, wl["w_out"], mode="nt", dep=token, out_dtype=BF16, name=t + "d_out_proj")
        dyc, dya, dyl, du, dnc, dna, dnl = _mix_out_bwd(
            dy, s["y_conv"], s["y_attn"], s["y_lru"], s["ua"], s["ub"], p["out_norm_conv"][l], p["out_norm_attn"][l],
            p["out_norm_lru"][l], name=t + "mix_out_bwd")
        small["out_norm_conv"][l], small["out_norm_attn"][l], small["out_norm_lru"][l] = dnc[0], dna[0], dnl[0]
        dd, dpw, dlg, dlb = _conv_bwd_a(s["ua"], dyc, p["conv_dw_w"][l], p["conv_dw_b"][l], p["conv_ln_g"][l], p["conv_ln_b"][l],
                                        wl["conv_pw_w"], name=t + "conv_bwd_a")
        du, ddw, ddb = _conv_bwd_b(s["ua"], dd, p["conv_dw_w"][l], du, name=t + "conv_bwd_b")
        small["conv_ln_g"][l], small["conv_ln_b"][l], small["conv_dw_w"][l], small["conv_dw_b"][l] = dlg[0], dlb[0], ddw, ddb[0]
        du = _sb_bwd(s["qkv"], dya, s["sbw"], du, 3 * c, heads, name=t + "sb_bwd")
        du, dcw, dcb, dwa, dba, dwx, dbx, dlam = _lru_bwd(
            s["ub"], aw // c, s["y_lru"], dyl, p["lru_conv_w"][l], p["lru_conv_b"][l], p["lru_wa"][l], p["lru_ba"][l],
            p["lru_wx"][l], p["lru_bx"][l], p["lru_lambda"][l], du, (3 * c + 4 * aw) // c, name=t + "lru_bwd")
        small["lru_conv_w"][l], small["lru_conv_b"][l], small["lru_wa"][l], small["lru_ba"][l] = dcw, dcb[0], dwa, dba[0]
        small["lru_wx"][l], small["lru_bx"][l], small["lru_lambda"][l] = dwx, dbx[0], dlam[0]
        dwin = _matmul(s["h1"], du, mode="tn", out_dtype=BF16, tk=4096, name=t + "dw_in")
        token = grads_ready(IN_GROUP, l, dict(w_in=dwin, conv_pw_w=_cast_bf16(dpw, name=t + "cast_dpw")))
        dh1 = _matmul(du, wl["w_in"], mode="nt", dep=token, tk=3328, out_dtype=BF16, name=t + "d_in")
        dx, dg = _rmsnorm_bwd(dh1, s["x"], s["r1"], p["mix_norm_g"][l], dx2, name=t + "mix_norm_bwd")
        small["mix_norm_g"][l] = dg[0]
    small = {k: jnp.stack(v) for k, v in small.items()}
    small["final_norm_g"] = dg_final[0]
    return loss, dx, small


def _pack(arrs):
    flat = jnp.concatenate([a.reshape(-1) for a in arrs])
    pad = (-flat.shape[0]) % (PACK_ROWS * LANES)
    return jnp.pad(flat, (0, pad)).reshape(-1, LANES)


def _unpack(packed, like):
    flat = packed.reshape(-1)
    out, at = [], 0
    for a in like:
        out.append(flat[at:at + a.size].reshape(a.shape))
        at += a.size
    return out


def _as_rows(a):
    return a.reshape(-1, a.shape[-1])


def kernel(x, mem, mix_norm_g, w_in, conv_dw_w, conv_dw_b, conv_ln_g, conv_ln_b, conv_pw_w, lru_conv_w, lru_conv_b, lru_wa, lru_ba, lru_wx, lru_bx, lru_lambda, out_norm_conv, out_norm_attn, out_norm_lru, w_out, xattn_norm_g, mem_norm_g, xattn_wq, xattn_wkv, xattn_wo, final_norm_g, loss_target, m_mix_norm_g, m_w_in, m_conv_dw_w, m_conv_dw_b, m_conv_ln_g, m_conv_ln_b, m_conv_pw_w, m_lru_conv_w, m_lru_conv_b, m_lru_wa, m_lru_ba, m_lru_wx, m_lru_bx, m_lru_lambda, m_out_norm_conv, m_out_norm_attn, m_out_norm_lru, m_w_out, m_xattn_norm_g, m_mem_norm_g, m_xattn_wq, m_xattn_wkv, m_xattn_wo, m_final_norm_g, v_mix_norm_g, v_w_in, v_conv_dw_w, v_conv_dw_b, v_conv_ln_g, v_conv_ln_b, v_conv_pw_w, v_lru_conv_w, v_lru_conv_b, v_lru_wa, v_lru_ba, v_lru_wx, v_lru_bx, v_lru_lambda, v_out_norm_conv, v_out_norm_attn, v_out_norm_lru, v_w_out, v_xattn_norm_g, v_mem_norm_g, v_xattn_wq, v_xattn_wkv, v_xattn_wo, v_final_norm_g):
    given = dict(locals())
    w = {k: given[k] for k in WEIGHTS}
    m = {k: given["m_" + k] for k in WEIGHTS}
    v = {k: given["v_" + k] for k in WEIGHTS}
    depth = mix_norm_g.shape[0]
    chip = 2 * lax.axis_index("x") + lax.axis_index("y")

    chip_arr = chip.astype(jnp.int32).reshape(1)

    p = dict(w)
    p.update(zip(SMALL_SHARDED, _allgather_chips([w[k] for k in SMALL_SHARDED], list(SMALL_SHARDED.values()), name="gather_small")))
    axis2d = {k: BIG[k] - 1 for k in BIG}
    groups = [(IN_GROUP, 0), (REST_GROUP, 0)] + [(IN_GROUP + REST_GROUP, l) for l in range(1, depth)]
    pending, token = {}, p[next(iter(SMALL_SHARDED))]
    for names, l in groups:
        arrs = [_cast_place(w[k], l, axis2d[k], chip_arr, token, name=f"place{l}_{k}") for k in names]
        axes = [axis2d[k] for k in names]
        sizes = [a.shape[ax] // N_CHIPS for a, ax in zip(arrs, axes)]
        halves = (names, l) == groups[0]
        start = functools.partial(_gather_copy, axes=axes, sizes=sizes, arriving=False, halves=halves)
        land = functools.partial(_gather_copy, axes=axes, sizes=sizes, arriving=True, halves=halves)
        send, recv, arrs, token = _split_start(arrs, start, len(arrs), token, name=f"gather_start{l}_{names[0]}")
        pending[(names[0], l)] = (names, arrs, send, recv, land, axes if halves else None)
    last_token = token
    have = {}

    def fetch(group, l, after):
        key = (group[0], l)
        if key in pending:
            names, arrs, send, recv, land, swap_axes = pending.pop(key)
            after = last_token if (group, l) == groups[0] else after
            arrs = _split_wait(arrs, send, recv, land, len(arrs), after, name=f"gather_wait{l}_{names[0]}")
            if swap_axes is not None:
                arrs = _swap_halves(arrs, swap_axes, name=f"gather_swap{l}_{names[0]}")
            have.update({(k, l): a for k, a in zip(names, arrs)})
        return {k: have[(k, l)] for k in group}

    flying = []
    held = {}

    def grads_ready(group, l, grads):
        held.update({(k, l): g for k, g in grads.items()})
        if l > 0 and group == REST_GROUP:
            return None
        names = [k for k in (IN_GROUP + REST_GROUP if l > 0 else group)]
        srcs = [held[(k, l)] for k in names]
        axes = [axis2d[k] for k in names]
        sizes = [g.shape[ax] // N_CHIPS for g, ax in zip(srcs, axes)]
        lands = [lax.empty((len(OTHER_CHIPS),) + tuple(sz if i == ax else d for i, d in enumerate(g.shape)), g.dtype)
                 for g, ax, sz in zip(srcs, axes, sizes)]
        n = len(names)
        copy = lambda refs, a, j, ss, rs_: _scatter_copy(refs[:n], refs[n:], a, j, axes, sizes, ss, rs_)
        send, recv, arrs, token = _split_start(srcs + lands, copy, n, jnp.zeros((8, LANES), F32), name=f"scatter_start{l}_{names[0]}")
        flying.append((names, l, axes, arrs, send, recv, copy))
        return token

    loss, grad_x, small = _trunk(x[0], mem[0], loss_target[0], p, fetch, grads_ready)

    sums = {}
    out = {}

    def arrive(entry, after):
        names, l, axes, arrs, send, recv, copy = entry
        n = len(names)
        arrs = _split_wait(arrs, send, recv, copy, n, after, name=f"scatter_wait{l}_{names[0]}")
        for k, ax, g, ld in zip(names, axes, arrs[:n], arrs[n:]):
            ld = ld.reshape((len(OTHER_CHIPS), -1, ld.shape[-1]))
            sums[(k, l)] = _sum_own_and_slots(g, ld, ax, chip_arr, name=f"sum{l}_{k}")
        return sums[(names[-1], l)]

    def update(names):
        mine = [sums[(k, l)] for k in names for l in range(depth)]
        theirs = _swap_sibling(mine, name="swap_sums_" + names[0])
        for i, k in enumerate(names):
            gs = [[mine[i * depth + l], theirs[i * depth + l]] for l in range(depth)]
            out[k] = _adamw_layers(w[k], m[k], v[k], gs, name="adamw_" + k)

    after = grad_x
    for entry in flying[:-1]:
        after = arrive(entry, after)
    update(REST_GROUP)

    small_names = [k for k in WEIGHTS if k not in BIG]
    parts = [small[k] for k in small_names] + [loss[0, :1]]
    total = _unpack(_allreduce_small(_pack(parts), name="allreduce_small"), parts)
    loss = total[-1][0]
    g_small = dict(zip(small_names, total[:-1]))
    for k, ax in SMALL_SHARDED.items():
        size = w[k].shape[ax]
        g_small[k] = lax.dynamic_slice_in_dim(g_small[k], chip * size, size, axis=ax)
    res = _adamw(_pack([w[k] for k in small_names]), _pack([m[k] for k in small_names]), _pack([v[k] for k in small_names]),
                 [_pack([g_small[k] for k in small_names])], name="adamw_small")
    last = res[0]
    res = [_unpack(r, [w[k] for k in small_names]) for r in res]
    for i, k in enumerate(small_names):
        out[k] = [r[i] for r in res]

    arrive(flying[-1], last)
    update(IN_GROUP)

    outs = [loss, grad_x[None]]
    for part in range(4):
        outs += [out[k][part] for k in WEIGHTS]
    return tuple(outs)
```

```python
import functools

import jax
import jax.numpy as jnp
from jax import lax
from jax.experimental import pallas as pl
from jax.experimental.pallas import tpu as pltpu

F32 = jnp.float32
BF16 = jnp.bfloat16
MESH = pl.DeviceIdType.MESH

V7X_VMEM_LIMIT_BYTES = 56 * 1024 * 1024
LANES = 128
HEAD_DIM = 128
LRU_C = 8.0
RMS_EPS = 1e-6
LN_EPS = 1e-5
CONV_HALO = 32
LRU_HALO = 8
ADAM_LR = 0.001
ADAM_B1 = 0.9
ADAM_B2 = 0.999
ADAM_EPS = 1e-08
ADAM_WD = 0.01
ADAM_STEP = 10


def _cparams(*sem):
    return pltpu.CompilerParams(dimension_semantics=sem, vmem_limit_bytes=V7X_VMEM_LIMIT_BYTES)


def _tile(n, pref):
    if n <= pref:
        return n
    for t in range(pref - pref % LANES, 0, -LANES):
        if n % t == 0:
            return t
    t = pref
    while n % t:
        t //= 2
    return t


def _dot(a, b, dims):
    return lax.dot_general(a, b, (dims, ((), ())), preferred_element_type=F32)


def _nn(a, b):
    return _dot(a, b, ((1,), (0,)))


def _nt(a, b):
    return _dot(a, b, ((1,), (1,)))


def _tn(a, b):
    return _dot(a, b, ((0,), (0,)))


def _sigmoid(x):
    return jax.nn.sigmoid(x)


def _silu_and_grad(x):
    s = _sigmoid(x)
    return x * s, s * (1.0 + x * (1.0 - s))


def _matmul(a, b, *, mode, name, layer=None, n=None, b_off=0, add=None, dep=None, out_dtype=F32, tm=1024, tn=1024, tk=2048):
    bshape = b.shape if layer is None else b.shape[1:]
    if mode == "nn":
        m, k = a.shape
        n = bshape[1] if n is None else n
    elif mode == "nt":
        m, k = a.shape
        n = bshape[0]
    else:
        k, m = a.shape
        n = bshape[1]
    tm, tk = _tile(m, tm), _tile(k, tk)
    tn = _tile(n, tn)
    while b_off % tn or n % tn:
        tn -= LANES
    nk = k // tk
    off = b_off // tn
    lead = () if layer is None else (None,)
    li = () if layer is None else (layer,)
    if mode == "nn":
        a_spec = pl.BlockSpec((tm, tk), lambda i, j, kk: (i, kk))
        b_spec = pl.BlockSpec(lead + (tk, tn), lambda i, j, kk: li + (kk, j + off))
        dot = _nn
    elif mode == "nt":
        a_spec = pl.BlockSpec((tm, tk), lambda i, j, kk: (i, kk))
        b_spec = pl.BlockSpec(lead + (tn, tk), lambda i, j, kk: li + (j, kk))
        dot = _nt
    else:
        a_spec = pl.BlockSpec((tk, tm), lambda i, j, kk: (kk, i))
        b_spec = pl.BlockSpec(lead + (tk, tn), lambda i, j, kk: li + (kk, j))
        dot = _tn
    o_spec = pl.BlockSpec((tm, tn), lambda i, j, kk: (i, j))
    has_add = add is not None

    def body(*refs):
        refs = refs[:-3] + refs[-2:] if dep is not None else refs
        if has_add:
            a_ref, b_ref, add_ref, o_ref, acc_ref = refs
        else:
            a_ref, b_ref, o_ref, acc_ref = refs
        kk = pl.program_id(2)
        part = dot(a_ref[...].astype(BF16), b_ref[...].astype(BF16))

        @pl.when(kk == 0)
        def _():
            acc_ref[...] = part

        @pl.when(kk > 0)
        def _():
            acc_ref[...] += part

        @pl.when(kk == nk - 1)
        def _():
            r = acc_ref[...]
            if has_add:
                r = r + add_ref[...]
            o_ref[...] = r.astype(o_ref.dtype)

    ins = [a, b] + ([add] if has_add else [])
    specs = [a_spec, b_spec] + ([o_spec] if has_add else [])
    if dep is not None:
        ins.append(dep)
        specs.append(pl.BlockSpec((8, LANES), lambda i, j, kk: (0, 0)))
    return pl.pallas_call(
        body, name=name, grid=(m // tm, n // tn, nk), in_specs=specs, out_specs=o_spec,
        out_shape=jax.ShapeDtypeStruct((m, n), out_dtype), scratch_shapes=[pltpu.VMEM((tm, tn), F32)],
        compiler_params=_cparams("parallel", "parallel", "arbitrary"))(*ins)


def _rmsnorm_fwd(x, g, *, name):
    s, d = x.shape
    tm = _tile(s, 256)

    def body(x_ref, g_ref, h_ref, r_ref):
        xf = x_ref[...]
        r = lax.rsqrt(jnp.mean(xf * xf, axis=-1, keepdims=True) + RMS_EPS)
        h_ref[...] = ((xf * r) * g_ref[...]).astype(h_ref.dtype)
        r_ref[...] = r

    return pl.pallas_call(
        body, name=name, grid=(s // tm,),
        in_specs=[pl.BlockSpec((tm, d), lambda i: (i, 0)), pl.BlockSpec((1, d), lambda i: (0, 0))],
        out_specs=[pl.BlockSpec((tm, d), lambda i: (i, 0)), pl.BlockSpec((tm, 1), lambda i: (i, 0))],
        out_shape=[jax.ShapeDtypeStruct((s, d), BF16), jax.ShapeDtypeStruct((s, 1), F32)],
        compiler_params=_cparams("parallel"))(x, g.reshape(1, d))


def _rms_bwd_math(dh, x, r, g):
    xr = x * r
    dyg = dh * g
    m = jnp.mean(dyg * xr, axis=-1, keepdims=True)
    return r * (dyg - xr * m), dh * xr


def _rmsnorm_bwd(dh, x, r, g, dres, *, name):
    s, d = x.shape
    tm = _tile(s, 256)
    has_res = dres is not None

    def body(*refs):
        if has_res:
            dh_ref, x_ref, r_ref, g_ref, res_ref, dx_ref, dg_ref = refs
        else:
            dh_ref, x_ref, r_ref, g_ref, dx_ref, dg_ref = refs
        dx, dgp = _rms_bwd_math(dh_ref[...].astype(F32), x_ref[...], r_ref[...], g_ref[...])
        if has_res:
            dx = dx + res_ref[...]
        dx_ref[...] = dx

        @pl.when(pl.program_id(0) == 0)
        def _():
            dg_ref[...] = jnp.zeros_like(dg_ref)

        dg_ref[...] += jnp.sum(dgp, axis=0, keepdims=True)

    row = pl.BlockSpec((tm, d), lambda i: (i, 0))
    vec = pl.BlockSpec((1, d), lambda i: (0, 0))
    ins = [dh, x, r, g.reshape(1, d)] + ([dres] if has_res else [])
    specs = [row, row, pl.BlockSpec((tm, 1), lambda i: (i, 0)), vec] + ([row] if has_res else [])
    return pl.pallas_call(
        body, name=name, grid=(s // tm,), in_specs=specs, out_specs=[row, vec],
        out_shape=[jax.ShapeDtypeStruct((s, d), F32), jax.ShapeDtypeStruct((1, d), F32)],
        compiler_params=_cparams("arbitrary"))(*ins)


def _final_loss(x, g, target, *, name):
    s, d = x.shape
    tm = _tile(s, 256)

    def body(x_ref, g_ref, t_ref, loss_ref, dx_ref, dg_ref):
        xf = x_ref[...]
        gv = g_ref[...]
        r = lax.rsqrt(jnp.mean(xf * xf, axis=-1, keepdims=True) + RMS_EPS)
        diff = (xf * r) * gv - t_ref[...]
        part = 0.5 * jnp.sum(jnp.mean(diff * diff, axis=-1, keepdims=True))
        dx, dgp = _rms_bwd_math(diff * (1.0 / d), xf, r, gv)
        dx_ref[...] = dx

        @pl.when(pl.program_id(0) == 0)
        def _():
            dg_ref[...] = jnp.zeros_like(dg_ref)
            loss_ref[...] = jnp.zeros_like(loss_ref)

        dg_ref[...] += jnp.sum(dgp, axis=0, keepdims=True)
        loss_ref[...] += part

    row = pl.BlockSpec((tm, d), lambda i: (i, 0))
    vec = pl.BlockSpec((1, d), lambda i: (0, 0))
    return pl.pallas_call(
        body, name=name, grid=(s // tm,), in_specs=[row, vec, row],
        out_specs=[pl.BlockSpec((8, LANES), lambda i: (0, 0)), row, vec],
        out_shape=[jax.ShapeDtypeStruct((8, LANES), F32), jax.ShapeDtypeStruct((s, d), F32), jax.ShapeDtypeStruct((1, d), F32)],
        compiler_params=_cparams("arbitrary"))(x, g.reshape(1, d), target)


SUBLANES = 8
TAP_GROUPS = 4


def _shift_scratch(tm, c):
    return pltpu.VMEM((SUBLANES - 1, tm + CONV_HALO - SUBLANES, c), F32)


def _shift_copies(src_ref, sh_ref):
    rows = sh_ref.shape[1]
    for r in range(1, SUBLANES):
        sh_ref[r - 1] = src_ref[pl.ds(r, rows), :]


def _read_shifted(src_ref, sh_ref, off, r0):
    r = off % SUBLANES
    base = off - r + r0
    return src_ref[pl.ds(base, SUBLANES), :] if r == 0 else sh_ref[r - 1, pl.ds(base, SUBLANES), :]


def _tap_rows(w):
    return [jnp.broadcast_to(w[k:k + 1, :], (SUBLANES, w.shape[1])) for k in range(w.shape[0])]


def _tap_sum(src_ref, sh_ref, wk, offs, init, tm):
    out = []
    for r0 in range(0, tm, SUBLANES * TAP_GROUPS):
        accs = [init] * TAP_GROUPS
        for wv, off in zip(wk, offs):
            accs = [acc + wv * _read_shifted(src_ref, sh_ref, off, r0 + SUBLANES * g) for g, acc in enumerate(accs)]
        out += accs
    return jnp.concatenate(out, axis=0)


def _conv_taps(gp_ref, sh_ref, w, bias, taps, tm):
    halo = gp_ref.shape[0] - tm
    _shift_copies(gp_ref, sh_ref)
    offs = [halo - (taps - 1) + k for k in range(taps)]
    return _tap_sum(gp_ref, sh_ref, _tap_rows(w), offs, jnp.broadcast_to(bias, (SUBLANES, w.shape[1])), tm)


def _conv_core(val, glu, valh, gluh, first, gp_ref, sh_ref, w, bias, lg, lb, taps, tm):
    sg = _sigmoid(glu)
    g = val * sg
    gh = jnp.where(first, 0.0, valh * _sigmoid(gluh))
    gp_ref[0:CONV_HALO, :] = gh
    gp_ref[CONV_HALO:, :] = g
    d = _conv_taps(gp_ref, sh_ref, w, bias, taps, tm)
    mu = jnp.mean(d, axis=-1, keepdims=True)
    dc = d - mu
    rstd = lax.rsqrt(jnp.mean(dc * dc, axis=-1, keepdims=True) + LN_EPS)
    xhat = dc * rstd
    ln = xhat * lg + lb
    return sg, xhat, rstd, ln


def _conv_fwd(ua, dw_w, dw_b, ln_g, ln_b, pw, *, name):
    s = ua.shape[0]
    taps, c = dw_w.shape
    tm = _tile(s, 512)
    hb = tm // CONV_HALO

    def body(val_ref, glu_ref, valh_ref, gluh_ref, w_ref, b_ref, lg_ref, lb_ref, pw_ref, y_ref, gp_ref, sh_ref):
        first = pl.program_id(0) == 0
        _, _, _, ln = _conv_core(val_ref[...], glu_ref[...], valh_ref[...], gluh_ref[...], first, gp_ref, sh_ref,
                                 w_ref[...], b_ref[...], lg_ref[...], lb_ref[...], taps, tm)
        sw = ln * _sigmoid(ln)
        y_ref[...] = _nn(sw.astype(BF16), pw_ref[...])

    cur = lambda col: pl.BlockSpec((tm, c), lambda i: (i, col))
    prev = lambda col: pl.BlockSpec((CONV_HALO, c), lambda i: (jnp.maximum(i * hb - 1, 0), col))
    full = lambda a: pl.BlockSpec(a.shape, lambda i: (0,) * a.ndim)
    vecs = [dw_w, dw_b.reshape(1, c), ln_g.reshape(1, c), ln_b.reshape(1, c), pw]
    return pl.pallas_call(
        body, name=name, grid=(s // tm,),
        in_specs=[cur(0), cur(1), prev(0), prev(1)] + [full(a) for a in vecs],
        out_specs=pl.BlockSpec((tm, c), lambda i: (i, 0)),
        out_shape=jax.ShapeDtypeStruct((s, c), F32),
        scratch_shapes=[pltpu.VMEM((tm + CONV_HALO, c), F32), _shift_scratch(tm, c)],
        compiler_params=_cparams("parallel"))(ua, ua, ua, ua, *vecs)


def _conv_bwd_a(ua, dy, dw_w, dw_b, ln_g, ln_b, pw, *, name):
    s = ua.shape[0]
    taps, c = dw_w.shape
    tm = _tile(s, 512)
    hb = tm // CONV_HALO

    def body(val_ref, glu_ref, valh_ref, gluh_ref, dy_ref, w_ref, b_ref, lg_ref, lb_ref, pw_ref,
             dd_ref, dpw_ref, dlg_ref, dlb_ref, gp_ref, sh_ref):
        first = pl.program_id(0) == 0
        lg = lg_ref[...]
        _, xhat, rstd, ln = _conv_core(val_ref[...], glu_ref[...], valh_ref[...], gluh_ref[...], first, gp_ref, sh_ref,
                                       w_ref[...], b_ref[...], lg, lb_ref[...], taps, tm)
        sw, dsw = _silu_and_grad(ln)
        dyb = dy_ref[...].astype(BF16)
        ds = _nt(dyb, pw_ref[...])
        dln = ds * dsw
        dxhat = dln * lg
        m1 = jnp.mean(dxhat, axis=-1, keepdims=True)
        m2 = jnp.mean(dxhat * xhat, axis=-1, keepdims=True)
        dd_ref[...] = rstd * (dxhat - m1 - xhat * m2)

        @pl.when(first)
        def _():
            dpw_ref[...] = jnp.zeros_like(dpw_ref)
            dlg_ref[...] = jnp.zeros_like(dlg_ref)
            dlb_ref[...] = jnp.zeros_like(dlb_ref)

        dpw_ref[...] += _tn(sw.astype(BF16), dyb)
        dlg_ref[...] += jnp.sum(dln * xhat, axis=0, keepdims=True)
        dlb_ref[...] += jnp.sum(dln, axis=0, keepdims=True)

    cur = lambda col: pl.BlockSpec((tm, c), lambda i: (i, col))
    prev = lambda col: pl.BlockSpec((CONV_HALO, c), lambda i: (jnp.maximum(i * hb - 1, 0), col))
    full = lambda a: pl.BlockSpec(a.shape, lambda i: (0,) * a.ndim)
    vec = pl.BlockSpec((1, c), lambda i: (0, 0))
    vecs = [dw_w, dw_b.reshape(1, c), ln_g.reshape(1, c), ln_b.reshape(1, c), pw]
    return pl.pallas_call(
        body, name=name, grid=(s // tm,),
        in_specs=[cur(0), cur(1), prev(0), prev(1), pl.BlockSpec((tm, c), lambda i: (i, 0))] + [full(a) for a in vecs],
        out_specs=[pl.BlockSpec((tm, c), lambda i: (i, 0)), pl.BlockSpec((c, c), lambda i: (0, 0)), vec, vec],
        out_shape=[jax.ShapeDtypeStruct((s, c), F32), jax.ShapeDtypeStruct((c, c), F32),
                   jax.ShapeDtypeStruct((1, c), F32), jax.ShapeDtypeStruct((1, c), F32)],
        scratch_shapes=[pltpu.VMEM((tm + CONV_HALO, c), F32), _shift_scratch(tm, c)],
        compiler_params=_cparams("arbitrary"))(ua, ua, ua, ua, dy, *vecs)


def _conv_bwd_b(ua, dd, dw_w, du, *, name):
    s = ua.shape[0]
    taps, c = dw_w.shape
    tm = _tile(s, 512)
    hb = tm // CONV_HALO
    nt = s // tm

    def body(val_ref, glu_ref, valh_ref, gluh_ref, dd_ref, ddn_ref, w_ref, du_in, du_ref, dw_ref, db_ref,
             gp_ref, ddp_ref, shg_ref, shd_ref):
        i = pl.program_id(0)
        val = val_ref[...]
        sg = _sigmoid(glu_ref[...])
        gp_ref[0:CONV_HALO, :] = jnp.where(i == 0, 0.0, valh_ref[...] * _sigmoid(gluh_ref[...]))
        gp_ref[CONV_HALO:, :] = val * sg
        dd = dd_ref[...]
        ddp_ref[0:tm, :] = dd
        ddp_ref[tm:, :] = jnp.where(i == nt - 1, 0.0, ddn_ref[...])
        _shift_copies(gp_ref, shg_ref)
        _shift_copies(ddp_ref, shd_ref)
        zero = jnp.zeros((SUBLANES, c), F32)
        dg = _tap_sum(ddp_ref, shd_ref, _tap_rows(w_ref[...]), [taps - 1 - k for k in range(taps)], zero, tm)
        dws = []
        for k in range(taps):
            accs = [zero] * TAP_GROUPS
            for n, r0 in enumerate(range(0, tm, SUBLANES)):
                accs[n % TAP_GROUPS] = accs[n % TAP_GROUPS] + ddp_ref[pl.ds(r0, SUBLANES), :] * _read_shifted(
                    gp_ref, shg_ref, CONV_HALO - (taps - 1) + k, r0)
            dws.append(jnp.sum(sum(accs[1:], accs[0]), axis=0, keepdims=True))
        du_ref[:, 0:c] = (dg * sg).astype(du_ref.dtype)
        du_ref[:, c:] = (dg * val * sg * (1.0 - sg)).astype(du_ref.dtype)

        @pl.when(i == 0)
        def _():
            dw_ref[...] = jnp.zeros_like(dw_ref)
            db_ref[...] = jnp.zeros_like(db_ref)

        dw_ref[...] += jnp.concatenate(dws, axis=0)
        db_ref[...] += jnp.sum(dd, axis=0, keepdims=True)

    cur = lambda col: pl.BlockSpec((tm, c), lambda i: (i, col))
    prev = lambda col: pl.BlockSpec((CONV_HALO, c), lambda i: (jnp.maximum(i * hb - 1, 0), col))
    nxt = pl.BlockSpec((CONV_HALO, c), lambda i: (jnp.minimum((i + 1) * hb, s // CONV_HALO - 1), 0))
    return pl.pallas_call(
        body, name=name, grid=(nt,),
        in_specs=[cur(0), cur(1), prev(0), prev(1), pl.BlockSpec((tm, c), lambda i: (i, 0)), nxt,
                  pl.BlockSpec((taps, c), lambda i: (0, 0)), ANY],
        out_specs=[pl.BlockSpec((tm, 2 * c), lambda i: (i, 0)),
                   pl.BlockSpec((taps, c), lambda i: (0, 0)), pl.BlockSpec((1, c), lambda i: (0, 0))],
        out_shape=[jax.ShapeDtypeStruct(du.shape, du.dtype), jax.ShapeDtypeStruct((taps, c), F32), jax.ShapeDtypeStruct((1, c), F32)],
        input_output_aliases={7: 0},
        scratch_shapes=[pltpu.VMEM((tm + CONV_HALO, c), F32), pltpu.VMEM((tm + CONV_HALO, c), F32),
                        _shift_scratch(tm, c), _shift_scratch(tm, c)],
        compiler_params=_cparams("arbitrary"))(ua, ua, ua, ua, dd, dd, dw_w, du)


LOG2_E = 1.4426950408889634
SB_HEADS_PER_STEP = 4


def _sb_logs(qk, mask):
    z = qk
    ls = jnp.minimum(z, 0.0) - jnp.log2(1.0 + jnp.exp2(-jnp.abs(z)))
    lm = ls - z
    if mask is not None:
        lm = jnp.where(mask, lm, 0.0)
    return ls, lm


def _diag_mask(b):
    return lax.broadcasted_iota(jnp.int32, (b, b), 1) < lax.broadcasted_iota(jnp.int32, (b, b), 0)


def _split_dot(x, tri):
    hi = x.astype(BF16)
    lo = (x - hi.astype(F32)).astype(BF16)
    return _nn(hi, tri) + _nn(lo, tri)


def _tri(bk, cmp):
    r = lax.broadcasted_iota(jnp.int32, (bk, bk), 0)
    c = lax.broadcasted_iota(jnp.int32, (bk, bk), 1)
    return cmp(r, c).astype(BF16)


def _sb_fwd(qkv, heads, *, name, blk=256):
    s = qkv.shape[0]
    b = _tile(s, blk)
    nq = s // b
    hp = min(SB_HEADS_PER_STEP, heads)
    assert heads % hp == 0
    groups = heads // hp
    wide = hp * HEAD_DIM

    def body(q_ref, k_ref, v_ref, o_ref, w_hbm, stage, sems):
        g = pl.program_id(0)
        i = pl.program_id(1)
        sls = [slice(n * HEAD_DIM, (n + 1) * HEAD_DIM) for n in range(hp)]
        qs = [(q_ref[:, sl].astype(F32) * (HEAD_DIM ** -0.5 * LOG2_E)).astype(BF16) for sl in sls]
        tri = _tri(b, lambda r, c: r > c)
        diag = _diag_mask(b)
        r0 = pl.multiple_of(i * b, b)

        def saves(slot, j):
            c0 = pl.multiple_of(j * b, b)
            return [pltpu.make_async_copy(stage.at[slot, n, w], w_hbm.at[w, g * hp + n, pl.ds(r0, b), pl.ds(c0, b)], sems.at[slot])
                    for n in range(hp) for w in range(2)]

        def tile(t, j, carry, masked):
            slot = t % 2
            if not masked:
                @pl.when(t >= 2)
                def _():
                    for cp in saves(slot, j):
                        cp.wait()

            s0 = pl.multiple_of(j * b, b)
            kbs = [k_ref[pl.ds(s0, b), sl] for sl in sls]
            vbs = [v_ref[pl.ds(s0, b), sl] for sl in sls]
            zs = [_nt(q, kb) for q, kb in zip(qs, kbs)]
            sc = [_sb_logs(z, diag if masked else None) for z in zs]
            after = [_split_dot(lm, tri) for _, lm in sc]
            out = []
            for n, ((ls, lm), af, (acc, c)) in enumerate(zip(sc, after, carry)):
                a = jnp.exp2(ls + (af + c))
                if masked:
                    a = jnp.where(diag, a, 0.0)
                ab = a.astype(BF16)
                stage[slot, n, 0] = ab
                stage[slot, n, 1] = jnp.exp2(ls).astype(BF16)
                out.append((ab, acc, c + jnp.sum(lm, axis=1, keepdims=True)))
            for cp in saves(slot, j):
                cp.start()
            return tuple((acc + _nn(ab, vb), c) for vb, (ab, acc, c) in zip(vbs, out))

        zero = tuple((jnp.zeros((b, HEAD_DIM), F32), jnp.zeros((b, 1), F32)) for _ in range(hp))
        carry = tile(0, i, zero, True)
        carry = lax.fori_loop(0, i, lambda jj, cr: tile(jj + 1, i - 1 - jj, cr, False), carry)
        for sl, (acc, _) in zip(sls, carry):
            o_ref[:, sl] = acc
        for cp in saves(i % 2, 0):
            cp.wait()

        @pl.when(i >= 1)
        def _():
            for cp in saves((i + 1) % 2, 0):
                cp.wait()

    return pl.pallas_call(
        body, name=name, grid=(groups, nq),
        in_specs=[pl.BlockSpec((b, wide), lambda g, i: (i, g)),
                  pl.BlockSpec((s, wide), lambda g, i: (0, groups + g)),
                  pl.BlockSpec((s, wide), lambda g, i: (0, 2 * groups + g))],
        out_specs=[pl.BlockSpec((b, wide), lambda g, i: (i, g)), ANY],
        out_shape=[jax.ShapeDtypeStruct((s, heads * HEAD_DIM), F32), jax.ShapeDtypeStruct((2, heads, s, s), BF16)],
        scratch_shapes=[pltpu.VMEM((2, hp, 2, b, b), BF16), pltpu.SemaphoreType.DMA((2,))],
        compiler_params=_cparams("parallel", "arbitrary"))(qkv, qkv, qkv)


def _sb_bwd(qkv, do, saved, du, du_col, heads, *, name, blk=256):
    s = qkv.shape[0]
    b = _tile(s, blk)
    nq = s // b
    hp = min(SB_HEADS_PER_STEP, heads)
    assert heads % hp == 0
    groups = heads // hp
    wide = hp * HEAD_DIM
    scale = HEAD_DIM ** -0.5

    def body(q_ref, k_ref, v_ref, do_ref, w_hbm, du_in, du_ref, dk_acc, dv_acc, stage, sems, dq_out, dkv_out, out_sems):
        g = pl.program_id(0)
        i = pl.program_id(1)

        @pl.when(i == 0)
        def _():
            dk_acc[...] = jnp.zeros_like(dk_acc)
            dv_acc[...] = jnp.zeros_like(dv_acc)

        sls = [slice(n * HEAD_DIM, (n + 1) * HEAD_DIM) for n in range(hp)]
        qs = [q_ref[:, sl] for sl in sls]
        dos = [do_ref[:, sl].astype(BF16) for sl in sls]
        tri_excl = _tri(b, lambda r, c: r < c)
        diag = _diag_mask(b)
        r0 = pl.multiple_of(i * b, b)

        def loads(slot, j):
            c0 = pl.multiple_of(j * b, b)
            return [pltpu.make_async_copy(w_hbm.at[w, g * hp + n, pl.ds(r0, b), pl.ds(c0, b)], stage.at[slot, n, w], sems.at[slot])
                    for n in range(hp) for w in range(2)]

        def tile(j, carry, masked):
            slot = j % 2

            @pl.when(j < i)
            def _():
                for cp in loads(1 - slot, j + 1):
                    cp.start()

            for cp in loads(slot, j):
                cp.wait()
            s0 = pl.multiple_of(j * b, b)
            kbs = [k_ref[pl.ds(s0, b), sl] for sl in sls]
            vbs = [v_ref[pl.ds(s0, b), sl] for sl in sls]
            ab = [stage[slot, n, 0] for n in range(hp)]
            ps = [_nt(dob, vb) for dob, vb in zip(dos, vbs)]
            gs = [a.astype(F32) * p for a, p in zip(ab, ps)]
            hs = [_nn(gg.astype(BF16), tri_excl) for gg in gs]
            dzb = []
            for n, (gg, h, (_, cg)) in enumerate(zip(gs, hs, carry)):
                dz = (gg - (gg + (h + cg)) * stage[slot, n, 1].astype(F32)) * scale
                if masked:
                    dz = jnp.where(diag, dz, 0.0)
                dzb.append(dz.astype(BF16))
            out = tuple((dq + _nn(dz, kb), cg + jnp.sum(gg, axis=1, keepdims=True))
                        for dz, kb, gg, (dq, cg) in zip(dzb, kbs, gs, carry))
            for sl, dz, a, q, dob in zip(sls, dzb, ab, qs, dos):
                dk_acc[pl.ds(s0, b), sl] += _tn(dz, q)
                dv_acc[pl.ds(s0, b), sl] += _tn(a, dob)
            return out

        for cp in loads(0, 0):
            cp.start()
        carry = tuple((jnp.zeros((b, HEAD_DIM), F32), jnp.zeros((b, 1), F32)) for _ in range(hp))
        carry = lax.fori_loop(0, i, lambda j, cr: tile(j, cr, False), carry)
        carry = tile(i, carry, True)
        cols = [pl.multiple_of(du_col + n * heads * HEAD_DIM + g * wide, LANES) for n in range(3)]
        for sl, (dq, _) in zip(sls, carry):
            dq_out[:, sl] = dq.astype(dq_out.dtype)
        put = pltpu.make_async_copy(dq_out, du_ref.at[pl.ds(r0, b), pl.ds(cols[0], wide)], out_sems.at[0])
        put.start()
        put.wait()

        @pl.when(i == nq - 1)
        def _():
            dkv_out[0] = dk_acc[...].astype(dkv_out.dtype)
            dkv_out[1] = dv_acc[...].astype(dkv_out.dtype)
            puts = [pltpu.make_async_copy(dkv_out.at[n], du_ref.at[:, pl.ds(cols[1 + n], wide)], out_sems.at[1 + n]) for n in range(2)]
            for cp in puts:
                cp.start()
            for cp in puts:
                cp.wait()

    row = pl.BlockSpec((b, wide), lambda g, i: (i, g))
    col = lambda off: pl.BlockSpec((s, wide), lambda g, i: (0, off + g), pipeline_mode=pl.Buffered(1))
    return pl.pallas_call(
        body, name=name, grid=(groups, nq),
        in_specs=[row, col(groups), col(2 * groups), row, ANY, ANY],
        out_specs=ANY, out_shape=jax.ShapeDtypeStruct(du.shape, du.dtype), input_output_aliases={5: 0},
        scratch_shapes=[pltpu.VMEM((s, wide), F32), pltpu.VMEM((s, wide), F32),
                        pltpu.VMEM((2, hp, 2, b, b), BF16), pltpu.SemaphoreType.DMA((2,)),
                        pltpu.VMEM((b, wide), BF16), pltpu.VMEM((2, s, wide), BF16), pltpu.SemaphoreType.DMA((3,))],
        compiler_params=_cparams("arbitrary", "arbitrary"))(qkv, qkv, qkv, do, saved, du)


def _shift_rows(x, n, fill, *, down):
    rows = x.shape[0]
    if n % 8 == 0:
        pad = jnp.full((n, x.shape[1]), fill, x.dtype)
        return jnp.concatenate([pad, x[:rows - n]], axis=0) if down else jnp.concatenate([x[n:], pad], axis=0)
    t = lax.broadcasted_iota(jnp.int32, x.shape, 0)
    if down:
        return jnp.where(t >= n, pltpu.roll(x, n, 0), fill)
    return jnp.where(t < rows - n, pltpu.roll(x, rows - n, 0), fill)


def _scan_rows(a, b, *, reverse):
    n = 1
    while n < a.shape[0]:
        b = a * _shift_rows(b, n, 0.0, down=not reverse) + b
        a = a * _shift_rows(a, n, 1.0, down=not reverse)
        n *= 2
    return a, b


def _neg_expm1(x):
    p = 1.0 + x * (1.0 / 7.0)
    for k in (6.0, 5.0, 4.0, 3.0, 2.0):
        p = 1.0 + x * (1.0 / k) * p
    return jnp.where(x > -0.25, -(x * p), 1.0 - jnp.exp(x))


def _softplus_neg(lam):
    z = -lam
    e = jnp.exp(-jnp.abs(z))
    u = 1.0 + e
    d = u - 1.0
    log1p_e = jnp.where(d == 0.0, e, jnp.log(u) * (e / jnp.where(d == 0.0, 1.0, d)))
    return jnp.maximum(z, 0.0) + log1p_e


def _lru_gates(xp_ref, w, bias, wa_ref, ba, wx_ref, bx, sp, taps, tm, heads):
    halo = xp_ref.shape[0] - tm
    xc = jnp.broadcast_to(bias, (tm, w.shape[1]))
    for k in range(taps):
        xc = xc + w[k:k + 1, :] * xp_ref[pl.ds(halo - (taps - 1) + k, tm), :]
    xb = xc.astype(BF16)
    pr, pi = [], []
    for n in range(heads):
        xh = xb[:, n * HEAD_DIM:(n + 1) * HEAD_DIM]
        pr.append(_nn(xh, wa_ref[n]))
        pi.append(_nn(xh, wx_ref[n]))
    r = _sigmoid(jnp.concatenate(pr, axis=1) + ba)
    ig = _sigmoid(jnp.concatenate(pi, axis=1) + bx)
    log_a = (-LRU_C) * r * sp
    a = jnp.exp(log_a)
    mult = jnp.sqrt(_neg_expm1(2.0 * log_a))
    return xc, r, ig, a, mult


def _lru_fwd(ub, x_col, conv_w, conv_b, wa, ba, wx, bx, lam, *, name):
    s = ub.shape[0]
    taps, w = conv_w.shape
    heads = w // HEAD_DIM
    tm = _tile(s, 256)
    hb = tm // LRU_HALO

    def body(x_ref, xh_ref, cw_ref, cb_ref, wa_ref, ba_ref, wx_ref, bx_ref, lam_ref, h_ref, xp_ref, carry_ref):
        i = pl.program_id(0)

        @pl.when(i == 0)
        def _():
            carry_ref[...] = jnp.zeros_like(carry_ref)

        xp_ref[0:LRU_HALO, :] = jnp.where(i == 0, 0.0, xh_ref[...])
        xp_ref[LRU_HALO:, :] = x_ref[...]
        sp = _softplus_neg(lam_ref[...])
        xc, _, ig, a, mult = _lru_gates(xp_ref, cw_ref[...], cb_ref[...], wa_ref, ba_ref[...], wx_ref, bx_ref[...],
                                        sp, taps, tm, heads)
        ac, bc = _scan_rows(a, mult * (ig * xc), reverse=False)
        h = ac * carry_ref[0:1, :] + bc
        h_ref[...] = h
        carry_ref[...] = jnp.broadcast_to(h[tm - 1:tm, :], carry_ref.shape)

    full = lambda arr: pl.BlockSpec(arr.shape, lambda i: (0,) * arr.ndim)
    vecs = [conv_w, conv_b.reshape(1, w), wa.astype(BF16), ba.reshape(1, w), wx.astype(BF16), bx.reshape(1, w), lam.reshape(1, w)]
    return pl.pallas_call(
        body, name=name, grid=(s // tm,),
        in_specs=[pl.BlockSpec((tm, w), lambda i: (i, x_col)),
                  pl.BlockSpec((LRU_HALO, w), lambda i: (jnp.maximum(i * hb - 1, 0), x_col))] + [full(v) for v in vecs],
        out_specs=pl.BlockSpec((tm, w), lambda i: (i, 0)),
        out_shape=jax.ShapeDtypeStruct((s, w), F32),
        scratch_shapes=[pltpu.VMEM((tm + LRU_HALO, w), F32), pltpu.VMEM((8, w), F32)],
        compiler_params=_cparams("arbitrary"))(ub, ub, *vecs)


def _lru_bwd(ub, x_col, h, dh, conv_w, conv_b, wa, ba, wx, bx, lam, du, du_col, *, name):
    s = ub.shape[0]
    taps, w = conv_w.shape
    heads = w // HEAD_DIM
    tm = _tile(s, 256)
    hb = tm // LRU_HALO
    nt = s // tm

    def body(x_ref, xh_ref, h_ref, hh_ref, dh_ref, cw_ref, cb_ref, wa_ref, ba_ref, wx_ref, bx_ref, lam_ref, du_in,
             dx_ref, dcw_ref, dcb_ref, dwa_ref, dba_ref, dwx_ref, dbx_ref, dlam_ref,
             xp_ref, dxp_ref, dlt_ref, afirst_ref, dxc_next_ref, dsp_ref):
        step = pl.program_id(0)
        i = nt - 1 - step

        @pl.when(step == 0)
        def _():
            for ref in (dcw_ref, dcb_ref, dwa_ref, dba_ref, dwx_ref, dbx_ref, dlam_ref, dlt_ref, dxc_next_ref, dsp_ref):
                ref[...] = jnp.zeros_like(ref)
            afirst_ref[...] = jnp.ones_like(afirst_ref)

        xp_ref[0:LRU_HALO, :] = jnp.where(i == 0, 0.0, xh_ref[...])
        xp_ref[LRU_HALO:, :] = x_ref[...]
        cw = cw_ref[...]
        lam_v = lam_ref[...]
        sp = _softplus_neg(lam_v)
        xc, r, ig, a, mult = _lru_gates(xp_ref, cw, cb_ref[...], wa_ref, ba_ref[...], wx_ref, bx_ref[...], sp, taps, tm, heads)
        rows = lax.broadcasted_iota(jnp.int32, (tm, w), 0)
        a_next = jnp.where(rows == tm - 1, afirst_ref[0:1, :], _shift_rows(a, 1, 1.0, down=False))
        ac, bc = _scan_rows(a_next, dh_ref[...], reverse=True)
        delta = ac * dlt_ref[0:1, :] + bc
        hv = h_ref[...]
        h_last_prev = jnp.where(i == 0, 0.0, hh_ref[LRU_HALO - 1:LRU_HALO, :])
        h_prev = jnp.where(rows == 0, h_last_prev, _shift_rows(hv, 1, 0.0, down=True))
        gated = ig * xc
        da = delta * h_prev
        dmult = delta * gated
        dgated = delta * mult
        dlog_a = da * a - dmult * (a * a) / mult
        dpr = dlog_a * ((-LRU_C) * sp) * r * (1.0 - r)
        dpi = dgated * xc * ig * (1.0 - ig)
        dxc = dgated * ig
        dsp_ref[...] += jnp.sum(dlog_a * ((-LRU_C) * r), axis=0, keepdims=True)
        dba_ref[...] += jnp.sum(dpr, axis=0, keepdims=True)
        dbx_ref[...] += jnp.sum(dpi, axis=0, keepdims=True)
        xb = xc.astype(BF16)
        dprb = dpr.astype(BF16)
        dpib = dpi.astype(BF16)
        back = []
        for n in range(heads):
            sl = slice(n * HEAD_DIM, (n + 1) * HEAD_DIM)
            dwa_ref[n] += _tn(xb[:, sl], dprb[:, sl])
            dwx_ref[n] += _tn(xb[:, sl], dpib[:, sl])
            back.append(_nt(dprb[:, sl], wa_ref[n]) + _nt(dpib[:, sl], wx_ref[n]))
        dxc = dxc + jnp.concatenate(back, axis=1)
        dxp_ref[0:tm, :] = dxc
        dxp_ref[tm:, :] = dxc_next_ref[...]
        dx = jnp.zeros((tm, w), F32)
        dws = []
        for k in range(taps):
            dx = dx + cw[k:k + 1, :] * dxp_ref[pl.ds(taps - 1 - k, tm), :]
            dws.append(jnp.sum(dxc * xp_ref[pl.ds(LRU_HALO - (taps - 1) + k, tm), :], axis=0, keepdims=True))
        dx_ref[...] = dx.astype(dx_ref.dtype)
        dcw_ref[...] += jnp.concatenate(dws, axis=0)
        dcb_ref[...] += jnp.sum(dxc, axis=0, keepdims=True)
        dlt_ref[...] = jnp.broadcast_to(delta[0:1, :], dlt_ref.shape)
        afirst_ref[...] = jnp.broadcast_to(a[0:1, :], afirst_ref.shape)
        dxc_next_ref[...] = dxc[0:LRU_HALO, :]

        @pl.when(step == nt - 1)
        def _():
            dlam_ref[...] = dsp_ref[...] * (-_sigmoid(-lam_v))

    rev = lambda col: pl.BlockSpec((tm, w), lambda st: (nt - 1 - st, col))
    prev = lambda col: pl.BlockSpec((LRU_HALO, w), lambda st: (jnp.maximum((nt - 1 - st) * hb - 1, 0), col))
    full = lambda arr: pl.BlockSpec(arr.shape, lambda st: (0,) * arr.ndim)
    vec = pl.BlockSpec((1, w), lambda st: (0, 0))
    vecs = [conv_w, conv_b.reshape(1, w), wa.astype(BF16), ba.reshape(1, w), wx.astype(BF16), bx.reshape(1, w), lam.reshape(1, w)]
    vshape = jax.ShapeDtypeStruct((1, w), F32)
    return pl.pallas_call(
        body, name=name, grid=(nt,),
        in_specs=[rev(x_col), prev(x_col), rev(0), prev(0), rev(0)] + [full(v) for v in vecs] + [ANY],
        out_specs=[rev(du_col), full(conv_w), vec, full(wa), vec, full(wx), vec, vec],
        out_shape=[jax.ShapeDtypeStruct(du.shape, du.dtype), jax.ShapeDtypeStruct(conv_w.shape, F32), vshape,
                   jax.ShapeDtypeStruct(wa.shape, F32), vshape, jax.ShapeDtypeStruct(wx.shape, F32), vshape, vshape],
        input_output_aliases={5 + len(vecs): 0},
        scratch_shapes=[pltpu.VMEM((tm + LRU_HALO, w), F32), pltpu.VMEM((tm + LRU_HALO, w), F32),
                        pltpu.VMEM((8, w), F32), pltpu.VMEM((8, w), F32), pltpu.VMEM((LRU_HALO, w), F32), pltpu.VMEM((1, w), F32)],
        compiler_params=_cparams("arbitrary"))(ub, ub, h, h, dh, *vecs, du)


def _group_fwd(y, w, gate):
    r = lax.rsqrt(jnp.mean(y * y, axis=-1, keepdims=True) + RMS_EPS)
    return ((y * r) * w) * (gate * _sigmoid(gate))


def _mix_out_fwd(y_conv, y_attn, y_lru, ua, ub, n_conv, n_attn, n_lru, *, name):
    s, c = y_conv.shape
    wa_ = y_attn.shape[1]
    d = 2 * c + wa_
    tm = _tile(s, 256)
    assert wa_ == 2 * c

    def body(yc_ref, ya_ref, yl_ref, gc_ref, ga_ref, gl_ref, nc_ref, na_ref, nl_ref, o_ref):
        o_ref[:, 0:c] = _group_fwd(yc_ref[...], nc_ref[...], gc_ref[...]).astype(o_ref.dtype)
        o_ref[:, c:c + wa_] = _group_fwd(ya_ref[...], na_ref[...], ga_ref[...]).astype(o_ref.dtype)
        o_ref[:, c + wa_:] = _group_fwd(yl_ref[...], nl_ref[...], gl_ref[...]).astype(o_ref.dtype)

    blk = lambda width, col: pl.BlockSpec((tm, width), lambda i: (i, col))
    vec = lambda width: pl.BlockSpec((1, width), lambda i: (0, 0))
    return pl.pallas_call(
        body, name=name, grid=(s // tm,),
        in_specs=[blk(c, 0), blk(wa_, 0), blk(c, 0), blk(c, 2), blk(wa_, 0), blk(c, 3), vec(c), vec(wa_), vec(c)],
        out_specs=blk(d, 0), out_shape=jax.ShapeDtypeStruct((s, d), BF16),
        compiler_params=_cparams("parallel"))(y_conv, y_attn, y_lru, ua, ub, ub, n_conv.reshape(1, c), n_attn.reshape(1, wa_), n_lru.reshape(1, c))


def _group_bwd(dout, y, w, gate):
    r = lax.rsqrt(jnp.mean(y * y, axis=-1, keepdims=True) + RMS_EPS)
    silu, dsilu = _silu_and_grad(gate)
    dy, dwp = _rms_bwd_math(dout * silu, y, r, w)
    return dy, dout * ((y * r) * w) * dsilu, dwp


def _mix_out_bwd(dy, y_conv, y_attn, y_lru, ua, ub, n_conv, n_attn, n_lru, *, name):
    s, c = y_conv.shape
    wa_ = y_attn.shape[1]
    tm = _tile(s, 256)

    du_width = 5 * c + 4 * wa_
    gate_cols = (2 * c, 3 * c + 3 * wa_, 4 * c + 4 * wa_)

    def body(dy_ref, yc_ref, ya_ref, yl_ref, gc_ref, ga_ref, gl_ref, nc_ref, na_ref, nl_ref,
             dyc_ref, dya_ref, dyl_ref, du_ref, dnc_ref, dna_ref, dnl_ref):
        @pl.when(pl.program_id(0) == 0)
        def _():
            for ref in (dnc_ref, dna_ref, dnl_ref):
                ref[...] = jnp.zeros_like(ref)

        groups = ((dy_ref[:, 0:c], yc_ref, nc_ref, gc_ref, dyc_ref, dnc_ref),
                  (dy_ref[:, c:c + wa_], ya_ref, na_ref, ga_ref, dya_ref, dna_ref),
                  (dy_ref[:, c + wa_:], yl_ref, nl_ref, gl_ref, dyl_ref, dnl_ref))
        for col, (dout, y_ref, n_ref, g_ref, dyo_ref, dn_ref) in zip(gate_cols, groups):
            dyv, dgv, dwp = _group_bwd(dout.astype(F32), y_ref[...], n_ref[...], g_ref[...])
            dyo_ref[...] = dyv.astype(dyo_ref.dtype)
            du_ref[:, col:col + dgv.shape[1]] = dgv.astype(du_ref.dtype)
            dn_ref[...] += jnp.sum(dwp, axis=0, keepdims=True)

    blk = lambda width, col: pl.BlockSpec((tm, width), lambda i: (i, col))
    vec = lambda width: pl.BlockSpec((1, width), lambda i: (0, 0))
    sh = lambda width, dt: jax.ShapeDtypeStruct((s, width), dt)
    vs = lambda width: jax.ShapeDtypeStruct((1, width), F32)
    return pl.pallas_call(
        body, name=name, grid=(s // tm,),
        in_specs=[blk(2 * c + wa_, 0), blk(c, 0), blk(wa_, 0), blk(c, 0), blk(c, 2), blk(wa_, 0), blk(c, 3), vec(c), vec(wa_), vec(c)],
        out_specs=[blk(c, 0), blk(wa_, 0), blk(c, 0), blk(du_width, 0), vec(c), vec(wa_), vec(c)],
        out_shape=[sh(c, BF16), sh(wa_, BF16), sh(c, F32), sh(du_width, BF16), vs(c), vs(wa_), vs(c)],
        compiler_params=_cparams("arbitrary"))(dy, y_conv, y_attn, y_lru, ua, ub, ub, n_conv.reshape(1, c), n_attn.reshape(1, wa_), n_lru.reshape(1, c))


def _xattn_probs(qh, kh):
    sc = _nt(qh, kh) * (HEAD_DIM ** -0.5)
    e = jnp.exp(sc - jnp.max(sc, axis=-1, keepdims=True))
    return e / jnp.sum(e, axis=-1, keepdims=True)


def _xattn_fwd(q, kv, *, name):
    s, w = q.shape
    heads = w // HEAD_DIM
    tm = _tile(s, 512)

    def body(q_ref, kv_ref, o_ref):
        for n in range(heads):
            sl = slice(n * HEAD_DIM, (n + 1) * HEAD_DIM)
            p = _xattn_probs(q_ref[:, sl], kv_ref[:, sl])
            o_ref[:, sl] = _nn(p.astype(BF16), kv_ref[:, w + n * HEAD_DIM:w + (n + 1) * HEAD_DIM]).astype(o_ref.dtype)

    return pl.pallas_call(
        body, name=name, grid=(s // tm,),
        in_specs=[pl.BlockSpec((tm, w), lambda i: (i, 0)), pl.BlockSpec(kv.shape, lambda i: (0, 0))],
        out_specs=pl.BlockSpec((tm, w), lambda i: (i, 0)), out_shape=jax.ShapeDtypeStruct((s, w), BF16),
        compiler_params=_cparams("parallel"))(q, kv)


def _xattn_bwd(q, kv, do, *, name):
    s, w = q.shape
    heads = w // HEAD_DIM
    tm = _tile(s, 512)
    scale = HEAD_DIM ** -0.5

    def body(q_ref, kv_ref, do_ref, dq_ref, dkv_ref):
        @pl.when(pl.program_id(0) == 0)
        def _():
            dkv_ref[...] = jnp.zeros_like(dkv_ref)

        for n in range(heads):
            sl = slice(n * HEAD_DIM, (n + 1) * HEAD_DIM)
            vsl = slice(w + n * HEAD_DIM, w + (n + 1) * HEAD_DIM)
            qh, kh, vh, doh = q_ref[:, sl], kv_ref[:, sl], kv_ref[:, vsl], do_ref[:, sl]
            p = _xattn_probs(qh, kh)
            dp = _nt(doh, vh)
            ds = (p * (dp - jnp.sum(dp * p, axis=-1, keepdims=True)) * scale).astype(BF16)
            dq_ref[:, sl] = _nn(ds, kh).astype(dq_ref.dtype)
            dkv_ref[:, sl] += _tn(ds, qh)
            dkv_ref[:, vsl] += _tn(p.astype(BF16), doh)

    row = pl.BlockSpec((tm, w), lambda i: (i, 0))
    kvs = pl.BlockSpec(kv.shape, lambda i: (0, 0))
    return pl.pallas_call(
        body, name=name, grid=(s // tm,), in_specs=[row, kvs, row], out_specs=[row, kvs],
        out_shape=[jax.ShapeDtypeStruct((s, w), BF16), jax.ShapeDtypeStruct(kv.shape, F32)],
        compiler_params=_cparams("arbitrary"))(q, kv, do)


ROW_BLOCK_BYTES = 4 << 20
PACK_ROWS = 512


def _row_tile(rows, cols):
    limit = max(8, ROW_BLOCK_BYTES // (4 * cols))
    t = 8
    while t * 2 <= limit and rows % (t * 2) == 0:
        t *= 2
    assert rows % t == 0
    return t


def _cast_bf16(w, *, name):
    rows, cols = w.shape
    tr = _row_tile(rows, cols)

    def body(w_ref, o_ref):
        o_ref[...] = w_ref[...].astype(BF16)

    blk = pl.BlockSpec((tr, cols), lambda i: (i, 0))
    return pl.pallas_call(body, name=name, grid=(rows // tr,), in_specs=[blk], out_specs=blk,
                          out_shape=jax.ShapeDtypeStruct((rows, cols), BF16), compiler_params=_cparams("parallel"))(w)


def _adamw(w, m, v, gs, *, name):
    rows, cols = w.shape
    tr = _row_tile(rows, cols * 4)
    ng = len(gs)

    def body(*refs):
        w_ref, m_ref, v_ref = refs[:3]
        g_refs = refs[3:3 + ng]
        g_out, d_out, m_out, v_out = refs[3 + ng:]
        g = g_refs[0][...]
        for r in g_refs[1:]:
            g = g + r[...]
        mn = ADAM_B1 * m_ref[...] + (1.0 - ADAM_B1) * g
        vn = ADAM_B2 * v_ref[...] + (1.0 - ADAM_B2) * (g * g)
        m_hat = mn / (1.0 - ADAM_B1 ** ADAM_STEP)
        v_hat = vn / (1.0 - ADAM_B2 ** ADAM_STEP)
        g_out[...] = g
        d_out[...] = -ADAM_LR * (m_hat / (jnp.sqrt(v_hat) + ADAM_EPS) + ADAM_WD * w_ref[...])
        m_out[...] = mn
        v_out[...] = vn

    blk = pl.BlockSpec((tr, cols), lambda i: (i, 0))
    shp = jax.ShapeDtypeStruct((rows, cols), F32)
    return pl.pallas_call(body, name=name, grid=(rows // tr,), in_specs=[blk] * (3 + ng), out_specs=[blk] * 4,
                          out_shape=[shp] * 4, compiler_params=_cparams("parallel"))(w, m, v, *gs)


def _adamw_layers(w, m, v, gs, *, name):
    layers, rows, cols = w.shape
    tr = _row_tile(rows, cols * 4)
    nblk = rows // tr
    ng = len(gs[0])

    def body(*refs):
        w_ref, m_ref, v_ref = refs[:3]
        g_refs = refs[3:3 + layers * ng]
        g_out, d_out, m_out, v_out = refs[3 + layers * ng:]
        l = pl.program_id(0)
        g = jnp.zeros((tr, cols), F32)
        for ll in range(layers):
            gl = g_refs[ll * ng][...].astype(F32)
            for r in g_refs[ll * ng + 1:(ll + 1) * ng]:
                gl = gl + r[...].astype(F32)
            g = jnp.where(l == ll, gl, g)
        mn = ADAM_B1 * m_ref[...] + (1.0 - ADAM_B1) * g
        vn = ADAM_B2 * v_ref[...] + (1.0 - ADAM_B2) * (g * g)
        m_hat = mn / (1.0 - ADAM_B1 ** ADAM_STEP)
        v_hat = vn / (1.0 - ADAM_B2 ** ADAM_STEP)
        g_out[...] = g
        d_out[...] = -ADAM_LR * (m_hat / (jnp.sqrt(v_hat) + ADAM_EPS) + ADAM_WD * w_ref[...])
        m_out[...] = mn
        v_out[...] = vn

    blk = pl.BlockSpec((None, tr, cols), lambda l, i: (l, i, 0))

    def g_spec(ll):
        return pl.BlockSpec((tr, cols), lambda l, i: (jnp.where(l == ll, i, jnp.where(l < ll, 0, nblk - 1)), 0))

    shp = jax.ShapeDtypeStruct(w.shape, F32)
    return pl.pallas_call(
        body, name=name, grid=(layers, nblk), in_specs=[blk] * 3 + [g_spec(ll) for ll in range(layers) for _ in range(ng)],
        out_specs=[blk] * 4, out_shape=[shp] * 4, compiler_params=_cparams("arbitrary", "arbitrary"),
    )(w, m, v, *[g for gl in gs for g in gl])


OTHER_CHIPS = ((1, 0), (0, 1), (1, 1))
N_CHIPS = 4
ANY = pl.BlockSpec(memory_space=pl.ANY)


def _place():
    return lax.axis_index("x"), lax.axis_index("y"), lax.axis_index("c")


def _flip(v, f):
    return 1 - v if f else v


def _part(ref, lead, axis, chip, size):
    idx = list(lead) + [slice(None)] * (len(ref.shape) - len(lead))
    idx[len(lead) + axis] = pl.ds(pl.multiple_of(chip * size, size), size)
    return ref.at[tuple(idx)]


def _allgather_chips(shards, axes, *, name):
    n = len(shards)
    sizes = [sh.shape[ax] for sh, ax in zip(shards, axes)]

    def full_shape(sh, ax):
        return tuple(d * N_CHIPS if i == ax else d for i, d in enumerate(sh.shape))

    def body(*refs):
        ins, outs = refs[:n], refs[n:2 * n]
        send_sems, recv_sems, loc_sems = refs[2 * n:]
        x, y, c = _place()
        me = 2 * x + y
        local = []
        for a in range(n):
            cp = pltpu.make_async_copy(ins[a], _part(outs[a], (), axes[a], me, sizes[a]), loc_sems.at[a])
            cp.start()
            local.append(cp)

        def remote(a, j, chip):
            fx, fy = OTHER_CHIPS[j]
            return pltpu.make_async_remote_copy(
                src_ref=ins[a], dst_ref=_part(outs[a], (), axes[a], chip, sizes[a]),
                send_sem=send_sems.at[a, j], recv_sem=recv_sems.at[a, j],
                device_id=(_flip(x, fx), _flip(y, fy), c), device_id_type=MESH)

        for a in range(n):
            for j in range(len(OTHER_CHIPS)):
                remote(a, j, me).start()
        for a in range(n):
            for j, (fx, fy) in enumerate(OTHER_CHIPS):
                remote(a, j, 2 * _flip(x, fx) + _flip(y, fy)).wait()
        for cp in local:
            cp.wait()

    return pl.pallas_call(
        body, name=name, in_specs=[ANY] * n, out_specs=[ANY] * n,
        out_shape=[jax.ShapeDtypeStruct(full_shape(sh, ax), sh.dtype) for sh, ax in zip(shards, axes)],
        scratch_shapes=[pltpu.SemaphoreType.DMA((n, 3)), pltpu.SemaphoreType.DMA((n, 3)), pltpu.SemaphoreType.DMA((n,))],
    )(*shards)


HBM = pl.BlockSpec(memory_space=pltpu.HBM)
SEM = pl.BlockSpec(memory_space=pltpu.SEMAPHORE)
SPLIT_COPY = pltpu.CompilerParams(has_side_effects=pltpu.SideEffectType.DATAFLOW_SIDE_EFFECTING)


def _cast_place(w, layer, axis, chip, dep, *, name):
    _, rows, cols = w.shape
    tr = _row_tile(rows, cols)
    nblk = rows // tr

    def body(chip_ref, w_ref, dep_ref, o_ref):
        o_ref[...] = w_ref[...].astype(BF16)

    if axis == 1:
        shape = (rows, cols * N_CHIPS)
        o_spec = pl.BlockSpec((tr, cols), lambda i, chip_ref: (i, chip_ref[0]))
    else:
        shape = (rows * N_CHIPS, cols)
        o_spec = pl.BlockSpec((tr, cols), lambda i, chip_ref: (chip_ref[0] * nblk + i, 0))
    return pl.pallas_call(
        body, name=name,
        grid_spec=pltpu.PrefetchScalarGridSpec(
            num_scalar_prefetch=1, grid=(nblk,),
            in_specs=[pl.BlockSpec((None, tr, cols), lambda i, chip_ref: (layer, i, 0)), ANY], out_specs=o_spec),
        out_shape=jax.ShapeDtypeStruct(shape, BF16), compiler_params=_cparams("parallel"))(chip, w, dep)


def _half_rows(ref, axis, chip, size, half):
    rows = ref.shape[0] // (N_CHIPS if axis == 0 else 1)
    r0 = chip * rows if axis == 0 else 0
    if half is not None:
        rows //= 2
        r0 = r0 + half * rows
    rsl = pl.ds(r0 if isinstance(r0, int) else pl.multiple_of(r0, 16), rows)
    return ref.at[rsl, :] if axis == 0 else ref.at[rsl, pl.ds(pl.multiple_of(chip * size, LANES), size)]


def _gather_copy(refs, a, j, send_sems, recv_sems, *, axes, sizes, arriving, halves=False):
    x, y, c = _place()
    fx, fy = OTHER_CHIPS[j]
    px, py = _flip(x, fx), _flip(y, fy)
    part = _half_rows(refs[a], axes[a], (2 * px + py) if arriving else (2 * x + y), sizes[a], c if halves else None)
    k = a * len(OTHER_CHIPS) + j
    return pltpu.make_async_remote_copy(src_ref=part, dst_ref=part, send_sem=send_sems.at[k], recv_sem=recv_sems.at[k],
                                        device_id=(px, py, c), device_id_type=MESH)


def _swap_halves(arrs, axes, *, name):
    n = len(arrs)
    sizes = [a.shape[ax] // N_CHIPS for a, ax in zip(arrs, axes)]

    def body(*refs):
        ins = refs[:n]
        send_sems, recv_sems = refs[2 * n:]
        x, y, c = _place()

        def copy(a, j, half):
            fx, fy = OTHER_CHIPS[j]
            part = _half_rows(ins[a], axes[a], 2 * _flip(x, fx) + _flip(y, fy), sizes[a], half)
            k = a * len(OTHER_CHIPS) + j
            return pltpu.make_async_remote_copy(src_ref=part, dst_ref=part, send_sem=send_sems.at[k], recv_sem=recv_sems.at[k],
                                                device_id=(x, y, 1 - c), device_id_type=MESH)

        todo = [(a, j) for a in range(n) for j in range(len(OTHER_CHIPS))]
        for a, j in todo:
            copy(a, j, c).start()
        for a, j in todo:
            copy(a, j, c).wait_send()
            copy(a, j, 1 - c).wait_recv()

    res = pl.pallas_call(
        body, name=name, in_specs=[ANY] * n, out_specs=[ANY] * n, out_shape=[jax.ShapeDtypeStruct(a.shape, a.dtype) for a in arrs],
        input_output_aliases={a: a for a in range(n)},
        scratch_shapes=[pltpu.SemaphoreType.DMA((n * len(OTHER_CHIPS),)), pltpu.SemaphoreType.DMA((n * len(OTHER_CHIPS),))],
    )(*arrs)
    return list(res)


def _scatter_copy(srcs, lands, a, j, axes, sizes, send_sems, recv_sems):
    x, y, c = _place()
    fx, fy = OTHER_CHIPS[j]
    px, py = _flip(x, fx), _flip(y, fy)
    k = a * len(OTHER_CHIPS) + j
    return pltpu.make_async_remote_copy(src_ref=_part(srcs[a], (), axes[a], 2 * px + py, sizes[a]), dst_ref=lands[a].at[j],
                                        send_sem=send_sems.at[k], recv_sem=recv_sems.at[k],
                                        device_id=(px, py, c), device_id_type=MESH)


def _split_start(arrs, make_copy, ncopies, dep, *, name):
    n = len(arrs)

    def body(*refs):
        ins = refs[:n]
        send_sems, recv_sems = refs[n + 1], refs[n + 2]
        token = refs[n + 3 + n]
        for a in range(ncopies):
            for j in range(len(OTHER_CHIPS)):
                make_copy(ins, a, j, send_sems, recv_sems).start()
        token[...] = jnp.zeros_like(token)

    sem = pltpu.SemaphoreType.DMA((ncopies * len(OTHER_CHIPS),))
    res = pl.pallas_call(
        body, name=name,
        out_shape=(sem, sem, *[pltpu.HBM(a.shape, a.dtype) for a in arrs], jax.ShapeDtypeStruct((8, LANES), F32)),
        in_specs=[HBM] * n + [pl.BlockSpec(memory_space=pl.ANY)],
        out_specs=(SEM, SEM, *[HBM] * n, pl.BlockSpec(memory_space=pltpu.VMEM)),
        input_output_aliases={a: 2 + a for a in range(n)}, compiler_params=SPLIT_COPY,
    )(*[pltpu.with_memory_space_constraint(a, pltpu.HBM) for a in arrs], dep)
    return res[0], res[1], list(res[2:2 + n]), res[2 + n]


def _split_wait(arrs, send_sems, recv_sems, make_copy, ncopies, after, *, name):
    n = len(arrs)

    def body(*refs):
        ins = refs[:n]
        send, recv = refs[n], refs[n + 1]
        for a in range(ncopies):
            for j in range(len(OTHER_CHIPS)):
                cp = make_copy(ins, a, j, send, recv)
                cp.wait_send()
                cp.wait_recv()

    res = pl.pallas_call(
        body, name=name, out_shape=tuple(pltpu.HBM(a.shape, a.dtype) for a in arrs),
        in_specs=[HBM] * n + [SEM, SEM, pl.BlockSpec(memory_space=pl.ANY)], out_specs=tuple([HBM] * n),
        input_output_aliases={a: a for a in range(n)}, compiler_params=SPLIT_COPY,
    )(*arrs, send_sems, recv_sems, after)
    return list(res)


def _sum_own_and_slots(g, land, axis, chip, *, name):
    slots, rows, cols = land.shape
    tr = _row_tile(rows, cols * 4)
    nblk = rows // tr

    def body(chip_ref, g_ref, l_ref, o_ref):
        acc = g_ref[...].astype(F32)
        for j in range(slots):
            acc = acc + l_ref[j].astype(F32)
        o_ref[...] = acc.astype(o_ref.dtype)

    if axis == 1:
        g_spec = pl.BlockSpec((tr, cols), lambda i, chip_ref: (i, chip_ref[0]))
    else:
        g_spec = pl.BlockSpec((tr, cols), lambda i, chip_ref: (chip_ref[0] * nblk + i, 0))
    return pl.pallas_call(
        body, name=name,
        grid_spec=pltpu.PrefetchScalarGridSpec(
            num_scalar_prefetch=1, grid=(nblk,),
            in_specs=[g_spec, pl.BlockSpec((slots, tr, cols), lambda i, chip_ref: (0, i, 0))],
            out_specs=pl.BlockSpec((tr, cols), lambda i, chip_ref: (i, 0))),
        out_shape=jax.ShapeDtypeStruct((rows, cols), BF16), compiler_params=_cparams("parallel"))(chip, g, land)


def _swap_sibling(arrs, *, name):
    n = len(arrs)

    def body(*refs):
        ins, outs = refs[:n], refs[n:2 * n]
        send_sems, recv_sems = refs[2 * n:]
        x, y, c = _place()
        copies = [pltpu.make_async_remote_copy(src_ref=ins[a], dst_ref=outs[a], send_sem=send_sems.at[a], recv_sem=recv_sems.at[a],
                                               device_id=(x, y, 1 - c), device_id_type=MESH) for a in range(n)]
        for cp in copies:
            cp.start()
        for cp in copies:
            cp.wait()

    return pl.pallas_call(
        body, name=name, in_specs=[ANY] * n, out_specs=[ANY] * n,
        out_shape=[jax.ShapeDtypeStruct(a.shape, a.dtype) for a in arrs],
        scratch_shapes=[pltpu.SemaphoreType.DMA((n,)), pltpu.SemaphoreType.DMA((n,))],
    )(*arrs)


def _allreduce_small(p, *, name):
    rows, cols = p.shape
    nrel = len(OTHER_CHIPS)

    def body(p_ref, o_ref, sib_ref, land_ref, send_sems, recv_sems):
        x, y, c = _place()
        me = 2 * x + y
        pair = pltpu.make_async_remote_copy(src_ref=p_ref, dst_ref=sib_ref, send_sem=send_sems.at[nrel], recv_sem=recv_sems.at[nrel],
                                            device_id=(x, y, 1 - c), device_id_type=MESH)
        pair.start()
        pair.wait()
        land_ref[nrel] = p_ref[...] + sib_ref[...]
        copies = []
        for j, (fx, fy) in enumerate(OTHER_CHIPS):
            copies.append(pltpu.make_async_remote_copy(src_ref=land_ref.at[nrel], dst_ref=land_ref.at[j], send_sem=send_sems.at[j],
                                                       recv_sem=recv_sems.at[j], device_id=(_flip(x, fx), _flip(y, fy), c),
                                                       device_id_type=MESH))
        for cp in copies:
            cp.start()
        for cp in copies:
            cp.wait()

        def slot_of(chip):
            r = jnp.bitwise_xor(chip, me)
            return jnp.where(r == 0, nrel, jnp.where(r == 2, 0, jnp.where(r == 1, 1, 2)))

        acc = land_ref[slot_of(0)]
        for chip in range(1, N_CHIPS):
            acc = acc + land_ref[slot_of(chip)]
        o_ref[...] = acc

    vm = pl.BlockSpec(memory_space=pltpu.VMEM)
    return pl.pallas_call(
        body, name=name, in_specs=[vm], out_specs=vm, out_shape=jax.ShapeDtypeStruct((rows, cols), F32),
        scratch_shapes=[pltpu.VMEM((rows, cols), F32), pltpu.VMEM((nrel + 1, rows, cols), F32),
                        pltpu.SemaphoreType.DMA((nrel + 1,)), pltpu.SemaphoreType.DMA((nrel + 1,))],
        compiler_params=pltpu.CompilerParams(vmem_limit_bytes=V7X_VMEM_LIMIT_BYTES))(p)


WEIGHTS = ("mix_norm_g", "w_in", "conv_dw_w", "conv_dw_b", "conv_ln_g", "conv_ln_b", "conv_pw_w", "lru_conv_w", "lru_conv_b",
           "lru_wa", "lru_ba", "lru_wx", "lru_bx", "lru_lambda", "out_norm_conv", "out_norm_attn", "out_norm_lru", "w_out",
           "xattn_norm_g", "mem_norm_g", "xattn_wq", "xattn_wkv", "xattn_wo", "final_norm_g")
BIG = {"w_in": 2, "conv_pw_w": 1, "w_out": 1, "xattn_wq": 1, "xattn_wkv": 1, "xattn_wo": 2}
SMALL_SHARDED = {"conv_dw_w": 2, "lru_conv_w": 2}


IN_GROUP = ("w_in", "conv_pw_w")
REST_GROUP = ("w_out", "xattn_wq", "xattn_wkv", "xattn_wo")


def _trunk(x, mem, target, p, fetch, grads_ready):
    depth = p["mix_norm_g"].shape[0]
    c = p["conv_dw_w"].shape[2]
    aw = p["out_norm_attn"].shape[1]
    heads = aw // HEAD_DIM
    saved = []
    for l in range(depth):
        t = f"l{l}_"
        h1, r1 = _rmsnorm_fwd(x, p["mix_norm_g"][l], name=t + "mix_norm")
        wl = dict(fetch(IN_GROUP, l, r1))
        ua = _matmul(h1, wl["w_in"], mode="nn", n=3 * c, b_off=0, name=t + "in_conv")
        qkv = _matmul(h1, wl["w_in"], mode="nn", n=3 * aw, b_off=3 * c, out_dtype=BF16, name=t + "in_qkv")
        ub = _matmul(h1, wl["w_in"], mode="nn", n=aw + 2 * c, b_off=3 * c + 3 * aw, name=t + "in_gates")
        y_conv = _conv_fwd(ua, p["conv_dw_w"][l], p["conv_dw_b"][l], p["conv_ln_g"][l], p["conv_ln_b"][l], wl["conv_pw_w"],
                           name=t + "conv_fwd")
        y_attn, sbw = _sb_fwd(qkv, heads, name=t + "sb_fwd")
        y_lru = _lru_fwd(ub, aw // c, p["lru_conv_w"][l], p["lru_conv_b"][l], p["lru_wa"][l], p["lru_ba"][l], p["lru_wx"][l],
                         p["lru_bx"][l], p["lru_lambda"][l], name=t + "lru_fwd")
        y = _mix_out_fwd(y_conv, y_attn, y_lru, ua, ub, p["out_norm_conv"][l], p["out_norm_attn"][l], p["out_norm_lru"][l],
                         name=t + "mix_out_fwd")
        wl.update(fetch(REST_GROUP, l, y))
        x2 = _matmul(y, wl["w_out"], mode="nn", add=x, name=t + "out_proj")
        h2, r2 = _rmsnorm_fwd(x2, p["xattn_norm_g"][l], name=t + "xattn_norm")
        qx = _matmul(h2, wl["xattn_wq"], mode="nn", out_dtype=BF16, name=t + "xattn_q")
        memn, rm = _rmsnorm_fwd(mem, p["mem_norm_g"][l], name=t + "mem_norm")
        kv = _matmul(memn, wl["xattn_wkv"], mode="nn", out_dtype=BF16, name=t + "xattn_kv")
        o = _xattn_fwd(qx, kv, name=t + "xattn_fwd")
        x3 = _matmul(o, wl["xattn_wo"], mode="nn", add=x2, name=t + "xattn_o")
        saved.append(dict(x=x, h1=h1, r1=r1, ua=ua, qkv=qkv, ub=ub, y_conv=y_conv, y_attn=y_attn, sbw=sbw, y_lru=y_lru, y=y,
                          x2=x2, h2=h2, r2=r2, qx=qx, memn=memn, rm=rm, kv=kv, o=o, w=wl))
        x = x3

    loss, dx, dg_final = _final_loss(x, p["final_norm_g"], target, name="final_loss")
    small = {k: [None] * depth for k in WEIGHTS if k not in BIG and k != "final_norm_g"}
    token = None
    for l in reversed(range(depth)):
        t = f"l{l}_"
        s = saved[l]
        wl = s["w"]
        do = _matmul(dx, wl["xattn_wo"], mode="nt", out_dtype=BF16, dep=token, name=t + "d_xattn_o")
        dwo = _matmul(s["o"], dx, mode="tn", out_dtype=BF16, name=t + "dw_xattn_o")
        dqx, dkv = _xattn_bwd(s["qx"], s["kv"], do, name=t + "xattn_bwd")
        dwq = _matmul(s["h2"], dqx, mode="tn", out_dtype=BF16, name=t + "dw_xattn_q")
        dh2 = _matmul(dqx, wl["xattn_wq"], mode="nt", out_dtype=BF16, name=t + "d_xattn_q")
        dx2, dg = _rmsnorm_bwd(dh2, s["x2"], s["r2"], p["xattn_norm_g"][l], dx, name=t + "xattn_norm_bwd")
        small["xattn_norm_g"][l] = dg[0]
        dmemn = _matmul(dkv, wl["xattn_wkv"], mode="nt", name=t + "d_xattn_kv")
        dwkv = _matmul(s["memn"], dkv, mode="tn", out_dtype=BF16, name=t + "dw_xattn_kv")
        _, dg = _rmsnorm_bwd(dmemn, mem, s["rm"], p["mem_norm_g"][l], None, name=t + "mem_norm_bwd")
        small["mem_norm_g"][l] = dg[0]
        dwout = _matmul(s["y"], dx2, mode="tn", out_dtype=BF16, name=t + "dw_out_proj")
        token = grads_ready(REST_GROUP, l, dict(w_out=dwout, xattn_wq=dwq, xattn_wkv=dwkv, xattn_wo=dwo))
        dy = _matmul(dx2, wl["w_out"], mode="nt", dep=token, out_dtype=BF16, name=t + "d_out_proj")
        dyc, dya, dyl, du, dnc, dna, dnl = _mix_out_bwd(
            dy, s["y_conv"], s["y_attn"], s["y_lru"], s["ua"], s["ub"], p["out_norm_conv"][l], p["out_norm_attn"][l],
            p["out_norm_lru"][l], name=t + "mix_out_bwd")
        small["out_norm_conv"][l], small["out_norm_attn"][l], small["out_norm_lru"][l] = dnc[0], dna[0], dnl[0]
        dd, dpw, dlg, dlb = _conv_bwd_a(s["ua"], dyc, p["conv_dw_w"][l], p["conv_dw_b"][l], p["conv_ln_g"][l], p["conv_ln_b"][l],
                                        wl["conv_pw_w"], name=t + "conv_bwd_a")
        du, ddw, ddb = _conv_bwd_b(s["ua"], dd, p["conv_dw_w"][l], du, name=t + "conv_bwd_b")
        small["conv_ln_g"][l], small["conv_ln_b"][l], small["conv_dw_w"][l], small["conv_dw_b"][l] = dlg[0], dlb[0], ddw, ddb[0]
        du = _sb_bwd(s["qkv"], dya, s["sbw"], du, 3 * c, heads, name=t + "sb_bwd")
        du, dcw, dcb, dwa, dba, dwx, dbx, dlam = _lru_bwd(
            s["ub"], aw // c, s["y_lru"], dyl, p["lru_conv_w"][l], p["lru_conv_b"][l], p["lru_wa"][l], p["lru_ba"][l],
            p["lru_wx"][l], p["lru_bx"][l], p["lru_lambda"][l], du, (3 * c + 4 * aw) // c, name=t + "lru_bwd")
        small["lru_conv_w"][l], small["lru_conv_b"][l], small["lru_wa"][l], small["lru_ba"][l] = dcw, dcb[0], dwa, dba[0]
        small["lru_wx"][l], small["lru_bx"][l], small["lru_lambda"][l] = dwx, dbx[0], dlam[0]
        dwin = _matmul(s["h1"], du, mode="tn", out_dtype=BF16, tk=4096, name=t + "dw_in")
        token = grads_ready(IN_GROUP, l, dict(w_in=dwin, conv_pw_w=_cast_bf16(dpw, name=t + "cast_dpw")))
        dh1 = _matmul(du, wl["w_in"], mode="nt", dep=token, tk=3328, out_dtype=BF16, name=t + "d_in")
        dx, dg = _rmsnorm_bwd(dh1, s["x"], s["r1"], p["mix_norm_g"][l], dx2, name=t + "mix_norm_bwd")
        small["mix_norm_g"][l] = dg[0]
    small = {k: jnp.stack(v) for k, v in small.items()}
    small["final_norm_g"] = dg_final[0]
    return loss, dx, small


def _pack(arrs):
    flat = jnp.concatenate([a.reshape(-1) for a in arrs])
    pad = (-flat.shape[0]) % (PACK_ROWS * LANES)
    return jnp.pad(flat, (0, pad)).reshape(-1, LANES)


def _unpack(packed, like):
    flat = packed.reshape(-1)
    out, at = [], 0
    for a in like:
        out.append(flat[at:at + a.size].reshape(a.shape))
        at += a.size
    return out


def _as_rows(a):
    return a.reshape(-1, a.shape[-1])


def kernel(x, mem, mix_norm_g, w_in, conv_dw_w, conv_dw_b, conv_ln_g, conv_ln_b, conv_pw_w, lru_conv_w, lru_conv_b, lru_wa, lru_ba, lru_wx, lru_bx, lru_lambda, out_norm_conv, out_norm_attn, out_norm_lru, w_out, xattn_norm_g, mem_norm_g, xattn_wq, xattn_wkv, xattn_wo, final_norm_g, loss_target, m_mix_norm_g, m_w_in, m_conv_dw_w, m_conv_dw_b, m_conv_ln_g, m_conv_ln_b, m_conv_pw_w, m_lru_conv_w, m_lru_conv_b, m_lru_wa, m_lru_ba, m_lru_wx, m_lru_bx, m_lru_lambda, m_out_norm_conv, m_out_norm_attn, m_out_norm_lru, m_w_out, m_xattn_norm_g, m_mem_norm_g, m_xattn_wq, m_xattn_wkv, m_xattn_wo, m_final_norm_g, v_mix_norm_g, v_w_in, v_conv_dw_w, v_conv_dw_b, v_conv_ln_g, v_conv_ln_b, v_conv_pw_w, v_lru_conv_w, v_lru_conv_b, v_lru_wa, v_lru_ba, v_lru_wx, v_lru_bx, v_lru_lambda, v_out_norm_conv, v_out_norm_attn, v_out_norm_lru, v_w_out, v_xattn_norm_g, v_mem_norm_g, v_xattn_wq, v_xattn_wkv, v_xattn_wo, v_final_norm_g):
    given = dict(locals())
    w = {k: given[k] for k in WEIGHTS}
    m = {k: given["m_" + k] for k in WEIGHTS}
    v = {k: given["v_" + k] for k in WEIGHTS}
    depth = mix_norm_g.shape[0]
    chip = 2 * lax.axis_index("x") + lax.axis_index("y")

    chip_arr = chip.astype(jnp.int32).reshape(1)

    p = dict(w)
    p.update(zip(SMALL_SHARDED, _allgather_chips([w[k] for k in SMALL_SHARDED], list(SMALL_SHARDED.values()), name="gather_small")))
    axis2d = {k: BIG[k] - 1 for k in BIG}
    groups = [(IN_GROUP, 0), (REST_GROUP, 0)] + [(IN_GROUP + REST_GROUP, l) for l in range(1, depth)]
    pending, token = {}, p[next(iter(SMALL_SHARDED))]
    for names, l in groups:
        arrs = [_cast_place(w[k], l, axis2d[k], chip_arr, token, name=f"place{l}_{k}") for k in names]
        axes = [axis2d[k] for k in names]
        sizes = [a.shape[ax] // N_CHIPS for a, ax in zip(arrs, axes)]
        halves = (names, l) == groups[0]
        start = functools.partial(_gather_copy, axes=axes, sizes=sizes, arriving=False, halves=halves)
        land = functools.partial(_gather_copy, axes=axes, sizes=sizes, arriving=True, halves=halves)
        send, recv, arrs, token = _split_start(arrs, start, len(arrs), token, name=f"gather_start{l}_{names[0]}")
        pending[(names[0], l)] = (names, arrs, send, recv, land, axes if halves else None)
    last_token = token
    have = {}

    def fetch(group, l, after):
        key = (group[0], l)
        if key in pending:
            names, arrs, send, recv, land, swap_axes = pending.pop(key)
            after = last_token if (group, l) == groups[0] else after
            arrs = _split_wait(arrs, send, recv, land, len(arrs), after, name=f"gather_wait{l}_{names[0]}")
            if swap_axes is not None:
                arrs = _swap_halves(arrs, swap_axes, name=f"gather_swap{l}_{names[0]}")
            have.update({(k, l): a for k, a in zip(names, arrs)})
        return {k: have[(k, l)] for k in group}

    flying = []
    held = {}

    def grads_ready(group, l, grads):
        held.update({(k, l): g for k, g in grads.items()})
        if l > 0 and group == REST_GROUP:
            return None
        names = [k for k in (IN_GROUP + REST_GROUP if l > 0 else group)]
        srcs = [held[(k, l)] for k in names]
        axes = [axis2d[k] for k in names]
        sizes = [g.shape[ax] // N_CHIPS for g, ax in zip(srcs, axes)]
        lands = [lax.empty((len(OTHER_CHIPS),) + tuple(sz if i == ax else d for i, d in enumerate(g.shape)), g.dtype)
                 for g, ax, sz in zip(srcs, axes, sizes)]
        n = len(names)
        copy = lambda refs, a, j, ss, rs_: _scatter_copy(refs[:n], refs[n:], a, j, axes, sizes, ss, rs_)
        send, recv, arrs, token = _split_start(srcs + lands, copy, n, jnp.zeros((8, LANES), F32), name=f"scatter_start{l}_{names[0]}")
        flying.append((names, l, axes, arrs, send, recv, copy))
        return token

    loss, grad_x, small = _trunk(x[0], mem[0], loss_target[0], p, fetch, grads_ready)

    sums = {}
    out = {}

    def arrive(entry, after):
        names, l, axes, arrs, send, recv, copy = entry
        n = len(names)
        arrs = _split_wait(arrs, send, recv, copy, n, after, name=f"scatter_wait{l}_{names[0]}")
        for k, ax, g, ld in zip(names, axes, arrs[:n], arrs[n:]):
            ld = ld.reshape((len(OTHER_CHIPS), -1, ld.shape[-1]))
            sums[(k, l)] = _sum_own_and_slots(g, ld, ax, chip_arr, name=f"sum{l}_{k}")
        return sums[(names[-1], l)]

    def update(names):
        mine = [sums[(k, l)] for k in names for l in range(depth)]
        theirs = _swap_sibling(mine, name="swap_sums_" + names[0])
        for i, k in enumerate(names):
            gs = [[mine[i * depth + l], theirs[i * depth + l]] for l in range(depth)]
            out[k] = _adamw_layers(w[k], m[k], v[k], gs, name="adamw_" + k)

    after = grad_x
    for entry in flying[:-1]:
        after = arrive(entry, after)
    update(REST_GROUP)

    small_names = [k for k in WEIGHTS if k not in BIG]
    parts = [small[k] for k in small_names] + [loss[0, :1]]
    total = _unpack(_allreduce_small(_pack(parts), name="allreduce_small"), parts)
    loss = total[-1][0]
    g_small = dict(zip(small_names, total[:-1]))
    for k, ax in SMALL_SHARDED.items():
        size = w[k].shape[ax]
        g_small[k] = lax.dynamic_slice_in_dim(g_small[k], chip * size, size, axis=ax)
    res = _adamw(_pack([w[k] for k in small_names]), _pack([m[k] for k in small_names]), _pack([v[k] for k in small_names]),
                 [_pack([g_small[k] for k in small_names])], name="adamw_small")
    last = res[0]
    res = [_unpack(r, [w[k] for k in small_names]) for r in res]
    for i, k in enumerate(small_names):
        out[k] = [r[i] for r in res]

    arrive(flying[-1], last)
    update(IN_GROUP)

    outs = [loss, grad_x[None]]
    for part in range(4):
        outs += [out[k][part] for k in WEIGHTS]
    return tuple(outs)
```
